```python
import math
import jax, jax.numpy as jnp
from jax import lax
import numpy as np

D_MODEL = 2048
BATCH = 32
SEQ = 256
DEPTH = 2
DEC_BATCH = 2
DEC_SEQ = 2048
PAST_LEN = 512

GRID_W = 64
N_HEADS = D_MODEL // 256
HEAD_DIM = 128
ATT_W = N_HEADS * HEAD_DIM
MAX_KH = 8
KW = 16
QB = 16
KSPAN = QB + KW
N_QBLK = GRID_W // QB
Q_BLOCK = 128
FNET_GROUPS = 4
FNET_GDIM = D_MODEL // 16
FNET_W = FNET_GROUPS * FNET_GDIM
HY_W = D_MODEL // 4
HYENA_ORDER = 2
FILTER_EMB = 33
FILTER_FF = 64
MIN_DECAY = math.log(1e-2) / 1.5
MAX_DECAY = math.log(1e-2) / 0.3
W_MIX = ATT_W + FNET_W + HY_W
OFF_Q = 0
OFF_K = OFF_Q + ATT_W
OFF_V = OFF_K + ATT_W
OFF_GA = OFF_V + ATT_W
OFF_UB = OFF_GA + ATT_W
OFF_GB = OFF_UB + FNET_W
OFF_HC = OFF_GB + FNET_W
OFF_GC = OFF_HC + 3 * HY_W
OFF_MG = OFF_GC + HY_W
N_IN = OFF_MG + 3 * D_MODEL
EPS = 1e-6
NEG = -1e30

kernel_name = 'hybrid_natten_fnet_hyena_dit_step'


def _rmsnorm(x, g):
    xf = x.astype(jnp.float32)
    y = xf * lax.rsqrt(jnp.mean(xf * xf, axis=-1, keepdims=True) + EPS)
    return (y * g.astype(jnp.float32)).astype(x.dtype)


def _modulated_proj(x, cvec, norm_g, w_ada, b_ada, w_in):
    ada = jax.nn.silu(cvec) @ w_ada + b_ada
    shift, scale, gate = jnp.split(ada, 3, axis=-1)
    h = _rmsnorm(x, norm_g) * (1.0 + scale[:, None, :]) + shift[:, None, :]
    return h @ w_in, gate


def _qkv(proj, q_g, k_g):
    B, L, _ = proj.shape
    q = proj[..., OFF_Q:OFF_K].reshape(B, L, N_HEADS, HEAD_DIM)
    k = proj[..., OFF_K:OFF_V].reshape(B, L, N_HEADS, HEAD_DIM)
    v = proj[..., OFF_V:OFF_GA].reshape(B, L, N_HEADS, HEAD_DIM)
    return _rmsnorm(q, q_g), _rmsnorm(k, k_g), v


def _context_attention(q, k, v):
    B, L, H, Dh = q.shape
    scale = Dh ** -0.5
    qb = q.reshape(B, L // Q_BLOCK, Q_BLOCK, H, Dh).swapaxes(0, 1)

    def blk(qi):
        s = jnp.einsum('bqhd,bkhd->bhqk', qi, k).astype(jnp.float32) * scale
        p = jax.nn.softmax(s, axis=-1).astype(v.dtype)
        return jnp.einsum('bhqk,bkhd->bqhd', p, v)

    o = lax.map(blk, qb)
    return o.swapaxes(0, 1).reshape(B, L, H, Dh)


def _latent_attention(q, k, v, k_ctx, v_ctx, rpb):
    B, L, H, Dh = q.shape
    rows = L // GRID_W
    kh = min(MAX_KH, rows)
    scale = Dh ** -0.5
    qg = q.reshape(B, rows, GRID_W, H, Dh)
    kg = k.reshape(B, rows, GRID_W, H, Dh)
    vg = v.reshape(B, rows, GRID_W, H, Dh)
    cols = np.arange(GRID_W)
    col_start = np.clip(cols - KW // 2, 0, GRID_W - KW)
    blks = np.arange(N_QBLK)
    span_start = np.clip(blks * QB - KW // 2, 0, GRID_W - KSPAN)
    span_cols = span_start[:, None] + np.arange(KSPAN)[None, :]
    q_cols = blks[:, None] * QB + np.arange(QB)[None, :]
    q_cs = col_start[q_cols]
    kc = span_cols[:, None, :]
    col_valid = (kc >= q_cs[..., None]) & (kc < q_cs[..., None] + KW)
    dc_idx = np.clip(kc - q_cols[..., None] + KW - 1, 0, 2 * KW - 2)
    col_bias = rpb[:, :, dc_idx]
    valid = col_valid[:, :, None, :]
    n_win = kh * KSPAN

    def row(r):
        rs = jnp.clip(r - kh // 2, 0, rows - kh)
        q_r = lax.dynamic_index_in_dim(qg, r, axis=1, keepdims=False)
        q_r = q_r.reshape(B, N_QBLK, QB, H, Dh)
        k_blk = lax.dynamic_slice_in_dim(kg, rs, kh, axis=1)[:, :, span_cols]
        v_blk = lax.dynamic_slice_in_dim(vg, rs, kh, axis=1)[:, :, span_cols]
        dr_idx = rs + jnp.arange(kh) - r + (MAX_KH - 1)
        bias = jnp.take(col_bias, dr_idx, axis=1).transpose(0, 2, 3, 1, 4)
        s_win = (jnp.einsum('bnqhd,binkhd->bhnqik', q_r, k_blk).astype(jnp.float32) * scale
                 + bias.astype(jnp.float32))
        s_win = jnp.where(valid, s_win, NEG)
        s_ctx = jnp.einsum('bnqhd,bchd->bhnqc', q_r, k_ctx).astype(jnp.float32) * scale
        s = jnp.concatenate([s_win.reshape(B, H, N_QBLK, QB, n_win), s_ctx], axis=-1)
        p = jax.nn.softmax(s, axis=-1).astype(v.dtype)
        p_win = p[..., :n_win].reshape(B, H, N_QBLK, QB, kh, KSPAN)
        p_ctx = p[..., n_win:]
        o = (jnp.einsum('bhnqik,binkhd->bnqhd', p_win, v_blk)
             + jnp.einsum('bhnqc,bchd->bnqhd', p_ctx, v_ctx))
        return o.reshape(B, GRID_W, H, Dh)

    o = lax.map(row, jnp.arange(rows))
    return o.swapaxes(0, 1).reshape(B, L, H, Dh)


def _fourier_mix(u):
    B, L, _ = u.shape
    uf = u.astype(jnp.float32).reshape(B, L, FNET_GROUPS, FNET_GDIM)
    y = jnp.fft.fft2(uf, axes=(1, 3), norm='ortho').real
    return y.reshape(B, L, FNET_W).astype(u.dtype)


def _short_conv(x, w, b):
    L = x.shape[1]
    xp = jnp.pad(x, ((0, 0), (1, 1), (0, 0)))
    return xp[:, 0:L] * w[0] + xp[:, 1:L + 1] * w[1] + xp[:, 2:L + 2] * w[2] + b


def _hyena_filter_fft(L, f_w1, f_b1, f_freq, f_w2, f_b2, f_w3):
    t = jnp.linspace(0.0, 1.0, L, dtype=jnp.float32)[:, None]
    bands = (FILTER_EMB - 1) // 2
    w = (2.0 * math.pi / L) * jnp.arange(L, dtype=jnp.float32)[:, None]
    f = jnp.linspace(1e-4, bands - 1, bands, dtype=jnp.float32)[None, :]
    z = jnp.concatenate([t, jnp.cos(w * f), -jnp.sin(w * f)], axis=-1)
    h = jnp.sin(f_freq * (z @ f_w1 + f_b1))
    h = jnp.sin(f_freq * (h @ f_w2 + f_b2))
    h = (h @ f_w3).astype(jnp.float32).reshape(L, HYENA_ORDER, 2, HY_W)
    deltas = jnp.abs(jnp.linspace(MIN_DECAY, MAX_DECAY, HY_W, dtype=jnp.float32))
    h = h * jnp.exp(-t[:, :, None, None] * deltas)
    h = h / (jnp.sum(jnp.abs(h), axis=0, keepdims=True) + EPS)
    fwd = h[:, :, 0]
    bwd = h[:, :, 1]
    k2 = jnp.concatenate([fwd, jnp.zeros_like(fwd[:1]), bwd[:0:-1]], axis=0)
    return jnp.fft.rfft(k2, axis=0)


def _long_conv(z, kf):
    L = z.shape[1]
    zf = jnp.fft.rfft(z.astype(jnp.float32), n=2 * L, axis=1)
    y = jnp.fft.irfft(zf * kf[None], n=2 * L, axis=1)[:, :L]
    return y.astype(z.dtype)


def _hyena_mix(hin, conv_w, conv_b, kf, hy_bias):
    hc = _short_conv(hin, conv_w, conv_b)
    v, x1, x2 = jnp.split(hc, 3, axis=-1)
    z = v
    for o, xg in enumerate((x1, x2)):
        z = xg * (_long_conv(z, kf[:, o]) + hy_bias[o] * z)
    return z


def _branches_residual(x, proj, gate, att, conv_w, conv_b, kf, hy_bias, w_br, w_out):
    B, L, _ = x.shape
    y_a = att.reshape(B, L, ATT_W) * jax.nn.silu(proj[..., OFF_GA:OFF_UB])
    y_b = _fourier_mix(proj[..., OFF_UB:OFF_GB]) * jax.nn.silu(proj[..., OFF_GB:OFF_HC])
    y_c = (_hyena_mix(proj[..., OFF_HC:OFF_GC], conv_w, conv_b, kf, hy_bias)
           * jax.nn.silu(proj[..., OFF_GC:OFF_MG]))
    g_a, g_b, g_c = jnp.split(jax.nn.sigmoid(proj[..., OFF_MG:]), 3, axis=-1)
    merged = (g_a * (y_a @ w_br[:ATT_W])
              + g_b * (y_b @ w_br[ATT_W:ATT_W + FNET_W])
              + g_c * (y_c @ w_br[ATT_W + FNET_W:]))
    return x + gate[:, None, :] * (merged @ w_out)


def setup_inputs(seed: int = 0) -> dict:
    key = jax.random.key(seed)
    ks = jax.random.split(key, 24)
    nrm = jax.random.normal
    f32 = jnp.float32
    return {
        'x_prompt': nrm(ks[0], (BATCH, SEQ, D_MODEL), f32),
        'x_sample': nrm(ks[1], (DEC_BATCH, DEC_SEQ, D_MODEL), f32),
        'cache_k': nrm(ks[2], (DEC_BATCH, DEPTH, PAST_LEN, N_HEADS, HEAD_DIM), f32),
        'cache_v': nrm(ks[3], (DEC_BATCH, DEPTH, PAST_LEN, N_HEADS, HEAD_DIM), f32),
        'c': nrm(ks[4], (DEC_BATCH, D_MODEL), f32),
        'c_ctx': nrm(ks[5], (D_MODEL,), f32),
        'norm_g': 1.0 + 0.01 * nrm(ks[6], (DEPTH, D_MODEL), f32),
        'w_ada': nrm(ks[7], (DEPTH, D_MODEL, 3 * D_MODEL), f32) * D_MODEL ** -0.5,
        'b_ada': 0.02 * nrm(ks[8], (DEPTH, 3 * D_MODEL), f32),
        'w_in': nrm(ks[9], (DEPTH, D_MODEL, N_IN), f32) * D_MODEL ** -0.5,
        'q_norm_g': 1.0 + 0.01 * nrm(ks[10], (DEPTH, HEAD_DIM), f32),
        'k_norm_g': 1.0 + 0.01 * nrm(ks[11], (DEPTH, HEAD_DIM), f32),
        'rpb': 0.1 * nrm(ks[12], (DEPTH, N_HEADS, 2 * MAX_KH - 1, 2 * KW - 1), f32),
        'conv_w': nrm(ks[13], (DEPTH, 3, 3 * HY_W), f32) * 3 ** -0.5,
        'conv_b': 0.02 * nrm(ks[14], (DEPTH, 3 * HY_W), f32),
        'f_w1': nrm(ks[15], (DEPTH, FILTER_EMB, FILTER_FF), f32) * FILTER_EMB ** -0.5,
        'f_b1': 0.1 * nrm(ks[16], (DEPTH, FILTER_FF), f32),
        'f_freq': 1.0 + 0.01 * nrm(ks[17], (DEPTH, FILTER_FF), f32),
        'f_w2': nrm(ks[18], (DEPTH, FILTER_FF, FILTER_FF), f32) * FILTER_FF ** -0.5,
        'f_b2': 0.1 * nrm(ks[19], (DEPTH, FILTER_FF), f32),
        'f_w3': nrm(ks[20], (DEPTH, FILTER_FF, HYENA_ORDER * 2 * HY_W), f32) * FILTER_FF ** -0.5,
        'hy_bias': 0.5 * nrm(ks[21], (DEPTH, HYENA_ORDER, HY_W), f32),
        'w_br': nrm(ks[22], (DEPTH, W_MIX, D_MODEL), f32) * ATT_W ** -0.5,
        'w_out': nrm(ks[23], (DEPTH, D_MODEL, D_MODEL), f32) * D_MODEL ** -0.5,
    }


def reference(x_prompt, x_sample, cache_k, cache_v, c, c_ctx, norm_g, w_ada, b_ada, w_in,
              q_norm_g, k_norm_g, rpb, conv_w, conv_b, f_w1, f_b1, f_freq, f_w2, f_b2,
              f_w3, hy_bias, w_br, w_out):
    len_prompt = x_prompt.shape[1]
    len_sample = x_sample.shape[1]

    y = x_prompt
    ks, vs = [], []
    for l in range(DEPTH):
        proj, gate = _modulated_proj(y, c_ctx[None, :], norm_g[l], w_ada[l], b_ada[l], w_in[l])
        q, k, v = _qkv(proj, q_norm_g[l], k_norm_g[l])
        ks.append(k)
        vs.append(v)
        att = _context_attention(q, k, v)
        kf = _hyena_filter_fft(len_prompt, f_w1[l], f_b1[l], f_freq[l], f_w2[l], f_b2[l], f_w3[l])
        y = _branches_residual(y, proj, gate, att, conv_w[l], conv_b[l], kf, hy_bias[l],
                               w_br[l], w_out[l])
    new_k = jnp.stack(ks, axis=1)
    new_v = jnp.stack(vs, axis=1)

    z = x_sample
    for l in range(DEPTH):
        proj, gate = _modulated_proj(z, c, norm_g[l], w_ada[l], b_ada[l], w_in[l])
        q, k, v = _qkv(proj, q_norm_g[l], k_norm_g[l])
        att = _latent_attention(q, k, v, cache_k[:, l], cache_v[:, l], rpb[l])
        kf = _hyena_filter_fft(len_sample, f_w1[l], f_b1[l], f_freq[l], f_w2[l], f_b2[l], f_w3[l])
        z = _branches_residual(z, proj, gate, att, conv_w[l], conv_b[l], kf, hy_bias[l],
                               w_br[l], w_out[l])

    return (y, z, new_k, new_v)
```

```python
import functools
import math

import jax
import jax.numpy as jnp
from jax import lax
from jax.experimental import pallas as pl
from jax.experimental.pallas import tpu as pltpu

F32 = jnp.float32
BF16 = jnp.bfloat16

D_MODEL = 2048
GRID_W = 64
N_HEADS = 8
HEAD_DIM = 128
ATT_W = N_HEADS * HEAD_DIM
MAX_KH = 8
KW = 16
FNET_GROUPS = 4
FNET_GDIM = 128
FNET_W = FNET_GROUPS * FNET_GDIM
HY_W = 512
HYENA_ORDER = 2
FILTER_EMB = 33
FILTER_FF = 64
MIN_DECAY = math.log(1e-2) / 1.5
MAX_DECAY = math.log(1e-2) / 0.3
OFF_Q = 0
OFF_K = OFF_Q + ATT_W
OFF_V = OFF_K + ATT_W
OFF_GA = OFF_V + ATT_W
OFF_UB = OFF_GA + ATT_W
OFF_GB = OFF_UB + FNET_W
OFF_HC = OFF_GB + FNET_W
OFF_GC = OFF_HC + 3 * HY_W
OFF_MG = OFF_GC + HY_W
N_IN = OFF_MG + 3 * D_MODEL
EPS = 1e-6
NEG = -1e30
ATT_SCALE = HEAD_DIM ** -0.5

ADA_ROWS = 8
FREQ_CHUNK = 512
MIB = 1024 * 1024


def _params(semantics, vmem_mib):
    return pltpu.CompilerParams(dimension_semantics=semantics,
                                vmem_limit_bytes=vmem_mib * MIB)


def _silu(x):
    return x * jax.nn.sigmoid(x)


def _dot(a, b):
    return jnp.dot(a, b, preferred_element_type=F32)


def _dot_nt(a, b):
    return lax.dot_general(a, b, (((1,), (1,)), ((), ())), preferred_element_type=F32)


def _ada_kernel(cv_ref, w_ref, b_ref, o_ref):
    s = _silu(cv_ref[...]).astype(BF16)
    o_ref[...] = _dot(s, w_ref[...].astype(BF16)) + b_ref[...]


def _ada(cv, w_ada, b_ada):
    depth, d, n = w_ada.shape
    tn = 512
    return pl.pallas_call(
        _ada_kernel,
        grid=(depth, n // tn),
        in_specs=[
            pl.BlockSpec((ADA_ROWS, d), lambda l, j: (0, 0)),
            pl.BlockSpec((None, d, tn), lambda l, j: (l, 0, j)),
            pl.BlockSpec((None, 1, tn), lambda l, j: (l, 0, j)),
        ],
        out_specs=pl.BlockSpec((None, ADA_ROWS, tn), lambda l, j: (l, 0, j)),
        out_shape=jax.ShapeDtypeStruct((depth, ADA_ROWS, n), F32),
        compiler_params=_params(("arbitrary", "arbitrary"), 32),
        name="ada",
    )(cv, w_ada, b_ada.reshape(depth, 1, n))


def _inproj_kernel(x_ref, g_ref, ada_ref, w_ref, o_ref, h_ref, *, rows):
    @pl.when(pl.program_id(1) == 0)
    def _():
        g = g_ref[...]
        shift = ada_ref[:, 0:D_MODEL]
        scale1 = 1.0 + ada_ref[:, D_MODEL:2 * D_MODEL]

        def body(r, carry):
            sl = pl.ds(pl.multiple_of(r * rows, rows), rows)
            x = x_ref[sl, :]
            ms = jnp.mean(x * x, axis=-1, keepdims=True)
            y = x * lax.rsqrt(ms + EPS) * g
            h_ref[sl, :] = (y * scale1 + shift).astype(BF16)
            return carry

        lax.fori_loop(0, x_ref.shape[0] // rows, body, 0)

    o_ref[...] = _dot(h_ref[...], w_ref[...])


def _inproj(x2d, norm_g3, ada4, ada_row, w_in_bf, l, seq):
    m, d = x2d.shape
    tm = min(1024, seq)
    tn = 1024
    per_seq = seq // tm
    return pl.pallas_call(
        functools.partial(_inproj_kernel, rows=128),
        grid=(m // tm, N_IN // tn),
        in_specs=[
            pl.BlockSpec((tm, d), lambda i, j: (i, 0)),
            pl.BlockSpec((None, 1, d), lambda i, j: (l, 0, 0)),
            pl.BlockSpec((None, None, 1, 3 * d), lambda i, j: (l, ada_row(i // per_seq), 0, 0)),
            pl.BlockSpec((None, d, tn), lambda i, j: (l, 0, j)),
        ],
        out_specs=pl.BlockSpec((tm, tn), lambda i, j: (i, j)),
        out_shape=jax.ShapeDtypeStruct((m, N_IN), F32),
        scratch_shapes=[pltpu.VMEM((tm, d), BF16)],
        compiler_params=_params(("arbitrary", "arbitrary"), 48),
        name="inproj",
    )(x2d, norm_g3, ada4, w_in_bf)


def _head_norm_store(x_ref, g, out_refs):
    for h in range(N_HEADS):
        sl = slice(h * HEAD_DIM, (h + 1) * HEAD_DIM)
        x = x_ref[:, sl]
        ms = jnp.mean(x * x, axis=-1, keepdims=True)
        y = x * lax.rsqrt(ms + EPS) * g
        for ref in out_refs:
            ref[:, sl] = y.astype(ref.dtype)


def _prep_ctx_kernel(q_ref, k_ref, v_ref, qg_ref, kg_ref, *rest):
    qn_ref, kn_ref, vb_ref, nk_ref, nv_ref = rest[-5:]
    _head_norm_store(q_ref, qg_ref[...], (qn_ref,))
    _head_norm_store(k_ref, kg_ref[...], (kn_ref, nk_ref))
    v = v_ref[...]
    vb_ref[...] = v.astype(BF16)
    nv_ref[...] = v


def _prep_lat_kernel(q_ref, k_ref, v_ref, qg_ref, kg_ref, qn_ref, kn_ref, vb_ref):
    _head_norm_store(q_ref, qg_ref[...], (qn_ref,))
    _head_norm_store(k_ref, kg_ref[...], (kn_ref,))
    vb_ref[...] = v_ref[...].astype(BF16)


def _prep(proj, qg3, kg3, l, seq, new_kv):
    m = proj.shape[0]
    depth = qg3.shape[0]
    tm = min(256, seq)
    col = lambda c: pl.BlockSpec((tm, ATT_W), lambda i: (i, c))
    gspec = pl.BlockSpec((None, 1, HEAD_DIM), lambda i: (l, 0, 0))
    in_specs = [col(OFF_Q // ATT_W), col(OFF_K // ATT_W), col(OFF_V // ATT_W), gspec, gspec]
    bf_shape = jax.ShapeDtypeStruct((m, ATT_W), BF16)
    bf_spec = pl.BlockSpec((tm, ATT_W), lambda i: (i, 0))
    if new_kv is None:
        return pl.pallas_call(
            _prep_lat_kernel, grid=(m // tm,), in_specs=in_specs,
            out_specs=[bf_spec] * 3, out_shape=[bf_shape] * 3,
            compiler_params=_params(("arbitrary",), 32), name="prep_lat",
        )(proj, proj, proj, qg3, kg3)
    assert tm == seq
    batch = m // seq
    kv_shape = jax.ShapeDtypeStruct((batch, depth, seq, ATT_W), F32)
    kv_spec = pl.BlockSpec((None, None, seq, ATT_W), lambda i: (i, l, 0, 0))
    return pl.pallas_call(
        _prep_ctx_kernel, grid=(m // tm,),
        in_specs=in_specs + [pl.BlockSpec(memory_space=pl.ANY)] * 2,
        out_specs=[bf_spec] * 3 + [kv_spec] * 2,
        out_shape=[bf_shape] * 3 + [kv_shape] * 2,
        input_output_aliases={5: 3, 6: 4},
        compiler_params=_params(("arbitrary",), 32), name="prep_ctx",
    )(proj, proj, proj, qg3, kg3, *new_kv)


def _ctx_attn_kernel(q_ref, k_ref, v_ref, ga_ref, o_ref):
    for h in range(N_HEADS):
        sl = slice(h * HEAD_DIM, (h + 1) * HEAD_DIM)
        s = _dot_nt(q_ref[:, sl], k_ref[:, sl]) * ATT_SCALE
        p = jnp.exp(s - jnp.max(s, axis=-1, keepdims=True))
        inv = 1.0 / jnp.sum(p, axis=-1, keepdims=True)
        o = _dot(p.astype(BF16), v_ref[:, sl]) * inv
        o_ref[:, sl] = (o * _silu(ga_ref[:, sl])).astype(BF16)


def _ctx_attn(qn, kn, vb, proj, seq):
    m = qn.shape[0]
    spec = pl.BlockSpec((seq, ATT_W), lambda b: (b, 0))
    return pl.pallas_call(
        _ctx_attn_kernel, grid=(m // seq,),
        in_specs=[spec, spec, spec, pl.BlockSpec((seq, ATT_W), lambda b: (b, OFF_GA // ATT_W))],
        out_specs=spec,
        out_shape=jax.ShapeDtypeStruct((m, ATT_W), BF16),
        compiler_params=_params(("arbitrary",), 32), name="ctx_attn",
    )(qn, kn, vb, proj)


def _window_bias(rpb_l):
    c = jnp.arange(GRID_W)[:, None]
    kc = jnp.arange(GRID_W)[None, :]
    cs = jnp.clip(c - KW // 2, 0, GRID_W - KW)
    valid = (kc >= cs) & (kc < cs + KW)
    dc = jnp.clip(kc - c + KW - 1, 0, 2 * KW - 2)
    cb = jnp.where(valid[None, None], rpb_l[:, :, dc], NEG)
    return jnp.concatenate([cb[:, :-1], cb[:, 1:]], axis=-1).astype(F32)


def _lat_attn_kernel(q_ref, k_ref, v_ref, ck_ref, cv_ref, bias_ref, ga_ref, o_ref,
                     ckb_ref, cvb_ref, *, rows, kh):
    r = pl.program_id(1)

    @pl.when(r == 0)
    def _():
        ckb_ref[...] = ck_ref[...].astype(BF16)
        cvb_ref[...] = cv_ref[...].astype(BF16)

    rs = jnp.clip(r - kh // 2, 0, rows - kh)
    win = pl.ds(pl.multiple_of(rs * GRID_W, GRID_W), kh * GRID_W)
    d0 = rs - r + (MAX_KH - 1)
    for h in range(N_HEADS):
        sl = slice(h * HEAD_DIM, (h + 1) * HEAD_DIM)
        q = q_ref[:, sl]
        bias = jnp.concatenate([bias_ref[h, d0 + 2 * j] for j in range(kh // 2)], axis=-1)
        s_win = _dot_nt(q, k_ref[win, sl]) * ATT_SCALE + bias
        s_ctx = _dot_nt(q, ckb_ref[:, sl]) * ATT_SCALE
        mx = jnp.maximum(jnp.max(s_win, axis=-1, keepdims=True),
                         jnp.max(s_ctx, axis=-1, keepdims=True))
        p_win = jnp.exp(s_win - mx)
        p_ctx = jnp.exp(s_ctx - mx)
        inv = 1.0 / (jnp.sum(p_win, axis=-1, keepdims=True) + jnp.sum(p_ctx, axis=-1, keepdims=True))
        o = (_dot(p_win.astype(BF16), v_ref[win, sl]) + _dot(p_ctx.astype(BF16), cvb_ref[:, sl])) * inv
        o_ref[:, sl] = (o * _silu(ga_ref[:, sl])).astype(BF16)


def _lat_attn(qn, kn, vb, cache_k4, cache_v4, bias, proj, l, seq):
    m = qn.shape[0]
    batch = m // seq
    rows = seq // GRID_W
    kh = min(MAX_KH, rows)
    past = cache_k4.shape[2]
    qspec = pl.BlockSpec((GRID_W, ATT_W), lambda b, r: (b * rows + r, 0))
    kvspec = pl.BlockSpec((seq, ATT_W), lambda b, r: (b, 0))
    cspec = pl.BlockSpec((None, None, past, ATT_W), lambda b, r: (b, l, 0, 0))
    return pl.pallas_call(
        functools.partial(_lat_attn_kernel, rows=rows, kh=kh),
        grid=(batch, rows),
        in_specs=[qspec, kvspec, kvspec, cspec, cspec,
                  pl.BlockSpec(bias.shape, lambda b, r: (0, 0, 0, 0)),
                  pl.BlockSpec((GRID_W, ATT_W), lambda b, r: (b * rows + r, OFF_GA // ATT_W))],
        out_specs=qspec,
        out_shape=jax.ShapeDtypeStruct((m, ATT_W), BF16),
        scratch_shapes=[pltpu.VMEM((past, ATT_W), BF16), pltpu.VMEM((past, ATT_W), BF16)],
        compiler_params=_params(("arbitrary", "arbitrary"), 48), name="lat_attn",
    )(qn, kn, vb, cache_k4, cache_v4, bias, proj)


def _fnet_tables(seq):
    def cs(n):
        i = jnp.arange(n, dtype=jnp.int32)
        ang = ((i[:, None] * i[None, :]) % n).astype(F32) * (2.0 * math.pi / n)
        return jnp.cos(ang), jnp.sin(ang)

    cl, sl = cs(seq)
    cc, sc = cs(FNET_GDIM)
    return (jnp.concatenate([cl, -sl], axis=1).astype(BF16), cc.astype(BF16), sc.astype(BF16))


def _fnet_kernel(u_ref, gb_ref, tl_ref, cc_ref, sc_ref, o_ref, t_ref, *, seq, tl):
    i = pl.program_id(1)

    @pl.when(i == 0)
    def _():
        for g in range(FNET_GROUPS):
            sl = slice(g * FNET_GDIM, (g + 1) * FNET_GDIM)
            ug = u_ref[:, sl].astype(BF16)
            t_ref[0:seq, sl] = _dot(ug, cc_ref[...]).astype(BF16)
            t_ref[seq:2 * seq, sl] = _dot(ug, sc_ref[...]).astype(BF16)

    y = _dot(tl_ref[...], t_ref[...]) * ((seq * FNET_GDIM) ** -0.5)
    o_ref[...] = (y * _silu(gb_ref[...])).astype(BF16)


def _fnet(proj, tables, seq):
    m = proj.shape[0]
    tab_l, cc, sc = tables
    tl = min(512, seq)
    nt = seq // tl
    return pl.pallas_call(
        functools.partial(_fnet_kernel, seq=seq, tl=tl),
        grid=(m // seq, nt),
        in_specs=[
            pl.BlockSpec((seq, FNET_W), lambda b, i: (b, OFF_UB // FNET_W)),
            pl.BlockSpec((tl, FNET_W), lambda b, i: (b * nt + i, OFF_GB // FNET_W)),
            pl.BlockSpec((tl, 2 * seq), lambda b, i: (i, 0)),
            pl.BlockSpec((FNET_GDIM, FNET_GDIM), lambda b, i: (0, 0)),
            pl.BlockSpec((FNET_GDIM, FNET_GDIM), lambda b, i: (0, 0)),
        ],
        out_specs=pl.BlockSpec((tl, FNET_W), lambda b, i: (b * nt + i, 0)),
        out_shape=jax.ShapeDtypeStruct((m, FNET_W), BF16),
        scratch_shapes=[pltpu.VMEM((2 * seq, FNET_W), BF16)],
        compiler_params=_params(("arbitrary", "arbitrary"), 40), name="fnet",
    )(proj, proj, tab_l, cc, sc)


def _hyena_tables(seq):
    fc = min(FREQ_CHUNK, seq)
    nch = seq // fc
    n = 2 * seq
    i = jnp.arange(seq, dtype=jnp.int32)
    ang = ((i[:, None] * i[None, :]) % n).astype(F32) * (math.pi / seq)
    c = jnp.cos(ang)
    s = jnp.sin(ang)
    sgn = (1 - 2 * (i % 2)).astype(F32)
    first = (i == 0)
    sec = jnp.where(first[:, None], sgn[None, :], -s)
    fwd = jnp.concatenate([c.reshape(nch, fc, seq), sec.reshape(nch, fc, seq)], axis=1)
    w = jnp.where(first, 1.0 / n, 2.0 / n).astype(F32)
    ic = c * w[None, :]
    isec = jnp.where(first[None, :], sgn[:, None] / n, -s * (2.0 / n))
    inv = jnp.concatenate([ic.reshape(seq, nch, fc), isec.reshape(seq, nch, fc)], axis=2)
    return fwd.astype(BF16), inv.transpose(1, 0, 2).astype(BF16)


def _filter_features(seq):
    t = jnp.linspace(0.0, 1.0, seq, dtype=F32)[:, None]
    bands = (FILTER_EMB - 1) // 2
    w = (2.0 * math.pi / seq) * jnp.arange(seq, dtype=F32)[:, None]
    f = jnp.linspace(1e-4, bands - 1, bands, dtype=F32)[None, :]
    z = jnp.concatenate([t, jnp.cos(w * f), -jnp.sin(w * f)], axis=-1)
    z = jnp.pad(z, ((0, 0), (0, 128 - FILTER_EMB)))
    deltas = jnp.abs(jnp.linspace(MIN_DECAY, MAX_DECAY, HY_W, dtype=F32))[None, :]
    return z, t, deltas


def _dot_f32(a, b):
    return jnp.dot(a, b, preferred_element_type=F32, precision=lax.Precision.HIGHEST)


def _filter_kernel(z_ref, t_ref, dl_ref, w1_ref, b1_ref, fr_ref, w2_ref, b2_ref, w3_ref,
                   fwd_ref, o_ref, h_ref, sd_ref, nyq_ref, *, seq, fc, rows):
    j = pl.program_id(0)
    nblk = 2 * HYENA_ORDER
    nrc = seq // rows

    @pl.when(j == 0)
    def _():
        fr = fr_ref[...]

        def taps(r, acc):
            sl = pl.ds(pl.multiple_of(r * rows, rows), rows)
            h = jnp.sin(fr * (_dot_f32(z_ref[sl, :], w1_ref[...]) + b1_ref[...]))
            h = jnp.sin(fr * (_dot_f32(h, w2_ref[...]) + b2_ref[...]))
            h = _dot_f32(h, w3_ref[...])
            decay = jnp.exp(-t_ref[sl, :] * dl_ref[...])
            h = h * jnp.concatenate([decay] * nblk, axis=-1)
            h_ref[sl, :] = h
            return acc + jnp.sum(jnp.abs(h), axis=0, keepdims=True)

        tot = lax.fori_loop(0, nrc, taps, jnp.zeros((1, nblk * HY_W), F32))
        inv = 1.0 / (tot + EPS)
        sgn = (1 - 2 * (lax.broadcasted_iota(jnp.int32, (rows, 1), 0) % 2)).astype(F32)

        def fold(r, acc):
            sl = pl.ds(pl.multiple_of(r * rows, rows), rows)
            h = h_ref[sl, :] * inv
            pos = r * rows + lax.broadcasted_iota(jnp.int32, (rows, 1), 0)
            sums = []
            for o in range(HYENA_ORDER):
                fw = h[:, (2 * o) * HY_W:(2 * o + 1) * HY_W]
                bw = jnp.where(pos == 0, 0.0, h[:, (2 * o + 1) * HY_W:(2 * o + 2) * HY_W])
                sums.append(fw + bw)
                sd_ref[0, sl, o * HY_W:(o + 1) * HY_W] = (fw + bw).astype(BF16)
                sd_ref[1, sl, o * HY_W:(o + 1) * HY_W] = (fw - bw).astype(BF16)
            return acc + jnp.sum(jnp.concatenate(sums, axis=-1) * sgn, axis=0, keepdims=True)

        nyq_ref[...] = lax.fori_loop(0, nrc, fold, jnp.zeros((1, HYENA_ORDER * HY_W), F32))

    re = _dot(fwd_ref[0:fc, :], sd_ref[0])
    sec = _dot(fwd_ref[fc:2 * fc, :], sd_ref[1])
    row = lax.broadcasted_iota(jnp.int32, (fc, 1), 0)
    sec = jnp.where((row == 0) & (j == 0), nyq_ref[...], sec)
    o_ref[0:fc, :] = re
    o_ref[fc:2 * fc, :] = sec


def _filter(feats, fwd, w1p, b1, fr, w2, b2, w3, l, seq):
    z, t, deltas = feats
    nch, fc2, _ = fwd.shape
    fc = fc2 // 2
    width = HYENA_ORDER * HY_W
    rows = min(256, seq)
    full = lambda a: pl.BlockSpec(a.shape, lambda j: (0,) * a.ndim)
    lay = lambda a: pl.BlockSpec((None,) + a.shape[1:], lambda j: (l,) + (0,) * (a.ndim - 1))
    return pl.pallas_call(
        functools.partial(_filter_kernel, seq=seq, fc=fc, rows=rows),
        grid=(nch,),
        in_specs=[full(z), full(t), full(deltas), lay(w1p), lay(b1), lay(fr), lay(w2), lay(b2), lay(w3),
                  pl.BlockSpec((None, 2 * fc, seq), lambda j: (j, 0, 0))],
        out_specs=pl.BlockSpec((None, 2 * fc, width), lambda j: (j, 0, 0)),
        out_shape=jax.ShapeDtypeStruct((nch, 2 * fc, width), F32),
        scratch_shapes=[pltpu.VMEM((seq, 2 * width), F32),
                        pltpu.VMEM((2, seq, width), BF16),
                        pltpu.VMEM((1, width), F32)],
        compiler_params=_params(("arbitrary",), 48), name="hyena_filter",
    )(z, t, deltas, w1p, b1, fr, w2, b2, w3, fwd)


def _hyena_kernel(v_ref, x1_ref, x2_ref, gc_ref, cw_ref, cb_ref, hb_ref, fwd_ref, inv_ref, kf_ref,
                  o_ref, z_ref, zb_ref, y_ref, *, seq, fc):
    o = pl.program_id(2)
    j = pl.program_id(3)
    last_j = pl.num_programs(3) - 1

    def short_conv(ref, k):
        x = ref[...]
        pos = lax.broadcasted_iota(jnp.int32, (seq, 1), 0)
        prev = jnp.where(pos == 0, 0.0, pltpu.roll(x, 1, 0))
        nxt = jnp.where(pos == seq - 1, 0.0, pltpu.roll(x, seq - 1, 0))
        return (prev * cw_ref[0, k:k + 1, :] + x * cw_ref[1, k:k + 1, :]
                + nxt * cw_ref[2, k:k + 1, :] + cb_ref[k:k + 1, :])

    def start(z):
        z_ref[...] = z
        zb_ref[...] = z.astype(BF16)
        y_ref[...] = jnp.zeros_like(y_ref)

    @pl.when((o == 0) & (j == 0))
    def _():
        start(short_conv(v_ref, 0))

    @pl.when((o == 1) & (j == 0))
    def _():
        start(short_conv(x1_ref, 1) * (y_ref[...] + hb_ref[0:1, :] * z_ref[...]))

    zf = _dot(fwd_ref[...], zb_ref[...])
    zr, zs = zf[0:fc], zf[fc:2 * fc]
    kr, ks = kf_ref[0:fc, :], kf_ref[fc:2 * fc, :]
    nyq = (lax.broadcasted_iota(jnp.int32, (fc, 1), 0) == 0) & (j == 0)
    ss = zs * ks
    yr = zr * kr - jnp.where(nyq, 0.0, ss)
    ys = jnp.where(nyq, ss, zr * ks + zs * kr)
    yf = jnp.concatenate([yr, ys], axis=0).astype(BF16)
    y_ref[...] += _dot(inv_ref[...], yf)

    @pl.when((o == 1) & (j == last_j))
    def _():
        z = short_conv(x2_ref, 2) * (y_ref[...] + hb_ref[1:2, :] * z_ref[...])
        o_ref[...] = (z * _silu(gc_ref[...])).astype(BF16)


def _hyena(proj, cw, cb, hb, fwd, inv, kf, seq):
    m = proj.shape[0]
    nch, fc2, _ = fwd.shape
    fc = fc2 // 2
    tc = 256 if seq > 1024 else HY_W
    nct = HY_W // tc
    hc0 = OFF_HC // tc
    col = lambda k: pl.BlockSpec((seq, tc), lambda b, c, o, j: (b, hc0 + k * nct + c))
    return pl.pallas_call(
        functools.partial(_hyena_kernel, seq=seq, fc=fc),
        grid=(m // seq, nct, HYENA_ORDER, nch),
        in_specs=[
            col(0), col(1), col(2),
            pl.BlockSpec((seq, tc), lambda b, c, o, j: (b, OFF_GC // tc + c)),
            pl.BlockSpec((3, 3, tc), lambda b, c, o, j: (0, 0, c)),
            pl.BlockSpec((3, tc), lambda b, c, o, j: (0, c)),
            pl.BlockSpec((HYENA_ORDER, tc), lambda b, c, o, j: (0, c)),
            pl.BlockSpec((None, 2 * fc, seq), lambda b, c, o, j: (j, 0, 0)),
            pl.BlockSpec((None, seq, 2 * fc), lambda b, c, o, j: (j, 0, 0)),
            pl.BlockSpec((None, 2 * fc, tc), lambda b, c, o, j: (j, 0, o * nct + c)),
        ],
        out_specs=pl.BlockSpec((seq, tc), lambda b, c, o, j: (b, c)),
        out_shape=jax.ShapeDtypeStruct((m, HY_W), BF16),
        scratch_shapes=[pltpu.VMEM((seq, tc), F32), pltpu.VMEM((seq, tc), BF16),
                        pltpu.VMEM((seq, tc), F32)],
        compiler_params=_params(("arbitrary",) * 4, 56), name="hyena",
    )(proj, proj, proj, proj, cw, cb, hb, fwd, inv, kf)


def _merge_kernel(ya_ref, yb_ref, yc_ref, ga_ref, gb_ref, gc_ref, wa_ref, wb_ref, wc_ref, o_ref):
    acc = jax.nn.sigmoid(ga_ref[...]) * _dot(ya_ref[...], wa_ref[...])
    acc += jax.nn.sigmoid(gb_ref[...]) * _dot(yb_ref[...], wb_ref[...])
    acc += jax.nn.sigmoid(gc_ref[...]) * _dot(yc_ref[...], wc_ref[...])
    o_ref[...] = acc.astype(BF16)


def _merge(ya, yb, yc, proj, w_br_bf, l, seq):
    m = ya.shape[0]
    tm = min(1024, seq)
    tn = 512
    d = D_MODEL
    row = lambda w: pl.BlockSpec((tm, w), lambda i, j: (i, 0))
    gate = lambda k: pl.BlockSpec((tm, tn), lambda i, j: (i, (OFF_MG + k * d) // tn + j))
    return pl.pallas_call(
        _merge_kernel, grid=(m // tm, d // tn),
        in_specs=[row(ATT_W), row(FNET_W), row(HY_W), gate(0), gate(1), gate(2),
                  pl.BlockSpec((None, ATT_W, tn), lambda i, j: (l, 0, j)),
                  pl.BlockSpec((None, FNET_W, tn), lambda i, j: (l, ATT_W // FNET_W, j)),
                  pl.BlockSpec((None, HY_W, tn), lambda i, j: (l, (ATT_W + FNET_W) // HY_W, j))],
        out_specs=pl.BlockSpec((tm, tn), lambda i, j: (i, j)),
        out_shape=jax.ShapeDtypeStruct((m, d), BF16),
        compiler_params=_params(("arbitrary", "arbitrary"), 40), name="merge",
    )(ya, yb, yc, proj, proj, proj, w_br_bf, w_br_bf, w_br_bf)


def _outproj_kernel(mg_ref, w_ref, x_ref, ada_ref, o_ref):
    o_ref[...] = x_ref[...] + ada_ref[...] * _dot(mg_ref[...], w_ref[...])


def _outproj(merged, x2d, ada4, ada_row, w_out_bf, l, seq):
    m, d = x2d.shape
    tm = min(1024, seq)
    tn = 512
    per_seq = seq // tm
    return pl.pallas_call(
        _outproj_kernel, grid=(m // tm, d // tn),
        in_specs=[pl.BlockSpec((tm, d), lambda i, j: (i, 0)),
                  pl.BlockSpec((None, d, tn), lambda i, j: (l, 0, j)),
                  pl.BlockSpec((tm, tn), lambda i, j: (i, j)),
                  pl.BlockSpec((None, None, 1, tn),
                               lambda i, j: (l, ada_row(i // per_seq), 0, 2 * d // tn + j))],
        out_specs=pl.BlockSpec((tm, tn), lambda i, j: (i, j)),
        out_shape=jax.ShapeDtypeStruct((m, d), F32),
        compiler_params=_params(("arbitrary", "arbitrary"), 40), name="outproj",
    )(merged, w_out_bf, x2d, ada4)


def kernel(x_prompt, x_sample, cache_k, cache_v, c, c_ctx, norm_g, w_ada, b_ada, w_in, q_norm_g, k_norm_g, rpb, conv_w, conv_b, f_w1, f_b1, f_freq, f_w2, f_b2, f_w3, hy_bias, w_br, w_out):
    batch, seq, d = x_prompt.shape
    dec_batch, dec_seq, _ = x_sample.shape
    depth = norm_g.shape[0]
    past = cache_k.shape[2]
    assert dec_batch + 1 <= ADA_ROWS

    cv = jnp.zeros((ADA_ROWS, d), F32).at[0].set(c_ctx).at[1:1 + dec_batch].set(c)
    ada4 = _ada(cv, w_ada, b_ada).reshape(depth, ADA_ROWS, 1, 3 * d)
    ctx_row = lambda b: 0
    lat_row = lambda b: 1 + b

    w_in_bf = w_in.astype(BF16)
    w_br_bf = w_br.astype(BF16)
    w_out_bf = w_out.astype(BF16)
    norm_g3 = norm_g.reshape(depth, 1, d)
    qg3 = q_norm_g.reshape(depth, 1, HEAD_DIM)
    kg3 = k_norm_g.reshape(depth, 1, HEAD_DIM)
    cache_k4 = cache_k.reshape(dec_batch, depth, past, ATT_W)
    cache_v4 = cache_v.reshape(dec_batch, depth, past, ATT_W)
    w1p = jnp.pad(f_w1, ((0, 0), (0, 128 - FILTER_EMB), (0, 0)))
    b1 = f_b1.reshape(depth, 1, FILTER_FF)
    fr = f_freq.reshape(depth, 1, FILTER_FF)
    b2 = f_b2.reshape(depth, 1, FILTER_FF)

    groups = []
    for x, s, row in ((x_prompt, seq, ctx_row), (x_sample, dec_seq, lat_row)):
        fwd, inv = _hyena_tables(s)
        groups.append(dict(x=x.reshape(-1, d), seq=s, row=row, fnet=_fnet_tables(s),
                           fwd=fwd, inv=inv, feats=_filter_features(s)))

    new_kv = tuple(jnp.zeros((batch, depth, seq, ATT_W), F32) for _ in range(2))
    for l in range(depth):
        cw = conv_w[l].reshape(3, 3, HY_W)
        cb = conv_b[l].reshape(3, HY_W)
        hb = hy_bias[l]
        for gi, g in enumerate(groups):
            s = g["seq"]
            proj = _inproj(g["x"], norm_g3, ada4, g["row"], w_in_bf, l, s)
            if gi == 0:
                qn, kn, vb, nk, nv = _prep(proj, qg3, kg3, l, s, new_kv)
                new_kv = (nk, nv)
                ya = _ctx_attn(qn, kn, vb, proj, s)
            else:
                qn, kn, vb = _prep(proj, qg3, kg3, l, s, None)
                ya = _lat_attn(qn, kn, vb, cache_k4, cache_v4, _window_bias(rpb[l]), proj, l, s)
            yb = _fnet(proj, g["fnet"], s)
            kf = _filter(g["feats"], g["fwd"], w1p, b1, fr, f_w2, b2, f_w3, l, s)
            yc = _hyena(proj, cw, cb, hb, g["fwd"], g["inv"], kf, s)
            merged = _merge(ya, yb, yc, proj, w_br_bf, l, s)
            g["x"] = _outproj(merged, g["x"], ada4, g["row"], w_out_bf, l, s)

    nk, nv = new_kv
    return (groups[0]["x"].reshape(batch, seq, d),
            groups[1]["x"].reshape(dec_batch, dec_seq, d),
            nk.reshape(batch, depth, seq, N_HEADS, HEAD_DIM),
            nv.reshape(batch, depth, seq, N_HEADS, HEAD_DIM))
```

```python
import functools
import math

import jax
import jax.numpy as jnp
import numpy as np
from jax import lax
from jax.experimental import pallas as pl
from jax.experimental.pallas import tpu as pltpu

F32 = jnp.float32
BF16 = jnp.bfloat16

D_MODEL = 2048
GRID_W = 64
N_HEADS = 8
HEAD_DIM = 128
ATT_W = N_HEADS * HEAD_DIM
MAX_KH = 8
KW = 16
FNET_GROUPS = 4
FNET_GDIM = 128
FNET_W = FNET_GROUPS * FNET_GDIM
HY_W = 512
HYENA_ORDER = 2
FILTER_EMB = 33
FILTER_FF = 64
MIN_DECAY = math.log(1e-2) / 1.5
MAX_DECAY = math.log(1e-2) / 0.3
OFF_Q = 0
OFF_K = OFF_Q + ATT_W
OFF_V = OFF_K + ATT_W
OFF_GA = OFF_V + ATT_W
OFF_UB = OFF_GA + ATT_W
OFF_GB = OFF_UB + FNET_W
OFF_HC = OFF_GB + FNET_W
OFF_GC = OFF_HC + 3 * HY_W
OFF_MG = OFF_GC + HY_W
N_IN = OFF_MG + 3 * D_MODEL
EPS = 1e-6
NEG = -1e30
ATT_SCALE = HEAD_DIM ** -0.5

ADA_ROWS = 8
FREQ_CHUNK = 512
TRIG_BLOCK = 64
ROW_TILE = 1024
MIB = 1024 * 1024


def _params(semantics, vmem_mib):
    return pltpu.CompilerParams(dimension_semantics=semantics,
                                vmem_limit_bytes=vmem_mib * MIB)


def _silu(x):
    return x * jax.nn.sigmoid(x)


def _dot(a, b):
    return jnp.dot(a, b, preferred_element_type=F32)


def _dot_nt(a, b):
    return lax.dot_general(a, b, (((1,), (1,)), ((), ())), preferred_element_type=F32)


def _ada_kernel(cv_ref, w_ref, b_ref, o_ref):
    s = _silu(cv_ref[...]).astype(BF16)
    o_ref[...] = _dot(s, w_ref[...].astype(BF16)) + b_ref[...]


def _ada(cv, w_ada, b_ada):
    depth, d, n = w_ada.shape
    tn = 512
    return pl.pallas_call(
        _ada_kernel,
        grid=(depth, n // tn),
        in_specs=[
            pl.BlockSpec((ADA_ROWS, d), lambda l, j: (0, 0)),
            pl.BlockSpec((None, d, tn), lambda l, j: (l, 0, j)),
            pl.BlockSpec((None, 1, tn), lambda l, j: (l, 0, j)),
        ],
        out_specs=pl.BlockSpec((None, ADA_ROWS, tn), lambda l, j: (l, 0, j)),
        out_shape=jax.ShapeDtypeStruct((depth, ADA_ROWS, n), F32),
        compiler_params=_params(("arbitrary", "arbitrary"), 32),
        name="ada",
    )(cv, w_ada, b_ada.reshape(depth, 1, n))


def _inproj_kernel(x_ref, g_ref, ada_ref, w_ref, o_ref, h_ref, *, rows):
    @pl.when(pl.program_id(1) == 0)
    def _():
        g = g_ref[...]
        shift = ada_ref[:, 0:D_MODEL]
        scale1 = 1.0 + ada_ref[:, D_MODEL:2 * D_MODEL]

        def body(r, carry):
            sl = pl.ds(pl.multiple_of(r * rows, rows), rows)
            x = x_ref[sl, :]
            ms = jnp.mean(x * x, axis=-1, keepdims=True)
            y = x * lax.rsqrt(ms + EPS) * g
            h_ref[sl, :] = (y * scale1 + shift).astype(BF16)
            return carry

        lax.fori_loop(0, x_ref.shape[0] // rows, body, 0)

    o_ref[...] = _dot(h_ref[...], w_ref[...])


def _inproj(x2d, norm_g3, ada4, ada_row, w_in_bf, l, seq):
    m, d = x2d.shape
    tm = min(ROW_TILE, m)
    tn = 1024
    return pl.pallas_call(
        functools.partial(_inproj_kernel, rows=128),
        grid=(m // tm, N_IN // tn),
        in_specs=[
            pl.BlockSpec((tm, d), lambda i, j: (i, 0)),
            pl.BlockSpec((None, 1, d), lambda i, j: (l, 0, 0)),
            pl.BlockSpec((None, None, 1, 3 * d), lambda i, j: (l, ada_row(i * tm), 0, 0)),
            pl.BlockSpec((None, d, tn), lambda i, j: (l, 0, j)),
        ],
        out_specs=pl.BlockSpec((tm, tn), lambda i, j: (i, j)),
        out_shape=jax.ShapeDtypeStruct((m, N_IN), F32),
        scratch_shapes=[pltpu.VMEM((tm, d), BF16)],
        compiler_params=_params(("arbitrary", "arbitrary"), 48),
        name="inproj",
    )(x2d, norm_g3, ada4, w_in_bf)


def _head_norm_store(x_ref, g, out_refs):
    for h in range(N_HEADS):
        sl = slice(h * HEAD_DIM, (h + 1) * HEAD_DIM)
        x = x_ref[:, sl]
        ms = jnp.mean(x * x, axis=-1, keepdims=True)
        y = x * lax.rsqrt(ms + EPS) * g
        for ref in out_refs:
            ref[:, sl] = y.astype(ref.dtype)


def _prep_ctx_kernel(q_ref, k_ref, v_ref, qg_ref, kg_ref, *rest):
    qn_ref, kn_ref, vb_ref, nk_ref, nv_ref = rest[-5:]
    _head_norm_store(q_ref, qg_ref[...], (qn_ref,))
    _head_norm_store(k_ref, kg_ref[...], (kn_ref, nk_ref))
    v = v_ref[...]
    vb_ref[...] = v.astype(BF16)
    nv_ref[...] = v


def _prep_lat_kernel(q_ref, k_ref, v_ref, qg_ref, kg_ref, qn_ref, kn_ref, vb_ref):
    _head_norm_store(q_ref, qg_ref[...], (qn_ref,))
    _head_norm_store(k_ref, kg_ref[...], (kn_ref,))
    vb_ref[...] = v_ref[...].astype(BF16)


def _prep(proj, qg3, kg3, l, seq, new_kv):
    m = proj.shape[0]
    depth = qg3.shape[0]
    tm = min(256, seq)
    col = lambda c: pl.BlockSpec((tm, ATT_W), lambda i: (i, c))
    gspec = pl.BlockSpec((None, 1, HEAD_DIM), lambda i: (l, 0, 0))
    in_specs = [col(OFF_Q // ATT_W), col(OFF_K // ATT_W), col(OFF_V // ATT_W), gspec, gspec]
    bf_shape = jax.ShapeDtypeStruct((m, ATT_W), BF16)
    bf_spec = pl.BlockSpec((tm, ATT_W), lambda i: (i, 0))
    if new_kv is None:
        return pl.pallas_call(
            _prep_lat_kernel, grid=(m // tm,), in_specs=in_specs,
            out_specs=[bf_spec] * 3, out_shape=[bf_shape] * 3,
            compiler_params=_params(("arbitrary",), 32), name="prep_lat",
        )(proj, proj, proj, qg3, kg3)
    assert tm == seq
    batch = m // seq
    kv_shape = jax.ShapeDtypeStruct((batch, depth, seq, ATT_W), F32)
    kv_spec = pl.BlockSpec((None, None, seq, ATT_W), lambda i: (i, l, 0, 0))
    return pl.pallas_call(
        _prep_ctx_kernel, grid=(m // tm,),
        in_specs=in_specs + [pl.BlockSpec(memory_space=pl.ANY)] * 2,
        out_specs=[bf_spec] * 3 + [kv_spec] * 2,
        out_shape=[bf_shape] * 3 + [kv_shape] * 2,
        input_output_aliases={5: 3, 6: 4},
        compiler_params=_params(("arbitrary",), 32), name="prep_ctx",
    )(proj, proj, proj, qg3, kg3, *new_kv)


def _ctx_attn_kernel(q_ref, k_ref, v_ref, ga_ref, o_ref):
    for h in range(N_HEADS):
        sl = slice(h * HEAD_DIM, (h + 1) * HEAD_DIM)
        s = _dot_nt(q_ref[:, sl], k_ref[:, sl]) * ATT_SCALE
        p = jnp.exp(s - jnp.max(s, axis=-1, keepdims=True))
        inv = 1.0 / jnp.sum(p, axis=-1, keepdims=True)
        o = _dot(p.astype(BF16), v_ref[:, sl]) * inv
        o_ref[:, sl] = (o * _silu(ga_ref[:, sl])).astype(BF16)


def _ctx_attn(qn, kn, vb, proj, seq):
    m = qn.shape[0]
    spec = pl.BlockSpec((seq, ATT_W), lambda b: (b, 0))
    return pl.pallas_call(
        _ctx_attn_kernel, grid=(m // seq,),
        in_specs=[spec, spec, spec, pl.BlockSpec((seq, ATT_W), lambda b: (b, OFF_GA // ATT_W))],
        out_specs=spec,
        out_shape=jax.ShapeDtypeStruct((m, ATT_W), BF16),
        compiler_params=_params(("arbitrary",), 32), name="ctx_attn",
    )(qn, kn, vb, proj)


def _window_bias(rpb):
    depth, heads, n_dr, n_dc = rpb.shape
    c = np.arange(GRID_W)[:, None]
    kc = np.arange(GRID_W)[None, :]
    cs = np.clip(c - KW // 2, 0, GRID_W - KW)
    valid = (kc >= cs) & (kc < cs + KW)
    dc = np.clip(kc - c + KW - 1, 0, n_dc - 1).reshape(-1)
    onehot = (jnp.asarray(dc)[None, :] == jnp.arange(n_dc)[:, None]).astype(F32)
    cb = jnp.einsum("lhdm,mx->lhdx", rpb, onehot, precision=lax.Precision.HIGHEST)
    cb = jnp.where(jnp.asarray(valid), cb.reshape(depth, heads, n_dr, GRID_W, GRID_W), NEG)
    return jnp.concatenate([cb[:, :, :-1], cb[:, :, 1:]], axis=-1)


def _lat_attn_kernel(q_ref, k_ref, v_ref, ck_ref, cv_ref, bias_ref, ga_ref, o_ref,
                     ckb_ref, cvb_ref, *, rows, kh):
    r = pl.program_id(1)

    @pl.when(r == 0)
    def _():
        ckb_ref[...] = ck_ref[...].astype(BF16)
        cvb_ref[...] = cv_ref[...].astype(BF16)

    rs = jnp.clip(r - kh // 2, 0, rows - kh)
    win = pl.ds(pl.multiple_of(rs * GRID_W, GRID_W), kh * GRID_W)
    d0 = rs - r + (MAX_KH - 1)
    for h in range(N_HEADS):
        sl = slice(h * HEAD_DIM, (h + 1) * HEAD_DIM)
        q = q_ref[:, sl]
        bias = jnp.concatenate([bias_ref[h, d0 + 2 * j] for j in range(kh // 2)], axis=-1)
        s_win = _dot_nt(q, k_ref[win, sl]) * ATT_SCALE + bias
        s_ctx = _dot_nt(q, ckb_ref[:, sl]) * ATT_SCALE
        mx = jnp.maximum(jnp.max(s_win, axis=-1, keepdims=True),
                         jnp.max(s_ctx, axis=-1, keepdims=True))
        p_win = jnp.exp(s_win - mx)
        p_ctx = jnp.exp(s_ctx - mx)
        inv = 1.0 / (jnp.sum(p_win, axis=-1, keepdims=True) + jnp.sum(p_ctx, axis=-1, keepdims=True))
        o = (_dot(p_win.astype(BF16), v_ref[win, sl]) + _dot(p_ctx.astype(BF16), cvb_ref[:, sl])) * inv
        o_ref[:, sl] = (o * _silu(ga_ref[:, sl])).astype(BF16)


def _lat_attn(qn, kn, vb, cache_k4, cache_v4, bias, proj, l, seq):
    m = qn.shape[0]
    batch = m // seq
    rows = seq // GRID_W
    kh = min(MAX_KH, rows)
    past = cache_k4.shape[2]
    qspec = pl.BlockSpec((GRID_W, ATT_W), lambda b, r: (b * rows + r, 0))
    kvspec = pl.BlockSpec((seq, ATT_W), lambda b, r: (b, 0))
    cspec = pl.BlockSpec((None, None, past, ATT_W), lambda b, r: (b, l, 0, 0))
    return pl.pallas_call(
        functools.partial(_lat_attn_kernel, rows=rows, kh=kh),
        grid=(batch, rows),
        in_specs=[qspec, kvspec, kvspec, cspec, cspec,
                  pl.BlockSpec((None,) + bias.shape[1:], lambda b, r: (l, 0, 0, 0, 0)),
                  pl.BlockSpec((GRID_W, ATT_W), lambda b, r: (b * rows + r, OFF_GA // ATT_W))],
        out_specs=qspec,
        out_shape=jax.ShapeDtypeStruct((m, ATT_W), BF16),
        scratch_shapes=[pltpu.VMEM((past, ATT_W), BF16), pltpu.VMEM((past, ATT_W), BF16)],
        compiler_params=_params(("arbitrary", "arbitrary"), 48), name="lat_attn",
    )(qn, kn, vb, cache_k4, cache_v4, bias, proj)


def _cos_sin_table(r, ncols, period):
    def factor(v):
        ang = ((r[..., None] * v) % period).astype(F32) * (2.0 * math.pi / period)
        return jnp.cos(ang), jnp.sin(ang)

    ca, sa = factor(jnp.arange(ncols // TRIG_BLOCK, dtype=jnp.int32) * TRIG_BLOCK)
    cb, sb = factor(jnp.arange(TRIG_BLOCK, dtype=jnp.int32))
    ca, sa, cb, sb = ca[..., :, None], sa[..., :, None], cb[..., None, :], sb[..., None, :]
    shape = r.shape + (ncols,)
    return (ca * cb - sa * sb).reshape(shape), (sa * cb + ca * sb).reshape(shape)


def _fnet_tables(seq):
    cl, sl = _cos_sin_table(jnp.arange(seq, dtype=jnp.int32), seq, seq)
    cc, sc = _cos_sin_table(jnp.arange(FNET_GDIM, dtype=jnp.int32), FNET_GDIM, FNET_GDIM)
    return (jnp.concatenate([cl, -sl], axis=1).astype(BF16), cc.astype(BF16), sc.astype(BF16))


def _fnet_kernel(u_ref, gb_ref, tl_ref, cc_ref, sc_ref, o_ref, t_ref, *, seq, tl):
    i = pl.program_id(1)

    @pl.when(i == 0)
    def _():
        for g in range(FNET_GROUPS):
            sl = slice(g * FNET_GDIM, (g + 1) * FNET_GDIM)
            ug = u_ref[:, sl].astype(BF16)
            t_ref[0:seq, sl] = _dot(ug, cc_ref[...]).astype(BF16)
            t_ref[seq:2 * seq, sl] = _dot(ug, sc_ref[...]).astype(BF16)

    y = _dot(tl_ref[...], t_ref[...]) * ((seq * FNET_GDIM) ** -0.5)
    o_ref[...] = (y * _silu(gb_ref[...])).astype(BF16)


def _fnet(proj, tables, seq):
    m = proj.shape[0]
    tab_l, cc, sc = tables
    tl = min(512, seq)
    nt = seq // tl
    return pl.pallas_call(
        functools.partial(_fnet_kernel, seq=seq, tl=tl),
        grid=(m // seq, nt),
        in_specs=[
            pl.BlockSpec((seq, FNET_W), lambda b, i: (b, OFF_UB // FNET_W)),
            pl.BlockSpec((tl, FNET_W), lambda b, i: (b * nt + i, OFF_GB // FNET_W)),
            pl.BlockSpec((tl, 2 * seq), lambda b, i: (i, 0)),
            pl.BlockSpec((FNET_GDIM, FNET_GDIM), lambda b, i: (0, 0)),
            pl.BlockSpec((FNET_GDIM, FNET_GDIM), lambda b, i: (0, 0)),
        ],
        out_specs=pl.BlockSpec((tl, FNET_W), lambda b, i: (b * nt + i, 0)),
        out_shape=jax.ShapeDtypeStruct((m, FNET_W), BF16),
        scratch_shapes=[pltpu.VMEM((2 * seq, FNET_W), BF16)],
        compiler_params=_params(("arbitrary", "arbitrary"), 40), name="fnet",
    )(proj, proj, tab_l, cc, sc)


def _hyena_tables(seq):
    fc = min(FREQ_CHUNK, seq)
    nch = seq // fc
    k = jnp.arange(2 * fc, dtype=jnp.int32)
    f = jnp.arange(nch, dtype=jnp.int32)[:, None] * fc + (k % fc)[None, :]
    is_sec = (k >= fc)[None, :, None]
    nyq = is_sec & (f == 0)[:, :, None]
    c, s = _cos_sin_table(f, seq, 2 * seq)
    sgn = (1 - 2 * (jnp.arange(seq, dtype=jnp.int32) % 2)).astype(F32)
    fwd = jnp.where(is_sec, jnp.where(nyq, sgn, -s), c).astype(BF16)
    return fwd, fwd.transpose(0, 2, 1)


def _filter_features(seq):
    t = jnp.linspace(0.0, 1.0, seq, dtype=F32)[:, None]
    bands = (FILTER_EMB - 1) // 2
    w = (2.0 * math.pi / seq) * jnp.arange(seq, dtype=F32)[:, None]
    f = jnp.linspace(1e-4, bands - 1, bands, dtype=F32)[None, :]
    z = jnp.concatenate([t, jnp.cos(w * f), -jnp.sin(w * f)], axis=-1)
    z = jnp.pad(z, ((0, 0), (0, 128 - FILTER_EMB)))
    deltas = jnp.abs(jnp.linspace(MIN_DECAY, MAX_DECAY, HY_W, dtype=F32))[None, :]
    return z, t, deltas


def _dot_f32(a, b):
    return jnp.dot(a, b, preferred_element_type=F32, precision=lax.Precision.HIGHEST)


def _filter_kernel(z_ref, t_ref, dl_ref, w1_ref, b1_ref, fr_ref, w2_ref, b2_ref, w3_ref,
                   fwd_ref, o_ref, h_ref, sd_ref, nyq_ref, *, seq, fc, rows):
    j = pl.program_id(0)
    nblk = 2 * HYENA_ORDER
    nrc = seq // rows

    @pl.when(j == 0)
    def _():
        fr = fr_ref[...]

        def taps(r, acc):
            sl = pl.ds(pl.multiple_of(r * rows, rows), rows)
            h = jnp.sin(fr * (_dot_f32(z_ref[sl, :], w1_ref[...]) + b1_ref[...]))
            h = jnp.sin(fr * (_dot_f32(h, w2_ref[...]) + b2_ref[...]))
            h = _dot_f32(h, w3_ref[...])
            decay = jnp.exp(-t_ref[sl, :] * dl_ref[...])
            h = h * jnp.concatenate([decay] * nblk, axis=-1)
            h_ref[sl, :] = h
            return acc + jnp.sum(jnp.abs(h), axis=0, keepdims=True)

        tot = lax.fori_loop(0, nrc, taps, jnp.zeros((1, nblk * HY_W), F32))
        inv = 1.0 / (tot + EPS)
        sgn = (1 - 2 * (lax.broadcasted_iota(jnp.int32, (rows, 1), 0) % 2)).astype(F32)

        def fold(r, acc):
            sl = pl.ds(pl.multiple_of(r * rows, rows), rows)
            h = h_ref[sl, :] * inv
            pos = r * rows + lax.broadcasted_iota(jnp.int32, (rows, 1), 0)
            sums = []
            for o in range(HYENA_ORDER):
                fw = h[:, (2 * o) * HY_W:(2 * o + 1) * HY_W]
                bw = jnp.where(pos == 0, 0.0, h[:, (2 * o + 1) * HY_W:(2 * o + 2) * HY_W])
                sums.append(fw + bw)
                sd_ref[0, sl, o * HY_W:(o + 1) * HY_W] = (fw + bw).astype(BF16)
                sd_ref[1, sl, o * HY_W:(o + 1) * HY_W] = (fw - bw).astype(BF16)
            return acc + jnp.sum(jnp.concatenate(sums, axis=-1) * sgn, axis=0, keepdims=True)

        nyq_ref[...] = lax.fori_loop(0, nrc, fold, jnp.zeros((1, HYENA_ORDER * HY_W), F32))

    re = _dot(fwd_ref[0:fc, :], sd_ref[0])
    sec = _dot(fwd_ref[fc:2 * fc, :], sd_ref[1])
    first = (lax.broadcasted_iota(jnp.int32, (fc, 1), 0) == 0) & (j == 0)
    sec = jnp.where(first, nyq_ref[...], sec)
    weight = jnp.where(first, 0.5 / seq, 1.0 / seq)
    o_ref[0:fc, :] = re * weight
    o_ref[fc:2 * fc, :] = sec * weight


def _filter(feats, fwd, w1p, b1, fr, w2, b2, w3, l, seq):
    z, t, deltas = feats
    nch, fc2, _ = fwd.shape
    fc = fc2 // 2
    width = HYENA_ORDER * HY_W
    rows = min(256, seq)
    full = lambda a: pl.BlockSpec(a.shape, lambda j: (0,) * a.ndim)
    lay = lambda a: pl.BlockSpec((None,) + a.shape[1:], lambda j: (l,) + (0,) * (a.ndim - 1))
    return pl.pallas_call(
        functools.partial(_filter_kernel, seq=seq, fc=fc, rows=rows),
        grid=(nch,),
        in_specs=[full(z), full(t), full(deltas), lay(w1p), lay(b1), lay(fr), lay(w2), lay(b2), lay(w3),
                  pl.BlockSpec((None, 2 * fc, seq), lambda j: (j, 0, 0))],
        out_specs=pl.BlockSpec((None, 2 * fc, width), lambda j: (j, 0, 0)),
        out_shape=jax.ShapeDtypeStruct((nch, 2 * fc, width), F32),
        scratch_shapes=[pltpu.VMEM((seq, 2 * width), F32),
                        pltpu.VMEM((2, seq, width), BF16),
                        pltpu.VMEM((1, width), F32)],
        compiler_params=_params(("arbitrary",), 48), name="hyena_filter",
    )(z, t, deltas, w1p, b1, fr, w2, b2, w3, fwd)


def _hyena_kernel(v_ref, x1_ref, x2_ref, gc_ref, cw_ref, cb_ref, hb_ref, fwd_ref, inv_ref, kf_ref,
                  o_ref, z_ref, zb_ref, y_ref, *, seq, fc):
    o = pl.program_id(2)
    j = pl.program_id(3)
    last_j = pl.num_programs(3) - 1

    def short_conv(ref, k):
        x = ref[...]
        pos = lax.broadcasted_iota(jnp.int32, (seq, 1), 0)
        prev = jnp.where(pos == 0, 0.0, pltpu.roll(x, 1, 0))
        nxt = jnp.where(pos == seq - 1, 0.0, pltpu.roll(x, seq - 1, 0))
        return (prev * cw_ref[0, k:k + 1, :] + x * cw_ref[1, k:k + 1, :]
                + nxt * cw_ref[2, k:k + 1, :] + cb_ref[k:k + 1, :])

    def start(z):
        z_ref[...] = z
        zb_ref[...] = z.astype(BF16)
        y_ref[...] = jnp.zeros_like(y_ref)

    @pl.when((o == 0) & (j == 0))
    def _():
        start(short_conv(v_ref, 0))

    @pl.when((o == 1) & (j == 0))
    def _():
        start(short_conv(x1_ref, 1) * (y_ref[...] + hb_ref[0:1, :] * z_ref[...]))

    zf = _dot(fwd_ref[...], zb_ref[...])
    zr, zs = zf[0:fc], zf[fc:2 * fc]
    kr, ks = kf_ref[0:fc, :], kf_ref[fc:2 * fc, :]
    nyq = (lax.broadcasted_iota(jnp.int32, (fc, 1), 0) == 0) & (j == 0)
    ss = zs * ks
    yr = zr * kr - jnp.where(nyq, 0.0, ss)
    ys = jnp.where(nyq, ss, zr * ks + zs * kr)
    yf = jnp.concatenate([yr, ys], axis=0).astype(BF16)
    y_ref[...] += _dot(inv_ref[...], yf)

    @pl.when((o == 1) & (j == last_j))
    def _():
        z = short_conv(x2_ref, 2) * (y_ref[...] + hb_ref[1:2, :] * z_ref[...])
        o_ref[...] = (z * _silu(gc_ref[...])).astype(BF16)


def _hyena(proj, cw, cb, hb, fwd, inv, kf, seq):
    m = proj.shape[0]
    nch, fc2, _ = fwd.shape
    fc = fc2 // 2
    tc = 256 if seq > 1024 else HY_W
    nct = HY_W // tc
    hc0 = OFF_HC // tc
    col = lambda k: pl.BlockSpec((seq, tc), lambda b, c, o, j: (b, hc0 + k * nct + c))
    return pl.pallas_call(
        functools.partial(_hyena_kernel, seq=seq, fc=fc),
        grid=(m // seq, nct, HYENA_ORDER, nch),
        in_specs=[
            col(0), col(1), col(2),
            pl.BlockSpec((seq, tc), lambda b, c, o, j: (b, OFF_GC // tc + c)),
            pl.BlockSpec((3, 3, tc), lambda b, c, o, j: (0, 0, c)),
            pl.BlockSpec((3, tc), lambda b, c, o, j: (0, c)),
            pl.BlockSpec((HYENA_ORDER, tc), lambda b, c, o, j: (0, c)),
            pl.BlockSpec((None, 2 * fc, seq), lambda b, c, o, j: (j, 0, 0)),
            pl.BlockSpec((None, seq, 2 * fc), lambda b, c, o, j: (j, 0, 0)),
            pl.BlockSpec((None, 2 * fc, tc), lambda b, c, o, j: (j, 0, o * nct + c)),
        ],
        out_specs=pl.BlockSpec((seq, tc), lambda b, c, o, j: (b, c)),
        out_shape=jax.ShapeDtypeStruct((m, HY_W), BF16),
        scratch_shapes=[pltpu.VMEM((seq, tc), F32), pltpu.VMEM((seq, tc), BF16),
                        pltpu.VMEM((seq, tc), F32)],
        compiler_params=_params(("arbitrary",) * 4, 56), name="hyena",
    )(proj, proj, proj, proj, cw, cb, hb, fwd, inv, kf)


def _merge_kernel(ya_ref, yb_ref, yc_ref, ga_ref, gb_ref, gc_ref, wa_ref, wb_ref, wc_ref, o_ref):
    acc = jax.nn.sigmoid(ga_ref[...]) * _dot(ya_ref[...], wa_ref[...])
    acc += jax.nn.sigmoid(gb_ref[...]) * _dot(yb_ref[...], wb_ref[...])
    acc += jax.nn.sigmoid(gc_ref[...]) * _dot(yc_ref[...], wc_ref[...])
    o_ref[...] = acc.astype(BF16)


def _merge(ya, yb, yc, proj, w_br_bf, l, seq):
    m = ya.shape[0]
    tm = min(ROW_TILE, m)
    tn = 512
    d = D_MODEL
    row = lambda w: pl.BlockSpec((tm, w), lambda i, j: (i, 0))
    gate = lambda k: pl.BlockSpec((tm, tn), lambda i, j: (i, (OFF_MG + k * d) // tn + j))
    return pl.pallas_call(
        _merge_kernel, grid=(m // tm, d // tn),
        in_specs=[row(ATT_W), row(FNET_W), row(HY_W), gate(0), gate(1), gate(2),
                  pl.BlockSpec((None, ATT_W, tn), lambda i, j: (l, 0, j)),
                  pl.BlockSpec((None, FNET_W, tn), lambda i, j: (l, ATT_W // FNET_W, j)),
                  pl.BlockSpec((None, HY_W, tn), lambda i, j: (l, (ATT_W + FNET_W) // HY_W, j))],
        out_specs=pl.BlockSpec((tm, tn), lambda i, j: (i, j)),
        out_shape=jax.ShapeDtypeStruct((m, d), BF16),
        compiler_params=_params(("arbitrary", "arbitrary"), 40), name="merge",
    )(ya, yb, yc, proj, proj, proj, w_br_bf, w_br_bf, w_br_bf)


def _outproj_kernel(mg_ref, w_ref, x_ref, ada_ref, o_ref):
    o_ref[...] = x_ref[...] + ada_ref[...] * _dot(mg_ref[...], w_ref[...])


def _outproj(merged, x2d, ada4, ada_row, w_out_bf, l, seq):
    m, d = x2d.shape
    tm = min(ROW_TILE, m)
    tn = 512
    return pl.pallas_call(
        _outproj_kernel, grid=(m // tm, d // tn),
        in_specs=[pl.BlockSpec((tm, d), lambda i, j: (i, 0)),
                  pl.BlockSpec((None, d, tn), lambda i, j: (l, 0, j)),
                  pl.BlockSpec((tm, tn), lambda i, j: (i, j)),
                  pl.BlockSpec((None, None, 1, tn),
                               lambda i, j: (l, ada_row(i * tm), 0, 2 * d // tn + j))],
        out_specs=pl.BlockSpec((tm, tn), lambda i, j: (i, j)),
        out_shape=jax.ShapeDtypeStruct((m, d), F32),
        compiler_params=_params(("arbitrary", "arbitrary"), 40), name="outproj",
    )(merged, w_out_bf, x2d, ada4)


def kernel(x_prompt, x_sample, cache_k, cache_v, c, c_ctx, norm_g, w_ada, b_ada, w_in, q_norm_g, k_norm_g, rpb, conv_w, conv_b, f_w1, f_b1, f_freq, f_w2, f_b2, f_w3, hy_bias, w_br, w_out):
    batch, seq, d = x_prompt.shape
    dec_batch, dec_seq, _ = x_sample.shape
    depth = norm_g.shape[0]
    past = cache_k.shape[2]
    assert dec_batch + 1 <= ADA_ROWS

    cv = jnp.zeros((ADA_ROWS, d), F32).at[0].set(c_ctx).at[1:1 + dec_batch].set(c)
    ada4 = _ada(cv, w_ada, b_ada).reshape(depth, ADA_ROWS, 1, 3 * d)
    ctx_row = lambda tok: 0
    lat_row = lambda tok: 1 + tok // dec_seq

    w_in_bf = w_in.astype(BF16)
    w_br_bf = w_br.astype(BF16)
    w_out_bf = w_out.astype(BF16)
    norm_g3 = norm_g.reshape(depth, 1, d)
    qg3 = q_norm_g.reshape(depth, 1, HEAD_DIM)
    kg3 = k_norm_g.reshape(depth, 1, HEAD_DIM)
    cache_k4 = cache_k.reshape(dec_batch, depth, past, ATT_W)
    cache_v4 = cache_v.reshape(dec_batch, depth, past, ATT_W)
    w1p = jnp.pad(f_w1, ((0, 0), (0, 128 - FILTER_EMB), (0, 0)))
    b1 = f_b1.reshape(depth, 1, FILTER_FF)
    fr = f_freq.reshape(depth, 1, FILTER_FF)
    b2 = f_b2.reshape(depth, 1, FILTER_FF)
    win_bias = _window_bias(rpb)

    groups = []
    for x, s, row in ((x_prompt, seq, ctx_row), (x_sample, dec_seq, lat_row)):
        fwd, inv = _hyena_tables(s)
        groups.append(dict(x=x.reshape(-1, d), seq=s, row=row, fnet=_fnet_tables(s),
                           fwd=fwd, inv=inv, feats=_filter_features(s)))

    new_kv = tuple(jnp.zeros((batch, depth, seq, ATT_W), F32) for _ in range(2))
    for l in range(depth):
        cw = conv_w[l].reshape(3, 3, HY_W)
        cb = conv_b[l].reshape(3, HY_W)
        hb = hy_bias[l]
        for gi, g in enumerate(groups):
            s = g["seq"]
            proj = _inproj(g["x"], norm_g3, ada4, g["row"], w_in_bf, l, s)
            if gi == 0:
                qn, kn, vb, nk, nv = _prep(proj, qg3, kg3, l, s, new_kv)
                new_kv = (nk, nv)
                ya = _ctx_attn(qn, kn, vb, proj, s)
            else:
                qn, kn, vb = _prep(proj, qg3, kg3, l, s, None)
                ya = _lat_attn(qn, kn, vb, cache_k4, cache_v4, win_bias, proj, l, s)
            yb = _fnet(proj, g["fnet"], s)
            kf = _filter(g["feats"], g["fwd"], w1p, b1, fr, f_w2, b2, f_w3, l, s)
            yc = _hyena(proj, cw, cb, hb, g["fwd"], g["inv"], kf, s)
            merged = _merge(ya, yb, yc, proj, w_br_bf, l, s)
            g["x"] = _outproj(merged, g["x"], ada4, g["row"], w_out_bf, l, s)

    nk, nv = new_kv
    return (groups[0]["x"].reshape(batch, seq, d),
            groups[1]["x"].reshape(dec_batch, dec_seq, d),
            nk.reshape(batch, depth, seq, N_HEADS, HEAD_DIM),
            nv.reshape(batch, depth, seq, N_HEADS, HEAD_DIM))
```

```python
import functools
import math

import jax
import jax.numpy as jnp
import numpy as np
from jax import lax
from jax.experimental import pallas as pl
from jax.experimental.pallas import tpu as pltpu

F32 = jnp.float32
BF16 = jnp.bfloat16

D_MODEL = 2048
GRID_W = 64
N_HEADS = 8
HEAD_DIM = 128
ATT_W = N_HEADS * HEAD_DIM
MAX_KH = 8
KW = 16
FNET_GROUPS = 4
FNET_GDIM = 128
FNET_W = FNET_GROUPS * FNET_GDIM
HY_W = 512
HYENA_ORDER = 2
FILTER_EMB = 33
FILTER_FF = 64
MIN_DECAY = math.log(1e-2) / 1.5
MAX_DECAY = math.log(1e-2) / 0.3
OFF_Q = 0
OFF_K = OFF_Q + ATT_W
OFF_V = OFF_K + ATT_W
OFF_GA = OFF_V + ATT_W
OFF_UB = OFF_GA + ATT_W
OFF_GB = OFF_UB + FNET_W
OFF_HC = OFF_GB + FNET_W
OFF_GC = OFF_HC + 3 * HY_W
OFF_MG = OFF_GC + HY_W
N_IN = OFF_MG + 3 * D_MODEL
EPS = 1e-6
NEG = -1e30
ATT_SCALE = HEAD_DIM ** -0.5

ADA_ROWS = 8
FREQ_CHUNK = 512
TRIG_BLOCK = 64
LAT_ROW_BLOCK = 4
ROW_TILE = 1024
MIB = 1024 * 1024


def _params(semantics, vmem_mib):
    return pltpu.CompilerParams(dimension_semantics=semantics,
                                vmem_limit_bytes=vmem_mib * MIB)


def _silu(x):
    return x * jax.nn.sigmoid(x)


def _dot(a, b):
    return jnp.dot(a, b, preferred_element_type=F32)


def _dot_nt(a, b):
    return lax.dot_general(a, b, (((1,), (1,)), ((), ())), preferred_element_type=F32)


def _ada_kernel(cv_ref, w_ref, b_ref, o_ref):
    s = _silu(cv_ref[...]).astype(BF16)
    o_ref[...] = _dot(s, w_ref[...].astype(BF16)) + b_ref[...]


def _ada(cv, w_ada, b_ada):
    depth, d, n = w_ada.shape
    tn = 512
    return pl.pallas_call(
        _ada_kernel,
        grid=(depth, n // tn),
        in_specs=[
            pl.BlockSpec((ADA_ROWS, d), lambda l, j: (0, 0)),
            pl.BlockSpec((None, d, tn), lambda l, j: (l, 0, j)),
            pl.BlockSpec((None, 1, tn), lambda l, j: (l, 0, j)),
        ],
        out_specs=pl.BlockSpec((None, ADA_ROWS, tn), lambda l, j: (l, 0, j)),
        out_shape=jax.ShapeDtypeStruct((depth, ADA_ROWS, n), F32),
        compiler_params=_params(("arbitrary", "arbitrary"), 32),
        name="ada",
    )(cv, w_ada, b_ada.reshape(depth, 1, n))


def _inproj_kernel(x_ref, g_ref, ada_ref, w_ref, o_ref, h_ref, *, rows):
    @pl.when(pl.program_id(1) == 0)
    def _():
        g = g_ref[...]
        shift = ada_ref[:, 0:D_MODEL]
        scale1 = 1.0 + ada_ref[:, D_MODEL:2 * D_MODEL]

        def body(r, carry):
            sl = pl.ds(pl.multiple_of(r * rows, rows), rows)
            x = x_ref[sl, :]
            ms = jnp.mean(x * x, axis=-1, keepdims=True)
            y = x * lax.rsqrt(ms + EPS) * g
            h_ref[sl, :] = (y * scale1 + shift).astype(BF16)
            return carry

        lax.fori_loop(0, x_ref.shape[0] // rows, body, 0)

    o_ref[...] = _dot(h_ref[...], w_ref[...])


def _inproj(x2d, norm_g3, ada4, ada_row, w_in_bf, l, seq):
    m, d = x2d.shape
    tm = min(ROW_TILE, m)
    tn = 1024
    return pl.pallas_call(
        functools.partial(_inproj_kernel, rows=128),
        grid=(m // tm, N_IN // tn),
        in_specs=[
            pl.BlockSpec((tm, d), lambda i, j: (i, 0)),
            pl.BlockSpec((None, 1, d), lambda i, j: (l, 0, 0)),
            pl.BlockSpec((None, None, 1, 3 * d), lambda i, j: (l, ada_row(i * tm), 0, 0)),
            pl.BlockSpec((None, d, tn), lambda i, j: (l, 0, j)),
        ],
        out_specs=pl.BlockSpec((tm, tn), lambda i, j: (i, j)),
        out_shape=jax.ShapeDtypeStruct((m, N_IN), F32),
        scratch_shapes=[pltpu.VMEM((tm, d), BF16)],
        compiler_params=_params(("arbitrary", "arbitrary"), 48),
        name="inproj",
    )(x2d, norm_g3, ada4, w_in_bf)


def _head_norm_store(x_ref, g, out_refs):
    for h in range(N_HEADS):
        sl = slice(h * HEAD_DIM, (h + 1) * HEAD_DIM)
        x = x_ref[:, sl]
        ms = jnp.mean(x * x, axis=-1, keepdims=True)
        y = x * lax.rsqrt(ms + EPS) * g
        for ref in out_refs:
            ref[:, sl] = y.astype(ref.dtype)


def _prep_ctx_kernel(q_ref, k_ref, v_ref, qg_ref, kg_ref, *rest):
    qn_ref, kn_ref, vb_ref, nk_ref, nv_ref = rest[-5:]
    _head_norm_store(q_ref, qg_ref[...], (qn_ref,))
    _head_norm_store(k_ref, kg_ref[...], (kn_ref, nk_ref))
    v = v_ref[...]
    vb_ref[...] = v.astype(BF16)
    nv_ref[...] = v


def _prep_lat_kernel(q_ref, k_ref, v_ref, qg_ref, kg_ref, qn_ref, kn_ref, vb_ref):
    _head_norm_store(q_ref, qg_ref[...], (qn_ref,))
    _head_norm_store(k_ref, kg_ref[...], (kn_ref,))
    vb_ref[...] = v_ref[...].astype(BF16)


def _prep(proj, qg3, kg3, l, seq, new_kv):
    m = proj.shape[0]
    depth = qg3.shape[0]
    tm = min(256, seq)
    col = lambda c: pl.BlockSpec((tm, ATT_W), lambda i: (i, c))
    gspec = pl.BlockSpec((None, 1, HEAD_DIM), lambda i: (l, 0, 0))
    in_specs = [col(OFF_Q // ATT_W), col(OFF_K // ATT_W), col(OFF_V // ATT_W), gspec, gspec]
    bf_shape = jax.ShapeDtypeStruct((m, ATT_W), BF16)
    bf_spec = pl.BlockSpec((tm, ATT_W), lambda i: (i, 0))
    if new_kv is None:
        return pl.pallas_call(
            _prep_lat_kernel, grid=(m // tm,), in_specs=in_specs,
            out_specs=[bf_spec] * 3, out_shape=[bf_shape] * 3,
            compiler_params=_params(("arbitrary",), 32), name="prep_lat",
        )(proj, proj, proj, qg3, kg3)
    assert tm == seq
    batch = m // seq
    kv_shape = jax.ShapeDtypeStruct((batch, depth, seq, ATT_W), F32)
    kv_spec = pl.BlockSpec((None, None, seq, ATT_W), lambda i: (i, l, 0, 0))
    return pl.pallas_call(
        _prep_ctx_kernel, grid=(m // tm,),
        in_specs=in_specs + [pl.BlockSpec(memory_space=pl.ANY)] * 2,
        out_specs=[bf_spec] * 3 + [kv_spec] * 2,
        out_shape=[bf_shape] * 3 + [kv_shape] * 2,
        input_output_aliases={5: 3, 6: 4},
        compiler_params=_params(("arbitrary",), 32), name="prep_ctx",
    )(proj, proj, proj, qg3, kg3, *new_kv)


def _ctx_attn_kernel(q_ref, k_ref, v_ref, ga_ref, o_ref):
    for h in range(N_HEADS):
        sl = slice(h * HEAD_DIM, (h + 1) * HEAD_DIM)
        s = _dot_nt(q_ref[:, sl], k_ref[:, sl]) * ATT_SCALE
        p = jnp.exp(s - jnp.max(s, axis=-1, keepdims=True))
        inv = 1.0 / jnp.sum(p, axis=-1, keepdims=True)
        o = _dot(p.astype(BF16), v_ref[:, sl]) * inv
        o_ref[:, sl] = (o * _silu(ga_ref[:, sl])).astype(BF16)


def _ctx_attn(qn, kn, vb, proj, seq):
    m = qn.shape[0]
    spec = pl.BlockSpec((seq, ATT_W), lambda b: (b, 0))
    return pl.pallas_call(
        _ctx_attn_kernel, grid=(m // seq,),
        in_specs=[spec, spec, spec, pl.BlockSpec((seq, ATT_W), lambda b: (b, OFF_GA // ATT_W))],
        out_specs=spec,
        out_shape=jax.ShapeDtypeStruct((m, ATT_W), BF16),
        compiler_params=_params(("arbitrary",), 32), name="ctx_attn",
    )(qn, kn, vb, proj)


def _window_bias(rpb):
    depth, heads, n_dr, n_dc = rpb.shape
    c = np.arange(GRID_W)[:, None]
    kc = np.arange(GRID_W)[None, :]
    cs = np.clip(c - KW // 2, 0, GRID_W - KW)
    valid = (kc >= cs) & (kc < cs + KW)
    dc = np.clip(kc - c + KW - 1, 0, n_dc - 1).reshape(-1)
    onehot = (jnp.asarray(dc)[None, :] == jnp.arange(n_dc)[:, None]).astype(F32)
    cb = jnp.einsum("lhdm,mx->lhdx", rpb, onehot, precision=lax.Precision.HIGHEST)
    cb = jnp.where(jnp.asarray(valid), cb.reshape(depth, heads, n_dr, GRID_W, GRID_W), NEG)
    cb = jnp.pad(cb, ((0, 0), (0, 0), (1, 1), (0, 0), (0, 0)), constant_values=NEG)
    return jnp.concatenate([cb[:, :, :-1], cb[:, :, 1:]], axis=-1)


def _lat_attn_kernel(q_ref, k_ref, v_ref, ck_ref, cv_ref, bias_ref, ga_ref, o_ref,
                     ckb_ref, cvb_ref, *, rows, kh, rb, kr):
    g = pl.program_id(1)

    @pl.when(g == 0)
    def _():
        ckb_ref[...] = ck_ref[...].astype(BF16)
        cvb_ref[...] = cv_ref[...].astype(BF16)

    r0 = g * rb
    ws = jnp.clip(r0 - kh // 2, 0, rows - kr)
    span = pl.ds(pl.multiple_of(ws * GRID_W, GRID_W), kr * GRID_W)
    q_row = r0 + lax.broadcasted_iota(jnp.int32, (rb * GRID_W, 1), 0) // GRID_W
    k_row = ws + lax.broadcasted_iota(jnp.int32, (1, kr * GRID_W), 1) // GRID_W
    q_rs = jnp.clip(q_row - kh // 2, 0, rows - kh)
    in_window = (k_row >= q_rs) & (k_row < q_rs + kh)
    n_pairs = bias_ref.shape[1]
    for h in range(N_HEADS):
        sl = slice(h * HEAD_DIM, (h + 1) * HEAD_DIM)
        q = q_ref[:, sl]
        bias = jnp.concatenate(
            [jnp.concatenate(
                [bias_ref[h, jnp.clip(ws + 2 * j - (r0 + qi) + MAX_KH, 0, n_pairs - 1)]
                 for j in range(kr // 2)], axis=-1)
             for qi in range(rb)], axis=0)
        s_win = jnp.where(in_window, _dot_nt(q, k_ref[span, sl]) * ATT_SCALE + bias, NEG)
        s_ctx = _dot_nt(q, ckb_ref[:, sl]) * ATT_SCALE
        mx = jnp.maximum(jnp.max(s_win, axis=-1, keepdims=True),
                         jnp.max(s_ctx, axis=-1, keepdims=True))
        p_win = jnp.exp(s_win - mx)
        p_ctx = jnp.exp(s_ctx - mx)
        inv = 1.0 / (jnp.sum(p_win, axis=-1, keepdims=True) + jnp.sum(p_ctx, axis=-1, keepdims=True))
        o = (_dot(p_win.astype(BF16), v_ref[span, sl]) + _dot(p_ctx.astype(BF16), cvb_ref[:, sl])) * inv
        o_ref[:, sl] = (o * _silu(ga_ref[:, sl])).astype(BF16)


def _lat_attn(qn, kn, vb, cache_k4, cache_v4, bias, proj, l, seq):
    m = qn.shape[0]
    batch = m // seq
    rows = seq // GRID_W
    kh = min(MAX_KH, rows)
    rb = min(LAT_ROW_BLOCK, rows)
    kr = min(rows, kh + rb)
    assert rows % rb == 0 and kr % 2 == 0
    past = cache_k4.shape[2]
    qspec = pl.BlockSpec((rb * GRID_W, ATT_W), lambda b, g: (b * (rows // rb) + g, 0))
    kvspec = pl.BlockSpec((seq, ATT_W), lambda b, g: (b, 0))
    cspec = pl.BlockSpec((None, None, past, ATT_W), lambda b, g: (b, l, 0, 0))
    return pl.pallas_call(
        functools.partial(_lat_attn_kernel, rows=rows, kh=kh, rb=rb, kr=kr),
        grid=(batch, rows // rb),
        in_specs=[qspec, kvspec, kvspec, cspec, cspec,
                  pl.BlockSpec((None,) + bias.shape[1:], lambda b, g: (l, 0, 0, 0, 0)),
                  pl.BlockSpec((rb * GRID_W, ATT_W),
                               lambda b, g: (b * (rows // rb) + g, OFF_GA // ATT_W))],
        out_specs=qspec,
        out_shape=jax.ShapeDtypeStruct((m, ATT_W), BF16),
        scratch_shapes=[pltpu.VMEM((past, ATT_W), BF16), pltpu.VMEM((past, ATT_W), BF16)],
        compiler_params=_params(("arbitrary", "arbitrary"), 48), name="lat_attn",
    )(qn, kn, vb, cache_k4, cache_v4, bias, proj)


def _cos_sin_table(r, ncols, period):
    def factor(v):
        ang = ((r[..., None] * v) % period).astype(F32) * (2.0 * math.pi / period)
        return jnp.cos(ang), jnp.sin(ang)

    ca, sa = factor(jnp.arange(ncols // TRIG_BLOCK, dtype=jnp.int32) * TRIG_BLOCK)
    cb, sb = factor(jnp.arange(TRIG_BLOCK, dtype=jnp.int32))
    ca, sa, cb, sb = ca[..., :, None], sa[..., :, None], cb[..., None, :], sb[..., None, :]
    shape = r.shape + (ncols,)
    return (ca * cb - sa * sb).reshape(shape), (sa * cb + ca * sb).reshape(shape)


def _fnet_tables(seq):
    cl, sl = _cos_sin_table(jnp.arange(seq, dtype=jnp.int32), seq, seq)
    cc, sc = _cos_sin_table(jnp.arange(FNET_GDIM, dtype=jnp.int32), FNET_GDIM, FNET_GDIM)
    return (jnp.concatenate([cl, -sl], axis=1).astype(BF16), cc.astype(BF16), sc.astype(BF16))


def _fnet_kernel(u_ref, gb_ref, tl_ref, cc_ref, sc_ref, o_ref, t_ref, *, seq, tl):
    i = pl.program_id(1)

    @pl.when(i == 0)
    def _():
        for g in range(FNET_GROUPS):
            sl = slice(g * FNET_GDIM, (g + 1) * FNET_GDIM)
            ug = u_ref[:, sl].astype(BF16)
            t_ref[0:seq, sl] = _dot(ug, cc_ref[...]).astype(BF16)
            t_ref[seq:2 * seq, sl] = _dot(ug, sc_ref[...]).astype(BF16)

    y = _dot(tl_ref[...], t_ref[...]) * ((seq * FNET_GDIM) ** -0.5)
    o_ref[...] = (y * _silu(gb_ref[...])).astype(BF16)


def _fnet(proj, tables, seq):
    m = proj.shape[0]
    tab_l, cc, sc = tables
    tl = min(512, seq)
    nt = seq // tl
    return pl.pallas_call(
        functools.partial(_fnet_kernel, seq=seq, tl=tl),
        grid=(m // seq, nt),
        in_specs=[
            pl.BlockSpec((seq, FNET_W), lambda b, i: (b, OFF_UB // FNET_W)),
            pl.BlockSpec((tl, FNET_W), lambda b, i: (b * nt + i, OFF_GB // FNET_W)),
            pl.BlockSpec((tl, 2 * seq), lambda b, i: (i, 0)),
            pl.BlockSpec((FNET_GDIM, FNET_GDIM), lambda b, i: (0, 0)),
            pl.BlockSpec((FNET_GDIM, FNET_GDIM), lambda b, i: (0, 0)),
        ],
        out_specs=pl.BlockSpec((tl, FNET_W), lambda b, i: (b * nt + i, 0)),
        out_shape=jax.ShapeDtypeStruct((m, FNET_W), BF16),
        scratch_shapes=[pltpu.VMEM((2 * seq, FNET_W), BF16)],
        compiler_params=_params(("arbitrary", "arbitrary"), 40), name="fnet",
    )(proj, proj, tab_l, cc, sc)


def _hyena_tiling(seq):
    if seq > 1024:
        return min(FREQ_CHUNK // 2, seq), HY_W // 2, 2
    return min(FREQ_CHUNK, seq), HY_W, 4


def _hyena_tables(seq):
    fc = _hyena_tiling(seq)[0]
    nch = seq // fc
    k = jnp.arange(2 * fc, dtype=jnp.int32)
    f = jnp.arange(nch, dtype=jnp.int32)[:, None] * fc + (k % fc)[None, :]
    is_sec = (k >= fc)[None, :, None]
    nyq = is_sec & (f == 0)[:, :, None]
    c, s = _cos_sin_table(f, seq, 2 * seq)
    sgn = (1 - 2 * (jnp.arange(seq, dtype=jnp.int32) % 2)).astype(F32)
    fwd = jnp.where(is_sec, jnp.where(nyq, sgn, -s), c).astype(BF16)
    return fwd, fwd.transpose(0, 2, 1)


def _filter_features(seq):
    t = jnp.linspace(0.0, 1.0, seq, dtype=F32)[:, None]
    bands = (FILTER_EMB - 1) // 2
    w = (2.0 * math.pi / seq) * jnp.arange(seq, dtype=F32)[:, None]
    f = jnp.linspace(1e-4, bands - 1, bands, dtype=F32)[None, :]
    z = jnp.concatenate([t, jnp.cos(w * f), -jnp.sin(w * f)], axis=-1)
    z = jnp.pad(z, ((0, 0), (0, 128 - FILTER_EMB)))
    deltas = jnp.abs(jnp.linspace(MIN_DECAY, MAX_DECAY, HY_W, dtype=F32))[None, :]
    return z, t, deltas


def _dot_f32(a, b):
    return jnp.dot(a, b, preferred_element_type=F32, precision=lax.Precision.HIGHEST)


def _filter_kernel(z_ref, t_ref, dl_ref, w1_ref, b1_ref, fr_ref, w2_ref, b2_ref, w3_ref,
                   fwd_ref, o_ref, h_ref, sd_ref, nyq_ref, *, seq, fc, rows):
    j = pl.program_id(0)
    nblk = 2 * HYENA_ORDER
    nrc = seq // rows

    @pl.when(j == 0)
    def _():
        fr = fr_ref[...]

        def taps(r, acc):
            sl = pl.ds(pl.multiple_of(r * rows, rows), rows)
            h = jnp.sin(fr * (_dot_f32(z_ref[sl, :], w1_ref[...]) + b1_ref[...]))
            h = jnp.sin(fr * (_dot_f32(h, w2_ref[...]) + b2_ref[...]))
            h = _dot_f32(h, w3_ref[...])
            decay = jnp.exp(-t_ref[sl, :] * dl_ref[...])
            h = h * jnp.concatenate([decay] * nblk, axis=-1)
            h_ref[sl, :] = h
            return acc + jnp.sum(jnp.abs(h), axis=0, keepdims=True)

        tot = lax.fori_loop(0, nrc, taps, jnp.zeros((1, nblk * HY_W), F32))
        inv = 1.0 / (tot + EPS)
        sgn = (1 - 2 * (lax.broadcasted_iota(jnp.int32, (rows, 1), 0) % 2)).astype(F32)

        def fold(r, acc):
            sl = pl.ds(pl.multiple_of(r * rows, rows), rows)
            h = h_ref[sl, :] * inv
            pos = r * rows + lax.broadcasted_iota(jnp.int32, (rows, 1), 0)
            sums = []
            for o in range(HYENA_ORDER):
                fw = h[:, (2 * o) * HY_W:(2 * o + 1) * HY_W]
                bw = jnp.where(pos == 0, 0.0, h[:, (2 * o + 1) * HY_W:(2 * o + 2) * HY_W])
                sums.append(fw + bw)
                sd_ref[0, sl, o * HY_W:(o + 1) * HY_W] = (fw + bw).astype(BF16)
                sd_ref[1, sl, o * HY_W:(o + 1) * HY_W] = (fw - bw).astype(BF16)
            return acc + jnp.sum(jnp.concatenate(sums, axis=-1) * sgn, axis=0, keepdims=True)

        nyq_ref[...] = lax.fori_loop(0, nrc, fold, jnp.zeros((1, HYENA_ORDER * HY_W), F32))

    re = _dot(fwd_ref[0:fc, :], sd_ref[0])
    sec = _dot(fwd_ref[fc:2 * fc, :], sd_ref[1])
    first = (lax.broadcasted_iota(jnp.int32, (fc, 1), 0) == 0) & (j == 0)
    sec = jnp.where(first, nyq_ref[...], sec)
    weight = jnp.where(first, 0.5 / seq, 1.0 / seq)
    o_ref[0:fc, :] = re * weight
    o_ref[fc:2 * fc, :] = sec * weight


def _filter(feats, fwd, w1p, b1, fr, w2, b2, w3, l, seq):
    z, t, deltas = feats
    nch, fc2, _ = fwd.shape
    fc = fc2 // 2
    width = HYENA_ORDER * HY_W
    rows = min(256, seq)
    full = lambda a: pl.BlockSpec(a.shape, lambda j: (0,) * a.ndim)
    lay = lambda a: pl.BlockSpec((None,) + a.shape[1:], lambda j: (l,) + (0,) * (a.ndim - 1))
    return pl.pallas_call(
        functools.partial(_filter_kernel, seq=seq, fc=fc, rows=rows),
        grid=(nch,),
        in_specs=[full(z), full(t), full(deltas), lay(w1p), lay(b1), lay(fr), lay(w2), lay(b2), lay(w3),
                  pl.BlockSpec((None, 2 * fc, seq), lambda j: (j, 0, 0))],
        out_specs=pl.BlockSpec((None, 2 * fc, width), lambda j: (j, 0, 0)),
        out_shape=jax.ShapeDtypeStruct((nch, 2 * fc, width), F32),
        scratch_shapes=[pltpu.VMEM((seq, 2 * width), F32),
                        pltpu.VMEM((2, seq, width), BF16),
                        pltpu.VMEM((1, width), F32)],
        compiler_params=_params(("arbitrary",), 48), name="hyena_filter",
    )(z, t, deltas, w1p, b1, fr, w2, b2, w3, fwd)


def _hyena_kernel(v_ref, x1_ref, x2_ref, gc_ref, cw_ref, cb_ref, hb_ref, fwd_ref, inv_ref, kf_ref,
                  o_ref, z_ref, zb_ref, y_ref, *, seq, fc, sb, tc):
    o = pl.program_id(2)
    j = pl.program_id(3)
    last_j = pl.num_programs(3) - 1
    pos = lax.broadcasted_iota(jnp.int32, (seq, 1), 0)

    def short_conv(ref, k, s):
        x = ref[s * seq:(s + 1) * seq, :]
        prev = jnp.where(pos == 0, 0.0, pltpu.roll(x, 1, 0))
        nxt = jnp.where(pos == seq - 1, 0.0, pltpu.roll(x, seq - 1, 0))
        return (prev * cw_ref[0, k:k + 1, :] + x * cw_ref[1, k:k + 1, :]
                + nxt * cw_ref[2, k:k + 1, :] + cb_ref[k:k + 1, :])

    @pl.when((o == 0) & (j == 0))
    def _():
        for s in range(sb):
            lanes = slice(s * tc, (s + 1) * tc)
            z = short_conv(v_ref, 0, s)
            z_ref[:, lanes] = z
            zb_ref[:, lanes] = z.astype(BF16)
        y_ref[...] = jnp.zeros_like(y_ref)

    @pl.when((o == 1) & (j == 0))
    def _():
        for s in range(sb):
            lanes = slice(s * tc, (s + 1) * tc)
            z = short_conv(x1_ref, 1, s) * (y_ref[:, lanes] + hb_ref[0:1, :] * z_ref[:, lanes])
            z_ref[:, lanes] = z
            zb_ref[:, lanes] = z.astype(BF16)
        y_ref[...] = jnp.zeros_like(y_ref)

    zf = _dot(fwd_ref[...], zb_ref[...])
    kr, ks = kf_ref[0:fc, :], kf_ref[fc:2 * fc, :]
    nyq = (lax.broadcasted_iota(jnp.int32, (fc, 1), 0) == 0) & (j == 0)
    parts = []
    for s in range(sb):
        lanes = slice(s * tc, (s + 1) * tc)
        zr, zs = zf[0:fc, lanes], zf[fc:2 * fc, lanes]
        ss = zs * ks
        yr = zr * kr - jnp.where(nyq, 0.0, ss)
        ys = jnp.where(nyq, ss, zr * ks + zs * kr)
        parts.append(jnp.concatenate([yr, ys], axis=0).astype(BF16))
    y_ref[...] += _dot(inv_ref[...], jnp.concatenate(parts, axis=-1))

    @pl.when((o == 1) & (j == last_j))
    def _():
        for s in range(sb):
            lanes = slice(s * tc, (s + 1) * tc)
            z = short_conv(x2_ref, 2, s) * (y_ref[:, lanes] + hb_ref[1:2, :] * z_ref[:, lanes])
            o_ref[s * seq:(s + 1) * seq, :] = (z * _silu(gc_ref[s * seq:(s + 1) * seq, :])).astype(BF16)


def _hyena(proj, cw, cb, hb, fwd, inv, kf, seq):
    m = proj.shape[0]
    nch, fc2, _ = fwd.shape
    fc = fc2 // 2
    _, tc, sb = _hyena_tiling(seq)
    sb = min(sb, m // seq)
    nct = HY_W // tc
    hc0 = OFF_HC // tc
    once = pl.Buffered(1)
    col = lambda k: pl.BlockSpec((sb * seq, tc), lambda b, c, o, j: (b, hc0 + k * nct + c),
                                 pipeline_mode=once)
    return pl.pallas_call(
        functools.partial(_hyena_kernel, seq=seq, fc=fc, sb=sb, tc=tc),
        grid=(m // (sb * seq), nct, HYENA_ORDER, nch),
        in_specs=[
            col(0), col(1), col(2),
            pl.BlockSpec((sb * seq, tc), lambda b, c, o, j: (b, OFF_GC // tc + c), pipeline_mode=once),
            pl.BlockSpec((3, 3, tc), lambda b, c, o, j: (0, 0, c)),
            pl.BlockSpec((3, tc), lambda b, c, o, j: (0, c)),
            pl.BlockSpec((HYENA_ORDER, tc), lambda b, c, o, j: (0, c)),
            pl.BlockSpec((None, 2 * fc, seq), lambda b, c, o, j: (j, 0, 0)),
            pl.BlockSpec((None, seq, 2 * fc), lambda b, c, o, j: (j, 0, 0)),
            pl.BlockSpec((None, 2 * fc, tc), lambda b, c, o, j: (j, 0, o * nct + c)),
        ],
        out_specs=pl.BlockSpec((sb * seq, tc), lambda b, c, o, j: (b, c)),
        out_shape=jax.ShapeDtypeStruct((m, HY_W), BF16),
        scratch_shapes=[pltpu.VMEM((seq, sb * tc), F32), pltpu.VMEM((seq, sb * tc), BF16),
                        pltpu.VMEM((seq, sb * tc), F32)],
        compiler_params=_params(("arbitrary",) * 4, 56), name="hyena",
    )(proj, proj, proj, proj, cw, cb, hb, fwd, inv, kf)


def _merge_kernel(ya_ref, yb_ref, yc_ref, ga_ref, gb_ref, gc_ref, wa_ref, wb_ref, wc_ref, o_ref):
    acc = jax.nn.sigmoid(ga_ref[...]) * _dot(ya_ref[...], wa_ref[...])
    acc += jax.nn.sigmoid(gb_ref[...]) * _dot(yb_ref[...], wb_ref[...])
    acc += jax.nn.sigmoid(gc_ref[...]) * _dot(yc_ref[...], wc_ref[...])
    o_ref[...] = acc.astype(BF16)


def _merge(ya, yb, yc, proj, w_br_bf, l, seq):
    m = ya.shape[0]
    tm = min(ROW_TILE, m)
    tn = 512
    d = D_MODEL
    row = lambda w: pl.BlockSpec((tm, w), lambda i, j: (i, 0))
    gate = lambda k: pl.BlockSpec((tm, tn), lambda i, j: (i, (OFF_MG + k * d) // tn + j))
    return pl.pallas_call(
        _merge_kernel, grid=(m // tm, d // tn),
        in_specs=[row(ATT_W), row(FNET_W), row(HY_W), gate(0), gate(1), gate(2),
                  pl.BlockSpec((None, ATT_W, tn), lambda i, j: (l, 0, j)),
                  pl.BlockSpec((None, FNET_W, tn), lambda i, j: (l, ATT_W // FNET_W, j)),
                  pl.BlockSpec((None, HY_W, tn), lambda i, j: (l, (ATT_W + FNET_W) // HY_W, j))],
        out_specs=pl.BlockSpec((tm, tn), lambda i, j: (i, j)),
        out_shape=jax.ShapeDtypeStruct((m, d), BF16),
        compiler_params=_params(("arbitrary", "arbitrary"), 40), name="merge",
    )(ya, yb, yc, proj, proj, proj, w_br_bf, w_br_bf, w_br_bf)


def _outproj_kernel(mg_ref, w_ref, x_ref, ada_ref, o_ref):
    o_ref[...] = x_ref[...] + ada_ref[...] * _dot(mg_ref[...], w_ref[...])


def _outproj(merged, x2d, ada4, ada_row, w_out_bf, l, seq):
    m, d = x2d.shape
    tm = min(ROW_TILE, m)
    tn = 512
    return pl.pallas_call(
        _outproj_kernel, grid=(m // tm, d // tn),
        in_specs=[pl.BlockSpec((tm, d), lambda i, j: (i, 0)),
                  pl.BlockSpec((None, d, tn), lambda i, j: (l, 0, j)),
                  pl.BlockSpec((tm, tn), lambda i, j: (i, j)),
                  pl.BlockSpec((None, None, 1, tn),
                               lambda i, j: (l, ada_row(i * tm), 0, 2 * d // tn + j))],
        out_specs=pl.BlockSpec((tm, tn), lambda i, j: (i, j)),
        out_shape=jax.ShapeDtypeStruct((m, d), F32),
        compiler_params=_params(("arbitrary", "arbitrary"), 40), name="outproj",
    )(merged, w_out_bf, x2d, ada4)


def kernel(x_prompt, x_sample, cache_k, cache_v, c, c_ctx, norm_g, w_ada, b_ada, w_in, q_norm_g, k_norm_g, rpb, conv_w, conv_b, f_w1, f_b1, f_freq, f_w2, f_b2, f_w3, hy_bias, w_br, w_out):
    batch, seq, d = x_prompt.shape
    dec_batch, dec_seq, _ = x_sample.shape
    depth = norm_g.shape[0]
    past = cache_k.shape[2]
    assert dec_batch + 1 <= ADA_ROWS

    cv = jnp.zeros((ADA_ROWS, d), F32).at[0].set(c_ctx).at[1:1 + dec_batch].set(c)
    ada4 = _ada(cv, w_ada, b_ada).reshape(depth, ADA_ROWS, 1, 3 * d)
    ctx_row = lambda tok: 0
    lat_row = lambda tok: 1 + tok // dec_seq

    w_in_bf = w_in.astype(BF16)
    w_br_bf = w_br.astype(BF16)
    w_out_bf = w_out.astype(BF16)
    norm_g3 = norm_g.reshape(depth, 1, d)
    qg3 = q_norm_g.reshape(depth, 1, HEAD_DIM)
    kg3 = k_norm_g.reshape(depth, 1, HEAD_DIM)
    cache_k4 = cache_k.reshape(dec_batch, depth, past, ATT_W)
    cache_v4 = cache_v.reshape(dec_batch, depth, past, ATT_W)
    w1p = jnp.pad(f_w1, ((0, 0), (0, 128 - FILTER_EMB), (0, 0)))
    b1 = f_b1.reshape(depth, 1, FILTER_FF)
    fr = f_freq.reshape(depth, 1, FILTER_FF)
    b2 = f_b2.reshape(depth, 1, FILTER_FF)
    win_bias = _window_bias(rpb)

    groups = []
    for x, s, row in ((x_prompt, seq, ctx_row), (x_sample, dec_seq, lat_row)):
        fwd, inv = _hyena_tables(s)
        groups.append(dict(x=x.reshape(-1, d), seq=s, row=row, fnet=_fnet_tables(s),
                           fwd=fwd, inv=inv, feats=_filter_features(s)))

    new_kv = tuple(jnp.zeros((batch, depth, seq, ATT_W), F32) for _ in range(2))
    for l in range(depth):
        cw = conv_w[l].reshape(3, 3, HY_W)
        cb = conv_b[l].reshape(3, HY_W)
        hb = hy_bias[l]
        for gi, g in enumerate(groups):
            s = g["seq"]
            proj = _inproj(g["x"], norm_g3, ada4, g["row"], w_in_bf, l, s)
            if gi == 0:
                qn, kn, vb, nk, nv = _prep(proj, qg3, kg3, l, s, new_kv)
                new_kv = (nk, nv)
                ya = _ctx_attn(qn, kn, vb, proj, s)
            else:
                qn, kn, vb = _prep(proj, qg3, kg3, l, s, None)
                ya = _lat_attn(qn, kn, vb, cache_k4, cache_v4, win_bias, proj, l, s)
            yb = _fnet(proj, g["fnet"], s)
            kf = _filter(g["feats"], g["fwd"], w1p, b1, fr, f_w2, b2, f_w3, l, s)
            yc = _hyena(proj, cw, cb, hb, g["fwd"], g["inv"], kf, s)
            merged = _merge(ya, yb, yc, proj, w_br_bf, l, s)
            g["x"] = _outproj(merged, g["x"], ada4, g["row"], w_out_bf, l, s)

    nk, nv = new_kv
    return (groups[0]["x"].reshape(batch, seq, d),
            groups[1]["x"].reshape(dec_batch, dec_seq, d),
            nk.reshape(batch, depth, seq, N_HEADS, HEAD_DIM),
            nv.reshape(batch, depth, seq, N_HEADS, HEAD_DIM))
```

```python
import functools
import math

import jax
import jax.numpy as jnp
import numpy as np
from jax import lax
from jax.experimental import pallas as pl
from jax.experimental.pallas import tpu as pltpu

F32 = jnp.float32
BF16 = jnp.bfloat16

D_MODEL = 2048
GRID_W = 64
N_HEADS = 8
HEAD_DIM = 128
ATT_W = N_HEADS * HEAD_DIM
MAX_KH = 8
KW = 16
FNET_GROUPS = 4
FNET_GDIM = 128
FNET_W = FNET_GROUPS * FNET_GDIM
HY_W = 512
HYENA_ORDER = 2
FILTER_EMB = 33
FILTER_FF = 64
MIN_DECAY = math.log(1e-2) / 1.5
MAX_DECAY = math.log(1e-2) / 0.3
OFF_Q = 0
OFF_K = OFF_Q + ATT_W
OFF_V = OFF_K + ATT_W
OFF_GA = OFF_V + ATT_W
OFF_UB = OFF_GA + ATT_W
OFF_GB = OFF_UB + FNET_W
OFF_HC = OFF_GB + FNET_W
OFF_GC = OFF_HC + 3 * HY_W
OFF_MG = OFF_GC + HY_W
N_IN = OFF_MG + 3 * D_MODEL
N_REST = N_IN - OFF_GA
_REST_GA, _REST_UB, _REST_GB, _REST_HC, _REST_GC, _REST_MG = (
    off - OFF_GA for off in (OFF_GA, OFF_UB, OFF_GB, OFF_HC, OFF_GC, OFF_MG))
EPS = 1e-6
NEG = -1e30
ATT_SCALE = HEAD_DIM ** -0.5

ADA_ROWS = 8
FREQ_CHUNK = 512
TRIG_BLOCK = 64
LAT_ROW_BLOCK = 4
ROW_TILE = 1024
MIB = 1024 * 1024


def _params(semantics, vmem_mib):
    return pltpu.CompilerParams(dimension_semantics=semantics,
                                vmem_limit_bytes=vmem_mib * MIB)


def _silu(x):
    return x * jax.nn.sigmoid(x)


def _dot(a, b):
    return jnp.dot(a, b, preferred_element_type=F32)


def _dot_nt(a, b):
    return lax.dot_general(a, b, (((1,), (1,)), ((), ())), preferred_element_type=F32)


def _ada_kernel(cv_ref, w_ref, b_ref, o_ref):
    s = _silu(cv_ref[...]).astype(BF16)
    o_ref[...] = _dot(s, w_ref[...].astype(BF16)) + b_ref[...]


def _ada(cv, w_ada, b_ada):
    depth, d, n = w_ada.shape
    tn = 512
    return pl.pallas_call(
        _ada_kernel,
        grid=(depth, n // tn),
        in_specs=[
            pl.BlockSpec((ADA_ROWS, d), lambda l, j: (0, 0)),
            pl.BlockSpec((None, d, tn), lambda l, j: (l, 0, j)),
            pl.BlockSpec((None, 1, tn), lambda l, j: (l, 0, j)),
        ],
        out_specs=pl.BlockSpec((None, ADA_ROWS, tn), lambda l, j: (l, 0, j)),
        out_shape=jax.ShapeDtypeStruct((depth, ADA_ROWS, n), F32),
        compiler_params=_params(("arbitrary", "arbitrary"), 32),
        name="ada",
    )(cv, w_ada, b_ada.reshape(depth, 1, n))


def _inproj_kernel(x_ref, g_ref, ada_ref, w_ref, qg_ref, kg_ref, *refs, rows, seq, emit_kv):
    if emit_kv:
        qkv_ref, rest_ref, nk_ref, nv_ref, h_ref, acc_ref = refs[2:]
    else:
        qkv_ref, rest_ref, h_ref, acc_ref = refs
        nk_ref = nv_ref = None
    j = pl.program_id(1)
    tm = x_ref.shape[0]

    @pl.when(j == 0)
    def _():
        g = g_ref[...]
        shift = ada_ref[:, 0:D_MODEL]
        scale1 = 1.0 + ada_ref[:, D_MODEL:2 * D_MODEL]

        def body(r, carry):
            sl = pl.ds(pl.multiple_of(r * rows, rows), rows)
            x = x_ref[sl, :]
            ms = jnp.mean(x * x, axis=-1, keepdims=True)
            y = x * lax.rsqrt(ms + EPS) * g
            h_ref[sl, :] = (y * scale1 + shift).astype(BF16)
            return carry

        lax.fori_loop(0, tm // rows, body, 0)

    def head_norm(gain_ref, kv_ref):
        chunk = min(seq, tm)

        def body(r, carry):
            sl = pl.ds(pl.multiple_of(r * chunk, chunk), chunk)
            for h in range(N_HEADS):
                lanes = slice(h * HEAD_DIM, (h + 1) * HEAD_DIM)
                x = acc_ref[sl, lanes]
                ms = jnp.mean(x * x, axis=-1, keepdims=True)
                y = x * lax.rsqrt(ms + EPS) * gain_ref[...]
                qkv_ref[sl, lanes] = y.astype(BF16)
                if kv_ref is not None:
                    kv_ref[r, :, lanes] = y
            return carry

        lax.fori_loop(0, tm // chunk, body, 0)

    @pl.when(j < 3)
    def _():
        acc_ref[...] = _dot(h_ref[...], w_ref[...])

    @pl.when(j == 0)
    def _():
        head_norm(qg_ref, None)

    @pl.when(j == 1)
    def _():
        head_norm(kg_ref, nk_ref)

    @pl.when(j == 2)
    def _():
        qkv_ref[...] = acc_ref[...].astype(BF16)
        if emit_kv:
            for s in range(nv_ref.shape[0]):
                nv_ref[s] = acc_ref[s * seq:(s + 1) * seq, :]

    @pl.when(j >= 3)
    def _():
        rest_ref[...] = _dot(h_ref[...], w_ref[...])


def _inproj(x2d, norm_g3, ada4, ada_row, w_in_bf, qg3, kg3, l, seq, new_kv):
    m, d = x2d.shape
    tm = min(ROW_TILE, m)
    tn = ATT_W
    emit_kv = new_kv is not None
    gspec = pl.BlockSpec((None, 1, HEAD_DIM), lambda i, j: (l, 0, 0))
    in_specs = [
        pl.BlockSpec((tm, d), lambda i, j: (i, 0), pipeline_mode=pl.Buffered(1) if emit_kv else None),
        pl.BlockSpec((None, 1, d), lambda i, j: (l, 0, 0)),
        pl.BlockSpec((None, None, 1, 3 * d), lambda i, j: (l, ada_row(i * tm), 0, 0)),
        pl.BlockSpec((None, d, tn), lambda i, j: (l, 0, j)),
        gspec, gspec,
    ]
    out_specs = [pl.BlockSpec((tm, tn), lambda i, j: (i, jnp.minimum(j, 2))),
                 pl.BlockSpec((tm, tn), lambda i, j: (i, jnp.maximum(j - 3, 0)))]
    out_shape = [jax.ShapeDtypeStruct((m, 3 * ATT_W), BF16), jax.ShapeDtypeStruct((m, N_REST), F32)]
    args = [x2d, norm_g3, ada4, w_in_bf, qg3, kg3]
    aliases = {}
    if emit_kv:
        assert tm % seq == 0
        kv_spec = pl.BlockSpec((tm // seq, None, seq, ATT_W), lambda i, j: (i, l, 0, 0))
        in_specs += [pl.BlockSpec(memory_space=pl.ANY)] * 2
        out_specs += [kv_spec] * 2
        out_shape += [jax.ShapeDtypeStruct(a.shape, a.dtype) for a in new_kv]
        aliases = {6: 2, 7: 3}
        args += list(new_kv)
    return pl.pallas_call(
        functools.partial(_inproj_kernel, rows=128, seq=seq, emit_kv=emit_kv),
        grid=(m // tm, N_IN // tn),
        in_specs=in_specs, out_specs=out_specs, out_shape=out_shape,
        input_output_aliases=aliases,
        scratch_shapes=[pltpu.VMEM((tm, d), BF16), pltpu.VMEM((tm, tn), F32)],
        compiler_params=_params(("arbitrary", "arbitrary"), 56),
        name="inproj",
    )(*args)


def _ctx_attn_kernel(q_ref, k_ref, v_ref, ga_ref, o_ref):
    for h in range(N_HEADS):
        sl = slice(h * HEAD_DIM, (h + 1) * HEAD_DIM)
        s = _dot_nt(q_ref[:, sl], k_ref[:, sl]) * ATT_SCALE
        p = jnp.exp(s - jnp.max(s, axis=-1, keepdims=True))
        inv = 1.0 / jnp.sum(p, axis=-1, keepdims=True)
        o = _dot(p.astype(BF16), v_ref[:, sl]) * inv
        o_ref[:, sl] = (o * _silu(ga_ref[:, sl])).astype(BF16)


def _ctx_attn(qkv, rest, seq):
    m = qkv.shape[0]
    col = lambda c: pl.BlockSpec((seq, ATT_W), lambda b: (b, c))
    return pl.pallas_call(
        _ctx_attn_kernel, grid=(m // seq,),
        in_specs=[col(0), col(1), col(2), col(_REST_GA // ATT_W)],
        out_specs=col(0),
        out_shape=jax.ShapeDtypeStruct((m, ATT_W), BF16),
        compiler_params=_params(("arbitrary",), 32), name="ctx_attn",
    )(qkv, qkv, qkv, rest)


def _window_bias(rpb):
    depth, heads, n_dr, n_dc = rpb.shape
    c = np.arange(GRID_W)[:, None]
    kc = np.arange(GRID_W)[None, :]
    cs = np.clip(c - KW // 2, 0, GRID_W - KW)
    valid = (kc >= cs) & (kc < cs + KW)
    dc = np.clip(kc - c + KW - 1, 0, n_dc - 1).reshape(-1)
    onehot = (jnp.asarray(dc)[None, :] == jnp.arange(n_dc)[:, None]).astype(F32)
    cb = jnp.einsum("lhdm,mx->lhdx", rpb, onehot, precision=lax.Precision.HIGHEST)
    cb = jnp.where(jnp.asarray(valid), cb.reshape(depth, heads, n_dr, GRID_W, GRID_W), NEG)
    cb = jnp.pad(cb, ((0, 0), (0, 0), (1, 1), (0, 0), (0, 0)), constant_values=NEG)
    return jnp.concatenate([cb[:, :, :-1], cb[:, :, 1:]], axis=-1)


def _lat_attn_kernel(q_ref, k_ref, v_ref, ck_ref, cv_ref, bias_ref, ga_ref, o_ref,
                     ckb_ref, cvb_ref, *, rows, kh, rb, kr):
    g = pl.program_id(1)

    @pl.when(g == 0)
    def _():
        ckb_ref[...] = ck_ref[...].astype(BF16)
        cvb_ref[...] = cv_ref[...].astype(BF16)

    r0 = g * rb
    ws = jnp.clip(r0 - kh // 2, 0, rows - kr)
    span = pl.ds(pl.multiple_of(ws * GRID_W, GRID_W), kr * GRID_W)
    q_row = r0 + lax.broadcasted_iota(jnp.int32, (rb * GRID_W, 1), 0) // GRID_W
    k_row = ws + lax.broadcasted_iota(jnp.int32, (1, kr * GRID_W), 1) // GRID_W
    q_rs = jnp.clip(q_row - kh // 2, 0, rows - kh)
    in_window = (k_row >= q_rs) & (k_row < q_rs + kh)
    n_pairs = bias_ref.shape[1]
    for h in range(N_HEADS):
        sl = slice(h * HEAD_DIM, (h + 1) * HEAD_DIM)
        q = q_ref[:, sl]
        bias = jnp.concatenate(
            [jnp.concatenate(
                [bias_ref[h, jnp.clip(ws + 2 * j - (r0 + qi) + MAX_KH, 0, n_pairs - 1)]
                 for j in range(kr // 2)], axis=-1)
             for qi in range(rb)], axis=0)
        s_win = jnp.where(in_window, _dot_nt(q, k_ref[span, sl]) * ATT_SCALE + bias, NEG)
        s_ctx = _dot_nt(q, ckb_ref[:, sl]) * ATT_SCALE
        mx = jnp.maximum(jnp.max(s_win, axis=-1, keepdims=True),
                         jnp.max(s_ctx, axis=-1, keepdims=True))
        p_win = jnp.exp(s_win - mx)
        p_ctx = jnp.exp(s_ctx - mx)
        inv = 1.0 / (jnp.sum(p_win, axis=-1, keepdims=True) + jnp.sum(p_ctx, axis=-1, keepdims=True))
        o = (_dot(p_win.astype(BF16), v_ref[span, sl]) + _dot(p_ctx.astype(BF16), cvb_ref[:, sl])) * inv
        o_ref[:, sl] = (o * _silu(ga_ref[:, sl])).astype(BF16)


def _lat_attn(qkv, rest, cache_k4, cache_v4, bias, l, seq):
    m = qkv.shape[0]
    batch = m // seq
    rows = seq // GRID_W
    kh = min(MAX_KH, rows)
    rb = min(LAT_ROW_BLOCK, rows)
    kr = min(rows, kh + rb)
    assert rows % rb == 0 and kr % 2 == 0
    past = cache_k4.shape[2]
    qspec = pl.BlockSpec((rb * GRID_W, ATT_W), lambda b, g: (b * (rows // rb) + g, 0))
    kspec = pl.BlockSpec((seq, ATT_W), lambda b, g: (b, 1))
    vspec = pl.BlockSpec((seq, ATT_W), lambda b, g: (b, 2))
    cspec = pl.BlockSpec((None, None, past, ATT_W), lambda b, g: (b, l, 0, 0))
    return pl.pallas_call(
        functools.partial(_lat_attn_kernel, rows=rows, kh=kh, rb=rb, kr=kr),
        grid=(batch, rows // rb),
        in_specs=[qspec, kspec, vspec, cspec, cspec,
                  pl.BlockSpec((None,) + bias.shape[1:], lambda b, g: (l, 0, 0, 0, 0)),
                  pl.BlockSpec((rb * GRID_W, ATT_W),
                               lambda b, g: (b * (rows // rb) + g, _REST_GA // ATT_W))],
        out_specs=qspec,
        out_shape=jax.ShapeDtypeStruct((m, ATT_W), BF16),
        scratch_shapes=[pltpu.VMEM((past, ATT_W), BF16), pltpu.VMEM((past, ATT_W), BF16)],
        compiler_params=_params(("arbitrary", "arbitrary"), 48), name="lat_attn",
    )(qkv, qkv, qkv, cache_k4, cache_v4, bias, rest)


def _cos_sin_table(r, ncols, period, s_before_last=False):
    def factor(v):
        ang = ((r[..., None] * v) % period).astype(F32) * (2.0 * math.pi / period)
        return jnp.cos(ang), jnp.sin(ang)

    ca, sa = factor(jnp.arange(ncols // TRIG_BLOCK, dtype=jnp.int32) * TRIG_BLOCK)
    cb, sb = factor(jnp.arange(TRIG_BLOCK, dtype=jnp.int32))
    if s_before_last:
        ca, sa, cb, sb = (jnp.swapaxes(t, -1, -2) for t in (ca, sa, cb, sb))
        ca, sa, cb, sb = ca[..., :, None, :], sa[..., :, None, :], cb[..., None, :, :], sb[..., None, :, :]
        shape = r.shape[:-1] + (ncols, r.shape[-1])
    else:
        ca, sa, cb, sb = ca[..., :, None], sa[..., :, None], cb[..., None, :], sb[..., None, :]
        shape = r.shape + (ncols,)
    return (ca * cb - sa * sb).reshape(shape), (sa * cb + ca * sb).reshape(shape)


def _fnet_tables(seq):
    cl, sl = _cos_sin_table(jnp.arange(seq, dtype=jnp.int32), seq, seq)
    cc, sc = _cos_sin_table(jnp.arange(FNET_GDIM, dtype=jnp.int32), FNET_GDIM, FNET_GDIM)
    return (jnp.concatenate([cl, -sl], axis=1).astype(BF16), cc.astype(BF16), sc.astype(BF16))


def _fnet_kernel(u_ref, gb_ref, tl_ref, cc_ref, sc_ref, o_ref, t_ref, *, seq, tl):
    i = pl.program_id(1)

    @pl.when(i == 0)
    def _():
        for g in range(FNET_GROUPS):
            sl = slice(g * FNET_GDIM, (g + 1) * FNET_GDIM)
            ug = u_ref[:, sl].astype(BF16)
            t_ref[0:seq, sl] = _dot(ug, cc_ref[...]).astype(BF16)
            t_ref[seq:2 * seq, sl] = _dot(ug, sc_ref[...]).astype(BF16)

    y = _dot(tl_ref[...], t_ref[...]) * ((seq * FNET_GDIM) ** -0.5)
    o_ref[...] = (y * _silu(gb_ref[...])).astype(BF16)


def _fnet(rest, tables, seq):
    m = rest.shape[0]
    tab_l, cc, sc = tables
    tl = min(512, seq)
    nt = seq // tl
    return pl.pallas_call(
        functools.partial(_fnet_kernel, seq=seq, tl=tl),
        grid=(m // seq, nt),
        in_specs=[
            pl.BlockSpec((seq, FNET_W), lambda b, i: (b, _REST_UB // FNET_W)),
            pl.BlockSpec((tl, FNET_W), lambda b, i: (b * nt + i, _REST_GB // FNET_W)),
            pl.BlockSpec((tl, 2 * seq), lambda b, i: (i, 0)),
            pl.BlockSpec((FNET_GDIM, FNET_GDIM), lambda b, i: (0, 0)),
            pl.BlockSpec((FNET_GDIM, FNET_GDIM), lambda b, i: (0, 0)),
        ],
        out_specs=pl.BlockSpec((tl, FNET_W), lambda b, i: (b * nt + i, 0)),
        out_shape=jax.ShapeDtypeStruct((m, FNET_W), BF16),
        scratch_shapes=[pltpu.VMEM((2 * seq, FNET_W), BF16)],
        compiler_params=_params(("arbitrary", "arbitrary"), 40), name="fnet",
    )(rest, rest, tab_l, cc, sc)


def _hyena_tiling(seq):
    if seq > 1024:
        return min(FREQ_CHUNK // 2, seq), HY_W // 2, 2
    return min(FREQ_CHUNK, seq), HY_W, 4


def _hyena_tables(seq):
    fc = _hyena_tiling(seq)[0]
    nch = seq // fc
    k = jnp.arange(2 * fc, dtype=jnp.int32)
    f = jnp.arange(nch, dtype=jnp.int32)[:, None] * fc + (k % fc)[None, :]
    is_sec = (k >= fc)[None, :, None]
    nyq = is_sec & (f == 0)[:, :, None]
    sgn = (1 - 2 * (jnp.arange(seq, dtype=jnp.int32) % 2)).astype(F32)

    def table(transposed):
        c, s = _cos_sin_table(f, seq, 2 * seq, s_before_last=transposed)
        axes = (0, 2, 1) if transposed else (0, 1, 2)
        sec = jnp.where(nyq.transpose(axes), sgn.reshape((1, 1, seq)).transpose(axes), -s)
        return jnp.where(is_sec.transpose(axes), sec, c).astype(BF16)

    return table(False), table(True)


def _filter_features(seq):
    t = jnp.linspace(0.0, 1.0, seq, dtype=F32)[:, None]
    bands = (FILTER_EMB - 1) // 2
    w = (2.0 * math.pi / seq) * jnp.arange(seq, dtype=F32)[:, None]
    f = jnp.linspace(1e-4, bands - 1, bands, dtype=F32)[None, :]
    z = jnp.concatenate([t, jnp.cos(w * f), -jnp.sin(w * f)], axis=-1)
    z = jnp.pad(z, ((0, 0), (0, 128 - FILTER_EMB)))
    deltas = jnp.abs(jnp.linspace(MIN_DECAY, MAX_DECAY, HY_W, dtype=F32))[None, :]
    return z, t, deltas


def _dot_f32(a, b):
    return jnp.dot(a, b, preferred_element_type=F32, precision=lax.Precision.HIGHEST)


def _filter_kernel(z_ref, t_ref, dl_ref, w1_ref, b1_ref, fr_ref, w2_ref, b2_ref, w3_ref,
                   fwd_ref, o_ref, h_ref, sd_ref, nyq_ref, *, seq, fc, rows):
    j = pl.program_id(0)
    nblk = 2 * HYENA_ORDER
    nrc = seq // rows

    @pl.when(j == 0)
    def _():
        fr = fr_ref[...]

        def taps(r, acc):
            sl = pl.ds(pl.multiple_of(r * rows, rows), rows)
            h = jnp.sin(fr * (_dot_f32(z_ref[sl, :], w1_ref[...]) + b1_ref[...]))
            h = jnp.sin(fr * (_dot_f32(h, w2_ref[...]) + b2_ref[...]))
            h = _dot_f32(h, w3_ref[...])
            decay = jnp.exp(-t_ref[sl, :] * dl_ref[...])
            h = h * jnp.concatenate([decay] * nblk, axis=-1)
            h_ref[sl, :] = h
            return acc + jnp.sum(jnp.abs(h), axis=0, keepdims=True)

        tot = lax.fori_loop(0, nrc, taps, jnp.zeros((1, nblk * HY_W), F32))
        inv = 1.0 / (tot + EPS)
        sgn = (1 - 2 * (lax.broadcasted_iota(jnp.int32, (rows, 1), 0) % 2)).astype(F32)

        def fold(r, acc):
            sl = pl.ds(pl.multiple_of(r * rows, rows), rows)
            h = h_ref[sl, :] * inv
            pos = r * rows + lax.broadcasted_iota(jnp.int32, (rows, 1), 0)
            sums = []
            for o in range(HYENA_ORDER):
                fw = h[:, (2 * o) * HY_W:(2 * o + 1) * HY_W]
                bw = jnp.where(pos == 0, 0.0, h[:, (2 * o + 1) * HY_W:(2 * o + 2) * HY_W])
                sums.append(fw + bw)
                sd_ref[0, sl, o * HY_W:(o + 1) * HY_W] = (fw + bw).astype(BF16)
                sd_ref[1, sl, o * HY_W:(o + 1) * HY_W] = (fw - bw).astype(BF16)
            return acc + jnp.sum(jnp.concatenate(sums, axis=-1) * sgn, axis=0, keepdims=True)

        nyq_ref[...] = lax.fori_loop(0, nrc, fold, jnp.zeros((1, HYENA_ORDER * HY_W), F32))

    re = _dot(fwd_ref[0:fc, :], sd_ref[0])
    sec = _dot(fwd_ref[fc:2 * fc, :], sd_ref[1])
    first = (lax.broadcasted_iota(jnp.int32, (fc, 1), 0) == 0) & (j == 0)
    sec = jnp.where(first, nyq_ref[...], sec)
    weight = jnp.where(first, 0.5 / seq, 1.0 / seq)
    o_ref[0:fc, :] = re * weight
    o_ref[fc:2 * fc, :] = sec * weight


def _filter(feats, fwd, w1p, b1, fr, w2, b2, w3, l, seq):
    z, t, deltas = feats
    nch, fc2, _ = fwd.shape
    fc = fc2 // 2
    width = HYENA_ORDER * HY_W
    rows = min(256, seq)
    full = lambda a: pl.BlockSpec(a.shape, lambda j: (0,) * a.ndim)
    lay = lambda a: pl.BlockSpec((None,) + a.shape[1:], lambda j: (l,) + (0,) * (a.ndim - 1))
    return pl.pallas_call(
        functools.partial(_filter_kernel, seq=seq, fc=fc, rows=rows),
        grid=(nch,),
        in_specs=[full(z), full(t), full(deltas), lay(w1p), lay(b1), lay(fr), lay(w2), lay(b2), lay(w3),
                  pl.BlockSpec((None, 2 * fc, seq), lambda j: (j, 0, 0))],
        out_specs=pl.BlockSpec((None, 2 * fc, width), lambda j: (j, 0, 0)),
        out_shape=jax.ShapeDtypeStruct((nch, 2 * fc, width), F32),
        scratch_shapes=[pltpu.VMEM((seq, 2 * width), F32),
                        pltpu.VMEM((2, seq, width), BF16),
                        pltpu.VMEM((1, width), F32)],
        compiler_params=_params(("arbitrary",), 48), name="hyena_filter",
    )(z, t, deltas, w1p, b1, fr, w2, b2, w3, fwd)


def _hyena_kernel(v_ref, x1_ref, x2_ref, gc_ref, cw_ref, cb_ref, hb_ref, fwd_ref, inv_ref, kf_ref,
                  o_ref, z_ref, zb_ref, y_ref, *, seq, fc, sb, tc):
    o = pl.program_id(2)
    j = pl.program_id(3)
    last_j = pl.num_programs(3) - 1
    pos = lax.broadcasted_iota(jnp.int32, (seq, 1), 0)

    def short_conv(ref, k, s):
        x = ref[s * seq:(s + 1) * seq, :]
        prev = jnp.where(pos == 0, 0.0, pltpu.roll(x, 1, 0))
        nxt = jnp.where(pos == seq - 1, 0.0, pltpu.roll(x, seq - 1, 0))
        return (prev * cw_ref[0, k:k + 1, :] + x * cw_ref[1, k:k + 1, :]
                + nxt * cw_ref[2, k:k + 1, :] + cb_ref[k:k + 1, :])

    @pl.when((o == 0) & (j == 0))
    def _():
        for s in range(sb):
            lanes = slice(s * tc, (s + 1) * tc)
            z = short_conv(v_ref, 0, s)
            z_ref[:, lanes] = z
            zb_ref[:, lanes] = z.astype(BF16)
        y_ref[...] = jnp.zeros_like(y_ref)

    @pl.when((o == 1) & (j == 0))
    def _():
        for s in range(sb):
            lanes = slice(s * tc, (s + 1) * tc)
            z = short_conv(x1_ref, 1, s) * (y_ref[:, lanes] + hb_ref[0:1, :] * z_ref[:, lanes])
            z_ref[:, lanes] = z
            zb_ref[:, lanes] = z.astype(BF16)
        y_ref[...] = jnp.zeros_like(y_ref)

    zf = _dot(fwd_ref[...], zb_ref[...])
    kr, ks = kf_ref[0:fc, :], kf_ref[fc:2 * fc, :]
    nyq = (lax.broadcasted_iota(jnp.int32, (fc, 1), 0) == 0) & (j == 0)
    parts = []
    for s in range(sb):
        lanes = slice(s * tc, (s + 1) * tc)
        zr, zs = zf[0:fc, lanes], zf[fc:2 * fc, lanes]
        ss = zs * ks
        yr = zr * kr - jnp.where(nyq, 0.0, ss)
        ys = jnp.where(nyq, ss, zr * ks + zs * kr)
        parts.append(jnp.concatenate([yr, ys], axis=0).astype(BF16))
    y_ref[...] += _dot(inv_ref[...], jnp.concatenate(parts, axis=-1))

    @pl.when((o == 1) & (j == last_j))
    def _():
        for s in range(sb):
            lanes = slice(s * tc, (s + 1) * tc)
            z = short_conv(x2_ref, 2, s) * (y_ref[:, lanes] + hb_ref[1:2, :] * z_ref[:, lanes])
            o_ref[s * seq:(s + 1) * seq, :] = (z * _silu(gc_ref[s * seq:(s + 1) * seq, :])).astype(BF16)


def _hyena(rest, cw, cb, hb, fwd, inv, kf, seq):
    m = rest.shape[0]
    nch, fc2, _ = fwd.shape
    fc = fc2 // 2
    _, tc, sb = _hyena_tiling(seq)
    sb = min(sb, m // seq)
    nct = HY_W // tc
    hc0 = _REST_HC // tc
    once = pl.Buffered(1)
    col = lambda k: pl.BlockSpec((sb * seq, tc), lambda b, c, o, j: (b, hc0 + k * nct + c),
                                 pipeline_mode=once)
    return pl.pallas_call(
        functools.partial(_hyena_kernel, seq=seq, fc=fc, sb=sb, tc=tc),
        grid=(m // (sb * seq), nct, HYENA_ORDER, nch),
        in_specs=[
            col(0), col(1), col(2),
            pl.BlockSpec((sb * seq, tc), lambda b, c, o, j: (b, _REST_GC // tc + c), pipeline_mode=once),
            pl.BlockSpec((3, 3, tc), lambda b, c, o, j: (0, 0, c)),
            pl.BlockSpec((3, tc), lambda b, c, o, j: (0, c)),
            pl.BlockSpec((HYENA_ORDER, tc), lambda b, c, o, j: (0, c)),
            pl.BlockSpec((None, 2 * fc, seq), lambda b, c, o, j: (j, 0, 0)),
            pl.BlockSpec((None, seq, 2 * fc), lambda b, c, o, j: (j, 0, 0)),
            pl.BlockSpec((None, 2 * fc, tc), lambda b, c, o, j: (j, 0, o * nct + c)),
        ],
        out_specs=pl.BlockSpec((sb * seq, tc), lambda b, c, o, j: (b, c)),
        out_shape=jax.ShapeDtypeStruct((m, HY_W), BF16),
        scratch_shapes=[pltpu.VMEM((seq, sb * tc), F32), pltpu.VMEM((seq, sb * tc), BF16),
                        pltpu.VMEM((seq, sb * tc), F32)],
        compiler_params=_params(("arbitrary",) * 4, 56), name="hyena",
    )(rest, rest, rest, rest, cw, cb, hb, fwd, inv, kf)


def _merge_kernel(ya_ref, yb_ref, yc_ref, ga_ref, gb_ref, gc_ref, wa_ref, wb_ref, wc_ref, o_ref):
    acc = jax.nn.sigmoid(ga_ref[...]) * _dot(ya_ref[...], wa_ref[...])
    acc += jax.nn.sigmoid(gb_ref[...]) * _dot(yb_ref[...], wb_ref[...])
    acc += jax.nn.sigmoid(gc_ref[...]) * _dot(yc_ref[...], wc_ref[...])
    o_ref[...] = acc.astype(BF16)


def _merge(ya, yb, yc, rest, w_br_bf, l, seq):
    m = ya.shape[0]
    tm = min(ROW_TILE, m)
    tn = 512
    d = D_MODEL
    row = lambda w: pl.BlockSpec((tm, w), lambda i, j: (i, 0))
    gate = lambda k: pl.BlockSpec((tm, tn), lambda i, j: (i, (_REST_MG + k * d) // tn + j))
    return pl.pallas_call(
        _merge_kernel, grid=(m // tm, d // tn),
        in_specs=[row(ATT_W), row(FNET_W), row(HY_W), gate(0), gate(1), gate(2),
                  pl.BlockSpec((None, ATT_W, tn), lambda i, j: (l, 0, j)),
                  pl.BlockSpec((None, FNET_W, tn), lambda i, j: (l, ATT_W // FNET_W, j)),
                  pl.BlockSpec((None, HY_W, tn), lambda i, j: (l, (ATT_W + FNET_W) // HY_W, j))],
        out_specs=pl.BlockSpec((tm, tn), lambda i, j: (i, j)),
        out_shape=jax.ShapeDtypeStruct((m, d), BF16),
        compiler_params=_params(("arbitrary", "arbitrary"), 40), name="merge",
    )(ya, yb, yc, rest, rest, rest, w_br_bf, w_br_bf, w_br_bf)


def _outproj_kernel(mg_ref, w_ref, x_ref, ada_ref, o_ref):
    o_ref[...] = x_ref[...] + ada_ref[...] * _dot(mg_ref[...], w_ref[...])


def _outproj(merged, x2d, ada4, ada_row, w_out_bf, l, seq):
    m, d = x2d.shape
    tm = min(ROW_TILE, m)
    tn = 512
    return pl.pallas_call(
        _outproj_kernel, grid=(m // tm, d // tn),
        in_specs=[pl.BlockSpec((tm, d), lambda i, j: (i, 0)),
                  pl.BlockSpec((None, d, tn), lambda i, j: (l, 0, j)),
                  pl.BlockSpec((tm, tn), lambda i, j: (i, j)),
                  pl.BlockSpec((None, None, 1, tn),
                               lambda i, j: (l, ada_row(i * tm), 0, 2 * d // tn + j))],
        out_specs=pl.BlockSpec((tm, tn), lambda i, j: (i, j)),
        out_shape=jax.ShapeDtypeStruct((m, d), F32),
        compiler_params=_params(("arbitrary", "arbitrary"), 40), name="outproj",
    )(merged, w_out_bf, x2d, ada4)


def kernel(x_prompt, x_sample, cache_k, cache_v, c, c_ctx, norm_g, w_ada, b_ada, w_in, q_norm_g, k_norm_g, rpb, conv_w, conv_b, f_w1, f_b1, f_freq, f_w2, f_b2, f_w3, hy_bias, w_br, w_out):
    batch, seq, d = x_prompt.shape
    dec_batch, dec_seq, _ = x_sample.shape
    depth = norm_g.shape[0]
    past = cache_k.shape[2]
    assert dec_batch + 1 <= ADA_ROWS

    cv = jnp.zeros((ADA_ROWS, d), F32).at[0].set(c_ctx).at[1:1 + dec_batch].set(c)
    ada4 = _ada(cv, w_ada, b_ada).reshape(depth, ADA_ROWS, 1, 3 * d)
    ctx_row = lambda tok: 0
    lat_row = lambda tok: 1 + tok // dec_seq

    w_in_bf = w_in.astype(BF16)
    w_br_bf = w_br.astype(BF16)
    w_out_bf = w_out.astype(BF16)
    norm_g3 = norm_g.reshape(depth, 1, d)
    qg3 = q_norm_g.reshape(depth, 1, HEAD_DIM)
    kg3 = k_norm_g.reshape(depth, 1, HEAD_DIM)
    cache_k4 = cache_k.reshape(dec_batch, depth, past, ATT_W)
    cache_v4 = cache_v.reshape(dec_batch, depth, past, ATT_W)
    w1p = jnp.pad(f_w1, ((0, 0), (0, 128 - FILTER_EMB), (0, 0)))
    b1 = f_b1.reshape(depth, 1, FILTER_FF)
    fr = f_freq.reshape(depth, 1, FILTER_FF)
    b2 = f_b2.reshape(depth, 1, FILTER_FF)
    win_bias = _window_bias(rpb)

    groups = []
    for x, s, row in ((x_prompt, seq, ctx_row), (x_sample, dec_seq, lat_row)):
        fwd, inv = _hyena_tables(s)
        groups.append(dict(x=x.reshape(-1, d), seq=s, row=row, fnet=_fnet_tables(s),
                           fwd=fwd, inv=inv, feats=_filter_features(s)))

    new_kv = tuple(jnp.zeros((batch, depth, seq, ATT_W), F32) for _ in range(2))
    for l in range(depth):
        cw = conv_w[l].reshape(3, 3, HY_W)
        cb = conv_b[l].reshape(3, HY_W)
        hb = hy_bias[l]
        for gi, g in enumerate(groups):
            s = g["seq"]
            if gi == 0:
                qkv, rest, *new_kv = _inproj(g["x"], norm_g3, ada4, g["row"], w_in_bf, qg3, kg3, l, s, new_kv)
                ya = _ctx_attn(qkv, rest, s)
            else:
                qkv, rest = _inproj(g["x"], norm_g3, ada4, g["row"], w_in_bf, qg3, kg3, l, s, None)
                ya = _lat_attn(qkv, rest, cache_k4, cache_v4, win_bias, l, s)
            yb = _fnet(rest, g["fnet"], s)
            kf = _filter(g["feats"], g["fwd"], w1p, b1, fr, f_w2, b2, f_w3, l, s)
            yc = _hyena(rest, cw, cb, hb, g["fwd"], g["inv"], kf, s)
            merged = _merge(ya, yb, yc, rest, w_br_bf, l, s)
            g["x"] = _outproj(merged, g["x"], ada4, g["row"], w_out_bf, l, s)

    nk, nv = new_kv
    return (groups[0]["x"].reshape(batch, seq, d),
            groups[1]["x"].reshape(dec_batch, dec_seq, d),
            nk.reshape(batch, depth, seq, N_HEADS, HEAD_DIM),
            nv.reshape(batch, depth, seq, N_HEADS, HEAD_DIM))
```

```python
import functools
import math

import jax
import jax.numpy as jnp
import numpy as np
from jax import lax
from jax.experimental import pallas as pl
from jax.experimental.pallas import tpu as pltpu

F32 = jnp.float32
BF16 = jnp.bfloat16

D_MODEL = 2048
GRID_W = 64
N_HEADS = 8
HEAD_DIM = 128
ATT_W = N_HEADS * HEAD_DIM
MAX_KH = 8
KW = 16
FNET_GROUPS = 4
FNET_GDIM = 128
FNET_W = FNET_GROUPS * FNET_GDIM
HY_W = 512
HYENA_ORDER = 2
FILTER_EMB = 33
FILTER_FF = 64
MIN_DECAY = math.log(1e-2) / 1.5
MAX_DECAY = math.log(1e-2) / 0.3
OFF_Q = 0
OFF_K = OFF_Q + ATT_W
OFF_V = OFF_K + ATT_W
OFF_GA = OFF_V + ATT_W
OFF_UB = OFF_GA + ATT_W
OFF_GB = OFF_UB + FNET_W
OFF_HC = OFF_GB + FNET_W
OFF_GC = OFF_HC + 3 * HY_W
OFF_MG = OFF_GC + HY_W
N_IN = OFF_MG + 3 * D_MODEL
EPS = 1e-6
NEG = -1e30
ATT_SCALE = HEAD_DIM ** -0.5

ADA_ROWS = 8
FREQ_CHUNK = 512
TRIG_BLOCK = 64
LAT_ROW_BLOCK = 4
ROW_TILE = 1024
COL_TILE = 1024
MIB = 1024 * 1024


def _params(semantics, vmem_mib):
    return pltpu.CompilerParams(dimension_semantics=semantics,
                                vmem_limit_bytes=vmem_mib * MIB)


def _silu(x):
    return x * jax.nn.sigmoid(x)


def _dot(a, b):
    return jnp.dot(a, b, preferred_element_type=F32)


def _dot_nt(a, b):
    return lax.dot_general(a, b, (((1,), (1,)), ((), ())), preferred_element_type=F32)


def _ada_kernel(cv_ref, w_ref, b_ref, o_ref):
    s = _silu(cv_ref[...]).astype(BF16)
    o_ref[...] = _dot(s, w_ref[...].astype(BF16)) + b_ref[...]


def _ada(cv, w_ada, b_ada):
    depth, d, n = w_ada.shape
    tn = 512
    return pl.pallas_call(
        _ada_kernel,
        grid=(depth, n // tn),
        in_specs=[
            pl.BlockSpec((ADA_ROWS, d), lambda l, j: (0, 0)),
            pl.BlockSpec((None, d, tn), lambda l, j: (l, 0, j)),
            pl.BlockSpec((None, 1, tn), lambda l, j: (l, 0, j)),
        ],
        out_specs=pl.BlockSpec((None, ADA_ROWS, tn), lambda l, j: (l, 0, j)),
        out_shape=jax.ShapeDtypeStruct((depth, ADA_ROWS, n), F32),
        compiler_params=_params(("arbitrary", "arbitrary"), 32),
        name="ada",
    )(cv, w_ada, b_ada.reshape(depth, 1, n))


def _inproj_kernel(x_ref, g_ref, ada_ref, w_ref, o_ref, gate_ref, h_ref, *, rows, n_plain):
    j = pl.program_id(1)

    @pl.when(j == 0)
    def _():
        g = g_ref[...]
        shift = ada_ref[:, 0:D_MODEL]
        scale1 = 1.0 + ada_ref[:, D_MODEL:2 * D_MODEL]

        def body(r, carry):
            sl = pl.ds(pl.multiple_of(r * rows, rows), rows)
            x = x_ref[sl, :]
            ms = jnp.mean(x * x, axis=-1, keepdims=True)
            y = x * lax.rsqrt(ms + EPS) * g
            h_ref[sl, :] = (y * scale1 + shift).astype(BF16)
            return carry

        lax.fori_loop(0, x_ref.shape[0] // rows, body, 0)

    @pl.when(j < n_plain)
    def _():
        o_ref[...] = _dot(h_ref[...], w_ref[...])

    @pl.when(j >= n_plain)
    def _():
        gate_ref[...] = jax.nn.sigmoid(_dot(h_ref[...], w_ref[...])).astype(BF16)


def _inproj(x2d, norm_g3, ada4, ada_row, w_in_bf, l, seq):
    m, d = x2d.shape
    tm = min(ROW_TILE, m)
    tn = COL_TILE
    n_plain = OFF_MG // tn
    return pl.pallas_call(
        functools.partial(_inproj_kernel, rows=128, n_plain=n_plain),
        grid=(m // tm, N_IN // tn),
        in_specs=[
            pl.BlockSpec((tm, d), lambda i, j: (i, 0)),
            pl.BlockSpec((None, 1, d), lambda i, j: (l, 0, 0)),
            pl.BlockSpec((None, None, 1, 3 * d), lambda i, j: (l, ada_row(i * tm), 0, 0)),
            pl.BlockSpec((None, d, tn), lambda i, j: (l, 0, j)),
        ],
        out_specs=[pl.BlockSpec((tm, tn), lambda i, j: (i, jnp.minimum(j, n_plain - 1))),
                   pl.BlockSpec((tm, tn), lambda i, j: (i, jnp.maximum(j - n_plain, 0)))],
        out_shape=[jax.ShapeDtypeStruct((m, OFF_MG), F32),
                   jax.ShapeDtypeStruct((m, N_IN - OFF_MG), BF16)],
        scratch_shapes=[pltpu.VMEM((tm, d), BF16)],
        compiler_params=_params(("arbitrary", "arbitrary"), 48),
        name="inproj",
    )(x2d, norm_g3, ada4, w_in_bf)


def _head_norm(x, g):
    ms = jnp.mean(x * x, axis=-1, keepdims=True)
    return x * lax.rsqrt(ms + EPS) * g


def _prep_lat_kernel(q_ref, k_ref, v_ref, qg_ref, kg_ref, qn_ref, kn_ref, vb_ref):
    for h in range(N_HEADS):
        sl = slice(h * HEAD_DIM, (h + 1) * HEAD_DIM)
        qn_ref[:, sl] = _head_norm(q_ref[:, sl], qg_ref[...]).astype(BF16)
        kn_ref[:, sl] = _head_norm(k_ref[:, sl], kg_ref[...]).astype(BF16)
    vb_ref[...] = v_ref[...].astype(BF16)


def _prep_lat(proj, qg3, kg3, l):
    m = proj.shape[0]
    tm = min(256, m)
    col = lambda c: pl.BlockSpec((tm, ATT_W), lambda i: (i, c))
    gspec = pl.BlockSpec((None, 1, HEAD_DIM), lambda i: (l, 0, 0))
    return pl.pallas_call(
        _prep_lat_kernel, grid=(m // tm,),
        in_specs=[col(OFF_Q // ATT_W), col(OFF_K // ATT_W), col(OFF_V // ATT_W), gspec, gspec],
        out_specs=[col(0)] * 3, out_shape=[jax.ShapeDtypeStruct((m, ATT_W), BF16)] * 3,
        compiler_params=_params(("arbitrary",), 32), name="prep_lat",
    )(proj, proj, proj, qg3, kg3)


def _ctx_attn_kernel(q_ref, k_ref, v_ref, ga_ref, qg_ref, kg_ref, nk_in, nv_in, o_ref, nk_ref, nv_ref):
    del nk_in, nv_in
    for h in range(N_HEADS):
        sl = slice(h * HEAD_DIM, (h + 1) * HEAD_DIM)
        q = _head_norm(q_ref[:, sl], qg_ref[...])
        k = _head_norm(k_ref[:, sl], kg_ref[...])
        v = v_ref[:, sl]
        nk_ref[:, sl] = k
        nv_ref[:, sl] = v
        s = _dot_nt(q.astype(BF16), k.astype(BF16)) * ATT_SCALE
        p = jnp.exp(s - jnp.max(s, axis=-1, keepdims=True))
        inv = 1.0 / jnp.sum(p, axis=-1, keepdims=True)
        o = _dot(p.astype(BF16), v.astype(BF16)) * inv
        o_ref[:, sl] = (o * _silu(ga_ref[:, sl])).astype(BF16)


def _ctx_attn(proj, qg3, kg3, new_kv, l, seq):
    m = proj.shape[0]
    col = lambda c: pl.BlockSpec((seq, ATT_W), lambda b: (b, c))
    gspec = pl.BlockSpec((None, 1, HEAD_DIM), lambda b: (l, 0, 0))
    kv_spec = pl.BlockSpec((None, None, seq, ATT_W), lambda b: (b, l, 0, 0))
    return pl.pallas_call(
        _ctx_attn_kernel, grid=(m // seq,),
        in_specs=[col(OFF_Q // ATT_W), col(OFF_K // ATT_W), col(OFF_V // ATT_W), col(OFF_GA // ATT_W),
                  gspec, gspec] + [pl.BlockSpec(memory_space=pl.ANY)] * 2,
        out_specs=[col(0), kv_spec, kv_spec],
        out_shape=[jax.ShapeDtypeStruct((m, ATT_W), BF16)]
        + [jax.ShapeDtypeStruct(a.shape, a.dtype) for a in new_kv],
        input_output_aliases={6: 1, 7: 2},
        compiler_params=_params(("arbitrary",), 32), name="ctx_attn",
    )(proj, proj, proj, proj, qg3, kg3, *new_kv)


def _window_bias(rpb):
    depth, heads, n_dr, n_dc = rpb.shape
    c = np.arange(GRID_W)[:, None]
    kc = np.arange(GRID_W)[None, :]
    cs = np.clip(c - KW // 2, 0, GRID_W - KW)
    valid = (kc >= cs) & (kc < cs + KW)
    dc = np.clip(kc - c + KW - 1, 0, n_dc - 1).reshape(-1)
    onehot = (jnp.asarray(dc)[None, :] == jnp.arange(n_dc)[:, None]).astype(F32)
    cb = jnp.einsum("lhdm,mx->lhdx", rpb, onehot, precision=lax.Precision.HIGHEST)
    cb = jnp.where(jnp.asarray(valid), cb.reshape(depth, heads, n_dr, GRID_W, GRID_W), NEG)
    cb = jnp.pad(cb, ((0, 0), (0, 0), (1, 1), (0, 0), (0, 0)), constant_values=NEG)
    return jnp.concatenate([cb[:, :, :-1], cb[:, :, 1:]], axis=-1)


def _lat_attn_kernel(q_ref, k_ref, v_ref, ck_ref, cv_ref, bias_ref, ga_ref, o_ref,
                     ckb_ref, cvb_ref, *, rows, kh, rb, kr):
    g = pl.program_id(1)

    @pl.when(g == 0)
    def _():
        ckb_ref[...] = ck_ref[...].astype(BF16)
        cvb_ref[...] = cv_ref[...].astype(BF16)

    r0 = g * rb
    ws = jnp.clip(r0 - kh // 2, 0, rows - kr)
    span = pl.ds(pl.multiple_of(ws * GRID_W, GRID_W), kr * GRID_W)
    q_row = r0 + lax.broadcasted_iota(jnp.int32, (rb * GRID_W, 1), 0) // GRID_W
    k_row = ws + lax.broadcasted_iota(jnp.int32, (1, kr * GRID_W), 1) // GRID_W
    q_rs = jnp.clip(q_row - kh // 2, 0, rows - kh)
    in_window = (k_row >= q_rs) & (k_row < q_rs + kh)
    n_pairs = bias_ref.shape[1]
    for h in range(N_HEADS):
        sl = slice(h * HEAD_DIM, (h + 1) * HEAD_DIM)
        q = q_ref[:, sl]
        bias = jnp.concatenate(
            [jnp.concatenate(
                [bias_ref[h, jnp.clip(ws + 2 * j - (r0 + qi) + MAX_KH, 0, n_pairs - 1)]
                 for j in range(kr // 2)], axis=-1)
             for qi in range(rb)], axis=0)
        s_win = jnp.where(in_window, _dot_nt(q, k_ref[span, sl]) * ATT_SCALE + bias, NEG)
        s_ctx = _dot_nt(q, ckb_ref[:, sl]) * ATT_SCALE
        mx = jnp.maximum(jnp.max(s_win, axis=-1, keepdims=True),
                         jnp.max(s_ctx, axis=-1, keepdims=True))
        p_win = jnp.exp(s_win - mx)
        p_ctx = jnp.exp(s_ctx - mx)
        inv = 1.0 / (jnp.sum(p_win, axis=-1, keepdims=True) + jnp.sum(p_ctx, axis=-1, keepdims=True))
        o = (_dot(p_win.astype(BF16), v_ref[span, sl]) + _dot(p_ctx.astype(BF16), cvb_ref[:, sl])) * inv
        o_ref[:, sl] = (o * _silu(ga_ref[:, sl])).astype(BF16)


def _lat_attn(qn, kn, vb, cache_k4, cache_v4, bias, proj, l, seq):
    m = qn.shape[0]
    batch = m // seq
    rows = seq // GRID_W
    kh = min(MAX_KH, rows)
    rb = min(LAT_ROW_BLOCK, rows)
    kr = min(rows, kh + rb)
    assert rows % rb == 0 and kr % 2 == 0
    past = cache_k4.shape[2]
    qspec = pl.BlockSpec((rb * GRID_W, ATT_W), lambda b, g: (b * (rows // rb) + g, 0))
    kvspec = pl.BlockSpec((seq, ATT_W), lambda b, g: (b, 0))
    cspec = pl.BlockSpec((None, None, past, ATT_W), lambda b, g: (b, l, 0, 0))
    return pl.pallas_call(
        functools.partial(_lat_attn_kernel, rows=rows, kh=kh, rb=rb, kr=kr),
        grid=(batch, rows // rb),
        in_specs=[qspec, kvspec, kvspec, cspec, cspec,
                  pl.BlockSpec((None,) + bias.shape[1:], lambda b, g: (l, 0, 0, 0, 0)),
                  pl.BlockSpec((rb * GRID_W, ATT_W),
                               lambda b, g: (b * (rows // rb) + g, OFF_GA // ATT_W))],
        out_specs=qspec,
        out_shape=jax.ShapeDtypeStruct((m, ATT_W), BF16),
        scratch_shapes=[pltpu.VMEM((past, ATT_W), BF16), pltpu.VMEM((past, ATT_W), BF16)],
        compiler_params=_params(("arbitrary", "arbitrary"), 48), name="lat_attn",
    )(qn, kn, vb, cache_k4, cache_v4, bias, proj)


def _cos_sin_table(r, ncols, period, s_before_last=False):
    def factor(v):
        ang = ((r[..., None] * v) % period).astype(F32) * (2.0 * math.pi / period)
        return jnp.cos(ang), jnp.sin(ang)

    ca, sa = factor(jnp.arange(ncols // TRIG_BLOCK, dtype=jnp.int32) * TRIG_BLOCK)
    cb, sb = factor(jnp.arange(TRIG_BLOCK, dtype=jnp.int32))
    if s_before_last:
        ca, sa, cb, sb = (jnp.swapaxes(t, -1, -2) for t in (ca, sa, cb, sb))
        ca, sa, cb, sb = ca[..., :, None, :], sa[..., :, None, :], cb[..., None, :, :], sb[..., None, :, :]
        shape = r.shape[:-1] + (ncols, r.shape[-1])
    else:
        ca, sa, cb, sb = ca[..., :, None], sa[..., :, None], cb[..., None, :], sb[..., None, :]
        shape = r.shape + (ncols,)
    return (ca * cb - sa * sb).reshape(shape), (sa * cb + ca * sb).reshape(shape)


def _fnet_tables(seq):
    cl, sl = _cos_sin_table(jnp.arange(seq, dtype=jnp.int32), seq, seq)
    cc, sc = _cos_sin_table(jnp.arange(FNET_GDIM, dtype=jnp.int32), FNET_GDIM, FNET_GDIM)
    return (jnp.concatenate([cl, -sl], axis=1).astype(BF16), cc.astype(BF16), sc.astype(BF16))


def _fnet_kernel(u_ref, gb_ref, tl_ref, cc_ref, sc_ref, o_ref, t_ref, *, seq, tl):
    i = pl.program_id(1)

    @pl.when(i == 0)
    def _():
        for g in range(FNET_GROUPS):
            sl = slice(g * FNET_GDIM, (g + 1) * FNET_GDIM)
            ug = u_ref[:, sl].astype(BF16)
            t_ref[0:seq, sl] = _dot(ug, cc_ref[...]).astype(BF16)
            t_ref[seq:2 * seq, sl] = _dot(ug, sc_ref[...]).astype(BF16)

    y = _dot(tl_ref[...], t_ref[...]) * ((seq * FNET_GDIM) ** -0.5)
    o_ref[...] = (y * _silu(gb_ref[...])).astype(BF16)


def _fnet(proj, tables, seq):
    m = proj.shape[0]
    tab_l, cc, sc = tables
    tl = min(512, seq)
    nt = seq // tl
    return pl.pallas_call(
        functools.partial(_fnet_kernel, seq=seq, tl=tl),
        grid=(m // seq, nt),
        in_specs=[
            pl.BlockSpec((seq, FNET_W), lambda b, i: (b, OFF_UB // FNET_W)),
            pl.BlockSpec((tl, FNET_W), lambda b, i: (b * nt + i, OFF_GB // FNET_W)),
            pl.BlockSpec((tl, 2 * seq), lambda b, i: (i, 0)),
            pl.BlockSpec((FNET_GDIM, FNET_GDIM), lambda b, i: (0, 0)),
            pl.BlockSpec((FNET_GDIM, FNET_GDIM), lambda b, i: (0, 0)),
        ],
        out_specs=pl.BlockSpec((tl, FNET_W), lambda b, i: (b * nt + i, 0)),
        out_shape=jax.ShapeDtypeStruct((m, FNET_W), BF16),
        scratch_shapes=[pltpu.VMEM((2 * seq, FNET_W), BF16)],
        compiler_params=_params(("arbitrary", "arbitrary"), 40), name="fnet",
    )(proj, proj, tab_l, cc, sc)


def _hyena_tiling(seq):
    if seq > 1024:
        return min(FREQ_CHUNK // 2, seq), HY_W // 2, 2
    return min(FREQ_CHUNK, seq), HY_W, 4


def _hyena_tables(seq):
    fc = _hyena_tiling(seq)[0]
    nch = seq // fc
    k = jnp.arange(2 * fc, dtype=jnp.int32)
    f = jnp.arange(nch, dtype=jnp.int32)[:, None] * fc + (k % fc)[None, :]
    is_sec = (k >= fc)[None, :, None]
    nyq = is_sec & (f == 0)[:, :, None]
    sgn = (1 - 2 * (jnp.arange(seq, dtype=jnp.int32) % 2)).astype(F32)

    def table(transposed):
        c, s = _cos_sin_table(f, seq, 2 * seq, s_before_last=transposed)
        axes = (0, 2, 1) if transposed else (0, 1, 2)
        sec = jnp.where(nyq.transpose(axes), sgn.reshape((1, 1, seq)).transpose(axes), -s)
        return jnp.where(is_sec.transpose(axes), sec, c).astype(BF16)

    return table(False), table(True)


def _filter_features(seq):
    t = jnp.linspace(0.0, 1.0, seq, dtype=F32)[:, None]
    bands = (FILTER_EMB - 1) // 2
    w = (2.0 * math.pi / seq) * jnp.arange(seq, dtype=F32)[:, None]
    f = jnp.linspace(1e-4, bands - 1, bands, dtype=F32)[None, :]
    z = jnp.concatenate([t, jnp.cos(w * f), -jnp.sin(w * f)], axis=-1)
    z = jnp.pad(z, ((0, 0), (0, 128 - FILTER_EMB)))
    deltas = jnp.abs(jnp.linspace(MIN_DECAY, MAX_DECAY, HY_W, dtype=F32))[None, :]
    return z, t, deltas


def _dot_f32(a, b):
    return jnp.dot(a, b, preferred_element_type=F32, precision=lax.Precision.HIGHEST)


def _filter_kernel(z_ref, t_ref, dl_ref, w1_ref, b1_ref, fr_ref, w2_ref, b2_ref, w3_ref,
                   fwd_ref, o_ref, h_ref, sd_ref, nyq_ref, *, seq, fc, rows):
    j = pl.program_id(0)
    nblk = 2 * HYENA_ORDER
    nrc = seq // rows

    @pl.when(j == 0)
    def _():
        fr = fr_ref[...]

        def taps(r, acc):
            sl = pl.ds(pl.multiple_of(r * rows, rows), rows)
            h = jnp.sin(fr * (_dot_f32(z_ref[sl, :], w1_ref[...]) + b1_ref[...]))
            h = jnp.sin(fr * (_dot_f32(h, w2_ref[...]) + b2_ref[...]))
            h = _dot_f32(h, w3_ref[...])
            decay = jnp.exp(-t_ref[sl, :] * dl_ref[...])
            h = h * jnp.concatenate([decay] * nblk, axis=-1)
            h_ref[sl, :] = h
            return acc + jnp.sum(jnp.abs(h), axis=0, keepdims=True)

        tot = lax.fori_loop(0, nrc, taps, jnp.zeros((1, nblk * HY_W), F32))
        inv = 1.0 / (tot + EPS)
        sgn = (1 - 2 * (lax.broadcasted_iota(jnp.int32, (rows, 1), 0) % 2)).astype(F32)

        def fold(r, acc):
            sl = pl.ds(pl.multiple_of(r * rows, rows), rows)
            h = h_ref[sl, :] * inv
            pos = r * rows + lax.broadcasted_iota(jnp.int32, (rows, 1), 0)
            sums = []
            for o in range(HYENA_ORDER):
                fw = h[:, (2 * o) * HY_W:(2 * o + 1) * HY_W]
                bw = jnp.where(pos == 0, 0.0, h[:, (2 * o + 1) * HY_W:(2 * o + 2) * HY_W])
                sums.append(fw + bw)
                sd_ref[0, sl, o * HY_W:(o + 1) * HY_W] = (fw + bw).astype(BF16)
                sd_ref[1, sl, o * HY_W:(o + 1) * HY_W] = (fw - bw).astype(BF16)
            return acc + jnp.sum(jnp.concatenate(sums, axis=-1) * sgn, axis=0, keepdims=True)

        nyq_ref[...] = lax.fori_loop(0, nrc, fold, jnp.zeros((1, HYENA_ORDER * HY_W), F32))

    re = _dot(fwd_ref[0:fc, :], sd_ref[0])
    sec = _dot(fwd_ref[fc:2 * fc, :], sd_ref[1])
    first = (lax.broadcasted_iota(jnp.int32, (fc, 1), 0) == 0) & (j == 0)
    sec = jnp.where(first, nyq_ref[...], sec)
    weight = jnp.where(first, 0.5 / seq, 1.0 / seq)
    o_ref[0:fc, :] = re * weight
    o_ref[fc:2 * fc, :] = sec * weight


def _filter(feats, fwd, w1p, b1, fr, w2, b2, w3, l, seq):
    z, t, deltas = feats
    nch, fc2, _ = fwd.shape
    fc = fc2 // 2
    width = HYENA_ORDER * HY_W
    rows = min(256, seq)
    full = lambda a: pl.BlockSpec(a.shape, lambda j: (0,) * a.ndim)
    lay = lambda a: pl.BlockSpec((None,) + a.shape[1:], lambda j: (l,) + (0,) * (a.ndim - 1))
    return pl.pallas_call(
        functools.partial(_filter_kernel, seq=seq, fc=fc, rows=rows),
        grid=(nch,),
        in_specs=[full(z), full(t), full(deltas), lay(w1p), lay(b1), lay(fr), lay(w2), lay(b2), lay(w3),
                  pl.BlockSpec((None, 2 * fc, seq), lambda j: (j, 0, 0))],
        out_specs=pl.BlockSpec((None, 2 * fc, width), lambda j: (j, 0, 0)),
        out_shape=jax.ShapeDtypeStruct((nch, 2 * fc, width), F32),
        scratch_shapes=[pltpu.VMEM((seq, 2 * width), F32),
                        pltpu.VMEM((2, seq, width), BF16),
                        pltpu.VMEM((1, width), F32)],
        compiler_params=_params(("arbitrary",), 48), name="hyena_filter",
    )(z, t, deltas, w1p, b1, fr, w2, b2, w3, fwd)


def _hyena_kernel(v_ref, x1_ref, x2_ref, gc_ref, cw_ref, cb_ref, hb_ref, fwd_ref, inv_ref, kf_ref,
                  o_ref, z_ref, zb_ref, y_ref, *, seq, fc, sb, tc):
    o = pl.program_id(2)
    j = pl.program_id(3)
    last_j = pl.num_programs(3) - 1
    pos = lax.broadcasted_iota(jnp.int32, (seq, 1), 0)

    def short_conv(ref, k, s):
        x = ref[s * seq:(s + 1) * seq, :]
        prev = jnp.where(pos == 0, 0.0, pltpu.roll(x, 1, 0))
        nxt = jnp.where(pos == seq - 1, 0.0, pltpu.roll(x, seq - 1, 0))
        return (prev * cw_ref[0, k:k + 1, :] + x * cw_ref[1, k:k + 1, :]
                + nxt * cw_ref[2, k:k + 1, :] + cb_ref[k:k + 1, :])

    @pl.when((o == 0) & (j == 0))
    def _():
        for s in range(sb):
            lanes = slice(s * tc, (s + 1) * tc)
            z = short_conv(v_ref, 0, s)
            z_ref[:, lanes] = z
            zb_ref[:, lanes] = z.astype(BF16)
        y_ref[...] = jnp.zeros_like(y_ref)

    @pl.when((o == 1) & (j == 0))
    def _():
        for s in range(sb):
            lanes = slice(s * tc, (s + 1) * tc)
            z = short_conv(x1_ref, 1, s) * (y_ref[:, lanes] + hb_ref[0:1, :] * z_ref[:, lanes])
            z_ref[:, lanes] = z
            zb_ref[:, lanes] = z.astype(BF16)
        y_ref[...] = jnp.zeros_like(y_ref)

    zf = _dot(fwd_ref[...], zb_ref[...])
    kr, ks = kf_ref[0:fc, :], kf_ref[fc:2 * fc, :]
    nyq = (lax.broadcasted_iota(jnp.int32, (fc, 1), 0) == 0) & (j == 0)
    parts = []
    for s in range(sb):
        lanes = slice(s * tc, (s + 1) * tc)
        zr, zs = zf[0:fc, lanes], zf[fc:2 * fc, lanes]
        ss = zs * ks
        yr = zr * kr - jnp.where(nyq, 0.0, ss)
        ys = jnp.where(nyq, ss, zr * ks + zs * kr)
        parts.append(jnp.concatenate([yr, ys], axis=0).astype(BF16))
    y_ref[...] += _dot(inv_ref[...], jnp.concatenate(parts, axis=-1))

    @pl.when((o == 1) & (j == last_j))
    def _():
        for s in range(sb):
            lanes = slice(s * tc, (s + 1) * tc)
            z = short_conv(x2_ref, 2, s) * (y_ref[:, lanes] + hb_ref[1:2, :] * z_ref[:, lanes])
            o_ref[s * seq:(s + 1) * seq, :] = (z * _silu(gc_ref[s * seq:(s + 1) * seq, :])).astype(BF16)


def _hyena(proj, cw, cb, hb, fwd, inv, kf, seq):
    m = proj.shape[0]
    nch, fc2, _ = fwd.shape
    fc = fc2 // 2
    _, tc, sb = _hyena_tiling(seq)
    sb = min(sb, m // seq)
    nct = HY_W // tc
    hc0 = OFF_HC // tc
    once = pl.Buffered(1) if seq > 1024 else None
    col = lambda k: pl.BlockSpec((sb * seq, tc), lambda b, c, o, j: (b, hc0 + k * nct + c),
                                 pipeline_mode=once)
    return pl.pallas_call(
        functools.partial(_hyena_kernel, seq=seq, fc=fc, sb=sb, tc=tc),
        grid=(m // (sb * seq), nct, HYENA_ORDER, nch),
        in_specs=[
            col(0), col(1), col(2),
            pl.BlockSpec((sb * seq, tc), lambda b, c, o, j: (b, OFF_GC // tc + c), pipeline_mode=once),
            pl.BlockSpec((3, 3, tc), lambda b, c, o, j: (0, 0, c)),
            pl.BlockSpec((3, tc), lambda b, c, o, j: (0, c)),
            pl.BlockSpec((HYENA_ORDER, tc), lambda b, c, o, j: (0, c)),
            pl.BlockSpec((None, 2 * fc, seq), lambda b, c, o, j: (j, 0, 0)),
            pl.BlockSpec((None, seq, 2 * fc), lambda b, c, o, j: (j, 0, 0)),
            pl.BlockSpec((None, 2 * fc, tc), lambda b, c, o, j: (j, 0, o * nct + c)),
        ],
        out_specs=pl.BlockSpec((sb * seq, tc), lambda b, c, o, j: (b, c)),
        out_shape=jax.ShapeDtypeStruct((m, HY_W), BF16),
        scratch_shapes=[pltpu.VMEM((seq, sb * tc), F32), pltpu.VMEM((seq, sb * tc), BF16),
                        pltpu.VMEM((seq, sb * tc), F32)],
        compiler_params=_params(("arbitrary",) * 4, 56), name="hyena",
    )(proj, proj, proj, proj, cw, cb, hb, fwd, inv, kf)


def _merge_kernel(ya_ref, yb_ref, yc_ref, ga_ref, gb_ref, gc_ref, wa_ref, wb_ref, wc_ref, o_ref):
    acc = ga_ref[...].astype(F32) * _dot(ya_ref[...], wa_ref[...])
    acc += gb_ref[...].astype(F32) * _dot(yb_ref[...], wb_ref[...])
    acc += gc_ref[...].astype(F32) * _dot(yc_ref[...], wc_ref[...])
    o_ref[...] = acc.astype(BF16)


def _merge(ya, yb, yc, gates, w_br_bf, l, seq):
    m = ya.shape[0]
    tm = min(ROW_TILE, m)
    tn = COL_TILE
    d = D_MODEL
    row = lambda w: pl.BlockSpec((tm, w), lambda i, j: (i, 0))
    gate = lambda k: pl.BlockSpec((tm, tn), lambda i, j: (i, k * d // tn + j))
    return pl.pallas_call(
        _merge_kernel, grid=(m // tm, d // tn),
        in_specs=[row(ATT_W), row(FNET_W), row(HY_W), gate(0), gate(1), gate(2),
                  pl.BlockSpec((None, ATT_W, tn), lambda i, j: (l, 0, j)),
                  pl.BlockSpec((None, FNET_W, tn), lambda i, j: (l, ATT_W // FNET_W, j)),
                  pl.BlockSpec((None, HY_W, tn), lambda i, j: (l, (ATT_W + FNET_W) // HY_W, j))],
        out_specs=pl.BlockSpec((tm, tn), lambda i, j: (i, j)),
        out_shape=jax.ShapeDtypeStruct((m, d), BF16),
        compiler_params=_params(("arbitrary", "arbitrary"), 40), name="merge",
    )(ya, yb, yc, gates, gates, gates, w_br_bf, w_br_bf, w_br_bf)


def _outproj_kernel(mg_ref, w_ref, x_ref, ada_ref, o_ref):
    o_ref[...] = x_ref[...] + ada_ref[...] * _dot(mg_ref[...], w_ref[...])


def _outproj(merged, x2d, ada4, ada_row, w_out_bf, l, seq):
    m, d = x2d.shape
    tm = min(ROW_TILE, m)
    tn = COL_TILE
    return pl.pallas_call(
        _outproj_kernel, grid=(m // tm, d // tn),
        in_specs=[pl.BlockSpec((tm, d), lambda i, j: (i, 0)),
                  pl.BlockSpec((None, d, tn), lambda i, j: (l, 0, j)),
                  pl.BlockSpec((tm, tn), lambda i, j: (i, j)),
                  pl.BlockSpec((None, None, 1, tn),
                               lambda i, j: (l, ada_row(i * tm), 0, 2 * d // tn + j))],
        out_specs=pl.BlockSpec((tm, tn), lambda i, j: (i, j)),
        out_shape=jax.ShapeDtypeStruct((m, d), F32),
        compiler_params=_params(("arbitrary", "arbitrary"), 40), name="outproj",
    )(merged, w_out_bf, x2d, ada4)


def kernel(x_prompt, x_sample, cache_k, cache_v, c, c_ctx, norm_g, w_ada, b_ada, w_in, q_norm_g, k_norm_g, rpb, conv_w, conv_b, f_w1, f_b1, f_freq, f_w2, f_b2, f_w3, hy_bias, w_br, w_out):
    batch, seq, d = x_prompt.shape
    dec_batch, dec_seq, _ = x_sample.shape
    depth = norm_g.shape[0]
    past = cache_k.shape[2]
    assert dec_batch + 1 <= ADA_ROWS

    cv = jnp.zeros((ADA_ROWS, d), F32).at[0].set(c_ctx).at[1:1 + dec_batch].set(c)
    ada4 = _ada(cv, w_ada, b_ada).reshape(depth, ADA_ROWS, 1, 3 * d)
    ctx_row = lambda tok: 0
    lat_row = lambda tok: 1 + tok // dec_seq

    w_in_bf = w_in.astype(BF16)
    w_br_bf = w_br.astype(BF16)
    w_out_bf = w_out.astype(BF16)
    norm_g3 = norm_g.reshape(depth, 1, d)
    qg3 = q_norm_g.reshape(depth, 1, HEAD_DIM)
    kg3 = k_norm_g.reshape(depth, 1, HEAD_DIM)
    cache_k4 = cache_k.reshape(dec_batch, depth, past, ATT_W)
    cache_v4 = cache_v.reshape(dec_batch, depth, past, ATT_W)
    w1p = jnp.pad(f_w1, ((0, 0), (0, 128 - FILTER_EMB), (0, 0)))
    b1 = f_b1.reshape(depth, 1, FILTER_FF)
    fr = f_freq.reshape(depth, 1, FILTER_FF)
    b2 = f_b2.reshape(depth, 1, FILTER_FF)
    win_bias = _window_bias(rpb)

    groups = []
    for x, s, row in ((x_prompt, seq, ctx_row), (x_sample, dec_seq, lat_row)):
        fwd, inv = _hyena_tables(s)
        groups.append(dict(x=x.reshape(-1, d), seq=s, row=row, fnet=_fnet_tables(s),
                           fwd=fwd, inv=inv, feats=_filter_features(s)))

    new_kv = tuple(jnp.zeros((batch, depth, seq, ATT_W), F32) for _ in range(2))
    for l in range(depth):
        cw = conv_w[l].reshape(3, 3, HY_W)
        cb = conv_b[l].reshape(3, HY_W)
        hb = hy_bias[l]
        for gi, g in enumerate(groups):
            s = g["seq"]
            proj, gates = _inproj(g["x"], norm_g3, ada4, g["row"], w_in_bf, l, s)
            if gi == 0:
                ya, *new_kv = _ctx_attn(proj, qg3, kg3, new_kv, l, s)
            else:
                qn, kn, vb = _prep_lat(proj, qg3, kg3, l)
                ya = _lat_attn(qn, kn, vb, cache_k4, cache_v4, win_bias, proj, l, s)
            yb = _fnet(proj, g["fnet"], s)
            kf = _filter(g["feats"], g["fwd"], w1p, b1, fr, f_w2, b2, f_w3, l, s)
            yc = _hyena(proj, cw, cb, hb, g["fwd"], g["inv"], kf, s)
            merged = _merge(ya, yb, yc, gates, w_br_bf, l, s)
            g["x"] = _outproj(merged, g["x"], ada4, g["row"], w_out_bf, l, s)

    nk, nv = new_kv
    return (groups[0]["x"].reshape(batch, seq, d),
            groups[1]["x"].reshape(dec_batch, dec_seq, d),
            nk.reshape(batch, depth, seq, N_HEADS, HEAD_DIM),
            nv.reshape(batch, depth, seq, N_HEADS, HEAD_DIM))
```

```python
import functools
import math

import jax
import jax.numpy as jnp
import numpy as np
from jax import lax
from jax.experimental import pallas as pl
from jax.experimental.pallas import tpu as pltpu

F32 = jnp.float32
BF16 = jnp.bfloat16

D_MODEL = 2048
GRID_W = 64
N_HEADS = 8
HEAD_DIM = 128
ATT_W = N_HEADS * HEAD_DIM
MAX_KH = 8
KW = 16
FNET_GROUPS = 4
FNET_GDIM = 128
FNET_W = FNET_GROUPS * FNET_GDIM
HY_W = 512
HYENA_ORDER = 2
FILTER_EMB = 33
FILTER_FF = 64
MIN_DECAY = math.log(1e-2) / 1.5
MAX_DECAY = math.log(1e-2) / 0.3
OFF_Q = 0
OFF_K = OFF_Q + ATT_W
OFF_V = OFF_K + ATT_W
OFF_GA = OFF_V + ATT_W
OFF_UB = OFF_GA + ATT_W
OFF_GB = OFF_UB + FNET_W
OFF_HC = OFF_GB + FNET_W
OFF_GC = OFF_HC + 3 * HY_W
OFF_MG = OFF_GC + HY_W
N_IN = OFF_MG + 3 * D_MODEL
EPS = 1e-6
NEG = -1e30
ATT_SCALE = HEAD_DIM ** -0.5

ADA_ROWS = 8
FREQ_CHUNK = 512
TRIG_BLOCK = 64
LAT_ROW_BLOCK = 4
ROW_TILE = 1024
COL_TILE = 1024
MIB = 1024 * 1024


def _params(semantics, vmem_mib):
    return pltpu.CompilerParams(dimension_semantics=semantics,
                                vmem_limit_bytes=vmem_mib * MIB)


def _silu(x):
    return x * jax.nn.sigmoid(x)


def _dot(a, b):
    return jnp.dot(a, b, preferred_element_type=F32)


def _dot_nt(a, b):
    return lax.dot_general(a, b, (((1,), (1,)), ((), ())), preferred_element_type=F32)


def _ada_kernel(cv_ref, w_ref, b_ref, o_ref):
    s = _silu(cv_ref[...]).astype(BF16)
    o_ref[...] = _dot(s, w_ref[...].astype(BF16)) + b_ref[...]


def _ada(cv, w_ada, b_ada):
    depth, d, n = w_ada.shape
    tn = COL_TILE
    return pl.pallas_call(
        _ada_kernel,
        grid=(depth, n // tn),
        in_specs=[
            pl.BlockSpec((ADA_ROWS, d), lambda l, j: (0, 0)),
            pl.BlockSpec((None, d, tn), lambda l, j: (l, 0, j)),
            pl.BlockSpec((None, 1, tn), lambda l, j: (l, 0, j)),
        ],
        out_specs=pl.BlockSpec((None, ADA_ROWS, tn), lambda l, j: (l, 0, j)),
        out_shape=jax.ShapeDtypeStruct((depth, ADA_ROWS, n), F32),
        compiler_params=_params(("arbitrary", "arbitrary"), 32),
        name="ada",
    )(cv, w_ada, b_ada.reshape(depth, 1, n))


def _cast_next(src_ref, dst_ref):
    dst_ref[...] = src_ref[...].astype(BF16)


def _out_ref(refs):
    if len(refs) == 1:
        return refs[0]
    src_ref, o_ref, dst_ref = refs
    _cast_next(src_ref, dst_ref)
    return o_ref


def _next_weight_specs(w32, l, n_i, tn):
    if w32 is None or l + 1 >= w32.shape[0]:
        return [], [], [], []
    _, rows, cols = w32.shape
    rb = rows // n_i
    return ([w32], [pl.BlockSpec((None, rb, tn), lambda i, j: (l + 1, i, j))],
            [pl.BlockSpec((rb, tn), lambda i, j: (i, j))], [jax.ShapeDtypeStruct((rows, cols), BF16)])


def _inproj_kernel(x_ref, g_ref, ada_ref, w_ref, *refs, rows, n_plain):
    if len(refs) == 5:
        nxt32_ref, o_ref, gate_ref, nxt_ref, h_ref = refs
        _cast_next(nxt32_ref, nxt_ref)
    else:
        o_ref, gate_ref, h_ref = refs
    j = pl.program_id(1)

    @pl.when(j == 0)
    def _():
        g = g_ref[...]
        shift = ada_ref[:, 0:D_MODEL]
        scale1 = 1.0 + ada_ref[:, D_MODEL:2 * D_MODEL]

        def body(r, carry):
            sl = pl.ds(pl.multiple_of(r * rows, rows), rows)
            x = x_ref[sl, :]
            ms = jnp.mean(x * x, axis=-1, keepdims=True)
            y = x * lax.rsqrt(ms + EPS) * g
            h_ref[sl, :] = (y * scale1 + shift).astype(BF16)
            return carry

        lax.fori_loop(0, x_ref.shape[0] // rows, body, 0)

    @pl.when(j < n_plain)
    def _():
        o_ref[...] = _dot(h_ref[...], w_ref[...])

    @pl.when(j >= n_plain)
    def _():
        gate_ref[...] = jax.nn.sigmoid(_dot(h_ref[...], w_ref[...])).astype(BF16)


def _inproj(x2d, norm_g3, ada4, ada_row, w_bf, l, seq, w32=None):
    m, d = x2d.shape
    tm = min(ROW_TILE, m)
    tn = COL_TILE
    n_plain = OFF_MG // tn
    nxt_args, nxt_in, nxt_out, nxt_shape = _next_weight_specs(w32, l, m // tm, tn)
    return pl.pallas_call(
        functools.partial(_inproj_kernel, rows=128, n_plain=n_plain),
        grid=(m // tm, N_IN // tn),
        in_specs=[
            pl.BlockSpec((tm, d), lambda i, j: (i, 0)),
            pl.BlockSpec((None, 1, d), lambda i, j: (l, 0, 0)),
            pl.BlockSpec((None, None, 1, 3 * d), lambda i, j: (l, ada_row(i * tm), 0, 0)),
            pl.BlockSpec((d, tn), lambda i, j: (0, j)),
        ] + nxt_in,
        out_specs=[pl.BlockSpec((tm, tn), lambda i, j: (i, jnp.minimum(j, n_plain - 1))),
                   pl.BlockSpec((tm, tn), lambda i, j: (i, jnp.maximum(j - n_plain, 0)))] + nxt_out,
        out_shape=[jax.ShapeDtypeStruct((m, OFF_MG), F32),
                   jax.ShapeDtypeStruct((m, N_IN - OFF_MG), BF16)] + nxt_shape,
        scratch_shapes=[pltpu.VMEM((tm, d), BF16)],
        compiler_params=_params(("arbitrary", "arbitrary"), 52),
        name="inproj",
    )(x2d, norm_g3, ada4, w_bf, *nxt_args)


def _head_norm(x, g):
    ms = jnp.mean(x * x, axis=-1, keepdims=True)
    return x * lax.rsqrt(ms + EPS) * g


def _prep_lat_kernel(q_ref, k_ref, v_ref, qg_ref, kg_ref, qn_ref, kn_ref, vb_ref):
    for h in range(N_HEADS):
        sl = slice(h * HEAD_DIM, (h + 1) * HEAD_DIM)
        qn_ref[:, sl] = _head_norm(q_ref[:, sl], qg_ref[...]).astype(BF16)
        kn_ref[:, sl] = _head_norm(k_ref[:, sl], kg_ref[...]).astype(BF16)
    vb_ref[...] = v_ref[...].astype(BF16)


def _prep_lat(proj, qg3, kg3, l):
    m = proj.shape[0]
    tm = min(256, m)
    col = lambda c: pl.BlockSpec((tm, ATT_W), lambda i: (i, c))
    gspec = pl.BlockSpec((None, 1, HEAD_DIM), lambda i: (l, 0, 0))
    return pl.pallas_call(
        _prep_lat_kernel, grid=(m // tm,),
        in_specs=[col(OFF_Q // ATT_W), col(OFF_K // ATT_W), col(OFF_V // ATT_W), gspec, gspec],
        out_specs=[col(0)] * 3, out_shape=[jax.ShapeDtypeStruct((m, ATT_W), BF16)] * 3,
        compiler_params=_params(("arbitrary",), 32), name="prep_lat",
    )(proj, proj, proj, qg3, kg3)


def _ctx_attn_kernel(q_ref, k_ref, v_ref, ga_ref, qg_ref, kg_ref, nk_in, nv_in, o_ref, nk_ref, nv_ref):
    del nk_in, nv_in
    for h in range(N_HEADS):
        sl = slice(h * HEAD_DIM, (h + 1) * HEAD_DIM)
        q = _head_norm(q_ref[:, sl], qg_ref[...])
        k = _head_norm(k_ref[:, sl], kg_ref[...])
        v = v_ref[:, sl]
        nk_ref[:, sl] = k
        nv_ref[:, sl] = v
        s = _dot_nt(q.astype(BF16), k.astype(BF16)) * ATT_SCALE
        p = jnp.exp(s - jnp.max(s, axis=-1, keepdims=True))
        inv = 1.0 / jnp.sum(p, axis=-1, keepdims=True)
        o = _dot(p.astype(BF16), v.astype(BF16)) * inv
        o_ref[:, sl] = (o * _silu(ga_ref[:, sl])).astype(BF16)


def _ctx_attn(proj, qg3, kg3, new_kv, l, seq):
    m = proj.shape[0]
    col = lambda c: pl.BlockSpec((seq, ATT_W), lambda b: (b, c))
    gspec = pl.BlockSpec((None, 1, HEAD_DIM), lambda b: (l, 0, 0))
    kv_spec = pl.BlockSpec((None, None, seq, ATT_W), lambda b: (b, l, 0, 0))
    return pl.pallas_call(
        _ctx_attn_kernel, grid=(m // seq,),
        in_specs=[col(OFF_Q // ATT_W), col(OFF_K // ATT_W), col(OFF_V // ATT_W), col(OFF_GA // ATT_W),
                  gspec, gspec] + [pl.BlockSpec(memory_space=pl.ANY)] * 2,
        out_specs=[col(0), kv_spec, kv_spec],
        out_shape=[jax.ShapeDtypeStruct((m, ATT_W), BF16)]
        + [jax.ShapeDtypeStruct(a.shape, a.dtype) for a in new_kv],
        input_output_aliases={6: 1, 7: 2},
        compiler_params=_params(("arbitrary",), 32), name="ctx_attn",
    )(proj, proj, proj, proj, qg3, kg3, *new_kv)


def _window_bias(rpb):
    depth, heads, n_dr, n_dc = rpb.shape
    c = np.arange(GRID_W)[:, None]
    kc = np.arange(GRID_W)[None, :]
    cs = np.clip(c - KW // 2, 0, GRID_W - KW)
    valid = (kc >= cs) & (kc < cs + KW)
    dc = np.clip(kc - c + KW - 1, 0, n_dc - 1).reshape(-1)
    onehot = (jnp.asarray(dc)[None, :] == jnp.arange(n_dc)[:, None]).astype(F32)
    cb = jnp.einsum("lhdm,mx->lhdx", rpb, onehot, precision=lax.Precision.HIGHEST)
    cb = jnp.where(jnp.asarray(valid), cb.reshape(depth, heads, n_dr, GRID_W, GRID_W), NEG)
    cb = jnp.pad(cb, ((0, 0), (0, 0), (1, 1), (0, 0), (0, 0)), constant_values=NEG)
    return jnp.concatenate([cb[:, :, :-1], cb[:, :, 1:]], axis=-1)


def _lat_attn_kernel(q_ref, k_ref, v_ref, ck_ref, cv_ref, bias_ref, ga_ref, o_ref,
                     ckb_ref, cvb_ref, *, rows, kh, rb, kr):
    g = pl.program_id(1)

    @pl.when(g == 0)
    def _():
        ckb_ref[...] = ck_ref[...].astype(BF16)
        cvb_ref[...] = cv_ref[...].astype(BF16)

    r0 = g * rb
    ws = jnp.clip(r0 - kh // 2, 0, rows - kr)
    span = pl.ds(pl.multiple_of(ws * GRID_W, GRID_W), kr * GRID_W)
    q_row = r0 + lax.broadcasted_iota(jnp.int32, (rb * GRID_W, 1), 0) // GRID_W
    k_row = ws + lax.broadcasted_iota(jnp.int32, (1, kr * GRID_W), 1) // GRID_W
    q_rs = jnp.clip(q_row - kh // 2, 0, rows - kh)
    in_window = (k_row >= q_rs) & (k_row < q_rs + kh)
    n_pairs = bias_ref.shape[1]
    for h in range(N_HEADS):
        sl = slice(h * HEAD_DIM, (h + 1) * HEAD_DIM)
        q = q_ref[:, sl]
        bias = jnp.concatenate(
            [jnp.concatenate(
                [bias_ref[h, jnp.clip(ws + 2 * j - (r0 + qi) + MAX_KH, 0, n_pairs - 1)]
                 for j in range(kr // 2)], axis=-1)
             for qi in range(rb)], axis=0)
        s_win = jnp.where(in_window, _dot_nt(q, k_ref[span, sl]) * ATT_SCALE + bias, NEG)
        s_ctx = _dot_nt(q, ckb_ref[:, sl]) * ATT_SCALE
        mx = jnp.maximum(jnp.max(s_win, axis=-1, keepdims=True),
                         jnp.max(s_ctx, axis=-1, keepdims=True))
        p_win = jnp.exp(s_win - mx)
        p_ctx = jnp.exp(s_ctx - mx)
        inv = 1.0 / (jnp.sum(p_win, axis=-1, keepdims=True) + jnp.sum(p_ctx, axis=-1, keepdims=True))
        o = (_dot(p_win.astype(BF16), v_ref[span, sl]) + _dot(p_ctx.astype(BF16), cvb_ref[:, sl])) * inv
        o_ref[:, sl] = (o * _silu(ga_ref[:, sl])).astype(BF16)


def _lat_attn(qn, kn, vb, cache_k4, cache_v4, bias, proj, l, seq):
    m = qn.shape[0]
    batch = m // seq
    rows = seq // GRID_W
    kh = min(MAX_KH, rows)
    rb = min(LAT_ROW_BLOCK, rows)
    kr = min(rows, kh + rb)
    assert rows % rb == 0 and kr % 2 == 0
    past = cache_k4.shape[2]
    qspec = pl.BlockSpec((rb * GRID_W, ATT_W), lambda b, g: (b * (rows // rb) + g, 0))
    kvspec = pl.BlockSpec((seq, ATT_W), lambda b, g: (b, 0))
    cspec = pl.BlockSpec((None, None, past, ATT_W), lambda b, g: (b, l, 0, 0))
    return pl.pallas_call(
        functools.partial(_lat_attn_kernel, rows=rows, kh=kh, rb=rb, kr=kr),
        grid=(batch, rows // rb),
        in_specs=[qspec, kvspec, kvspec, cspec, cspec,
                  pl.BlockSpec((None,) + bias.shape[1:], lambda b, g: (l, 0, 0, 0, 0)),
                  pl.BlockSpec((rb * GRID_W, ATT_W),
                               lambda b, g: (b * (rows // rb) + g, OFF_GA // ATT_W))],
        out_specs=qspec,
        out_shape=jax.ShapeDtypeStruct((m, ATT_W), BF16),
        scratch_shapes=[pltpu.VMEM((past, ATT_W), BF16), pltpu.VMEM((past, ATT_W), BF16)],
        compiler_params=_params(("arbitrary", "arbitrary"), 48), name="lat_attn",
    )(qn, kn, vb, cache_k4, cache_v4, bias, proj)


def _cos_sin_table(r, ncols, period, s_before_last=False):
    def factor(v):
        ang = ((r[..., None] * v) % period).astype(F32) * (2.0 * math.pi / period)
        return jnp.cos(ang), jnp.sin(ang)

    ca, sa = factor(jnp.arange(ncols // TRIG_BLOCK, dtype=jnp.int32) * TRIG_BLOCK)
    cb, sb = factor(jnp.arange(TRIG_BLOCK, dtype=jnp.int32))
    if s_before_last:
        ca, sa, cb, sb = (jnp.swapaxes(t, -1, -2) for t in (ca, sa, cb, sb))
        ca, sa, cb, sb = ca[..., :, None, :], sa[..., :, None, :], cb[..., None, :, :], sb[..., None, :, :]
        shape = r.shape[:-1] + (ncols, r.shape[-1])
    else:
        ca, sa, cb, sb = ca[..., :, None], sa[..., :, None], cb[..., None, :], sb[..., None, :]
        shape = r.shape + (ncols,)
    return (ca * cb - sa * sb).reshape(shape), (sa * cb + ca * sb).reshape(shape)


def _fnet_tables(seq):
    cl, sl = _cos_sin_table(jnp.arange(seq, dtype=jnp.int32), seq, seq)
    cc, sc = _cos_sin_table(jnp.arange(FNET_GDIM, dtype=jnp.int32), FNET_GDIM, FNET_GDIM)
    return (jnp.concatenate([cl, -sl], axis=1).astype(BF16), cc.astype(BF16), sc.astype(BF16))


def _fnet_kernel(u_ref, gb_ref, tl_ref, cc_ref, sc_ref, o_ref, t_ref, *, seq, tl):
    i = pl.program_id(1)

    @pl.when(i == 0)
    def _():
        for g in range(FNET_GROUPS):
            sl = slice(g * FNET_GDIM, (g + 1) * FNET_GDIM)
            ug = u_ref[:, sl].astype(BF16)
            t_ref[0:seq, sl] = _dot(ug, cc_ref[...]).astype(BF16)
            t_ref[seq:2 * seq, sl] = _dot(ug, sc_ref[...]).astype(BF16)

    y = _dot(tl_ref[...], t_ref[...]) * ((seq * FNET_GDIM) ** -0.5)
    o_ref[...] = (y * _silu(gb_ref[...])).astype(BF16)


def _fnet(proj, tables, seq):
    m = proj.shape[0]
    tab_l, cc, sc = tables
    tl = min(512, seq)
    nt = seq // tl
    return pl.pallas_call(
        functools.partial(_fnet_kernel, seq=seq, tl=tl),
        grid=(m // seq, nt),
        in_specs=[
            pl.BlockSpec((seq, FNET_W), lambda b, i: (b, OFF_UB // FNET_W)),
            pl.BlockSpec((tl, FNET_W), lambda b, i: (b * nt + i, OFF_GB // FNET_W)),
            pl.BlockSpec((tl, 2 * seq), lambda b, i: (i, 0)),
            pl.BlockSpec((FNET_GDIM, FNET_GDIM), lambda b, i: (0, 0)),
            pl.BlockSpec((FNET_GDIM, FNET_GDIM), lambda b, i: (0, 0)),
        ],
        out_specs=pl.BlockSpec((tl, FNET_W), lambda b, i: (b * nt + i, 0)),
        out_shape=jax.ShapeDtypeStruct((m, FNET_W), BF16),
        scratch_shapes=[pltpu.VMEM((2 * seq, FNET_W), BF16)],
        compiler_params=_params(("arbitrary", "arbitrary"), 40), name="fnet",
    )(proj, proj, tab_l, cc, sc)


def _hyena_tiling(seq):
    if seq > 1024:
        return min(FREQ_CHUNK // 2, seq), HY_W // 2, 2
    return min(FREQ_CHUNK, seq), HY_W, 4


def _hyena_tables(seq):
    fc = _hyena_tiling(seq)[0]
    nch = seq // fc
    k = jnp.arange(2 * fc, dtype=jnp.int32)
    f = jnp.arange(nch, dtype=jnp.int32)[:, None] * fc + (k % fc)[None, :]
    is_sec = (k >= fc)[None, :, None]
    nyq = is_sec & (f == 0)[:, :, None]
    sgn = (1 - 2 * (jnp.arange(seq, dtype=jnp.int32) % 2)).astype(F32)

    def table(transposed):
        c, s = _cos_sin_table(f, seq, 2 * seq, s_before_last=transposed)
        axes = (0, 2, 1) if transposed else (0, 1, 2)
        sec = jnp.where(nyq.transpose(axes), sgn.reshape((1, 1, seq)).transpose(axes), -s)
        return jnp.where(is_sec.transpose(axes), sec, c).astype(BF16)

    return table(False), table(True)


def _filter_features(seq):
    t = jnp.linspace(0.0, 1.0, seq, dtype=F32)[:, None]
    bands = (FILTER_EMB - 1) // 2
    w = (2.0 * math.pi / seq) * jnp.arange(seq, dtype=F32)[:, None]
    f = jnp.linspace(1e-4, bands - 1, bands, dtype=F32)[None, :]
    z = jnp.concatenate([t, jnp.cos(w * f), -jnp.sin(w * f)], axis=-1)
    z = jnp.pad(z, ((0, 0), (0, 128 - FILTER_EMB)))
    deltas = jnp.abs(jnp.linspace(MIN_DECAY, MAX_DECAY, HY_W, dtype=F32))[None, :]
    return z, t, deltas


def _dot_f32(a, b):
    return jnp.dot(a, b, preferred_element_type=F32, precision=lax.Precision.HIGHEST)


def _filter_kernel(z_ref, t_ref, dl_ref, w1_ref, b1_ref, fr_ref, w2_ref, b2_ref, w3_ref,
                   fwd_ref, o_ref, h_ref, sd_ref, nyq_ref, *, seq, fc, rows):
    j = pl.program_id(0)
    nblk = 2 * HYENA_ORDER
    nrc = seq // rows

    @pl.when(j == 0)
    def _():
        fr = fr_ref[...]

        def taps(r, acc):
            sl = pl.ds(pl.multiple_of(r * rows, rows), rows)
            h = jnp.sin(fr * (_dot_f32(z_ref[sl, :], w1_ref[...]) + b1_ref[...]))
            h = jnp.sin(fr * (_dot_f32(h, w2_ref[...]) + b2_ref[...]))
            h = _dot_f32(h, w3_ref[...])
            decay = jnp.exp(-t_ref[sl, :] * dl_ref[...])
            h = h * jnp.concatenate([decay] * nblk, axis=-1)
            h_ref[sl, :] = h
            return acc + jnp.sum(jnp.abs(h), axis=0, keepdims=True)

        tot = lax.fori_loop(0, nrc, taps, jnp.zeros((1, nblk * HY_W), F32))
        inv = 1.0 / (tot + EPS)
        sgn = (1 - 2 * (lax.broadcasted_iota(jnp.int32, (rows, 1), 0) % 2)).astype(F32)

        def fold(r, acc):
            sl = pl.ds(pl.multiple_of(r * rows, rows), rows)
            h = h_ref[sl, :] * inv
            pos = r * rows + lax.broadcasted_iota(jnp.int32, (rows, 1), 0)
            sums = []
            for o in range(HYENA_ORDER):
                fw = h[:, (2 * o) * HY_W:(2 * o + 1) * HY_W]
                bw = jnp.where(pos == 0, 0.0, h[:, (2 * o + 1) * HY_W:(2 * o + 2) * HY_W])
                sums.append(fw + bw)
                sd_ref[0, sl, o * HY_W:(o + 1) * HY_W] = (fw + bw).astype(BF16)
                sd_ref[1, sl, o * HY_W:(o + 1) * HY_W] = (fw - bw).astype(BF16)
            return acc + jnp.sum(jnp.concatenate(sums, axis=-1) * sgn, axis=0, keepdims=True)

        nyq_ref[...] = lax.fori_loop(0, nrc, fold, jnp.zeros((1, HYENA_ORDER * HY_W), F32))

    re = _dot(fwd_ref[0:fc, :], sd_ref[0])
    sec = _dot(fwd_ref[fc:2 * fc, :], sd_ref[1])
    first = (lax.broadcasted_iota(jnp.int32, (fc, 1), 0) == 0) & (j == 0)
    sec = jnp.where(first, nyq_ref[...], sec)
    weight = jnp.where(first, 0.5 / seq, 1.0 / seq)
    o_ref[0:fc, :] = re * weight
    o_ref[fc:2 * fc, :] = sec * weight


def _filter(feats, fwd, w1p, b1, fr, w2, b2, w3, l, seq):
    z, t, deltas = feats
    nch, fc2, _ = fwd.shape
    fc = fc2 // 2
    width = HYENA_ORDER * HY_W
    rows = min(256, seq)
    full = lambda a: pl.BlockSpec(a.shape, lambda j: (0,) * a.ndim)
    lay = lambda a: pl.BlockSpec((None,) + a.shape[1:], lambda j: (l,) + (0,) * (a.ndim - 1))
    return pl.pallas_call(
        functools.partial(_filter_kernel, seq=seq, fc=fc, rows=rows),
        grid=(nch,),
        in_specs=[full(z), full(t), full(deltas), lay(w1p), lay(b1), lay(fr), lay(w2), lay(b2), lay(w3),
                  pl.BlockSpec((None, 2 * fc, seq), lambda j: (j, 0, 0))],
        out_specs=pl.BlockSpec((None, 2 * fc, width), lambda j: (j, 0, 0)),
        out_shape=jax.ShapeDtypeStruct((nch, 2 * fc, width), F32),
        scratch_shapes=[pltpu.VMEM((seq, 2 * width), F32),
                        pltpu.VMEM((2, seq, width), BF16),
                        pltpu.VMEM((1, width), F32)],
        compiler_params=_params(("arbitrary",), 48), name="hyena_filter",
    )(z, t, deltas, w1p, b1, fr, w2, b2, w3, fwd)


def _hyena_kernel(v_ref, x1_ref, x2_ref, gc_ref, cw_ref, cb_ref, hb_ref, fwd_ref, inv_ref, kf_ref,
                  o_ref, z_ref, zb_ref, y_ref, *, seq, fc, sb, tc):
    o = pl.program_id(2)
    j = pl.program_id(3)
    last_j = pl.num_programs(3) - 1
    pos = lax.broadcasted_iota(jnp.int32, (seq, 1), 0)

    def short_conv(ref, k, s):
        x = ref[s * seq:(s + 1) * seq, :]
        prev = jnp.where(pos == 0, 0.0, pltpu.roll(x, 1, 0))
        nxt = jnp.where(pos == seq - 1, 0.0, pltpu.roll(x, seq - 1, 0))
        return (prev * cw_ref[0, k:k + 1, :] + x * cw_ref[1, k:k + 1, :]
                + nxt * cw_ref[2, k:k + 1, :] + cb_ref[k:k + 1, :])

    @pl.when((o == 0) & (j == 0))
    def _():
        for s in range(sb):
            lanes = slice(s * tc, (s + 1) * tc)
            z = short_conv(v_ref, 0, s)
            z_ref[:, lanes] = z
            zb_ref[:, lanes] = z.astype(BF16)
        y_ref[...] = jnp.zeros_like(y_ref)

    @pl.when((o == 1) & (j == 0))
    def _():
        for s in range(sb):
            lanes = slice(s * tc, (s + 1) * tc)
            z = short_conv(x1_ref, 1, s) * (y_ref[:, lanes] + hb_ref[0:1, :] * z_ref[:, lanes])
            z_ref[:, lanes] = z
            zb_ref[:, lanes] = z.astype(BF16)
        y_ref[...] = jnp.zeros_like(y_ref)

    zf = _dot(fwd_ref[...], zb_ref[...])
    kr, ks = kf_ref[0:fc, :], kf_ref[fc:2 * fc, :]
    nyq = (lax.broadcasted_iota(jnp.int32, (fc, 1), 0) == 0) & (j == 0)
    parts = []
    for s in range(sb):
        lanes = slice(s * tc, (s + 1) * tc)
        zr, zs = zf[0:fc, lanes], zf[fc:2 * fc, lanes]
        ss = zs * ks
        yr = zr * kr - jnp.where(nyq, 0.0, ss)
        ys = jnp.where(nyq, ss, zr * ks + zs * kr)
        parts.append(jnp.concatenate([yr, ys], axis=0).astype(BF16))
    y_ref[...] += _dot(inv_ref[...], jnp.concatenate(parts, axis=-1))

    @pl.when((o == 1) & (j == last_j))
    def _():
        for s in range(sb):
            lanes = slice(s * tc, (s + 1) * tc)
            z = short_conv(x2_ref, 2, s) * (y_ref[:, lanes] + hb_ref[1:2, :] * z_ref[:, lanes])
            o_ref[s * seq:(s + 1) * seq, :] = (z * _silu(gc_ref[s * seq:(s + 1) * seq, :])).astype(BF16)


def _hyena(proj, cw, cb, hb, fwd, inv, kf, seq):
    m = proj.shape[0]
    nch, fc2, _ = fwd.shape
    fc = fc2 // 2
    _, tc, sb = _hyena_tiling(seq)
    sb = min(sb, m // seq)
    nct = HY_W // tc
    hc0 = OFF_HC // tc
    once = pl.Buffered(1) if seq > 1024 else None
    col = lambda k: pl.BlockSpec((sb * seq, tc), lambda b, c, o, j: (b, hc0 + k * nct + c),
                                 pipeline_mode=once)
    return pl.pallas_call(
        functools.partial(_hyena_kernel, seq=seq, fc=fc, sb=sb, tc=tc),
        grid=(m // (sb * seq), nct, HYENA_ORDER, nch),
        in_specs=[
            col(0), col(1), col(2),
            pl.BlockSpec((sb * seq, tc), lambda b, c, o, j: (b, OFF_GC // tc + c), pipeline_mode=once),
            pl.BlockSpec((3, 3, tc), lambda b, c, o, j: (0, 0, c)),
            pl.BlockSpec((3, tc), lambda b, c, o, j: (0, c)),
            pl.BlockSpec((HYENA_ORDER, tc), lambda b, c, o, j: (0, c)),
            pl.BlockSpec((None, 2 * fc, seq), lambda b, c, o, j: (j, 0, 0)),
            pl.BlockSpec((None, seq, 2 * fc), lambda b, c, o, j: (j, 0, 0)),
            pl.BlockSpec((None, 2 * fc, tc), lambda b, c, o, j: (j, 0, o * nct + c)),
        ],
        out_specs=pl.BlockSpec((sb * seq, tc), lambda b, c, o, j: (b, c)),
        out_shape=jax.ShapeDtypeStruct((m, HY_W), BF16),
        scratch_shapes=[pltpu.VMEM((seq, sb * tc), F32), pltpu.VMEM((seq, sb * tc), BF16),
                        pltpu.VMEM((seq, sb * tc), F32)],
        compiler_params=_params(("arbitrary",) * 4, 56), name="hyena",
    )(proj, proj, proj, proj, cw, cb, hb, fwd, inv, kf)


def _merge_kernel(ya_ref, yb_ref, yc_ref, ga_ref, gb_ref, gc_ref, wa_ref, wb_ref, wc_ref, *refs):
    o_ref = _out_ref(refs)
    acc = ga_ref[...].astype(F32) * _dot(ya_ref[...], wa_ref[...])
    acc += gb_ref[...].astype(F32) * _dot(yb_ref[...], wb_ref[...])
    acc += gc_ref[...].astype(F32) * _dot(yc_ref[...], wc_ref[...])
    o_ref[...] = acc.astype(BF16)


def _merge(ya, yb, yc, gates, w_bf, l, seq, w32=None):
    m = ya.shape[0]
    tm = min(ROW_TILE, m)
    tn = COL_TILE
    d = D_MODEL
    row = lambda w: pl.BlockSpec((tm, w), lambda i, j: (i, 0))
    gate = lambda k: pl.BlockSpec((tm, tn), lambda i, j: (i, k * d // tn + j))
    nxt_args, nxt_in, nxt_out, nxt_shape = _next_weight_specs(w32, l, m // tm, tn)
    return pl.pallas_call(
        _merge_kernel, grid=(m // tm, d // tn),
        in_specs=[row(ATT_W), row(FNET_W), row(HY_W), gate(0), gate(1), gate(2),
                  pl.BlockSpec((ATT_W, tn), lambda i, j: (0, j)),
                  pl.BlockSpec((FNET_W, tn), lambda i, j: (ATT_W // FNET_W, j)),
                  pl.BlockSpec((HY_W, tn), lambda i, j: ((ATT_W + FNET_W) // HY_W, j))] + nxt_in,
        out_specs=[pl.BlockSpec((tm, tn), lambda i, j: (i, j))] + nxt_out,
        out_shape=[jax.ShapeDtypeStruct((m, d), BF16)] + nxt_shape,
        compiler_params=_params(("arbitrary", "arbitrary"), 40), name="merge",
    )(ya, yb, yc, gates, gates, gates, w_bf, w_bf, w_bf, *nxt_args)


def _outproj_kernel(mg_ref, w_ref, x_ref, ada_ref, *refs):
    o_ref = _out_ref(refs)
    o_ref[...] = x_ref[...] + ada_ref[...] * _dot(mg_ref[...], w_ref[...])


def _outproj(merged, x2d, ada4, ada_row, w_bf, l, seq, w32=None):
    m, d = x2d.shape
    tm = min(ROW_TILE, m)
    tn = COL_TILE
    nxt_args, nxt_in, nxt_out, nxt_shape = _next_weight_specs(w32, l, m // tm, tn)
    return pl.pallas_call(
        _outproj_kernel, grid=(m // tm, d // tn),
        in_specs=[pl.BlockSpec((tm, d), lambda i, j: (i, 0)),
                  pl.BlockSpec((d, tn), lambda i, j: (0, j)),
                  pl.BlockSpec((tm, tn), lambda i, j: (i, j)),
                  pl.BlockSpec((None, None, 1, tn),
                               lambda i, j: (l, ada_row(i * tm), 0, 2 * d // tn + j))] + nxt_in,
        out_specs=[pl.BlockSpec((tm, tn), lambda i, j: (i, j))] + nxt_out,
        out_shape=[jax.ShapeDtypeStruct((m, d), F32)] + nxt_shape,
        compiler_params=_params(("arbitrary", "arbitrary"), 44), name="outproj",
    )(merged, w_bf, x2d, ada4, *nxt_args)


def kernel(x_prompt, x_sample, cache_k, cache_v, c, c_ctx, norm_g, w_ada, b_ada, w_in, q_norm_g, k_norm_g, rpb, conv_w, conv_b, f_w1, f_b1, f_freq, f_w2, f_b2, f_w3, hy_bias, w_br, w_out):
    batch, seq, d = x_prompt.shape
    dec_batch, dec_seq, _ = x_sample.shape
    depth = norm_g.shape[0]
    past = cache_k.shape[2]
    assert dec_batch + 1 <= ADA_ROWS

    cv = jnp.zeros((ADA_ROWS, d), F32).at[0].set(c_ctx).at[1:1 + dec_batch].set(c)
    ada4 = _ada(cv, w_ada, b_ada).reshape(depth, ADA_ROWS, 1, 3 * d)
    ctx_row = lambda tok: 0
    lat_row = lambda tok: 1 + tok // dec_seq

    w_in_bf, w_br_bf, w_out_bf = (w[0].astype(BF16) for w in (w_in, w_br, w_out))
    norm_g3 = norm_g.reshape(depth, 1, d)
    qg3 = q_norm_g.reshape(depth, 1, HEAD_DIM)
    kg3 = k_norm_g.reshape(depth, 1, HEAD_DIM)
    cache_k4 = cache_k.reshape(dec_batch, depth, past, ATT_W)
    cache_v4 = cache_v.reshape(dec_batch, depth, past, ATT_W)
    w1p = jnp.pad(f_w1, ((0, 0), (0, 128 - FILTER_EMB), (0, 0)))
    b1 = f_b1.reshape(depth, 1, FILTER_FF)
    fr = f_freq.reshape(depth, 1, FILTER_FF)
    b2 = f_b2.reshape(depth, 1, FILTER_FF)
    win_bias = _window_bias(rpb)

    groups = []
    for x, s, row in ((x_prompt, seq, ctx_row), (x_sample, dec_seq, lat_row)):
        fwd, inv = _hyena_tables(s)
        groups.append(dict(x=x.reshape(-1, d), seq=s, row=row, fnet=_fnet_tables(s),
                           fwd=fwd, inv=inv, feats=_filter_features(s)))

    new_kv = tuple(jnp.zeros((batch, depth, seq, ATT_W), F32) for _ in range(2))
    for l in range(depth):
        cw = conv_w[l].reshape(3, 3, HY_W)
        cb = conv_b[l].reshape(3, HY_W)
        hb = hy_bias[l]
        for gi, g in enumerate(groups):
            s = g["seq"]
            side = gi == 0
            proj, gates, *nxt_in = _inproj(g["x"], norm_g3, ada4, g["row"], w_in_bf, l, s,
                                           w_in if side else None)
            if gi == 0:
                ya, *new_kv = _ctx_attn(proj, qg3, kg3, new_kv, l, s)
            else:
                qn, kn, vb = _prep_lat(proj, qg3, kg3, l)
                ya = _lat_attn(qn, kn, vb, cache_k4, cache_v4, win_bias, proj, l, s)
            yb = _fnet(proj, g["fnet"], s)
            kf = _filter(g["feats"], g["fwd"], w1p, b1, fr, f_w2, b2, f_w3, l, s)
            yc = _hyena(proj, cw, cb, hb, g["fwd"], g["inv"], kf, s)
            merged, *nxt_br = _merge(ya, yb, yc, gates, w_br_bf, l, s, w_br if side else None)
            g["x"], *nxt_out = _outproj(merged, g["x"], ada4, g["row"], w_out_bf, l, s,
                                        w_out if side else None)
            if side:
                nxt = (nxt_in, nxt_br, nxt_out)
        if l + 1 < depth:
            (w_in_bf,), (w_br_bf,), (w_out_bf,) = nxt

    nk, nv = new_kv
    return (groups[0]["x"].reshape(batch, seq, d),
            groups[1]["x"].reshape(dec_batch, dec_seq, d),
            nk.reshape(batch, depth, seq, N_HEADS, HEAD_DIM),
            nv.reshape(batch, depth, seq, N_HEADS, HEAD_DIM))
```

```python
import functools
import math

import jax
import jax.numpy as jnp
import numpy as np
from jax import lax
from jax.experimental import pallas as pl
from jax.experimental.pallas import tpu as pltpu

F32 = jnp.float32
BF16 = jnp.bfloat16

D_MODEL = 2048
GRID_W = 64
N_HEADS = 8
HEAD_DIM = 128
ATT_W = N_HEADS * HEAD_DIM
MAX_KH = 8
KW = 16
FNET_GROUPS = 4
FNET_GDIM = 128
FNET_W = FNET_GROUPS * FNET_GDIM
HY_W = 512
HYENA_ORDER = 2
FILTER_EMB = 33
FILTER_FF = 64
MIN_DECAY = math.log(1e-2) / 1.5
MAX_DECAY = math.log(1e-2) / 0.3
OFF_Q = 0
OFF_K = OFF_Q + ATT_W
OFF_V = OFF_K + ATT_W
OFF_GA = OFF_V + ATT_W
OFF_UB = OFF_GA + ATT_W
OFF_GB = OFF_UB + FNET_W
OFF_HC = OFF_GB + FNET_W
OFF_GC = OFF_HC + 3 * HY_W
OFF_MG = OFF_GC + HY_W
N_IN = OFF_MG + 3 * D_MODEL
EPS = 1e-6
NEG = -1e30
ATT_SCALE = HEAD_DIM ** -0.5

LANES = 128
ADA_ROWS = 8
FREQ_CHUNK = 512
TRIG_BLOCK = 64
LAT_ROW_BLOCK = 4
ROW_TILE = 1024
COL_TILE = 1024
MIB = 1024 * 1024


def _params(semantics, vmem_mib):
    return pltpu.CompilerParams(dimension_semantics=semantics,
                                vmem_limit_bytes=vmem_mib * MIB)


def _silu(x):
    return x * jax.nn.sigmoid(x)


def _dot(a, b):
    return jnp.dot(a, b, preferred_element_type=F32)


def _dot_nt(a, b):
    return lax.dot_general(a, b, (((1,), (1,)), ((), ())), preferred_element_type=F32)


def _ada_kernel(cv_ref, w_ref, b_ref, o_ref):
    o_ref[...] = jnp.zeros(o_ref.shape, F32)
    for r in range(cv_ref.shape[0]):
        s = _silu(cv_ref[r])
        for c in range(w_ref.shape[1] // LANES):
            lanes = slice(c * LANES, (c + 1) * LANES)
            o_ref[r:r + 1, lanes] = (jnp.sum(w_ref[:, lanes] * s, axis=0, keepdims=True)
                                     + b_ref[:, lanes])


def _ada(conds, w_ada, b_ada):
    depth, d, n = w_ada.shape
    rows = conds.shape[0]
    assert rows <= ADA_ROWS
    tn = COL_TILE
    cv = jnp.broadcast_to(conds[:, :, None], (rows, d, LANES))
    return pl.pallas_call(
        _ada_kernel,
        grid=(depth, n // tn),
        in_specs=[
            pl.BlockSpec((rows, d, LANES), lambda l, j: (0, 0, 0)),
            pl.BlockSpec((None, d, tn), lambda l, j: (l, 0, j)),
            pl.BlockSpec((None, 1, tn), lambda l, j: (l, 0, j)),
        ],
        out_specs=pl.BlockSpec((None, ADA_ROWS, tn), lambda l, j: (l, 0, j)),
        out_shape=jax.ShapeDtypeStruct((depth, ADA_ROWS, n), F32),
        compiler_params=_params(("arbitrary", "arbitrary"), 32),
        name="ada",
    )(cv, w_ada, b_ada.reshape(depth, 1, n))


def _cast_next(src_ref, dst_ref):
    dst_ref[...] = src_ref[...].astype(BF16)


def _out_ref(refs):
    if len(refs) == 1:
        return refs[0]
    src_ref, o_ref, dst_ref = refs
    _cast_next(src_ref, dst_ref)
    return o_ref


def _next_weight_specs(w32, l, n_i, tn):
    if w32 is None or l + 1 >= w32.shape[0]:
        return [], [], [], []
    _, rows, cols = w32.shape
    rb = rows // n_i
    return ([w32], [pl.BlockSpec((None, rb, tn), lambda i, j: (l + 1, i, j))],
            [pl.BlockSpec((rb, tn), lambda i, j: (i, j))], [jax.ShapeDtypeStruct((rows, cols), BF16)])


def _inproj_kernel(x_ref, g_ref, ada_ref, w_ref, *refs, rows, n_plain):
    if len(refs) == 5:
        nxt32_ref, o_ref, gate_ref, nxt_ref, h_ref = refs
        _cast_next(nxt32_ref, nxt_ref)
    else:
        o_ref, gate_ref, h_ref = refs
    j = pl.program_id(1)

    @pl.when(j == 0)
    def _():
        g = g_ref[...]
        shift = ada_ref[:, 0:D_MODEL]
        scale1 = 1.0 + ada_ref[:, D_MODEL:2 * D_MODEL]

        def body(r, carry):
            sl = pl.ds(pl.multiple_of(r * rows, rows), rows)
            x = x_ref[sl, :]
            ms = jnp.mean(x * x, axis=-1, keepdims=True)
            y = x * lax.rsqrt(ms + EPS) * g
            h_ref[sl, :] = (y * scale1 + shift).astype(BF16)
            return carry

        lax.fori_loop(0, x_ref.shape[0] // rows, body, 0)

    @pl.when(j < n_plain)
    def _():
        o_ref[...] = _dot(h_ref[...], w_ref[...])

    @pl.when(j >= n_plain)
    def _():
        gate_ref[...] = jax.nn.sigmoid(_dot(h_ref[...], w_ref[...])).astype(BF16)


def _inproj(x2d, norm_g3, ada4, ada_row, w_bf, l, seq, w32=None):
    m, d = x2d.shape
    tm = min(ROW_TILE, m)
    tn = COL_TILE
    n_plain = OFF_MG // tn
    nxt_args, nxt_in, nxt_out, nxt_shape = _next_weight_specs(w32, l, m // tm, tn)
    return pl.pallas_call(
        functools.partial(_inproj_kernel, rows=128, n_plain=n_plain),
        grid=(m // tm, N_IN // tn),
        in_specs=[
            pl.BlockSpec((tm, d), lambda i, j: (i, 0)),
            pl.BlockSpec((None, 1, d), lambda i, j: (l, 0, 0)),
            pl.BlockSpec((None, None, 1, 3 * d), lambda i, j: (l, ada_row(i * tm), 0, 0)),
            pl.BlockSpec((d, tn), lambda i, j: (0, j)),
        ] + nxt_in,
        out_specs=[pl.BlockSpec((tm, tn), lambda i, j: (i, jnp.minimum(j, n_plain - 1))),
                   pl.BlockSpec((tm, tn), lambda i, j: (i, jnp.maximum(j - n_plain, 0)))] + nxt_out,
        out_shape=[jax.ShapeDtypeStruct((m, OFF_MG), F32),
                   jax.ShapeDtypeStruct((m, N_IN - OFF_MG), BF16)] + nxt_shape,
        scratch_shapes=[pltpu.VMEM((tm, d), BF16)],
        compiler_params=_params(("arbitrary", "arbitrary"), 52),
        name="inproj",
    )(x2d, norm_g3, ada4, w_bf, *nxt_args)


def _head_norm(x, g):
    ms = jnp.mean(x * x, axis=-1, keepdims=True)
    return x * lax.rsqrt(ms + EPS) * g


def _prep_lat_kernel(q_ref, k_ref, v_ref, qg_ref, kg_ref, qn_ref, kn_ref, vb_ref):
    for h in range(N_HEADS):
        sl = slice(h * HEAD_DIM, (h + 1) * HEAD_DIM)
        qn_ref[:, sl] = _head_norm(q_ref[:, sl], qg_ref[...]).astype(BF16)
        kn_ref[:, sl] = _head_norm(k_ref[:, sl], kg_ref[...]).astype(BF16)
    vb_ref[...] = v_ref[...].astype(BF16)


def _prep_lat(proj, qg3, kg3, l):
    m = proj.shape[0]
    tm = min(256, m)
    col = lambda c: pl.BlockSpec((tm, ATT_W), lambda i: (i, c))
    gspec = pl.BlockSpec((None, 1, HEAD_DIM), lambda i: (l, 0, 0))
    return pl.pallas_call(
        _prep_lat_kernel, grid=(m // tm,),
        in_specs=[col(OFF_Q // ATT_W), col(OFF_K // ATT_W), col(OFF_V // ATT_W), gspec, gspec],
        out_specs=[col(0)] * 3, out_shape=[jax.ShapeDtypeStruct((m, ATT_W), BF16)] * 3,
        compiler_params=_params(("arbitrary",), 32), name="prep_lat",
    )(proj, proj, proj, qg3, kg3)


def _ctx_attn_kernel(q_ref, k_ref, v_ref, ga_ref, qg_ref, kg_ref, *refs, layer):
    o_ref, nk_ref, nv_ref = refs[-3:]
    if len(refs) == 3:
        for other in range(nk_ref.shape[0]):
            if other != layer:
                nk_ref[other] = jnp.zeros(nk_ref.shape[1:], F32)
                nv_ref[other] = jnp.zeros(nv_ref.shape[1:], F32)
        nk_ref, nv_ref = nk_ref.at[layer], nv_ref.at[layer]
    for h in range(N_HEADS):
        sl = slice(h * HEAD_DIM, (h + 1) * HEAD_DIM)
        q = _head_norm(q_ref[:, sl], qg_ref[...])
        k = _head_norm(k_ref[:, sl], kg_ref[...])
        v = v_ref[:, sl]
        nk_ref[:, sl] = k
        nv_ref[:, sl] = v
        s = _dot_nt(q.astype(BF16), k.astype(BF16)) * ATT_SCALE
        p = jnp.exp(s - jnp.max(s, axis=-1, keepdims=True))
        inv = 1.0 / jnp.sum(p, axis=-1, keepdims=True)
        o = _dot(p.astype(BF16), v.astype(BF16)) * inv
        o_ref[:, sl] = (o * _silu(ga_ref[:, sl])).astype(BF16)


def _ctx_attn(proj, qg3, kg3, new_kv, l, seq, depth):
    m = proj.shape[0]
    kv_shape = jax.ShapeDtypeStruct((m // seq, depth, seq, ATT_W), F32)
    col = lambda c: pl.BlockSpec((seq, ATT_W), lambda b: (b, c))
    gspec = pl.BlockSpec((None, 1, HEAD_DIM), lambda b: (l, 0, 0))
    if new_kv:
        kv_spec = pl.BlockSpec((None, None, seq, ATT_W), lambda b: (b, l, 0, 0))
    else:
        kv_spec = pl.BlockSpec((None, depth, seq, ATT_W), lambda b: (b, 0, 0, 0))
    return pl.pallas_call(
        functools.partial(_ctx_attn_kernel, layer=l), grid=(m // seq,),
        in_specs=[col(OFF_Q // ATT_W), col(OFF_K // ATT_W), col(OFF_V // ATT_W), col(OFF_GA // ATT_W),
                  gspec, gspec] + [pl.BlockSpec(memory_space=pl.ANY)] * len(new_kv),
        out_specs=[col(0), kv_spec, kv_spec],
        out_shape=[jax.ShapeDtypeStruct((m, ATT_W), BF16), kv_shape, kv_shape],
        input_output_aliases={6: 1, 7: 2} if new_kv else {},
        compiler_params=_params(("arbitrary",), 32), name="ctx_attn",
    )(proj, proj, proj, proj, qg3, kg3, *new_kv)


def _window_bias(rpb):
    depth, heads, n_dr, n_dc = rpb.shape
    c = np.arange(GRID_W)[:, None]
    kc = np.arange(GRID_W)[None, :]
    cs = np.clip(c - KW // 2, 0, GRID_W - KW)
    valid = (kc >= cs) & (kc < cs + KW)
    dc = np.clip(kc - c + KW - 1, 0, n_dc - 1).reshape(-1)
    onehot = (jnp.asarray(dc)[None, :] == jnp.arange(n_dc)[:, None]).astype(F32)
    cb = jnp.einsum("lhdm,mx->lhdx", rpb, onehot, precision=lax.Precision.HIGHEST)
    cb = jnp.where(jnp.asarray(valid), cb.reshape(depth, heads, n_dr, GRID_W, GRID_W), NEG)
    cb = jnp.pad(cb, ((0, 0), (0, 0), (1, 1), (0, 0), (0, 0)), constant_values=NEG)
    return jnp.concatenate([cb[:, :, :-1], cb[:, :, 1:]], axis=-1)


def _lat_attn_kernel(q_ref, k_ref, v_ref, ck_ref, cv_ref, bias_ref, ga_ref, o_ref,
                     ckb_ref, cvb_ref, *, rows, kh, rb, kr):
    g = pl.program_id(1)

    @pl.when(g == 0)
    def _():
        ckb_ref[...] = ck_ref[...].astype(BF16)
        cvb_ref[...] = cv_ref[...].astype(BF16)

    r0 = g * rb
    ws = jnp.clip(r0 - kh // 2, 0, rows - kr)
    span = pl.ds(pl.multiple_of(ws * GRID_W, GRID_W), kr * GRID_W)
    q_row = r0 + lax.broadcasted_iota(jnp.int32, (rb * GRID_W, 1), 0) // GRID_W
    k_row = ws + lax.broadcasted_iota(jnp.int32, (1, kr * GRID_W), 1) // GRID_W
    q_rs = jnp.clip(q_row - kh // 2, 0, rows - kh)
    in_window = (k_row >= q_rs) & (k_row < q_rs + kh)
    n_pairs = bias_ref.shape[1]
    for h in range(N_HEADS):
        sl = slice(h * HEAD_DIM, (h + 1) * HEAD_DIM)
        q = q_ref[:, sl]
        bias = jnp.concatenate(
            [jnp.concatenate(
                [bias_ref[h, jnp.clip(ws + 2 * j - (r0 + qi) + MAX_KH, 0, n_pairs - 1)]
                 for j in range(kr // 2)], axis=-1)
             for qi in range(rb)], axis=0)
        s_win = jnp.where(in_window, _dot_nt(q, k_ref[span, sl]) * ATT_SCALE + bias, NEG)
        s_ctx = _dot_nt(q, ckb_ref[:, sl]) * ATT_SCALE
        mx = jnp.maximum(jnp.max(s_win, axis=-1, keepdims=True),
                         jnp.max(s_ctx, axis=-1, keepdims=True))
        p_win = jnp.exp(s_win - mx)
        p_ctx = jnp.exp(s_ctx - mx)
        inv = 1.0 / (jnp.sum(p_win, axis=-1, keepdims=True) + jnp.sum(p_ctx, axis=-1, keepdims=True))
        o = (_dot(p_win.astype(BF16), v_ref[span, sl]) + _dot(p_ctx.astype(BF16), cvb_ref[:, sl])) * inv
        o_ref[:, sl] = (o * _silu(ga_ref[:, sl])).astype(BF16)


def _lat_attn(qn, kn, vb, cache_k4, cache_v4, bias, proj, l, seq):
    m = qn.shape[0]
    batch = m // seq
    rows = seq // GRID_W
    kh = min(MAX_KH, rows)
    rb = min(LAT_ROW_BLOCK, rows)
    kr = min(rows, kh + rb)
    assert rows % rb == 0 and kr % 2 == 0
    past = cache_k4.shape[2]
    qspec = pl.BlockSpec((rb * GRID_W, ATT_W), lambda b, g: (b * (rows // rb) + g, 0))
    kvspec = pl.BlockSpec((seq, ATT_W), lambda b, g: (b, 0))
    cspec = pl.BlockSpec((None, None, past, ATT_W), lambda b, g: (b, l, 0, 0))
    return pl.pallas_call(
        functools.partial(_lat_attn_kernel, rows=rows, kh=kh, rb=rb, kr=kr),
        grid=(batch, rows // rb),
        in_specs=[qspec, kvspec, kvspec, cspec, cspec,
                  pl.BlockSpec((None,) + bias.shape[1:], lambda b, g: (l, 0, 0, 0, 0)),
                  pl.BlockSpec((rb * GRID_W, ATT_W),
                               lambda b, g: (b * (rows // rb) + g, OFF_GA // ATT_W))],
        out_specs=qspec,
        out_shape=jax.ShapeDtypeStruct((m, ATT_W), BF16),
        scratch_shapes=[pltpu.VMEM((past, ATT_W), BF16), pltpu.VMEM((past, ATT_W), BF16)],
        compiler_params=_params(("arbitrary", "arbitrary"), 48), name="lat_attn",
    )(qn, kn, vb, cache_k4, cache_v4, bias, proj)


def _cos_sin_table(r, ncols, period, s_before_last=False):
    def factor(v):
        ang = ((r[..., None] * v) % period).astype(F32) * (2.0 * math.pi / period)
        return jnp.cos(ang), jnp.sin(ang)

    ca, sa = factor(jnp.arange(ncols // TRIG_BLOCK, dtype=jnp.int32) * TRIG_BLOCK)
    cb, sb = factor(jnp.arange(TRIG_BLOCK, dtype=jnp.int32))
    if s_before_last:
        ca, sa, cb, sb = (jnp.swapaxes(t, -1, -2) for t in (ca, sa, cb, sb))
        ca, sa, cb, sb = ca[..., :, None, :], sa[..., :, None, :], cb[..., None, :, :], sb[..., None, :, :]
        shape = r.shape[:-1] + (ncols, r.shape[-1])
    else:
        ca, sa, cb, sb = ca[..., :, None], sa[..., :, None], cb[..., None, :], sb[..., None, :]
        shape = r.shape + (ncols,)
    return (ca * cb - sa * sb).reshape(shape), (sa * cb + ca * sb).reshape(shape)


def _fnet_tables(seq):
    cl, sl = _cos_sin_table(jnp.arange(seq, dtype=jnp.int32), seq, seq)
    cc, sc = _cos_sin_table(jnp.arange(FNET_GDIM, dtype=jnp.int32), FNET_GDIM, FNET_GDIM)
    return (jnp.concatenate([cl, -sl], axis=1).astype(BF16), cc.astype(BF16), sc.astype(BF16))


def _fnet_kernel(u_ref, gb_ref, tl_ref, cc_ref, sc_ref, o_ref, t_ref, *, seq, tl):
    i = pl.program_id(1)

    @pl.when(i == 0)
    def _():
        for g in range(FNET_GROUPS):
            sl = slice(g * FNET_GDIM, (g + 1) * FNET_GDIM)
            ug = u_ref[:, sl].astype(BF16)
            t_ref[0:seq, sl] = _dot(ug, cc_ref[...]).astype(BF16)
            t_ref[seq:2 * seq, sl] = _dot(ug, sc_ref[...]).astype(BF16)

    y = _dot(tl_ref[...], t_ref[...]) * ((seq * FNET_GDIM) ** -0.5)
    o_ref[...] = (y * _silu(gb_ref[...])).astype(BF16)


def _fnet(proj, tables, seq):
    m = proj.shape[0]
    tab_l, cc, sc = tables
    tl = min(512, seq)
    nt = seq // tl
    return pl.pallas_call(
        functools.partial(_fnet_kernel, seq=seq, tl=tl),
        grid=(m // seq, nt),
        in_specs=[
            pl.BlockSpec((seq, FNET_W), lambda b, i: (b, OFF_UB // FNET_W)),
            pl.BlockSpec((tl, FNET_W), lambda b, i: (b * nt + i, OFF_GB // FNET_W)),
            pl.BlockSpec((tl, 2 * seq), lambda b, i: (i, 0)),
            pl.BlockSpec((FNET_GDIM, FNET_GDIM), lambda b, i: (0, 0)),
            pl.BlockSpec((FNET_GDIM, FNET_GDIM), lambda b, i: (0, 0)),
        ],
        out_specs=pl.BlockSpec((tl, FNET_W), lambda b, i: (b * nt + i, 0)),
        out_shape=jax.ShapeDtypeStruct((m, FNET_W), BF16),
        scratch_shapes=[pltpu.VMEM((2 * seq, FNET_W), BF16)],
        compiler_params=_params(("arbitrary", "arbitrary"), 40), name="fnet",
    )(proj, proj, tab_l, cc, sc)


def _hyena_tiling(seq):
    if seq > 1024:
        return min(FREQ_CHUNK // 2, seq), HY_W // 2, 2
    return min(FREQ_CHUNK, seq), HY_W, 4


def _hyena_tables(seq):
    fc = _hyena_tiling(seq)[0]
    nch = seq // fc
    k = jnp.arange(2 * fc, dtype=jnp.int32)
    f = jnp.arange(nch, dtype=jnp.int32)[:, None] * fc + (k % fc)[None, :]
    is_sec = (k >= fc)[None, :, None]
    nyq = is_sec & (f == 0)[:, :, None]
    sgn = (1 - 2 * (jnp.arange(seq, dtype=jnp.int32) % 2)).astype(F32)

    def table(transposed):
        c, s = _cos_sin_table(f, seq, 2 * seq, s_before_last=transposed)
        axes = (0, 2, 1) if transposed else (0, 1, 2)
        sec = jnp.where(nyq.transpose(axes), sgn.reshape((1, 1, seq)).transpose(axes), -s)
        return jnp.where(is_sec.transpose(axes), sec, c).astype(BF16)

    return table(False), table(True)


def _filter_features(seq):
    t = jnp.linspace(0.0, 1.0, seq, dtype=F32)[:, None]
    bands = (FILTER_EMB - 1) // 2
    w = (2.0 * math.pi / seq) * jnp.arange(seq, dtype=F32)[:, None]
    f = jnp.linspace(1e-4, bands - 1, bands, dtype=F32)[None, :]
    z = jnp.concatenate([t, jnp.cos(w * f), -jnp.sin(w * f)], axis=-1)
    z = jnp.pad(z, ((0, 0), (0, 128 - FILTER_EMB)))
    deltas = jnp.abs(jnp.linspace(MIN_DECAY, MAX_DECAY, HY_W, dtype=F32))[None, :]
    return z, t, deltas


def _dot_f32(a, b):
    return jnp.dot(a, b, preferred_element_type=F32, precision=lax.Precision.HIGHEST)


def _filter_kernel(z_ref, t_ref, dl_ref, w1_ref, b1_ref, fr_ref, w2_ref, b2_ref, w3_ref,
                   fwd_ref, o_ref, h_ref, sd_ref, nyq_ref, *, seq, fc, rows):
    j = pl.program_id(0)
    nblk = 2 * HYENA_ORDER
    nrc = seq // rows

    @pl.when(j == 0)
    def _():
        fr = fr_ref[...]

        def taps(r, acc):
            sl = pl.ds(pl.multiple_of(r * rows, rows), rows)
            h = jnp.sin(fr * (_dot_f32(z_ref[sl, :], w1_ref[...]) + b1_ref[...]))
            h = jnp.sin(fr * (_dot_f32(h, w2_ref[...]) + b2_ref[...]))
            h = _dot_f32(h, w3_ref[...])
            decay = jnp.exp(-t_ref[sl, :] * dl_ref[...])
            h = h * jnp.concatenate([decay] * nblk, axis=-1)
            h_ref[sl, :] = h
            return acc + jnp.sum(jnp.abs(h), axis=0, keepdims=True)

        tot = lax.fori_loop(0, nrc, taps, jnp.zeros((1, nblk * HY_W), F32))
        inv = 1.0 / (tot + EPS)
        sgn = (1 - 2 * (lax.broadcasted_iota(jnp.int32, (rows, 1), 0) % 2)).astype(F32)

        def fold(r, acc):
            sl = pl.ds(pl.multiple_of(r * rows, rows), rows)
            h = h_ref[sl, :] * inv
            pos = r * rows + lax.broadcasted_iota(jnp.int32, (rows, 1), 0)
            sums = []
            for o in range(HYENA_ORDER):
                fw = h[:, (2 * o) * HY_W:(2 * o + 1) * HY_W]
                bw = jnp.where(pos == 0, 0.0, h[:, (2 * o + 1) * HY_W:(2 * o + 2) * HY_W])
                sums.append(fw + bw)
                sd_ref[0, sl, o * HY_W:(o + 1) * HY_W] = (fw + bw).astype(BF16)
                sd_ref[1, sl, o * HY_W:(o + 1) * HY_W] = (fw - bw).astype(BF16)
            return acc + jnp.sum(jnp.concatenate(sums, axis=-1) * sgn, axis=0, keepdims=True)

        nyq_ref[...] = lax.fori_loop(0, nrc, fold, jnp.zeros((1, HYENA_ORDER * HY_W), F32))

    re = _dot(fwd_ref[0:fc, :], sd_ref[0])
    sec = _dot(fwd_ref[fc:2 * fc, :], sd_ref[1])
    first = (lax.broadcasted_iota(jnp.int32, (fc, 1), 0) == 0) & (j == 0)
    sec = jnp.where(first, nyq_ref[...], sec)
    weight = jnp.where(first, 0.5 / seq, 1.0 / seq)
    o_ref[0:fc, :] = re * weight
    o_ref[fc:2 * fc, :] = sec * weight


def _filter(feats, fwd, w1p, b1, fr, w2, b2, w3, l, seq):
    z, t, deltas = feats
    nch, fc2, _ = fwd.shape
    fc = fc2 // 2
    width = HYENA_ORDER * HY_W
    rows = min(256, seq)
    full = lambda a: pl.BlockSpec(a.shape, lambda j: (0,) * a.ndim)
    lay = lambda a: pl.BlockSpec((None,) + a.shape[1:], lambda j: (l,) + (0,) * (a.ndim - 1))
    return pl.pallas_call(
        functools.partial(_filter_kernel, seq=seq, fc=fc, rows=rows),
        grid=(nch,),
        in_specs=[full(z), full(t), full(deltas), lay(w1p), lay(b1), lay(fr), lay(w2), lay(b2), lay(w3),
                  pl.BlockSpec((None, 2 * fc, seq), lambda j: (j, 0, 0))],
        out_specs=pl.BlockSpec((None, 2 * fc, width), lambda j: (j, 0, 0)),
        out_shape=jax.ShapeDtypeStruct((nch, 2 * fc, width), F32),
        scratch_shapes=[pltpu.VMEM((seq, 2 * width), F32),
                        pltpu.VMEM((2, seq, width), BF16),
                        pltpu.VMEM((1, width), F32)],
        compiler_params=_params(("arbitrary",), 48), name="hyena_filter",
    )(z, t, deltas, w1p, b1, fr, w2, b2, w3, fwd)


def _hyena_kernel(v_ref, x1_ref, x2_ref, gc_ref, cw_ref, cb_ref, hb_ref, fwd_ref, inv_ref, kf_ref,
                  o_ref, z_ref, zb_ref, y_ref, *, seq, fc, sb, tc):
    o = pl.program_id(2)
    j = pl.program_id(3)
    last_j = pl.num_programs(3) - 1
    pos = lax.broadcasted_iota(jnp.int32, (seq, 1), 0)

    def short_conv(ref, k, s):
        x = ref[s * seq:(s + 1) * seq, :]
        prev = jnp.where(pos == 0, 0.0, pltpu.roll(x, 1, 0))
        nxt = jnp.where(pos == seq - 1, 0.0, pltpu.roll(x, seq - 1, 0))
        return (prev * cw_ref[0, k:k + 1, :] + x * cw_ref[1, k:k + 1, :]
                + nxt * cw_ref[2, k:k + 1, :] + cb_ref[k:k + 1, :])

    @pl.when((o == 0) & (j == 0))
    def _():
        for s in range(sb):
            lanes = slice(s * tc, (s + 1) * tc)
            z = short_conv(v_ref, 0, s)
            z_ref[:, lanes] = z
            zb_ref[:, lanes] = z.astype(BF16)
        y_ref[...] = jnp.zeros_like(y_ref)

    @pl.when((o == 1) & (j == 0))
    def _():
        for s in range(sb):
            lanes = slice(s * tc, (s + 1) * tc)
            z = short_conv(x1_ref, 1, s) * (y_ref[:, lanes] + hb_ref[0:1, :] * z_ref[:, lanes])
            z_ref[:, lanes] = z
            zb_ref[:, lanes] = z.astype(BF16)
        y_ref[...] = jnp.zeros_like(y_ref)

    zf = _dot(fwd_ref[...], zb_ref[...])
    kr, ks = kf_ref[0:fc, :], kf_ref[fc:2 * fc, :]
    nyq = (lax.broadcasted_iota(jnp.int32, (fc, 1), 0) == 0) & (j == 0)
    parts = []
    for s in range(sb):
        lanes = slice(s * tc, (s + 1) * tc)
        zr, zs = zf[0:fc, lanes], zf[fc:2 * fc, lanes]
        ss = zs * ks
        yr = zr * kr - jnp.where(nyq, 0.0, ss)
        ys = jnp.where(nyq, ss, zr * ks + zs * kr)
        parts.append(jnp.concatenate([yr, ys], axis=0).astype(BF16))
    y_ref[...] += _dot(inv_ref[...], jnp.concatenate(parts, axis=-1))

    @pl.when((o == 1) & (j == last_j))
    def _():
        for s in range(sb):
            lanes = slice(s * tc, (s + 1) * tc)
            z = short_conv(x2_ref, 2, s) * (y_ref[:, lanes] + hb_ref[1:2, :] * z_ref[:, lanes])
            o_ref[s * seq:(s + 1) * seq, :] = (z * _silu(gc_ref[s * seq:(s + 1) * seq, :])).astype(BF16)


def _hyena(proj, cw, cb, hb, fwd, inv, kf, seq):
    m = proj.shape[0]
    nch, fc2, _ = fwd.shape
    fc = fc2 // 2
    _, tc, sb = _hyena_tiling(seq)
    sb = min(sb, m // seq)
    nct = HY_W // tc
    hc0 = OFF_HC // tc
    once = pl.Buffered(1) if seq > 1024 else None
    col = lambda k: pl.BlockSpec((sb * seq, tc), lambda b, c, o, j: (b, hc0 + k * nct + c),
                                 pipeline_mode=once)
    return pl.pallas_call(
        functools.partial(_hyena_kernel, seq=seq, fc=fc, sb=sb, tc=tc),
        grid=(m // (sb * seq), nct, HYENA_ORDER, nch),
        in_specs=[
            col(0), col(1), col(2),
            pl.BlockSpec((sb * seq, tc), lambda b, c, o, j: (b, OFF_GC // tc + c), pipeline_mode=once),
            pl.BlockSpec((3, 3, tc), lambda b, c, o, j: (0, 0, c)),
            pl.BlockSpec((3, tc), lambda b, c, o, j: (0, c)),
            pl.BlockSpec((HYENA_ORDER, tc), lambda b, c, o, j: (0, c)),
            pl.BlockSpec((None, 2 * fc, seq), lambda b, c, o, j: (j, 0, 0)),
            pl.BlockSpec((None, seq, 2 * fc), lambda b, c, o, j: (j, 0, 0)),
            pl.BlockSpec((None, 2 * fc, tc), lambda b, c, o, j: (j, 0, o * nct + c)),
        ],
        out_specs=pl.BlockSpec((sb * seq, tc), lambda b, c, o, j: (b, c)),
        out_shape=jax.ShapeDtypeStruct((m, HY_W), BF16),
        scratch_shapes=[pltpu.VMEM((seq, sb * tc), F32), pltpu.VMEM((seq, sb * tc), BF16),
                        pltpu.VMEM((seq, sb * tc), F32)],
        compiler_params=_params(("arbitrary",) * 4, 56), name="hyena",
    )(proj, proj, proj, proj, cw, cb, hb, fwd, inv, kf)


def _merge_kernel(ya_ref, yb_ref, yc_ref, ga_ref, gb_ref, gc_ref, wa_ref, wb_ref, wc_ref, *refs):
    o_ref = _out_ref(refs)
    acc = ga_ref[...].astype(F32) * _dot(ya_ref[...], wa_ref[...])
    acc += gb_ref[...].astype(F32) * _dot(yb_ref[...], wb_ref[...])
    acc += gc_ref[...].astype(F32) * _dot(yc_ref[...], wc_ref[...])
    o_ref[...] = acc.astype(BF16)


def _merge(ya, yb, yc, gates, w_bf, l, seq, w32=None):
    m = ya.shape[0]
    tm = min(ROW_TILE, m)
    tn = COL_TILE
    d = D_MODEL
    row = lambda w: pl.BlockSpec((tm, w), lambda i, j: (i, 0))
    gate = lambda k: pl.BlockSpec((tm, tn), lambda i, j: (i, k * d // tn + j))
    nxt_args, nxt_in, nxt_out, nxt_shape = _next_weight_specs(w32, l, m // tm, tn)
    return pl.pallas_call(
        _merge_kernel, grid=(m // tm, d // tn),
        in_specs=[row(ATT_W), row(FNET_W), row(HY_W), gate(0), gate(1), gate(2),
                  pl.BlockSpec((ATT_W, tn), lambda i, j: (0, j)),
                  pl.BlockSpec((FNET_W, tn), lambda i, j: (ATT_W // FNET_W, j)),
                  pl.BlockSpec((HY_W, tn), lambda i, j: ((ATT_W + FNET_W) // HY_W, j))] + nxt_in,
        out_specs=[pl.BlockSpec((tm, tn), lambda i, j: (i, j))] + nxt_out,
        out_shape=[jax.ShapeDtypeStruct((m, d), BF16)] + nxt_shape,
        compiler_params=_params(("arbitrary", "arbitrary"), 40), name="merge",
    )(ya, yb, yc, gates, gates, gates, w_bf, w_bf, w_bf, *nxt_args)


def _outproj_kernel(mg_ref, w_ref, x_ref, ada_ref, *refs):
    o_ref = _out_ref(refs)
    o_ref[...] = x_ref[...] + ada_ref[...] * _dot(mg_ref[...], w_ref[...])


def _outproj(merged, x2d, ada4, ada_row, w_bf, l, seq, w32=None):
    m, d = x2d.shape
    tm = min(ROW_TILE, m)
    tn = COL_TILE
    nxt_args, nxt_in, nxt_out, nxt_shape = _next_weight_specs(w32, l, m // tm, tn)
    return pl.pallas_call(
        _outproj_kernel, grid=(m // tm, d // tn),
        in_specs=[pl.BlockSpec((tm, d), lambda i, j: (i, 0)),
                  pl.BlockSpec((d, tn), lambda i, j: (0, j)),
                  pl.BlockSpec((tm, tn), lambda i, j: (i, j)),
                  pl.BlockSpec((None, None, 1, tn),
                               lambda i, j: (l, ada_row(i * tm), 0, 2 * d // tn + j))] + nxt_in,
        out_specs=[pl.BlockSpec((tm, tn), lambda i, j: (i, j))] + nxt_out,
        out_shape=[jax.ShapeDtypeStruct((m, d), F32)] + nxt_shape,
        compiler_params=_params(("arbitrary", "arbitrary"), 44), name="outproj",
    )(merged, w_bf, x2d, ada4, *nxt_args)


def kernel(x_prompt, x_sample, cache_k, cache_v, c, c_ctx, norm_g, w_ada, b_ada, w_in, q_norm_g, k_norm_g, rpb, conv_w, conv_b, f_w1, f_b1, f_freq, f_w2, f_b2, f_w3, hy_bias, w_br, w_out):
    batch, seq, d = x_prompt.shape
    dec_batch, dec_seq, _ = x_sample.shape
    depth = norm_g.shape[0]
    past = cache_k.shape[2]

    ada4 = _ada(jnp.concatenate([c_ctx[None, :], c], axis=0), w_ada, b_ada)
    ada4 = ada4.reshape(depth, ADA_ROWS, 1, 3 * d)
    ctx_row = lambda tok: 0
    lat_row = lambda tok: 1 + tok // dec_seq

    w_in_bf, w_br_bf, w_out_bf = (w[0].astype(BF16) for w in (w_in, w_br, w_out))
    norm_g3 = norm_g.reshape(depth, 1, d)
    qg3 = q_norm_g.reshape(depth, 1, HEAD_DIM)
    kg3 = k_norm_g.reshape(depth, 1, HEAD_DIM)
    cache_k4 = cache_k.reshape(dec_batch, depth, past, ATT_W)
    cache_v4 = cache_v.reshape(dec_batch, depth, past, ATT_W)
    w1p = jnp.pad(f_w1, ((0, 0), (0, 128 - FILTER_EMB), (0, 0)))
    b1 = f_b1.reshape(depth, 1, FILTER_FF)
    fr = f_freq.reshape(depth, 1, FILTER_FF)
    b2 = f_b2.reshape(depth, 1, FILTER_FF)
    win_bias = _window_bias(rpb)

    groups = []
    for x, s, row in ((x_prompt, seq, ctx_row), (x_sample, dec_seq, lat_row)):
        fwd, inv = _hyena_tables(s)
        groups.append(dict(x=x.reshape(-1, d), seq=s, row=row, fnet=_fnet_tables(s),
                           fwd=fwd, inv=inv, feats=_filter_features(s)))

    new_kv = ()
    for l in range(depth):
        cw = conv_w[l].reshape(3, 3, HY_W)
        cb = conv_b[l].reshape(3, HY_W)
        hb = hy_bias[l]
        for gi, g in enumerate(groups):
            s = g["seq"]
            side = gi == 0
            proj, gates, *nxt_in = _inproj(g["x"], norm_g3, ada4, g["row"], w_in_bf, l, s,
                                           w_in if side else None)
            if gi == 0:
                ya, *new_kv = _ctx_attn(proj, qg3, kg3, new_kv, l, s, depth)
            else:
                qn, kn, vb = _prep_lat(proj, qg3, kg3, l)
                ya = _lat_attn(qn, kn, vb, cache_k4, cache_v4, win_bias, proj, l, s)
            yb = _fnet(proj, g["fnet"], s)
            kf = _filter(g["feats"], g["fwd"], w1p, b1, fr, f_w2, b2, f_w3, l, s)
            yc = _hyena(proj, cw, cb, hb, g["fwd"], g["inv"], kf, s)
            merged, *nxt_br = _merge(ya, yb, yc, gates, w_br_bf, l, s, w_br if side else None)
            g["x"], *nxt_out = _outproj(merged, g["x"], ada4, g["row"], w_out_bf, l, s,
                                        w_out if side else None)
            if side:
                nxt = (nxt_in, nxt_br, nxt_out)
        if l + 1 < depth:
            (w_in_bf,), (w_br_bf,), (w_out_bf,) = nxt

    nk, nv = new_kv
    return (groups[0]["x"].reshape(batch, seq, d),
            groups[1]["x"].reshape(dec_batch, dec_seq, d),
            nk.reshape(batch, depth, seq, N_HEADS, HEAD_DIM),
            nv.reshape(batch, depth, seq, N_HEADS, HEAD_DIM))
```

```python
import functools
import math

import jax
import jax.numpy as jnp
import numpy as np
from jax import lax
from jax.experimental import pallas as pl
from jax.experimental.pallas import tpu as pltpu

F32 = jnp.float32
BF16 = jnp.bfloat16

D_MODEL = 2048
GRID_W = 64
N_HEADS = 8
HEAD_DIM = 128
ATT_W = N_HEADS * HEAD_DIM
MAX_KH = 8
KW = 16
FNET_GROUPS = 4
FNET_GDIM = 128
FNET_W = FNET_GROUPS * FNET_GDIM
HY_W = 512
HYENA_ORDER = 2
FILTER_EMB = 33
FILTER_FF = 64
MIN_DECAY = math.log(1e-2) / 1.5
MAX_DECAY = math.log(1e-2) / 0.3
OFF_Q = 0
OFF_K = OFF_Q + ATT_W
OFF_V = OFF_K + ATT_W
OFF_GA = OFF_V + ATT_W
OFF_UB = OFF_GA + ATT_W
OFF_GB = OFF_UB + FNET_W
OFF_HC = OFF_GB + FNET_W
OFF_GC = OFF_HC + 3 * HY_W
OFF_MG = OFF_GC + HY_W
N_IN = OFF_MG + 3 * D_MODEL
EPS = 1e-6
NEG = -1e30
ATT_SCALE = HEAD_DIM ** -0.5

LANES = 128
SUBLANES = 8
ADA_ROWS = 8
FREQ_CHUNK = 512
TRIG_BLOCK = 64
LAT_ROW_BLOCK = 4
ROW_TILE = 1024
COL_TILE = 1024
MIB = 1024 * 1024


def _params(semantics, vmem_mib):
    return pltpu.CompilerParams(dimension_semantics=semantics,
                                vmem_limit_bytes=vmem_mib * MIB)


def _silu(x):
    return x * jax.nn.sigmoid(x)


def _dot(a, b):
    return jnp.dot(a, b, preferred_element_type=F32)


def _dot_nt(a, b):
    return lax.dot_general(a, b, (((1,), (1,)), ((), ())), preferred_element_type=F32)


def _ada_kernel(cv_ref, w_ref, b_ref, o_ref, s_ref):
    n_rows, d, _ = cv_ref.shape
    tn = w_ref.shape[1]

    @pl.when((pl.program_id(0) == 0) & (pl.program_id(1) == 0))
    def _():
        s_ref[...] = _silu(cv_ref[...])

    def body(i, acc):
        rows = pl.ds(pl.multiple_of(i * SUBLANES, SUBLANES), SUBLANES)
        w = w_ref[rows, :]
        return tuple(acc[r] + w * jnp.concatenate([s_ref[r, rows, :]] * (tn // LANES), axis=-1)
                     for r in range(n_rows))

    zero = jnp.zeros((SUBLANES, tn), F32)
    acc = lax.fori_loop(0, d // SUBLANES, body, (zero,) * n_rows, unroll=4)
    o_ref[...] = jnp.zeros(o_ref.shape, F32)
    for r in range(n_rows):
        o_ref[r:r + 1, :] = jnp.sum(acc[r], axis=0, keepdims=True) + b_ref[...]


def _ada(conds, w_ada, b_ada):
    depth, d, n = w_ada.shape
    rows = conds.shape[0]
    assert rows <= ADA_ROWS
    tn = COL_TILE
    cv = jnp.broadcast_to(conds[:, :, None], (rows, d, LANES))
    return pl.pallas_call(
        _ada_kernel,
        grid=(depth, n // tn),
        in_specs=[
            pl.BlockSpec((rows, d, LANES), lambda l, j: (0, 0, 0)),
            pl.BlockSpec((None, d, tn), lambda l, j: (l, 0, j)),
            pl.BlockSpec((None, 1, tn), lambda l, j: (l, 0, j)),
        ],
        out_specs=pl.BlockSpec((None, ADA_ROWS, tn), lambda l, j: (l, 0, j)),
        out_shape=jax.ShapeDtypeStruct((depth, ADA_ROWS, n), F32),
        scratch_shapes=[pltpu.VMEM((rows, d, LANES), F32)],
        compiler_params=_params(("arbitrary", "arbitrary"), 32),
        name="ada",
    )(cv, w_ada, b_ada.reshape(depth, 1, n))


def _cast_next(src_ref, dst_ref):
    dst_ref[...] = src_ref[...].astype(BF16)


def _out_ref(refs):
    if len(refs) == 1:
        return refs[0]
    src_ref, o_ref, dst_ref = refs
    _cast_next(src_ref, dst_ref)
    return o_ref


def _next_weight_specs(w32, l, n_i, tn):
    if w32 is None or l + 1 >= w32.shape[0]:
        return [], [], [], []
    _, rows, cols = w32.shape
    rb = rows // n_i
    return ([w32], [pl.BlockSpec((None, rb, tn), lambda i, j: (l + 1, i, j))],
            [pl.BlockSpec((rb, tn), lambda i, j: (i, j))], [jax.ShapeDtypeStruct((rows, cols), BF16)])


def _inproj_kernel(x_ref, g_ref, ada_ref, w_ref, *refs, rows, n_plain):
    if len(refs) == 5:
        nxt32_ref, o_ref, gate_ref, nxt_ref, h_ref = refs
        _cast_next(nxt32_ref, nxt_ref)
    else:
        o_ref, gate_ref, h_ref = refs
    j = pl.program_id(1)

    @pl.when(j == 0)
    def _():
        g = g_ref[...]
        shift = ada_ref[:, 0:D_MODEL]
        scale1 = 1.0 + ada_ref[:, D_MODEL:2 * D_MODEL]

        def body(r, carry):
            sl = pl.ds(pl.multiple_of(r * rows, rows), rows)
            x = x_ref[sl, :]
            ms = jnp.mean(x * x, axis=-1, keepdims=True)
            y = x * lax.rsqrt(ms + EPS) * g
            h_ref[sl, :] = (y * scale1 + shift).astype(BF16)
            return carry

        lax.fori_loop(0, x_ref.shape[0] // rows, body, 0)

    @pl.when(j < n_plain)
    def _():
        o_ref[...] = _dot(h_ref[...], w_ref[...])

    @pl.when(j >= n_plain)
    def _():
        gate_ref[...] = jax.nn.sigmoid(_dot(h_ref[...], w_ref[...])).astype(BF16)


def _inproj(x2d, norm_g3, ada4, ada_row, w_bf, l, seq, w32=None):
    m, d = x2d.shape
    tm = min(ROW_TILE, m)
    tn = COL_TILE
    n_plain = OFF_MG // tn
    nxt_args, nxt_in, nxt_out, nxt_shape = _next_weight_specs(w32, l, m // tm, tn)
    return pl.pallas_call(
        functools.partial(_inproj_kernel, rows=128, n_plain=n_plain),
        grid=(m // tm, N_IN // tn),
        in_specs=[
            pl.BlockSpec((tm, d), lambda i, j: (i, 0)),
            pl.BlockSpec((None, 1, d), lambda i, j: (l, 0, 0)),
            pl.BlockSpec((None, None, 1, 3 * d), lambda i, j: (l, ada_row(i * tm), 0, 0)),
            pl.BlockSpec((d, tn), lambda i, j: (0, j)),
        ] + nxt_in,
        out_specs=[pl.BlockSpec((tm, tn), lambda i, j: (i, jnp.minimum(j, n_plain - 1))),
                   pl.BlockSpec((tm, tn), lambda i, j: (i, jnp.maximum(j - n_plain, 0)))] + nxt_out,
        out_shape=[jax.ShapeDtypeStruct((m, OFF_MG), F32),
                   jax.ShapeDtypeStruct((m, N_IN - OFF_MG), BF16)] + nxt_shape,
        scratch_shapes=[pltpu.VMEM((tm, d), BF16)],
        compiler_params=_params(("arbitrary", "arbitrary"), 52),
        name="inproj",
    )(x2d, norm_g3, ada4, w_bf, *nxt_args)


def _head_norm(x, g):
    ms = jnp.mean(x * x, axis=-1, keepdims=True)
    return x * lax.rsqrt(ms + EPS) * g


def _prep_lat_kernel(q_ref, k_ref, v_ref, qg_ref, kg_ref, qn_ref, kn_ref, vb_ref):
    for h in range(N_HEADS):
        sl = slice(h * HEAD_DIM, (h + 1) * HEAD_DIM)
        qn_ref[:, sl] = _head_norm(q_ref[:, sl], qg_ref[...]).astype(BF16)
        kn_ref[:, sl] = _head_norm(k_ref[:, sl], kg_ref[...]).astype(BF16)
    vb_ref[...] = v_ref[...].astype(BF16)


def _prep_lat(proj, qg3, kg3, l):
    m = proj.shape[0]
    tm = min(256, m)
    col = lambda c: pl.BlockSpec((tm, ATT_W), lambda i: (i, c))
    gspec = pl.BlockSpec((None, 1, HEAD_DIM), lambda i: (l, 0, 0))
    return pl.pallas_call(
        _prep_lat_kernel, grid=(m // tm,),
        in_specs=[col(OFF_Q // ATT_W), col(OFF_K // ATT_W), col(OFF_V // ATT_W), gspec, gspec],
        out_specs=[col(0)] * 3, out_shape=[jax.ShapeDtypeStruct((m, ATT_W), BF16)] * 3,
        compiler_params=_params(("arbitrary",), 32), name="prep_lat",
    )(proj, proj, proj, qg3, kg3)


def _ctx_attn_kernel(q_ref, k_ref, v_ref, ga_ref, qg_ref, kg_ref, *refs, layer):
    o_ref, nk_ref, nv_ref = refs[-3:]
    if len(refs) == 3:
        for other in range(nk_ref.shape[0]):
            if other != layer:
                nk_ref[other] = jnp.zeros(nk_ref.shape[1:], F32)
                nv_ref[other] = jnp.zeros(nv_ref.shape[1:], F32)
        nk_ref, nv_ref = nk_ref.at[layer], nv_ref.at[layer]
    for h in range(N_HEADS):
        sl = slice(h * HEAD_DIM, (h + 1) * HEAD_DIM)
        q = _head_norm(q_ref[:, sl], qg_ref[...])
        k = _head_norm(k_ref[:, sl], kg_ref[...])
        v = v_ref[:, sl]
        nk_ref[:, sl] = k
        nv_ref[:, sl] = v
        s = _dot_nt(q.astype(BF16), k.astype(BF16)) * ATT_SCALE
        p = jnp.exp(s - jnp.max(s, axis=-1, keepdims=True))
        inv = 1.0 / jnp.sum(p, axis=-1, keepdims=True)
        o = _dot(p.astype(BF16), v.astype(BF16)) * inv
        o_ref[:, sl] = (o * _silu(ga_ref[:, sl])).astype(BF16)


def _ctx_attn(proj, qg3, kg3, new_kv, l, seq, depth):
    m = proj.shape[0]
    kv_shape = jax.ShapeDtypeStruct((m // seq, depth, seq, ATT_W), F32)
    col = lambda c: pl.BlockSpec((seq, ATT_W), lambda b: (b, c))
    gspec = pl.BlockSpec((None, 1, HEAD_DIM), lambda b: (l, 0, 0))
    if new_kv:
        kv_spec = pl.BlockSpec((None, None, seq, ATT_W), lambda b: (b, l, 0, 0))
    else:
        kv_spec = pl.BlockSpec((None, depth, seq, ATT_W), lambda b: (b, 0, 0, 0))
    return pl.pallas_call(
        functools.partial(_ctx_attn_kernel, layer=l), grid=(m // seq,),
        in_specs=[col(OFF_Q // ATT_W), col(OFF_K // ATT_W), col(OFF_V // ATT_W), col(OFF_GA // ATT_W),
                  gspec, gspec] + [pl.BlockSpec(memory_space=pl.ANY)] * len(new_kv),
        out_specs=[col(0), kv_spec, kv_spec],
        out_shape=[jax.ShapeDtypeStruct((m, ATT_W), BF16), kv_shape, kv_shape],
        input_output_aliases={6: 1, 7: 2} if new_kv else {},
        compiler_params=_params(("arbitrary",), 32), name="ctx_attn",
    )(proj, proj, proj, proj, qg3, kg3, *new_kv)


def _window_bias(rpb):
    depth, heads, n_dr, n_dc = rpb.shape
    c = np.arange(GRID_W)[:, None]
    kc = np.arange(GRID_W)[None, :]
    cs = np.clip(c - KW // 2, 0, GRID_W - KW)
    valid = (kc >= cs) & (kc < cs + KW)
    dc = np.clip(kc - c + KW - 1, 0, n_dc - 1).reshape(-1)
    onehot = (jnp.asarray(dc)[None, :] == jnp.arange(n_dc)[:, None]).astype(F32)
    cb = jnp.einsum("lhdm,mx->lhdx", rpb, onehot, precision=lax.Precision.HIGHEST)
    cb = jnp.where(jnp.asarray(valid), cb.reshape(depth, heads, n_dr, GRID_W, GRID_W), NEG)
    cb = jnp.pad(cb, ((0, 0), (0, 0), (1, 1), (0, 0), (0, 0)), constant_values=NEG)
    return jnp.concatenate([cb[:, :, :-1], cb[:, :, 1:]], axis=-1)


def _lat_attn_kernel(q_ref, k_ref, v_ref, ck_ref, cv_ref, bias_ref, ga_ref, o_ref,
                     ckb_ref, cvb_ref, *, rows, kh, rb, kr):
    g = pl.program_id(1)

    @pl.when(g == 0)
    def _():
        ckb_ref[...] = ck_ref[...].astype(BF16)
        cvb_ref[...] = cv_ref[...].astype(BF16)

    r0 = g * rb
    ws = jnp.clip(r0 - kh // 2, 0, rows - kr)
    span = pl.ds(pl.multiple_of(ws * GRID_W, GRID_W), kr * GRID_W)
    q_row = r0 + lax.broadcasted_iota(jnp.int32, (rb * GRID_W, 1), 0) // GRID_W
    k_row = ws + lax.broadcasted_iota(jnp.int32, (1, kr * GRID_W), 1) // GRID_W
    q_rs = jnp.clip(q_row - kh // 2, 0, rows - kh)
    in_window = (k_row >= q_rs) & (k_row < q_rs + kh)
    n_pairs = bias_ref.shape[1]
    for h in range(N_HEADS):
        sl = slice(h * HEAD_DIM, (h + 1) * HEAD_DIM)
        q = q_ref[:, sl]
        bias = jnp.concatenate(
            [jnp.concatenate(
                [bias_ref[h, jnp.clip(ws + 2 * j - (r0 + qi) + MAX_KH, 0, n_pairs - 1)]
                 for j in range(kr // 2)], axis=-1)
             for qi in range(rb)], axis=0)
        s_win = jnp.where(in_window, _dot_nt(q, k_ref[span, sl]) * ATT_SCALE + bias, NEG)
        s_ctx = _dot_nt(q, ckb_ref[:, sl]) * ATT_SCALE
        mx = jnp.maximum(jnp.max(s_win, axis=-1, keepdims=True),
                         jnp.max(s_ctx, axis=-1, keepdims=True))
        p_win = jnp.exp(s_win - mx)
        p_ctx = jnp.exp(s_ctx - mx)
        inv = 1.0 / (jnp.sum(p_win, axis=-1, keepdims=True) + jnp.sum(p_ctx, axis=-1, keepdims=True))
        o = (_dot(p_win.astype(BF16), v_ref[span, sl]) + _dot(p_ctx.astype(BF16), cvb_ref[:, sl])) * inv
        o_ref[:, sl] = (o * _silu(ga_ref[:, sl])).astype(BF16)


def _lat_attn(qn, kn, vb, cache_k4, cache_v4, bias, proj, l, seq):
    m = qn.shape[0]
    batch = m // seq
    rows = seq // GRID_W
    kh = min(MAX_KH, rows)
    rb = min(LAT_ROW_BLOCK, rows)
    kr = min(rows, kh + rb)
    assert rows % rb == 0 and kr % 2 == 0
    past = cache_k4.shape[2]
    qspec = pl.BlockSpec((rb * GRID_W, ATT_W), lambda b, g: (b * (rows // rb) + g, 0))
    kvspec = pl.BlockSpec((seq, ATT_W), lambda b, g: (b, 0))
    cspec = pl.BlockSpec((None, None, past, ATT_W), lambda b, g: (b, l, 0, 0))
    return pl.pallas_call(
        functools.partial(_lat_attn_kernel, rows=rows, kh=kh, rb=rb, kr=kr),
        grid=(batch, rows // rb),
        in_specs=[qspec, kvspec, kvspec, cspec, cspec,
                  pl.BlockSpec((None,) + bias.shape[1:], lambda b, g: (l, 0, 0, 0, 0)),
                  pl.BlockSpec((rb * GRID_W, ATT_W),
                               lambda b, g: (b * (rows // rb) + g, OFF_GA // ATT_W))],
        out_specs=qspec,
        out_shape=jax.ShapeDtypeStruct((m, ATT_W), BF16),
        scratch_shapes=[pltpu.VMEM((past, ATT_W), BF16), pltpu.VMEM((past, ATT_W), BF16)],
        compiler_params=_params(("arbitrary", "arbitrary"), 48), name="lat_attn",
    )(qn, kn, vb, cache_k4, cache_v4, bias, proj)


def _cos_sin_table(r, ncols, period, s_before_last=False):
    def factor(v):
        ang = ((r[..., None] * v) % period).astype(F32) * (2.0 * math.pi / period)
        return jnp.cos(ang), jnp.sin(ang)

    ca, sa = factor(jnp.arange(ncols // TRIG_BLOCK, dtype=jnp.int32) * TRIG_BLOCK)
    cb, sb = factor(jnp.arange(TRIG_BLOCK, dtype=jnp.int32))
    if s_before_last:
        ca, sa, cb, sb = (jnp.swapaxes(t, -1, -2) for t in (ca, sa, cb, sb))
        ca, sa, cb, sb = ca[..., :, None, :], sa[..., :, None, :], cb[..., None, :, :], sb[..., None, :, :]
        shape = r.shape[:-1] + (ncols, r.shape[-1])
    else:
        ca, sa, cb, sb = ca[..., :, None], sa[..., :, None], cb[..., None, :], sb[..., None, :]
        shape = r.shape + (ncols,)
    return (ca * cb - sa * sb).reshape(shape), (sa * cb + ca * sb).reshape(shape)


def _fnet_tables(seq):
    cl, sl = _cos_sin_table(jnp.arange(seq, dtype=jnp.int32), seq, seq)
    cc, sc = _cos_sin_table(jnp.arange(FNET_GDIM, dtype=jnp.int32), FNET_GDIM, FNET_GDIM)
    return (jnp.concatenate([cl, -sl], axis=1).astype(BF16), cc.astype(BF16), sc.astype(BF16))


def _fnet_kernel(u_ref, gb_ref, tl_ref, cc_ref, sc_ref, o_ref, t_ref, *, seq, tl, sb):
    i = pl.program_id(1)

    @pl.when(i == 0)
    def _():
        for s in range(sb):
            for g in range(FNET_GROUPS):
                ug = u_ref[s * seq:(s + 1) * seq, g * FNET_GDIM:(g + 1) * FNET_GDIM].astype(BF16)
                lanes = slice(s * FNET_W + g * FNET_GDIM, s * FNET_W + (g + 1) * FNET_GDIM)
                t_ref[0:seq, lanes] = _dot(ug, cc_ref[...]).astype(BF16)
                t_ref[seq:2 * seq, lanes] = _dot(ug, sc_ref[...]).astype(BF16)

    y = _dot(tl_ref[...], t_ref[...]) * ((seq * FNET_GDIM) ** -0.5)
    for s in range(sb):
        rows = slice(s * tl, (s + 1) * tl)
        o_ref[rows, :] = (y[:, s * FNET_W:(s + 1) * FNET_W] * _silu(gb_ref[rows, :])).astype(BF16)


def _fnet(proj, tables, seq):
    m = proj.shape[0]
    tab_l, cc, sc = tables
    tl = min(512, seq)
    nt = seq // tl
    sb = min(4, m // seq) if nt == 1 else 1
    return pl.pallas_call(
        functools.partial(_fnet_kernel, seq=seq, tl=tl, sb=sb),
        grid=(m // (sb * seq), nt),
        in_specs=[
            pl.BlockSpec((sb * seq, FNET_W), lambda b, i: (b, OFF_UB // FNET_W)),
            pl.BlockSpec((sb * tl, FNET_W), lambda b, i: (b * nt + i, OFF_GB // FNET_W)),
            pl.BlockSpec((tl, 2 * seq), lambda b, i: (i, 0)),
            pl.BlockSpec((FNET_GDIM, FNET_GDIM), lambda b, i: (0, 0)),
            pl.BlockSpec((FNET_GDIM, FNET_GDIM), lambda b, i: (0, 0)),
        ],
        out_specs=pl.BlockSpec((sb * tl, FNET_W), lambda b, i: (b * nt + i, 0)),
        out_shape=jax.ShapeDtypeStruct((m, FNET_W), BF16),
        scratch_shapes=[pltpu.VMEM((2 * seq, sb * FNET_W), BF16)],
        compiler_params=_params(("arbitrary", "arbitrary"), 40), name="fnet",
    )(proj, proj, tab_l, cc, sc)


def _hyena_tiling(seq):
    if seq > 1024:
        return min(FREQ_CHUNK // 2, seq), HY_W // 2, 2
    return min(FREQ_CHUNK, seq), HY_W, 4


def _hyena_tables(seq):
    fc = _hyena_tiling(seq)[0]
    nch = seq // fc
    k = jnp.arange(2 * fc, dtype=jnp.int32)
    f = jnp.arange(nch, dtype=jnp.int32)[:, None] * fc + (k % fc)[None, :]
    is_sec = (k >= fc)[None, :, None]
    nyq = is_sec & (f == 0)[:, :, None]
    sgn = (1 - 2 * (jnp.arange(seq, dtype=jnp.int32) % 2)).astype(F32)

    def table(transposed):
        c, s = _cos_sin_table(f, seq, 2 * seq, s_before_last=transposed)
        axes = (0, 2, 1) if transposed else (0, 1, 2)
        sec = jnp.where(nyq.transpose(axes), sgn.reshape((1, 1, seq)).transpose(axes), -s)
        return jnp.where(is_sec.transpose(axes), sec, c).astype(BF16)

    return table(False), table(True)


def _filter_features(seq):
    t = jnp.linspace(0.0, 1.0, seq, dtype=F32)[:, None]
    bands = (FILTER_EMB - 1) // 2
    w = (2.0 * math.pi / seq) * jnp.arange(seq, dtype=F32)[:, None]
    f = jnp.linspace(1e-4, bands - 1, bands, dtype=F32)[None, :]
    z = jnp.concatenate([t, jnp.cos(w * f), -jnp.sin(w * f)], axis=-1)
    z = jnp.pad(z, ((0, 0), (0, 128 - FILTER_EMB)))
    deltas = jnp.abs(jnp.linspace(MIN_DECAY, MAX_DECAY, HY_W, dtype=F32))[None, :]
    return z, t, deltas


def _dot_f32(a, b):
    return jnp.dot(a, b, preferred_element_type=F32, precision=lax.Precision.HIGHEST)


def _split_bf16(x):
    hi = x.astype(BF16)
    return hi, (x - hi.astype(F32)).astype(BF16)


def _dot_3pass(a, b_hi, b_lo):
    a_hi, a_lo = _split_bf16(a)
    return _dot(a_hi, b_hi) + (_dot(a_hi, b_lo) + _dot(a_lo, b_hi))


def _filter_kernel(z_ref, t_ref, dl_ref, w1_ref, b1_ref, fr_ref, w2_ref, b2_ref, w3_ref,
                   fwd_ref, o_ref, h_ref, sd_ref, nyq_ref, *, seq, fc, rows):
    j = pl.program_id(0)
    nblk = 2 * HYENA_ORDER
    nrc = seq // rows

    @pl.when(j == 0)
    def _():
        fr = fr_ref[...]
        w3_hi, w3_lo = _split_bf16(w3_ref[...])

        def taps(r, acc):
            sl = pl.ds(pl.multiple_of(r * rows, rows), rows)
            h = jnp.sin(fr * (_dot_f32(z_ref[sl, :], w1_ref[...]) + b1_ref[...]))
            h = jnp.sin(fr * (_dot_f32(h, w2_ref[...]) + b2_ref[...]))
            h = _dot_3pass(h, w3_hi, w3_lo)
            decay = jnp.exp(-t_ref[sl, :] * dl_ref[...])
            h = h * jnp.concatenate([decay] * nblk, axis=-1)
            h_ref[sl, :] = h
            return acc + jnp.sum(jnp.abs(h), axis=0, keepdims=True)

        tot = lax.fori_loop(0, nrc, taps, jnp.zeros((1, nblk * HY_W), F32))
        inv = 1.0 / (tot + EPS)
        sgn = (1 - 2 * (lax.broadcasted_iota(jnp.int32, (rows, 1), 0) % 2)).astype(F32)

        def fold(r, acc):
            sl = pl.ds(pl.multiple_of(r * rows, rows), rows)
            h = h_ref[sl, :] * inv
            pos = r * rows + lax.broadcasted_iota(jnp.int32, (rows, 1), 0)
            sums = []
            for o in range(HYENA_ORDER):
                fw = h[:, (2 * o) * HY_W:(2 * o + 1) * HY_W]
                bw = jnp.where(pos == 0, 0.0, h[:, (2 * o + 1) * HY_W:(2 * o + 2) * HY_W])
                sums.append(fw + bw)
                sd_ref[0, sl, o * HY_W:(o + 1) * HY_W] = (fw + bw).astype(BF16)
                sd_ref[1, sl, o * HY_W:(o + 1) * HY_W] = (fw - bw).astype(BF16)
            return acc + jnp.sum(jnp.concatenate(sums, axis=-1) * sgn, axis=0, keepdims=True)

        nyq_ref[...] = lax.fori_loop(0, nrc, fold, jnp.zeros((1, HYENA_ORDER * HY_W), F32))

    re = _dot(fwd_ref[0:fc, :], sd_ref[0])
    sec = _dot(fwd_ref[fc:2 * fc, :], sd_ref[1])
    first = (lax.broadcasted_iota(jnp.int32, (fc, 1), 0) == 0) & (j == 0)
    sec = jnp.where(first, nyq_ref[...], sec)
    weight = jnp.where(first, 0.5 / seq, 1.0 / seq)
    o_ref[0:fc, :] = re * weight
    o_ref[fc:2 * fc, :] = sec * weight


def _filter(feats, fwd, w1p, b1, fr, w2, b2, w3, l, seq):
    z, t, deltas = feats
    nch, fc2, _ = fwd.shape
    fc = fc2 // 2
    width = HYENA_ORDER * HY_W
    rows = min(256, seq)
    full = lambda a: pl.BlockSpec(a.shape, lambda j: (0,) * a.ndim)
    lay = lambda a: pl.BlockSpec((None,) + a.shape[1:], lambda j: (l,) + (0,) * (a.ndim - 1))
    return pl.pallas_call(
        functools.partial(_filter_kernel, seq=seq, fc=fc, rows=rows),
        grid=(nch,),
        in_specs=[full(z), full(t), full(deltas), lay(w1p), lay(b1), lay(fr), lay(w2), lay(b2), lay(w3),
                  pl.BlockSpec((None, 2 * fc, seq), lambda j: (j, 0, 0))],
        out_specs=pl.BlockSpec((None, 2 * fc, width), lambda j: (j, 0, 0)),
        out_shape=jax.ShapeDtypeStruct((nch, 2 * fc, width), F32),
        scratch_shapes=[pltpu.VMEM((seq, 2 * width), F32),
                        pltpu.VMEM((2, seq, width), BF16),
                        pltpu.VMEM((1, width), F32)],
        compiler_params=_params(("arbitrary",), 48), name="hyena_filter",
    )(z, t, deltas, w1p, b1, fr, w2, b2, w3, fwd)


def _hyena_kernel(v_ref, x1_ref, x2_ref, gc_ref, cw_ref, cb_ref, hb_ref, fwd_ref, inv_ref, kf_ref,
                  o_ref, z_ref, zb_ref, y_ref, *, seq, fc, sb, tc):
    o = pl.program_id(2)
    j = pl.program_id(3)
    last_j = pl.num_programs(3) - 1
    pos = lax.broadcasted_iota(jnp.int32, (seq, 1), 0)

    def short_conv(ref, k, s):
        x = ref[s * seq:(s + 1) * seq, :]
        prev = jnp.where(pos == 0, 0.0, pltpu.roll(x, 1, 0))
        nxt = jnp.where(pos == seq - 1, 0.0, pltpu.roll(x, seq - 1, 0))
        return (prev * cw_ref[0, k:k + 1, :] + x * cw_ref[1, k:k + 1, :]
                + nxt * cw_ref[2, k:k + 1, :] + cb_ref[k:k + 1, :])

    @pl.when((o == 0) & (j == 0))
    def _():
        for s in range(sb):
            lanes = slice(s * tc, (s + 1) * tc)
            z = short_conv(v_ref, 0, s)
            z_ref[:, lanes] = z
            zb_ref[:, lanes] = z.astype(BF16)
        y_ref[...] = jnp.zeros_like(y_ref)

    @pl.when((o == 1) & (j == 0))
    def _():
        for s in range(sb):
            lanes = slice(s * tc, (s + 1) * tc)
            z = short_conv(x1_ref, 1, s) * (y_ref[:, lanes] + hb_ref[0:1, :] * z_ref[:, lanes])
            z_ref[:, lanes] = z
            zb_ref[:, lanes] = z.astype(BF16)
        y_ref[...] = jnp.zeros_like(y_ref)

    zf = _dot(fwd_ref[...], zb_ref[...])
    kr, ks = kf_ref[0:fc, :], kf_ref[fc:2 * fc, :]
    nyq = (lax.broadcasted_iota(jnp.int32, (fc, 1), 0) == 0) & (j == 0)
    parts = []
    for s in range(sb):
        lanes = slice(s * tc, (s + 1) * tc)
        zr, zs = zf[0:fc, lanes], zf[fc:2 * fc, lanes]
        ss = zs * ks
        yr = zr * kr - jnp.where(nyq, 0.0, ss)
        ys = jnp.where(nyq, ss, zr * ks + zs * kr)
        parts.append(jnp.concatenate([yr, ys], axis=0).astype(BF16))
    y_ref[...] += _dot(inv_ref[...], jnp.concatenate(parts, axis=-1))

    @pl.when((o == 1) & (j == last_j))
    def _():
        for s in range(sb):
            lanes = slice(s * tc, (s + 1) * tc)
            z = short_conv(x2_ref, 2, s) * (y_ref[:, lanes] + hb_ref[1:2, :] * z_ref[:, lanes])
            o_ref[s * seq:(s + 1) * seq, :] = (z * _silu(gc_ref[s * seq:(s + 1) * seq, :])).astype(BF16)


def _hyena(proj, cw, cb, hb, fwd, inv, kf, seq):
    m = proj.shape[0]
    nch, fc2, _ = fwd.shape
    fc = fc2 // 2
    _, tc, sb = _hyena_tiling(seq)
    sb = min(sb, m // seq)
    nct = HY_W // tc
    hc0 = OFF_HC // tc
    once = pl.Buffered(1) if seq > 1024 else None
    col = lambda k: pl.BlockSpec((sb * seq, tc), lambda b, c, o, j: (b, hc0 + k * nct + c),
                                 pipeline_mode=once)
    return pl.pallas_call(
        functools.partial(_hyena_kernel, seq=seq, fc=fc, sb=sb, tc=tc),
        grid=(m // (sb * seq), nct, HYENA_ORDER, nch),
        in_specs=[
            col(0), col(1), col(2),
            pl.BlockSpec((sb * seq, tc), lambda b, c, o, j: (b, OFF_GC // tc + c), pipeline_mode=once),
            pl.BlockSpec((3, 3, tc), lambda b, c, o, j: (0, 0, c)),
            pl.BlockSpec((3, tc), lambda b, c, o, j: (0, c)),
            pl.BlockSpec((HYENA_ORDER, tc), lambda b, c, o, j: (0, c)),
            pl.BlockSpec((None, 2 * fc, seq), lambda b, c, o, j: (j, 0, 0)),
            pl.BlockSpec((None, seq, 2 * fc), lambda b, c, o, j: (j, 0, 0)),
            pl.BlockSpec((None, 2 * fc, tc), lambda b, c, o, j: (j, 0, o * nct + c)),
        ],
        out_specs=pl.BlockSpec((sb * seq, tc), lambda b, c, o, j: (b, c)),
        out_shape=jax.ShapeDtypeStruct((m, HY_W), BF16),
        scratch_shapes=[pltpu.VMEM((seq, sb * tc), F32), pltpu.VMEM((seq, sb * tc), BF16),
                        pltpu.VMEM((seq, sb * tc), F32)],
        compiler_params=_params(("arbitrary",) * 4, 56), name="hyena",
    )(proj, proj, proj, proj, cw, cb, hb, fwd, inv, kf)


def _merge_kernel(ya_ref, yb_ref, yc_ref, ga_ref, gb_ref, gc_ref, wa_ref, wb_ref, wc_ref, *refs):
    o_ref = _out_ref(refs)
    acc = ga_ref[...].astype(F32) * _dot(ya_ref[...], wa_ref[...])
    acc += gb_ref[...].astype(F32) * _dot(yb_ref[...], wb_ref[...])
    acc += gc_ref[...].astype(F32) * _dot(yc_ref[...], wc_ref[...])
    o_ref[...] = acc.astype(BF16)


def _merge(ya, yb, yc, gates, w_bf, l, seq, w32=None):
    m = ya.shape[0]
    tm = min(ROW_TILE, m)
    tn = COL_TILE
    d = D_MODEL
    row = lambda w: pl.BlockSpec((tm, w), lambda i, j: (i, 0))
    gate = lambda k: pl.BlockSpec((tm, tn), lambda i, j: (i, k * d // tn + j))
    nxt_args, nxt_in, nxt_out, nxt_shape = _next_weight_specs(w32, l, m // tm, tn)
    return pl.pallas_call(
        _merge_kernel, grid=(m // tm, d // tn),
        in_specs=[row(ATT_W), row(FNET_W), row(HY_W), gate(0), gate(1), gate(2),
                  pl.BlockSpec((ATT_W, tn), lambda i, j: (0, j)),
                  pl.BlockSpec((FNET_W, tn), lambda i, j: (ATT_W // FNET_W, j)),
                  pl.BlockSpec((HY_W, tn), lambda i, j: ((ATT_W + FNET_W) // HY_W, j))] + nxt_in,
        out_specs=[pl.BlockSpec((tm, tn), lambda i, j: (i, j))] + nxt_out,
        out_shape=[jax.ShapeDtypeStruct((m, d), BF16)] + nxt_shape,
        compiler_params=_params(("arbitrary", "arbitrary"), 40), name="merge",
    )(ya, yb, yc, gates, gates, gates, w_bf, w_bf, w_bf, *nxt_args)


def _outproj_kernel(mg_ref, w_ref, x_ref, ada_ref, *refs):
    o_ref = _out_ref(refs)
    o_ref[...] = x_ref[...] + ada_ref[...] * _dot(mg_ref[...], w_ref[...])


def _outproj(merged, x2d, ada4, ada_row, w_bf, l, seq, w32=None):
    m, d = x2d.shape
    tm = min(ROW_TILE, m)
    tn = COL_TILE
    nxt_args, nxt_in, nxt_out, nxt_shape = _next_weight_specs(w32, l, m // tm, tn)
    return pl.pallas_call(
        _outproj_kernel, grid=(m // tm, d // tn),
        in_specs=[pl.BlockSpec((tm, d), lambda i, j: (i, 0)),
                  pl.BlockSpec((d, tn), lambda i, j: (0, j)),
                  pl.BlockSpec((tm, tn), lambda i, j: (i, j)),
                  pl.BlockSpec((None, None, 1, tn),
                               lambda i, j: (l, ada_row(i * tm), 0, 2 * d // tn + j))] + nxt_in,
        out_specs=[pl.BlockSpec((tm, tn), lambda i, j: (i, j))] + nxt_out,
        out_shape=[jax.ShapeDtypeStruct((m, d), F32)] + nxt_shape,
        compiler_params=_params(("arbitrary", "arbitrary"), 44), name="outproj",
    )(merged, w_bf, x2d, ada4, *nxt_args)


def kernel(x_prompt, x_sample, cache_k, cache_v, c, c_ctx, norm_g, w_ada, b_ada, w_in, q_norm_g, k_norm_g, rpb, conv_w, conv_b, f_w1, f_b1, f_freq, f_w2, f_b2, f_w3, hy_bias, w_br, w_out):
    batch, seq, d = x_prompt.shape
    dec_batch, dec_seq, _ = x_sample.shape
    depth = norm_g.shape[0]
    past = cache_k.shape[2]

    ada4 = _ada(jnp.concatenate([c_ctx[None, :], c], axis=0), w_ada, b_ada)
    ada4 = ada4.reshape(depth, ADA_ROWS, 1, 3 * d)
    ctx_row = lambda tok: 0
    lat_row = lambda tok: 1 + tok // dec_seq

    w_in_bf, w_br_bf, w_out_bf = (w[0].astype(BF16) for w in (w_in, w_br, w_out))
    norm_g3 = norm_g.reshape(depth, 1, d)
    qg3 = q_norm_g.reshape(depth, 1, HEAD_DIM)
    kg3 = k_norm_g.reshape(depth, 1, HEAD_DIM)
    cache_k4 = cache_k.reshape(dec_batch, depth, past, ATT_W)
    cache_v4 = cache_v.reshape(dec_batch, depth, past, ATT_W)
    w1p = jnp.pad(f_w1, ((0, 0), (0, 128 - FILTER_EMB), (0, 0)))
    b1 = f_b1.reshape(depth, 1, FILTER_FF)
    fr = f_freq.reshape(depth, 1, FILTER_FF)
    b2 = f_b2.reshape(depth, 1, FILTER_FF)
    win_bias = _window_bias(rpb)

    groups = []
    for x, s, row in ((x_prompt, seq, ctx_row), (x_sample, dec_seq, lat_row)):
        fwd, inv = _hyena_tables(s)
        groups.append(dict(x=x.reshape(-1, d), seq=s, row=row, fnet=_fnet_tables(s),
                           fwd=fwd, inv=inv, feats=_filter_features(s)))

    new_kv = ()
    for l in range(depth):
        cw = conv_w[l].reshape(3, 3, HY_W)
        cb = conv_b[l].reshape(3, HY_W)
        hb = hy_bias[l]
        for gi, g in enumerate(groups):
            s = g["seq"]
            side = gi == 0
            proj, gates, *nxt_in = _inproj(g["x"], norm_g3, ada4, g["row"], w_in_bf, l, s,
                                           w_in if side else None)
            if gi == 0:
                ya, *new_kv = _ctx_attn(proj, qg3, kg3, new_kv, l, s, depth)
            else:
                qn, kn, vb = _prep_lat(proj, qg3, kg3, l)
                ya = _lat_attn(qn, kn, vb, cache_k4, cache_v4, win_bias, proj, l, s)
            yb = _fnet(proj, g["fnet"], s)
            kf = _filter(g["feats"], g["fwd"], w1p, b1, fr, f_w2, b2, f_w3, l, s)
            yc = _hyena(proj, cw, cb, hb, g["fwd"], g["inv"], kf, s)
            merged, *nxt_br = _merge(ya, yb, yc, gates, w_br_bf, l, s, w_br if side else None)
            g["x"], *nxt_out = _outproj(merged, g["x"], ada4, g["row"], w_out_bf, l, s,
                                        w_out if side else None)
            if side:
                nxt = (nxt_in, nxt_br, nxt_out)
        if l + 1 < depth:
            (w_in_bf,), (w_br_bf,), (w_out_bf,) = nxt

    nk, nv = new_kv
    return (groups[0]["x"].reshape(batch, seq, d),
            groups[1]["x"].reshape(dec_batch, dec_seq, d),
            nk.reshape(batch, depth, seq, N_HEADS, HEAD_DIM),
            nv.reshape(batch, depth, seq, N_HEADS, HEAD_DIM))
```

```python
import functools
import math

import jax
import jax.numpy as jnp
import numpy as np
from jax import lax
from jax.experimental import pallas as pl
from jax.experimental.pallas import tpu as pltpu

F32 = jnp.float32
BF16 = jnp.bfloat16

D_MODEL = 2048
GRID_W = 64
N_HEADS = 8
HEAD_DIM = 128
ATT_W = N_HEADS * HEAD_DIM
MAX_KH = 8
KW = 16
FNET_GROUPS = 4
FNET_GDIM = 128
FNET_W = FNET_GROUPS * FNET_GDIM
HY_W = 512
HYENA_ORDER = 2
FILTER_EMB = 33
FILTER_FF = 64
MIN_DECAY = math.log(1e-2) / 1.5
MAX_DECAY = math.log(1e-2) / 0.3
OFF_Q = 0
OFF_K = OFF_Q + ATT_W
OFF_V = OFF_K + ATT_W
OFF_GA = OFF_V + ATT_W
OFF_UB = OFF_GA + ATT_W
OFF_GB = OFF_UB + FNET_W
OFF_HC = OFF_GB + FNET_W
OFF_GC = OFF_HC + 3 * HY_W
OFF_MG = OFF_GC + HY_W
N_IN = OFF_MG + 3 * D_MODEL
EPS = 1e-6
NEG = -1e30
ATT_SCALE = HEAD_DIM ** -0.5

LANES = 128
SUBLANES = 8
ADA_ROWS = 8
FREQ_CHUNK = 512
TRIG_BLOCK = 128
LAT_ROW_BLOCK = 4
ROW_TILE = 1024
COL_TILE = 1024
MIB = 1024 * 1024


def _params(semantics, vmem_mib):
    return pltpu.CompilerParams(dimension_semantics=semantics,
                                vmem_limit_bytes=vmem_mib * MIB)


def _silu(x):
    return x * jax.nn.sigmoid(x)


def _dot(a, b):
    return jnp.dot(a, b, preferred_element_type=F32)


def _dot_nt(a, b):
    return lax.dot_general(a, b, (((1,), (1,)), ((), ())), preferred_element_type=F32)


def _ada_kernel(cv_ref, w_ref, b_ref, o_ref, s_ref):
    n_rows, d, _ = cv_ref.shape
    tn = w_ref.shape[1]

    @pl.when((pl.program_id(0) == 0) & (pl.program_id(1) == 0))
    def _():
        s_ref[...] = _silu(cv_ref[...])

    def body(i, acc):
        rows = pl.ds(pl.multiple_of(i * SUBLANES, SUBLANES), SUBLANES)
        w = w_ref[rows, :]
        return tuple(acc[r] + w * jnp.concatenate([s_ref[r, rows, :]] * (tn // LANES), axis=-1)
                     for r in range(n_rows))

    zero = jnp.zeros((SUBLANES, tn), F32)
    acc = lax.fori_loop(0, d // SUBLANES, body, (zero,) * n_rows, unroll=4)
    o_ref[...] = jnp.zeros(o_ref.shape, F32)
    for r in range(n_rows):
        o_ref[r:r + 1, :] = jnp.sum(acc[r], axis=0, keepdims=True) + b_ref[...]


def _ada(conds, w_ada, b_ada):
    depth, d, n = w_ada.shape
    rows = conds.shape[0]
    assert rows <= ADA_ROWS
    tn = COL_TILE
    cv = jnp.broadcast_to(conds[:, :, None], (rows, d, LANES))
    return pl.pallas_call(
        _ada_kernel,
        grid=(depth, n // tn),
        in_specs=[
            pl.BlockSpec((rows, d, LANES), lambda l, j: (0, 0, 0)),
            pl.BlockSpec((None, d, tn), lambda l, j: (l, 0, j)),
            pl.BlockSpec((None, 1, tn), lambda l, j: (l, 0, j)),
        ],
        out_specs=pl.BlockSpec((None, ADA_ROWS, tn), lambda l, j: (l, 0, j)),
        out_shape=jax.ShapeDtypeStruct((depth, ADA_ROWS, n), F32),
        scratch_shapes=[pltpu.VMEM((rows, d, LANES), F32)],
        compiler_params=_params(("arbitrary", "arbitrary"), 32),
        name="ada",
    )(cv, w_ada, b_ada.reshape(depth, 1, n))


def _cast_next(src_ref, dst_ref):
    dst_ref[...] = src_ref[...].astype(BF16)


def _out_ref(refs):
    if len(refs) == 1:
        return refs[0]
    src_ref, o_ref, dst_ref = refs
    _cast_next(src_ref, dst_ref)
    return o_ref


def _next_weight_specs(w32, l, n_i, tn):
    if w32 is None or l + 1 >= w32.shape[0]:
        return [], [], [], []
    _, rows, cols = w32.shape
    rb = rows // n_i
    return ([w32], [pl.BlockSpec((None, rb, tn), lambda i, j: (l + 1, i, j))],
            [pl.BlockSpec((rb, tn), lambda i, j: (i, j))], [jax.ShapeDtypeStruct((rows, cols), BF16)])


def _inproj_kernel(x_ref, g_ref, ada_ref, w_ref, *refs, rows, n_plain):
    if len(refs) == 5:
        nxt32_ref, o_ref, gate_ref, nxt_ref, h_ref = refs
        _cast_next(nxt32_ref, nxt_ref)
    else:
        o_ref, gate_ref, h_ref = refs
    j = pl.program_id(1)

    @pl.when(j == 0)
    def _():
        g = g_ref[...]
        shift = ada_ref[:, 0:D_MODEL]
        scale1 = 1.0 + ada_ref[:, D_MODEL:2 * D_MODEL]

        def body(r, carry):
            sl = pl.ds(pl.multiple_of(r * rows, rows), rows)
            x = x_ref[sl, :]
            ms = jnp.mean(x * x, axis=-1, keepdims=True)
            y = x * lax.rsqrt(ms + EPS) * g
            h_ref[sl, :] = (y * scale1 + shift).astype(BF16)
            return carry

        lax.fori_loop(0, x_ref.shape[0] // rows, body, 0)

    @pl.when(j < n_plain)
    def _():
        o_ref[...] = _dot(h_ref[...], w_ref[...])

    @pl.when(j >= n_plain)
    def _():
        gate_ref[...] = jax.nn.sigmoid(_dot(h_ref[...], w_ref[...])).astype(BF16)


def _inproj(x2d, norm_g3, ada4, ada_row, w_bf, l, seq, w32=None):
    m, d = x2d.shape
    tm = min(ROW_TILE, m)
    tn = COL_TILE
    n_plain = OFF_MG // tn
    nxt_args, nxt_in, nxt_out, nxt_shape = _next_weight_specs(w32, l, m // tm, tn)
    return pl.pallas_call(
        functools.partial(_inproj_kernel, rows=128, n_plain=n_plain),
        grid=(m // tm, N_IN // tn),
        in_specs=[
            pl.BlockSpec((tm, d), lambda i, j: (i, 0)),
            pl.BlockSpec((None, 1, d), lambda i, j: (l, 0, 0)),
            pl.BlockSpec((None, None, 1, 3 * d), lambda i, j: (l, ada_row(i * tm), 0, 0)),
            pl.BlockSpec((d, tn), lambda i, j: (0, j)),
        ] + nxt_in,
        out_specs=[pl.BlockSpec((tm, tn), lambda i, j: (i, jnp.minimum(j, n_plain - 1))),
                   pl.BlockSpec((tm, tn), lambda i, j: (i, jnp.maximum(j - n_plain, 0)))] + nxt_out,
        out_shape=[jax.ShapeDtypeStruct((m, OFF_MG), F32),
                   jax.ShapeDtypeStruct((m, N_IN - OFF_MG), BF16)] + nxt_shape,
        scratch_shapes=[pltpu.VMEM((tm, d), BF16)],
        compiler_params=_params(("arbitrary", "arbitrary"), 52),
        name="inproj",
    )(x2d, norm_g3, ada4, w_bf, *nxt_args)


def _head_norm(x, g):
    ms = jnp.mean(x * x, axis=-1, keepdims=True)
    return x * lax.rsqrt(ms + EPS) * g


def _prep_lat_kernel(q_ref, k_ref, v_ref, qg_ref, kg_ref, qn_ref, kn_ref, vb_ref):
    for h in range(N_HEADS):
        sl = slice(h * HEAD_DIM, (h + 1) * HEAD_DIM)
        qn_ref[:, sl] = _head_norm(q_ref[:, sl], qg_ref[...]).astype(BF16)
        kn_ref[:, sl] = _head_norm(k_ref[:, sl], kg_ref[...]).astype(BF16)
    vb_ref[...] = v_ref[...].astype(BF16)


def _prep_lat(proj, qg3, kg3, l):
    m = proj.shape[0]
    tm = min(256, m)
    col = lambda c: pl.BlockSpec((tm, ATT_W), lambda i: (i, c))
    gspec = pl.BlockSpec((None, 1, HEAD_DIM), lambda i: (l, 0, 0))
    return pl.pallas_call(
        _prep_lat_kernel, grid=(m // tm,),
        in_specs=[col(OFF_Q // ATT_W), col(OFF_K // ATT_W), col(OFF_V // ATT_W), gspec, gspec],
        out_specs=[col(0)] * 3, out_shape=[jax.ShapeDtypeStruct((m, ATT_W), BF16)] * 3,
        compiler_params=_params(("arbitrary",), 32), name="prep_lat",
    )(proj, proj, proj, qg3, kg3)


def _ctx_attn_kernel(q_ref, k_ref, v_ref, ga_ref, qg_ref, kg_ref, *refs, layer):
    o_ref, nk_ref, nv_ref = refs[-3:]
    if len(refs) == 3:
        for other in range(nk_ref.shape[0]):
            if other != layer:
                nk_ref[other] = jnp.zeros(nk_ref.shape[1:], F32)
                nv_ref[other] = jnp.zeros(nv_ref.shape[1:], F32)
        nk_ref, nv_ref = nk_ref.at[layer], nv_ref.at[layer]
    for h in range(N_HEADS):
        sl = slice(h * HEAD_DIM, (h + 1) * HEAD_DIM)
        q = _head_norm(q_ref[:, sl], qg_ref[...])
        k = _head_norm(k_ref[:, sl], kg_ref[...])
        v = v_ref[:, sl]
        nk_ref[:, sl] = k
        nv_ref[:, sl] = v
        s = _dot_nt(q.astype(BF16), k.astype(BF16)) * ATT_SCALE
        p = jnp.exp(s - jnp.max(s, axis=-1, keepdims=True))
        inv = 1.0 / jnp.sum(p, axis=-1, keepdims=True)
        o = _dot(p.astype(BF16), v.astype(BF16)) * inv
        o_ref[:, sl] = (o * _silu(ga_ref[:, sl])).astype(BF16)


def _ctx_attn(proj, qg3, kg3, new_kv, l, seq, depth):
    m = proj.shape[0]
    kv_shape = jax.ShapeDtypeStruct((m // seq, depth, seq, ATT_W), F32)
    col = lambda c: pl.BlockSpec((seq, ATT_W), lambda b: (b, c))
    gspec = pl.BlockSpec((None, 1, HEAD_DIM), lambda b: (l, 0, 0))
    if new_kv:
        kv_spec = pl.BlockSpec((None, None, seq, ATT_W), lambda b: (b, l, 0, 0))
    else:
        kv_spec = pl.BlockSpec((None, depth, seq, ATT_W), lambda b: (b, 0, 0, 0))
    return pl.pallas_call(
        functools.partial(_ctx_attn_kernel, layer=l), grid=(m // seq,),
        in_specs=[col(OFF_Q // ATT_W), col(OFF_K // ATT_W), col(OFF_V // ATT_W), col(OFF_GA // ATT_W),
                  gspec, gspec] + [pl.BlockSpec(memory_space=pl.ANY)] * len(new_kv),
        out_specs=[col(0), kv_spec, kv_spec],
        out_shape=[jax.ShapeDtypeStruct((m, ATT_W), BF16), kv_shape, kv_shape],
        input_output_aliases={6: 1, 7: 2} if new_kv else {},
        compiler_params=_params(("arbitrary",), 32), name="ctx_attn",
    )(proj, proj, proj, proj, qg3, kg3, *new_kv)


def _window_bias(rpb):
    depth, heads, n_dr, n_dc = rpb.shape
    n_e = n_dr + 1
    c = np.arange(GRID_W)[:, None, None]
    p = np.arange(2)[None, :, None]
    kc = np.arange(GRID_W)[None, None, :]
    cs = np.clip(c - KW // 2, 0, GRID_W - KW)
    in_cols = np.broadcast_to((kc >= cs) & (kc < cs + KW), (GRID_W, 2, GRID_W)).reshape(-1)
    dc = np.broadcast_to(np.clip(kc - c + KW - 1, 0, n_dc - 1), (GRID_W, 2, GRID_W)).reshape(-1)
    member = np.broadcast_to(p, (GRID_W, 2, GRID_W)).reshape(-1)
    row_member = np.repeat(np.arange(2), n_dc)[:, None]
    row_dc = np.tile(np.arange(n_dc), 2)[:, None]
    onehot = ((jnp.asarray(row_member) == jnp.asarray(member)[None, :])
              & (jnp.asarray(row_dc) == jnp.asarray(dc)[None, :])).astype(F32)
    padded = jnp.pad(rpb, ((0, 0), (0, 0), (1, 1), (0, 0)))
    pairs = jnp.concatenate([padded[:, :, :-1], padded[:, :, 1:]], axis=-1)
    cb = jnp.einsum("lhek,kx->lhex", pairs, onehot, precision=lax.Precision.HIGHEST)
    row_ok = (np.arange(n_e)[:, None] + member[None, :] >= 1) & (np.arange(n_e)[:, None] + member[None, :] <= n_dr)
    cb = jnp.where(jnp.asarray(row_ok & in_cols[None, :]), cb, NEG)
    return cb.reshape(depth, heads, n_e, GRID_W, 2 * GRID_W)


def _lat_attn_kernel(q_ref, k_ref, v_ref, ck_ref, cv_ref, bias_ref, ga_ref, o_ref,
                     ckb_ref, cvb_ref, *, rows, kh, rb, kr):
    g = pl.program_id(1)

    @pl.when(g == 0)
    def _():
        ckb_ref[...] = ck_ref[...].astype(BF16)
        cvb_ref[...] = cv_ref[...].astype(BF16)

    r0 = g * rb
    ws = jnp.clip(r0 - kh // 2, 0, rows - kr)
    span = pl.ds(pl.multiple_of(ws * GRID_W, GRID_W), kr * GRID_W)
    q_row = r0 + lax.broadcasted_iota(jnp.int32, (rb * GRID_W, 1), 0) // GRID_W
    k_row = ws + lax.broadcasted_iota(jnp.int32, (1, kr * GRID_W), 1) // GRID_W
    q_rs = jnp.clip(q_row - kh // 2, 0, rows - kh)
    in_window = (k_row >= q_rs) & (k_row < q_rs + kh)
    n_pairs = bias_ref.shape[1]
    for h in range(N_HEADS):
        sl = slice(h * HEAD_DIM, (h + 1) * HEAD_DIM)
        q = q_ref[:, sl]
        bias = jnp.concatenate(
            [jnp.concatenate(
                [bias_ref[h, jnp.clip(ws + 2 * j - (r0 + qi) + MAX_KH, 0, n_pairs - 1)]
                 for j in range(kr // 2)], axis=-1)
             for qi in range(rb)], axis=0)
        s_win = jnp.where(in_window, _dot_nt(q, k_ref[span, sl]) * ATT_SCALE + bias, NEG)
        s_ctx = _dot_nt(q, ckb_ref[:, sl]) * ATT_SCALE
        mx = jnp.maximum(jnp.max(s_win, axis=-1, keepdims=True),
                         jnp.max(s_ctx, axis=-1, keepdims=True))
        p_win = jnp.exp(s_win - mx)
        p_ctx = jnp.exp(s_ctx - mx)
        inv = 1.0 / (jnp.sum(p_win, axis=-1, keepdims=True) + jnp.sum(p_ctx, axis=-1, keepdims=True))
        o = (_dot(p_win.astype(BF16), v_ref[span, sl]) + _dot(p_ctx.astype(BF16), cvb_ref[:, sl])) * inv
        o_ref[:, sl] = (o * _silu(ga_ref[:, sl])).astype(BF16)


def _lat_attn(qn, kn, vb, cache_k4, cache_v4, bias, proj, l, seq):
    m = qn.shape[0]
    batch = m // seq
    rows = seq // GRID_W
    kh = min(MAX_KH, rows)
    rb = min(LAT_ROW_BLOCK, rows)
    kr = min(rows, kh + rb)
    assert rows % rb == 0 and kr % 2 == 0
    past = cache_k4.shape[2]
    qspec = pl.BlockSpec((rb * GRID_W, ATT_W), lambda b, g: (b * (rows // rb) + g, 0))
    kvspec = pl.BlockSpec((seq, ATT_W), lambda b, g: (b, 0))
    cspec = pl.BlockSpec((None, None, past, ATT_W), lambda b, g: (b, l, 0, 0))
    return pl.pallas_call(
        functools.partial(_lat_attn_kernel, rows=rows, kh=kh, rb=rb, kr=kr),
        grid=(batch, rows // rb),
        in_specs=[qspec, kvspec, kvspec, cspec, cspec,
                  pl.BlockSpec((None,) + bias.shape[1:], lambda b, g: (l, 0, 0, 0, 0)),
                  pl.BlockSpec((rb * GRID_W, ATT_W),
                               lambda b, g: (b * (rows // rb) + g, OFF_GA // ATT_W))],
        out_specs=qspec,
        out_shape=jax.ShapeDtypeStruct((m, ATT_W), BF16),
        scratch_shapes=[pltpu.VMEM((past, ATT_W), BF16), pltpu.VMEM((past, ATT_W), BF16)],
        compiler_params=_params(("arbitrary", "arbitrary"), 48), name="lat_attn",
    )(qn, kn, vb, cache_k4, cache_v4, bias, proj)


def _cos_sin_table(r, ncols, period, s_before_last=False):
    def factor(v):
        ang = ((r[..., None] * v) % period).astype(F32) * (2.0 * math.pi / period)
        return jnp.cos(ang), jnp.sin(ang)

    ca, sa = factor(jnp.arange(ncols // TRIG_BLOCK, dtype=jnp.int32) * TRIG_BLOCK)
    cb, sb = factor(jnp.arange(TRIG_BLOCK, dtype=jnp.int32))
    if s_before_last:
        ca, sa, cb, sb = (jnp.swapaxes(t, -1, -2) for t in (ca, sa, cb, sb))
        ca, sa, cb, sb = ca[..., :, None, :], sa[..., :, None, :], cb[..., None, :, :], sb[..., None, :, :]
        shape = r.shape[:-1] + (ncols, r.shape[-1])
    else:
        ca, sa, cb, sb = ca[..., :, None], sa[..., :, None], cb[..., None, :], sb[..., None, :]
        shape = r.shape + (ncols,)
    return (ca * cb - sa * sb).reshape(shape), (sa * cb + ca * sb).reshape(shape)


def _fnet_tables(seq):
    cl, sl = _cos_sin_table(jnp.arange(seq, dtype=jnp.int32), seq, seq)
    cc, sc = _cos_sin_table(jnp.arange(FNET_GDIM, dtype=jnp.int32), FNET_GDIM, FNET_GDIM)
    return (jnp.concatenate([cl, -sl], axis=1).astype(BF16), cc.astype(BF16), sc.astype(BF16))


def _fnet_kernel(u_ref, gb_ref, tl_ref, cc_ref, sc_ref, o_ref, t_ref, *, seq, tl, sb):
    i = pl.program_id(1)

    @pl.when(i == 0)
    def _():
        for s in range(sb):
            for g in range(FNET_GROUPS):
                ug = u_ref[s * seq:(s + 1) * seq, g * FNET_GDIM:(g + 1) * FNET_GDIM].astype(BF16)
                lanes = slice(s * FNET_W + g * FNET_GDIM, s * FNET_W + (g + 1) * FNET_GDIM)
                t_ref[0:seq, lanes] = _dot(ug, cc_ref[...]).astype(BF16)
                t_ref[seq:2 * seq, lanes] = _dot(ug, sc_ref[...]).astype(BF16)

    y = _dot(tl_ref[...], t_ref[...]) * ((seq * FNET_GDIM) ** -0.5)
    for s in range(sb):
        rows = slice(s * tl, (s + 1) * tl)
        o_ref[rows, :] = (y[:, s * FNET_W:(s + 1) * FNET_W] * _silu(gb_ref[rows, :])).astype(BF16)


def _fnet(proj, tables, seq):
    m = proj.shape[0]
    tab_l, cc, sc = tables
    tl = min(512, seq)
    nt = seq // tl
    sb = min(4, m // seq) if nt == 1 else 1
    return pl.pallas_call(
        functools.partial(_fnet_kernel, seq=seq, tl=tl, sb=sb),
        grid=(m // (sb * seq), nt),
        in_specs=[
            pl.BlockSpec((sb * seq, FNET_W), lambda b, i: (b, OFF_UB // FNET_W)),
            pl.BlockSpec((sb * tl, FNET_W), lambda b, i: (b * nt + i, OFF_GB // FNET_W)),
            pl.BlockSpec((tl, 2 * seq), lambda b, i: (i, 0)),
            pl.BlockSpec((FNET_GDIM, FNET_GDIM), lambda b, i: (0, 0)),
            pl.BlockSpec((FNET_GDIM, FNET_GDIM), lambda b, i: (0, 0)),
        ],
        out_specs=pl.BlockSpec((sb * tl, FNET_W), lambda b, i: (b * nt + i, 0)),
        out_shape=jax.ShapeDtypeStruct((m, FNET_W), BF16),
        scratch_shapes=[pltpu.VMEM((2 * seq, sb * FNET_W), BF16)],
        compiler_params=_params(("arbitrary", "arbitrary"), 40), name="fnet",
    )(proj, proj, tab_l, cc, sc)


def _hyena_tiling(seq):
    if seq > 1024:
        return min(FREQ_CHUNK // 2, seq), HY_W // 2, 2
    return min(FREQ_CHUNK, seq), HY_W, 4


def _hyena_tables(seq):
    fc = _hyena_tiling(seq)[0]
    nch = seq // fc
    k = jnp.arange(2 * fc, dtype=jnp.int32)
    f = jnp.arange(nch, dtype=jnp.int32)[:, None] * fc + (k % fc)[None, :]
    is_sec = (k >= fc)[None, :, None]
    nyq = is_sec & (f == 0)[:, :, None]
    sgn = (1 - 2 * (jnp.arange(seq, dtype=jnp.int32) % 2)).astype(F32)

    def table(transposed):
        c, s = _cos_sin_table(f, seq, 2 * seq, s_before_last=transposed)
        axes = (0, 2, 1) if transposed else (0, 1, 2)
        sec = jnp.where(nyq.transpose(axes), sgn.reshape((1, 1, seq)).transpose(axes), -s)
        return jnp.where(is_sec.transpose(axes), sec, c).astype(BF16)

    return table(False), table(True)


def _filter_features(seq):
    t = jnp.linspace(0.0, 1.0, seq, dtype=F32)[:, None]
    bands = (FILTER_EMB - 1) // 2
    w = (2.0 * math.pi / seq) * jnp.arange(seq, dtype=F32)[:, None]
    f = jnp.linspace(1e-4, bands - 1, bands, dtype=F32)[None, :]
    z = jnp.concatenate([t, jnp.cos(w * f), -jnp.sin(w * f)], axis=-1)
    z = jnp.pad(z, ((0, 0), (0, 128 - FILTER_EMB)))
    deltas = jnp.abs(jnp.linspace(MIN_DECAY, MAX_DECAY, HY_W, dtype=F32))[None, :]
    return z, t, deltas


def _dot_f32(a, b):
    return jnp.dot(a, b, preferred_element_type=F32, precision=lax.Precision.HIGHEST)


def _split_bf16(x):
    hi = x.astype(BF16)
    return hi, (x - hi.astype(F32)).astype(BF16)


def _dot_3pass(a, b_hi, b_lo):
    a_hi, a_lo = _split_bf16(a)
    return _dot(a_hi, b_hi) + (_dot(a_hi, b_lo) + _dot(a_lo, b_hi))


def _filter_kernel(z_ref, t_ref, dl_ref, w1_ref, b1_ref, fr_ref, w2_ref, b2_ref, w3_ref,
                   fwd_ref, o_ref, h_ref, sd_ref, nyq_ref, *, seq, fc, rows):
    j = pl.program_id(0)
    nblk = 2 * HYENA_ORDER
    nrc = seq // rows

    @pl.when(j == 0)
    def _():
        fr = fr_ref[...]
        w3_hi, w3_lo = _split_bf16(w3_ref[...])

        def taps(r, acc):
            sl = pl.ds(pl.multiple_of(r * rows, rows), rows)
            h = jnp.sin(fr * (_dot_f32(z_ref[sl, :], w1_ref[...]) + b1_ref[...]))
            h = jnp.sin(fr * (_dot_f32(h, w2_ref[...]) + b2_ref[...]))
            h = _dot_3pass(h, w3_hi, w3_lo)
            decay = jnp.exp(-t_ref[sl, :] * dl_ref[...])
            h = h * jnp.concatenate([decay] * nblk, axis=-1)
            h_ref[sl, :] = h
            return acc + jnp.sum(jnp.abs(h), axis=0, keepdims=True)

        tot = lax.fori_loop(0, nrc, taps, jnp.zeros((1, nblk * HY_W), F32))
        inv = 1.0 / (tot + EPS)
        sgn = (1 - 2 * (lax.broadcasted_iota(jnp.int32, (rows, 1), 0) % 2)).astype(F32)

        def fold(r, acc):
            sl = pl.ds(pl.multiple_of(r * rows, rows), rows)
            h = h_ref[sl, :] * inv
            pos = r * rows + lax.broadcasted_iota(jnp.int32, (rows, 1), 0)
            sums = []
            for o in range(HYENA_ORDER):
                fw = h[:, (2 * o) * HY_W:(2 * o + 1) * HY_W]
                bw = jnp.where(pos == 0, 0.0, h[:, (2 * o + 1) * HY_W:(2 * o + 2) * HY_W])
                sums.append(fw + bw)
                sd_ref[0, sl, o * HY_W:(o + 1) * HY_W] = (fw + bw).astype(BF16)
                sd_ref[1, sl, o * HY_W:(o + 1) * HY_W] = (fw - bw).astype(BF16)
            return acc + jnp.sum(jnp.concatenate(sums, axis=-1) * sgn, axis=0, keepdims=True)

        nyq_ref[...] = lax.fori_loop(0, nrc, fold, jnp.zeros((1, HYENA_ORDER * HY_W), F32))

    re = _dot(fwd_ref[0:fc, :], sd_ref[0])
    sec = _dot(fwd_ref[fc:2 * fc, :], sd_ref[1])
    first = (lax.broadcasted_iota(jnp.int32, (fc, 1), 0) == 0) & (j == 0)
    sec = jnp.where(first, nyq_ref[...], sec)
    weight = jnp.where(first, 0.5 / seq, 1.0 / seq)
    o_ref[0:fc, :] = re * weight
    o_ref[fc:2 * fc, :] = sec * weight


def _filter(feats, fwd, w1p, b1, fr, w2, b2, w3, l, seq):
    z, t, deltas = feats
    nch, fc2, _ = fwd.shape
    fc = fc2 // 2
    width = HYENA_ORDER * HY_W
    rows = min(256, seq)
    full = lambda a: pl.BlockSpec(a.shape, lambda j: (0,) * a.ndim)
    lay = lambda a: pl.BlockSpec((None,) + a.shape[1:], lambda j: (l,) + (0,) * (a.ndim - 1))
    return pl.pallas_call(
        functools.partial(_filter_kernel, seq=seq, fc=fc, rows=rows),
        grid=(nch,),
        in_specs=[full(z), full(t), full(deltas), lay(w1p), lay(b1), lay(fr), lay(w2), lay(b2), lay(w3),
                  pl.BlockSpec((None, 2 * fc, seq), lambda j: (j, 0, 0))],
        out_specs=pl.BlockSpec((None, 2 * fc, width), lambda j: (j, 0, 0)),
        out_shape=jax.ShapeDtypeStruct((nch, 2 * fc, width), F32),
        scratch_shapes=[pltpu.VMEM((seq, 2 * width), F32),
                        pltpu.VMEM((2, seq, width), BF16),
                        pltpu.VMEM((1, width), F32)],
        compiler_params=_params(("arbitrary",), 48), name="hyena_filter",
    )(z, t, deltas, w1p, b1, fr, w2, b2, w3, fwd)


def _hyena_kernel(v_ref, x1_ref, x2_ref, gc_ref, cw_ref, cb_ref, hb_ref, fwd_ref, inv_ref, kf_ref,
                  o_ref, z_ref, zb_ref, y_ref, *, seq, fc, sb, tc):
    o = pl.program_id(2)
    j = pl.program_id(3)
    last_j = pl.num_programs(3) - 1
    pos = lax.broadcasted_iota(jnp.int32, (seq, 1), 0)

    def short_conv(ref, k, s):
        x = ref[s * seq:(s + 1) * seq, :]
        prev = jnp.where(pos == 0, 0.0, pltpu.roll(x, 1, 0))
        nxt = jnp.where(pos == seq - 1, 0.0, pltpu.roll(x, seq - 1, 0))
        return (prev * cw_ref[0, k:k + 1, :] + x * cw_ref[1, k:k + 1, :]
                + nxt * cw_ref[2, k:k + 1, :] + cb_ref[k:k + 1, :])

    @pl.when((o == 0) & (j == 0))
    def _():
        for s in range(sb):
            lanes = slice(s * tc, (s + 1) * tc)
            z = short_conv(v_ref, 0, s)
            z_ref[:, lanes] = z
            zb_ref[:, lanes] = z.astype(BF16)
        y_ref[...] = jnp.zeros_like(y_ref)

    @pl.when((o == 1) & (j == 0))
    def _():
        for s in range(sb):
            lanes = slice(s * tc, (s + 1) * tc)
            z = short_conv(x1_ref, 1, s) * (y_ref[:, lanes] + hb_ref[0:1, :] * z_ref[:, lanes])
            z_ref[:, lanes] = z
            zb_ref[:, lanes] = z.astype(BF16)
        y_ref[...] = jnp.zeros_like(y_ref)

    zf = _dot(fwd_ref[...], zb_ref[...])
    kr, ks = kf_ref[0:fc, :], kf_ref[fc:2 * fc, :]
    nyq = (lax.broadcasted_iota(jnp.int32, (fc, 1), 0) == 0) & (j == 0)
    parts = []
    for s in range(sb):
        lanes = slice(s * tc, (s + 1) * tc)
        zr, zs = zf[0:fc, lanes], zf[fc:2 * fc, lanes]
        ss = zs * ks
        yr = zr * kr - jnp.where(nyq, 0.0, ss)
        ys = jnp.where(nyq, ss, zr * ks + zs * kr)
        parts.append(jnp.concatenate([yr, ys], axis=0).astype(BF16))
    y_ref[...] += _dot(inv_ref[...], jnp.concatenate(parts, axis=-1))

    @pl.when((o == 1) & (j == last_j))
    def _():
        for s in range(sb):
            lanes = slice(s * tc, (s + 1) * tc)
            z = short_conv(x2_ref, 2, s) * (y_ref[:, lanes] + hb_ref[1:2, :] * z_ref[:, lanes])
            o_ref[s * seq:(s + 1) * seq, :] = (z * _silu(gc_ref[s * seq:(s + 1) * seq, :])).astype(BF16)


def _hyena(proj, cw, cb, hb, fwd, inv, kf, seq):
    m = proj.shape[0]
    nch, fc2, _ = fwd.shape
    fc = fc2 // 2
    _, tc, sb = _hyena_tiling(seq)
    sb = min(sb, m // seq)
    nct = HY_W // tc
    hc0 = OFF_HC // tc
    once = pl.Buffered(1) if seq > 1024 else None
    col = lambda k: pl.BlockSpec((sb * seq, tc), lambda b, c, o, j: (b, hc0 + k * nct + c),
                                 pipeline_mode=once)
    return pl.pallas_call(
        functools.partial(_hyena_kernel, seq=seq, fc=fc, sb=sb, tc=tc),
        grid=(m // (sb * seq), nct, HYENA_ORDER, nch),
        in_specs=[
            col(0), col(1), col(2),
            pl.BlockSpec((sb * seq, tc), lambda b, c, o, j: (b, OFF_GC // tc + c), pipeline_mode=once),
            pl.BlockSpec((3, 3, tc), lambda b, c, o, j: (0, 0, c)),
            pl.BlockSpec((3, tc), lambda b, c, o, j: (0, c)),
            pl.BlockSpec((HYENA_ORDER, tc), lambda b, c, o, j: (0, c)),
            pl.BlockSpec((None, 2 * fc, seq), lambda b, c, o, j: (j, 0, 0)),
            pl.BlockSpec((None, seq, 2 * fc), lambda b, c, o, j: (j, 0, 0)),
            pl.BlockSpec((None, 2 * fc, tc), lambda b, c, o, j: (j, 0, o * nct + c)),
        ],
        out_specs=pl.BlockSpec((sb * seq, tc), lambda b, c, o, j: (b, c)),
        out_shape=jax.ShapeDtypeStruct((m, HY_W), BF16),
        scratch_shapes=[pltpu.VMEM((seq, sb * tc), F32), pltpu.VMEM((seq, sb * tc), BF16),
                        pltpu.VMEM((seq, sb * tc), F32)],
        compiler_params=_params(("arbitrary",) * 4, 56), name="hyena",
    )(proj, proj, proj, proj, cw, cb, hb, fwd, inv, kf)


def _merge_kernel(ya_ref, yb_ref, yc_ref, ga_ref, gb_ref, gc_ref, wa_ref, wb_ref, wc_ref, *refs):
    o_ref = _out_ref(refs)
    acc = ga_ref[...].astype(F32) * _dot(ya_ref[...], wa_ref[...])
    acc += gb_ref[...].astype(F32) * _dot(yb_ref[...], wb_ref[...])
    acc += gc_ref[...].astype(F32) * _dot(yc_ref[...], wc_ref[...])
    o_ref[...] = acc.astype(BF16)


def _merge(ya, yb, yc, gates, w_bf, l, seq, w32=None):
    m = ya.shape[0]
    tm = min(ROW_TILE, m)
    tn = COL_TILE
    d = D_MODEL
    row = lambda w: pl.BlockSpec((tm, w), lambda i, j: (i, 0))
    gate = lambda k: pl.BlockSpec((tm, tn), lambda i, j: (i, k * d // tn + j))
    nxt_args, nxt_in, nxt_out, nxt_shape = _next_weight_specs(w32, l, m // tm, tn)
    return pl.pallas_call(
        _merge_kernel, grid=(m // tm, d // tn),
        in_specs=[row(ATT_W), row(FNET_W), row(HY_W), gate(0), gate(1), gate(2),
                  pl.BlockSpec((ATT_W, tn), lambda i, j: (0, j)),
                  pl.BlockSpec((FNET_W, tn), lambda i, j: (ATT_W // FNET_W, j)),
                  pl.BlockSpec((HY_W, tn), lambda i, j: ((ATT_W + FNET_W) // HY_W, j))] + nxt_in,
        out_specs=[pl.BlockSpec((tm, tn), lambda i, j: (i, j))] + nxt_out,
        out_shape=[jax.ShapeDtypeStruct((m, d), BF16)] + nxt_shape,
        compiler_params=_params(("arbitrary", "arbitrary"), 40), name="merge",
    )(ya, yb, yc, gates, gates, gates, w_bf, w_bf, w_bf, *nxt_args)


def _outproj_kernel(mg_ref, w_ref, x_ref, ada_ref, *refs):
    o_ref = _out_ref(refs)
    o_ref[...] = x_ref[...] + ada_ref[...] * _dot(mg_ref[...], w_ref[...])


def _outproj(merged, x2d, ada4, ada_row, w_bf, l, seq, w32=None):
    m, d = x2d.shape
    tm = min(ROW_TILE, m)
    tn = COL_TILE
    nxt_args, nxt_in, nxt_out, nxt_shape = _next_weight_specs(w32, l, m // tm, tn)
    return pl.pallas_call(
        _outproj_kernel, grid=(m // tm, d // tn),
        in_specs=[pl.BlockSpec((tm, d), lambda i, j: (i, 0)),
                  pl.BlockSpec((d, tn), lambda i, j: (0, j)),
                  pl.BlockSpec((tm, tn), lambda i, j: (i, j)),
                  pl.BlockSpec((None, None, 1, tn),
                               lambda i, j: (l, ada_row(i * tm), 0, 2 * d // tn + j))] + nxt_in,
        out_specs=[pl.BlockSpec((tm, tn), lambda i, j: (i, j))] + nxt_out,
        out_shape=[jax.ShapeDtypeStruct((m, d), F32)] + nxt_shape,
        compiler_params=_params(("arbitrary", "arbitrary"), 44), name="outproj",
    )(merged, w_bf, x2d, ada4, *nxt_args)


def kernel(x_prompt, x_sample, cache_k, cache_v, c, c_ctx, norm_g, w_ada, b_ada, w_in, q_norm_g, k_norm_g, rpb, conv_w, conv_b, f_w1, f_b1, f_freq, f_w2, f_b2, f_w3, hy_bias, w_br, w_out):
    batch, seq, d = x_prompt.shape
    dec_batch, dec_seq, _ = x_sample.shape
    depth = norm_g.shape[0]
    past = cache_k.shape[2]

    ada4 = _ada(jnp.concatenate([c_ctx[None, :], c], axis=0), w_ada, b_ada)
    ada4 = ada4.reshape(depth, ADA_ROWS, 1, 3 * d)
    ctx_row = lambda tok: 0
    lat_row = lambda tok: 1 + tok // dec_seq

    w_in_bf, w_br_bf, w_out_bf = (w[0].astype(BF16) for w in (w_in, w_br, w_out))
    norm_g3 = norm_g.reshape(depth, 1, d)
    qg3 = q_norm_g.reshape(depth, 1, HEAD_DIM)
    kg3 = k_norm_g.reshape(depth, 1, HEAD_DIM)
    cache_k4 = cache_k.reshape(dec_batch, depth, past, ATT_W)
    cache_v4 = cache_v.reshape(dec_batch, depth, past, ATT_W)
    w1p = jnp.pad(f_w1, ((0, 0), (0, 128 - FILTER_EMB), (0, 0)))
    b1 = f_b1.reshape(depth, 1, FILTER_FF)
    fr = f_freq.reshape(depth, 1, FILTER_FF)
    b2 = f_b2.reshape(depth, 1, FILTER_FF)
    win_bias = _window_bias(rpb)

    groups = []
    for x, s, row in ((x_prompt, seq, ctx_row), (x_sample, dec_seq, lat_row)):
        fwd, inv = _hyena_tables(s)
        groups.append(dict(x=x.reshape(-1, d), seq=s, row=row, fnet=_fnet_tables(s),
                           fwd=fwd, inv=inv, feats=_filter_features(s)))

    new_kv = ()
    for l in range(depth):
        cw = conv_w[l].reshape(3, 3, HY_W)
        cb = conv_b[l].reshape(3, HY_W)
        hb = hy_bias[l]
        for gi, g in enumerate(groups):
            s = g["seq"]
            side = gi == 0
            proj, gates, *nxt_in = _inproj(g["x"], norm_g3, ada4, g["row"], w_in_bf, l, s,
                                           w_in if side else None)
            if gi == 0:
                ya, *new_kv = _ctx_attn(proj, qg3, kg3, new_kv, l, s, depth)
            else:
                qn, kn, vb = _prep_lat(proj, qg3, kg3, l)
                ya = _lat_attn(qn, kn, vb, cache_k4, cache_v4, win_bias, proj, l, s)
            yb = _fnet(proj, g["fnet"], s)
            kf = _filter(g["feats"], g["fwd"], w1p, b1, fr, f_w2, b2, f_w3, l, s)
            yc = _hyena(proj, cw, cb, hb, g["fwd"], g["inv"], kf, s)
            merged, *nxt_br = _merge(ya, yb, yc, gates, w_br_bf, l, s, w_br if side else None)
            g["x"], *nxt_out = _outproj(merged, g["x"], ada4, g["row"], w_out_bf, l, s,
                                        w_out if side else None)
            if side:
                nxt = (nxt_in, nxt_br, nxt_out)
        if l + 1 < depth:
            (w_in_bf,), (w_br_bf,), (w_out_bf,) = nxt

    nk, nv = new_kv
    return (groups[0]["x"].reshape(batch, seq, d),
            groups[1]["x"].reshape(dec_batch, dec_seq, d),
            nk.reshape(batch, depth, seq, N_HEADS, HEAD_DIM),
            nv.reshape(batch, depth, seq, N_HEADS, HEAD_DIM))
```

```python
import functools
import math

import jax
import jax.numpy as jnp
import numpy as np
from jax import lax
from jax.experimental import pallas as pl
from jax.experimental.pallas import tpu as pltpu

F32 = jnp.float32
BF16 = jnp.bfloat16

D_MODEL = 2048
GRID_W = 64
N_HEADS = 8
HEAD_DIM = 128
ATT_W = N_HEADS * HEAD_DIM
MAX_KH = 8
KW = 16
FNET_GROUPS = 4
FNET_GDIM = 128
FNET_W = FNET_GROUPS * FNET_GDIM
HY_W = 512
HYENA_ORDER = 2
FILTER_EMB = 33
FILTER_FF = 64
MIN_DECAY = math.log(1e-2) / 1.5
MAX_DECAY = math.log(1e-2) / 0.3
OFF_Q = 0
OFF_K = OFF_Q + ATT_W
OFF_V = OFF_K + ATT_W
OFF_GA = OFF_V + ATT_W
OFF_UB = OFF_GA + ATT_W
OFF_GB = OFF_UB + FNET_W
OFF_HC = OFF_GB + FNET_W
OFF_GC = OFF_HC + 3 * HY_W
OFF_MG = OFF_GC + HY_W
N_IN = OFF_MG + 3 * D_MODEL
EPS = 1e-6
NEG = -1e30
ATT_SCALE = HEAD_DIM ** -0.5

LANES = 128
SUBLANES = 8
ADA_ROWS = 8
FREQ_CHUNK = 512
LAT_ROW_BLOCK = 4
ROW_TILE = 1024
COL_TILE = 1024
MIB = 1024 * 1024


def _params(semantics, vmem_mib):
    return pltpu.CompilerParams(dimension_semantics=semantics,
                                vmem_limit_bytes=vmem_mib * MIB)


def _silu(x):
    return x * jax.nn.sigmoid(x)


def _dot(a, b):
    return jnp.dot(a, b, preferred_element_type=F32)


def _dot_nt(a, b):
    return lax.dot_general(a, b, (((1,), (1,)), ((), ())), preferred_element_type=F32)


def _ada_kernel(cv_ref, w_ref, b_ref, o_ref, s_ref):
    n_rows, d, _ = cv_ref.shape
    tn = w_ref.shape[1]

    @pl.when((pl.program_id(0) == 0) & (pl.program_id(1) == 0))
    def _():
        s_ref[...] = _silu(cv_ref[...])

    def body(i, acc):
        rows = pl.ds(pl.multiple_of(i * SUBLANES, SUBLANES), SUBLANES)
        w = w_ref[rows, :]
        return tuple(acc[r] + w * jnp.concatenate([s_ref[r, rows, :]] * (tn // LANES), axis=-1)
                     for r in range(n_rows))

    zero = jnp.zeros((SUBLANES, tn), F32)
    acc = lax.fori_loop(0, d // SUBLANES, body, (zero,) * n_rows, unroll=4)
    o_ref[...] = jnp.zeros(o_ref.shape, F32)
    for r in range(n_rows):
        o_ref[r:r + 1, :] = jnp.sum(acc[r], axis=0, keepdims=True) + b_ref[...]


def _ada(conds, w_ada, b_ada):
    depth, d, n = w_ada.shape
    rows = conds.shape[0]
    assert rows <= ADA_ROWS
    tn = COL_TILE
    cv = jnp.broadcast_to(conds[:, :, None], (rows, d, LANES))
    return pl.pallas_call(
        _ada_kernel,
        grid=(depth, n // tn),
        in_specs=[
            pl.BlockSpec((rows, d, LANES), lambda l, j: (0, 0, 0)),
            pl.BlockSpec((None, d, tn), lambda l, j: (l, 0, j)),
            pl.BlockSpec((None, 1, tn), lambda l, j: (l, 0, j)),
        ],
        out_specs=pl.BlockSpec((None, ADA_ROWS, tn), lambda l, j: (l, 0, j)),
        out_shape=jax.ShapeDtypeStruct((depth, ADA_ROWS, n), F32),
        scratch_shapes=[pltpu.VMEM((rows, d, LANES), F32)],
        compiler_params=_params(("arbitrary", "arbitrary"), 32),
        name="ada",
    )(cv, w_ada, b_ada.reshape(depth, 1, n))


def _cast_next(src_ref, dst_ref):
    dst_ref[...] = src_ref[...].astype(BF16)


def _out_ref(refs):
    if len(refs) == 1:
        return refs[0]
    src_ref, o_ref, dst_ref = refs
    _cast_next(src_ref, dst_ref)
    return o_ref


def _next_weight_specs(w32, l, n_i, tn):
    if w32 is None or l + 1 >= w32.shape[0]:
        return [], [], [], []
    _, rows, cols = w32.shape
    rb = rows // n_i
    return ([w32], [pl.BlockSpec((None, rb, tn), lambda i, j: (l + 1, i, j))],
            [pl.BlockSpec((rb, tn), lambda i, j: (i, j))], [jax.ShapeDtypeStruct((rows, cols), BF16)])


def _inproj_kernel(x_ref, g_ref, ada_ref, w_ref, *refs, rows, n_plain):
    if len(refs) == 5:
        nxt32_ref, o_ref, gate_ref, nxt_ref, h_ref = refs
        _cast_next(nxt32_ref, nxt_ref)
    else:
        o_ref, gate_ref, h_ref = refs
    j = pl.program_id(1)

    @pl.when(j == 0)
    def _():
        g = g_ref[...]
        shift = ada_ref[:, 0:D_MODEL]
        scale1 = 1.0 + ada_ref[:, D_MODEL:2 * D_MODEL]

        def body(r, carry):
            sl = pl.ds(pl.multiple_of(r * rows, rows), rows)
            x = x_ref[sl, :]
            ms = jnp.mean(x * x, axis=-1, keepdims=True)
            y = x * lax.rsqrt(ms + EPS) * g
            h_ref[sl, :] = (y * scale1 + shift).astype(BF16)
            return carry

        lax.fori_loop(0, x_ref.shape[0] // rows, body, 0)

    @pl.when(j < n_plain)
    def _():
        o_ref[...] = _dot(h_ref[...], w_ref[...])

    @pl.when(j >= n_plain)
    def _():
        gate_ref[...] = jax.nn.sigmoid(_dot(h_ref[...], w_ref[...])).astype(BF16)


def _inproj(x2d, norm_g3, ada4, ada_row, w_bf, l, seq, w32=None):
    m, d = x2d.shape
    tm = min(ROW_TILE, m)
    tn = COL_TILE
    n_plain = OFF_MG // tn
    nxt_args, nxt_in, nxt_out, nxt_shape = _next_weight_specs(w32, l, m // tm, tn)
    return pl.pallas_call(
        functools.partial(_inproj_kernel, rows=128, n_plain=n_plain),
        grid=(m // tm, N_IN // tn),
        in_specs=[
            pl.BlockSpec((tm, d), lambda i, j: (i, 0)),
            pl.BlockSpec((None, 1, d), lambda i, j: (l, 0, 0)),
            pl.BlockSpec((None, None, 1, 3 * d), lambda i, j: (l, ada_row(i * tm), 0, 0)),
            pl.BlockSpec((d, tn), lambda i, j: (0, j)),
        ] + nxt_in,
        out_specs=[pl.BlockSpec((tm, tn), lambda i, j: (i, jnp.minimum(j, n_plain - 1))),
                   pl.BlockSpec((tm, tn), lambda i, j: (i, jnp.maximum(j - n_plain, 0)))] + nxt_out,
        out_shape=[jax.ShapeDtypeStruct((m, OFF_MG), F32),
                   jax.ShapeDtypeStruct((m, N_IN - OFF_MG), BF16)] + nxt_shape,
        scratch_shapes=[pltpu.VMEM((tm, d), BF16)],
        compiler_params=_params(("arbitrary", "arbitrary"), 52),
        name="inproj",
    )(x2d, norm_g3, ada4, w_bf, *nxt_args)


def _head_norm(x, g):
    ms = jnp.mean(x * x, axis=-1, keepdims=True)
    return x * lax.rsqrt(ms + EPS) * g


def _prep_lat_kernel(q_ref, k_ref, v_ref, qg_ref, kg_ref, qn_ref, kn_ref, vb_ref):
    for h in range(N_HEADS):
        sl = slice(h * HEAD_DIM, (h + 1) * HEAD_DIM)
        qn_ref[:, sl] = _head_norm(q_ref[:, sl], qg_ref[...]).astype(BF16)
        kn_ref[:, sl] = _head_norm(k_ref[:, sl], kg_ref[...]).astype(BF16)
    vb_ref[...] = v_ref[...].astype(BF16)


def _prep_lat(proj, qg3, kg3, l):
    m = proj.shape[0]
    tm = min(256, m)
    col = lambda c: pl.BlockSpec((tm, ATT_W), lambda i: (i, c))
    gspec = pl.BlockSpec((None, 1, HEAD_DIM), lambda i: (l, 0, 0))
    return pl.pallas_call(
        _prep_lat_kernel, grid=(m // tm,),
        in_specs=[col(OFF_Q // ATT_W), col(OFF_K // ATT_W), col(OFF_V // ATT_W), gspec, gspec],
        out_specs=[col(0)] * 3, out_shape=[jax.ShapeDtypeStruct((m, ATT_W), BF16)] * 3,
        compiler_params=_params(("arbitrary",), 32), name="prep_lat",
    )(proj, proj, proj, qg3, kg3)


def _ctx_attn_kernel(q_ref, k_ref, v_ref, ga_ref, qg_ref, kg_ref, *refs, layer):
    o_ref, nk_ref, nv_ref = refs[-3:]
    if len(refs) == 3:
        for other in range(nk_ref.shape[0]):
            if other != layer:
                nk_ref[other] = jnp.zeros(nk_ref.shape[1:], F32)
                nv_ref[other] = jnp.zeros(nv_ref.shape[1:], F32)
        nk_ref, nv_ref = nk_ref.at[layer], nv_ref.at[layer]
    for h in range(N_HEADS):
        sl = slice(h * HEAD_DIM, (h + 1) * HEAD_DIM)
        q = _head_norm(q_ref[:, sl], qg_ref[...])
        k = _head_norm(k_ref[:, sl], kg_ref[...])
        v = v_ref[:, sl]
        nk_ref[:, sl] = k
        nv_ref[:, sl] = v
        s = _dot_nt(q.astype(BF16), k.astype(BF16)) * ATT_SCALE
        p = jnp.exp(s - jnp.max(s, axis=-1, keepdims=True))
        inv = 1.0 / jnp.sum(p, axis=-1, keepdims=True)
        o = _dot(p.astype(BF16), v.astype(BF16)) * inv
        o_ref[:, sl] = (o * _silu(ga_ref[:, sl])).astype(BF16)


def _ctx_attn(proj, qg3, kg3, new_kv, l, seq, depth):
    m = proj.shape[0]
    kv_shape = jax.ShapeDtypeStruct((m // seq, depth, seq, ATT_W), F32)
    col = lambda c: pl.BlockSpec((seq, ATT_W), lambda b: (b, c))
    gspec = pl.BlockSpec((None, 1, HEAD_DIM), lambda b: (l, 0, 0))
    if new_kv:
        kv_spec = pl.BlockSpec((None, None, seq, ATT_W), lambda b: (b, l, 0, 0))
    else:
        kv_spec = pl.BlockSpec((None, depth, seq, ATT_W), lambda b: (b, 0, 0, 0))
    return pl.pallas_call(
        functools.partial(_ctx_attn_kernel, layer=l), grid=(m // seq,),
        in_specs=[col(OFF_Q // ATT_W), col(OFF_K // ATT_W), col(OFF_V // ATT_W), col(OFF_GA // ATT_W),
                  gspec, gspec] + [pl.BlockSpec(memory_space=pl.ANY)] * len(new_kv),
        out_specs=[col(0), kv_spec, kv_spec],
        out_shape=[jax.ShapeDtypeStruct((m, ATT_W), BF16), kv_shape, kv_shape],
        input_output_aliases={6: 1, 7: 2} if new_kv else {},
        compiler_params=_params(("arbitrary",), 32), name="ctx_attn",
    )(proj, proj, proj, proj, qg3, kg3, *new_kv)


def _window_bias(rpb):
    depth, heads, n_dr, n_dc = rpb.shape
    n_e = n_dr + 1
    c = np.arange(GRID_W)[:, None, None]
    p = np.arange(2)[None, :, None]
    kc = np.arange(GRID_W)[None, None, :]
    cs = np.clip(c - KW // 2, 0, GRID_W - KW)
    in_cols = np.broadcast_to((kc >= cs) & (kc < cs + KW), (GRID_W, 2, GRID_W)).reshape(-1)
    dc = np.broadcast_to(np.clip(kc - c + KW - 1, 0, n_dc - 1), (GRID_W, 2, GRID_W)).reshape(-1)
    member = np.broadcast_to(p, (GRID_W, 2, GRID_W)).reshape(-1)
    row_member = np.repeat(np.arange(2), n_dc)[:, None]
    row_dc = np.tile(np.arange(n_dc), 2)[:, None]
    onehot = ((jnp.asarray(row_member) == jnp.asarray(member)[None, :])
              & (jnp.asarray(row_dc) == jnp.asarray(dc)[None, :])).astype(F32)
    padded = jnp.pad(rpb, ((0, 0), (0, 0), (1, 1), (0, 0)))
    pairs = jnp.concatenate([padded[:, :, :-1], padded[:, :, 1:]], axis=-1)
    cb = jnp.einsum("lhek,kx->lhex", pairs, onehot, precision=lax.Precision.HIGHEST)
    row_ok = (np.arange(n_e)[:, None] + member[None, :] >= 1) & (np.arange(n_e)[:, None] + member[None, :] <= n_dr)
    cb = jnp.where(jnp.asarray(row_ok & in_cols[None, :]), cb, NEG)
    return cb.reshape(depth, heads, n_e, GRID_W, 2 * GRID_W)


def _lat_attn_kernel(q_ref, k_ref, v_ref, ck_ref, cv_ref, bias_ref, ga_ref, o_ref,
                     ckb_ref, cvb_ref, *, rows, kh, rb, kr):
    g = pl.program_id(1)

    @pl.when(g == 0)
    def _():
        ckb_ref[...] = ck_ref[...].astype(BF16)
        cvb_ref[...] = cv_ref[...].astype(BF16)

    r0 = g * rb
    ws = jnp.clip(r0 - kh // 2, 0, rows - kr)
    span = pl.ds(pl.multiple_of(ws * GRID_W, GRID_W), kr * GRID_W)
    q_row = r0 + lax.broadcasted_iota(jnp.int32, (rb * GRID_W, 1), 0) // GRID_W
    k_row = ws + lax.broadcasted_iota(jnp.int32, (1, kr * GRID_W), 1) // GRID_W
    q_rs = jnp.clip(q_row - kh // 2, 0, rows - kh)
    in_window = (k_row >= q_rs) & (k_row < q_rs + kh)
    n_pairs = bias_ref.shape[1]
    for h in range(N_HEADS):
        sl = slice(h * HEAD_DIM, (h + 1) * HEAD_DIM)
        q = q_ref[:, sl]
        bias = jnp.concatenate(
            [jnp.concatenate(
                [bias_ref[h, jnp.clip(ws + 2 * j - (r0 + qi) + MAX_KH, 0, n_pairs - 1)]
                 for j in range(kr // 2)], axis=-1)
             for qi in range(rb)], axis=0)
        s_win = jnp.where(in_window, _dot_nt(q, k_ref[span, sl]) * ATT_SCALE + bias, NEG)
        s_ctx = _dot_nt(q, ckb_ref[:, sl]) * ATT_SCALE
        mx = jnp.maximum(jnp.max(s_win, axis=-1, keepdims=True),
                         jnp.max(s_ctx, axis=-1, keepdims=True))
        p_win = jnp.exp(s_win - mx)
        p_ctx = jnp.exp(s_ctx - mx)
        inv = 1.0 / (jnp.sum(p_win, axis=-1, keepdims=True) + jnp.sum(p_ctx, axis=-1, keepdims=True))
        o = (_dot(p_win.astype(BF16), v_ref[span, sl]) + _dot(p_ctx.astype(BF16), cvb_ref[:, sl])) * inv
        o_ref[:, sl] = (o * _silu(ga_ref[:, sl])).astype(BF16)


def _lat_attn(qn, kn, vb, cache_k4, cache_v4, bias, proj, l, seq):
    m = qn.shape[0]
    batch = m // seq
    rows = seq // GRID_W
    kh = min(MAX_KH, rows)
    rb = min(LAT_ROW_BLOCK, rows)
    kr = min(rows, kh + rb)
    assert rows % rb == 0 and kr % 2 == 0
    past = cache_k4.shape[2]
    qspec = pl.BlockSpec((rb * GRID_W, ATT_W), lambda b, g: (b * (rows // rb) + g, 0))
    kvspec = pl.BlockSpec((seq, ATT_W), lambda b, g: (b, 0))
    cspec = pl.BlockSpec((None, None, past, ATT_W), lambda b, g: (b, l, 0, 0))
    return pl.pallas_call(
        functools.partial(_lat_attn_kernel, rows=rows, kh=kh, rb=rb, kr=kr),
        grid=(batch, rows // rb),
        in_specs=[qspec, kvspec, kvspec, cspec, cspec,
                  pl.BlockSpec((None,) + bias.shape[1:], lambda b, g: (l, 0, 0, 0, 0)),
                  pl.BlockSpec((rb * GRID_W, ATT_W),
                               lambda b, g: (b * (rows // rb) + g, OFF_GA // ATT_W))],
        out_specs=qspec,
        out_shape=jax.ShapeDtypeStruct((m, ATT_W), BF16),
        scratch_shapes=[pltpu.VMEM((past, ATT_W), BF16), pltpu.VMEM((past, ATT_W), BF16)],
        compiler_params=_params(("arbitrary", "arbitrary"), 48), name="lat_attn",
    )(qn, kn, vb, cache_k4, cache_v4, bias, proj)


def _phase_cos_sin(phase, period):
    ang = (phase % period).astype(F32) * (2.0 * math.pi / period)
    return jnp.cos(ang), jnp.sin(ang)


def _trig_kernel(ca_ref, sa_ref, cb_ref, sb_ref, *refs, patch_row):
    o_ref = refs[-1]
    cb, sb = cb_ref[...], sb_ref[...]
    for a in range(ca_ref.shape[0]):
        blk = ca_ref[a:a + 1, :] * cb - sa_ref[a:a + 1, :] * sb
        if patch_row is not None:
            row = (pl.program_id(1) * ca_ref.shape[0] + a) * LANES + lax.broadcasted_iota(jnp.int32, (LANES, 1), 0)
            blk = jnp.where(row == patch_row, refs[0][...], blk)
        o_ref[a * LANES:(a + 1) * LANES, :] = blk.astype(o_ref.dtype)


def _trig_table(blk_freq, blk_off, col_pos, col_off, period, patch=None):
    n_j, n_col = col_pos.shape
    n_blk = blk_freq.shape[0]
    ca, sa = _phase_cos_sin(col_pos[:, None, :] * blk_freq[None, :, None] + blk_off[None, :, None]
                            + col_off[:, None, :], period)
    cb, sb = _phase_cos_sin(col_pos[:, None, :] * jnp.arange(LANES, dtype=jnp.int32)[None, :, None], period)
    ab = min(SUBLANES, n_blk)
    blk_spec = pl.BlockSpec((None, ab, n_col), lambda j, i: (j, i, 0))
    tab_spec = pl.BlockSpec((None, LANES, n_col), lambda j, i: (j, 0, 0))
    extra, extra_specs = ([], []) if patch is None else ([patch[1]], [pl.BlockSpec((1, n_col), lambda j, i: (0, 0))])
    return pl.pallas_call(
        functools.partial(_trig_kernel, patch_row=None if patch is None else patch[0]),
        grid=(n_j, n_blk // ab),
        in_specs=[blk_spec, blk_spec, tab_spec, tab_spec] + extra_specs,
        out_specs=pl.BlockSpec((None, ab * LANES, n_col), lambda j, i: (j, i, 0)),
        out_shape=jax.ShapeDtypeStruct((n_j, n_blk * LANES, n_col), BF16),
        compiler_params=_params(("arbitrary", "arbitrary"), 32), name="trig_table",
    )(ca, sa, cb, sb, *extra)


def _fnet_tables(seq):
    col = jnp.arange(2 * seq, dtype=jnp.int32)[None, :]
    blk = jnp.arange(seq // LANES, dtype=jnp.int32) * LANES
    tab = _trig_table(blk, jnp.zeros_like(blk), col % seq, (col // seq) * (seq // 4), seq)[0]
    g = jnp.arange(FNET_GDIM, dtype=jnp.int32)
    cc, sc = _phase_cos_sin(g[:, None] * g[None, :], FNET_GDIM)
    return tab, cc.astype(BF16), sc.astype(BF16)


def _fnet_kernel(u_ref, gb_ref, tl_ref, cc_ref, sc_ref, o_ref, t_ref, *, seq, tl, sb):
    i = pl.program_id(1)

    @pl.when(i == 0)
    def _():
        for s in range(sb):
            for g in range(FNET_GROUPS):
                ug = u_ref[s * seq:(s + 1) * seq, g * FNET_GDIM:(g + 1) * FNET_GDIM].astype(BF16)
                lanes = slice(s * FNET_W + g * FNET_GDIM, s * FNET_W + (g + 1) * FNET_GDIM)
                t_ref[0:seq, lanes] = _dot(ug, cc_ref[...]).astype(BF16)
                t_ref[seq:2 * seq, lanes] = _dot(ug, sc_ref[...]).astype(BF16)

    y = _dot(tl_ref[...], t_ref[...]) * ((seq * FNET_GDIM) ** -0.5)
    for s in range(sb):
        rows = slice(s * tl, (s + 1) * tl)
        o_ref[rows, :] = (y[:, s * FNET_W:(s + 1) * FNET_W] * _silu(gb_ref[rows, :])).astype(BF16)


def _fnet(proj, tables, seq):
    m = proj.shape[0]
    tab_l, cc, sc = tables
    tl = min(512, seq)
    nt = seq // tl
    sb = min(4, m // seq) if nt == 1 else 1
    return pl.pallas_call(
        functools.partial(_fnet_kernel, seq=seq, tl=tl, sb=sb),
        grid=(m // (sb * seq), nt),
        in_specs=[
            pl.BlockSpec((sb * seq, FNET_W), lambda b, i: (b, OFF_UB // FNET_W)),
            pl.BlockSpec((sb * tl, FNET_W), lambda b, i: (b * nt + i, OFF_GB // FNET_W)),
            pl.BlockSpec((tl, 2 * seq), lambda b, i: (i, 0)),
            pl.BlockSpec((FNET_GDIM, FNET_GDIM), lambda b, i: (0, 0)),
            pl.BlockSpec((FNET_GDIM, FNET_GDIM), lambda b, i: (0, 0)),
        ],
        out_specs=pl.BlockSpec((sb * tl, FNET_W), lambda b, i: (b * nt + i, 0)),
        out_shape=jax.ShapeDtypeStruct((m, FNET_W), BF16),
        scratch_shapes=[pltpu.VMEM((2 * seq, sb * FNET_W), BF16)],
        compiler_params=_params(("arbitrary", "arbitrary"), 40), name="fnet",
    )(proj, proj, tab_l, cc, sc)


def _hyena_tiling(seq):
    if seq > 1024:
        return min(FREQ_CHUNK // 2, seq), HY_W // 2, 2
    return min(FREQ_CHUNK, seq), HY_W, 4


def _hyena_tables(seq):
    fc = _hyena_tiling(seq)[0]
    nch = seq // fc
    period = 2 * seq
    pos = jnp.arange(seq, dtype=jnp.int32)
    zeros = jnp.zeros_like(pos)
    blk = jnp.arange(2 * seq // LANES, dtype=jnp.int32) * LANES
    k0 = blk % (2 * fc)
    blk_freq = (blk // (2 * fc)) * fc + k0 % fc
    blk_off = (k0 // fc) * (period // 4)
    nyquist = (1 - 2 * (pos % 2)).astype(F32)[None, :]
    fwd = _trig_table(blk_freq, blk_off, pos[None, :], zeros[None, :], period, patch=(fc, nyquist))
    k = jnp.arange(2 * fc, dtype=jnp.int32)
    f = jnp.arange(nch, dtype=jnp.int32)[:, None] * fc + (k % fc)[None, :]
    is_sec = jnp.broadcast_to((k >= fc)[None, :], f.shape)
    nyq = is_sec & (f == 0)
    row_blk = jnp.arange(seq // LANES, dtype=jnp.int32) * LANES
    inv = _trig_table(row_blk, jnp.zeros_like(row_blk), jnp.where(nyq, seq, f),
                      jnp.where(is_sec & ~nyq, period // 4, 0), period)
    return fwd.reshape(nch, 2 * fc, seq), inv


def _filter_features(seq):
    t = jnp.linspace(0.0, 1.0, seq, dtype=F32)[:, None]
    bands = (FILTER_EMB - 1) // 2
    w = (2.0 * math.pi / seq) * jnp.arange(seq, dtype=F32)[:, None]
    f = jnp.linspace(1e-4, bands - 1, bands, dtype=F32)[None, :]
    z = jnp.concatenate([t, jnp.cos(w * f), -jnp.sin(w * f)], axis=-1)
    z = jnp.pad(z, ((0, 0), (0, 128 - FILTER_EMB)))
    deltas = jnp.abs(jnp.linspace(MIN_DECAY, MAX_DECAY, HY_W, dtype=F32))[None, :]
    return z, t, deltas


def _dot_f32(a, b):
    return jnp.dot(a, b, preferred_element_type=F32, precision=lax.Precision.HIGHEST)


def _split_bf16(x):
    hi = x.astype(BF16)
    return hi, (x - hi.astype(F32)).astype(BF16)


def _dot_3pass(a, b_hi, b_lo):
    a_hi, a_lo = _split_bf16(a)
    return _dot(a_hi, b_hi) + (_dot(a_hi, b_lo) + _dot(a_lo, b_hi))


def _filter_kernel(z_ref, t_ref, dl_ref, w1_ref, b1_ref, fr_ref, w2_ref, b2_ref, w3_ref,
                   fwd_ref, o_ref, h_ref, sd_ref, nyq_ref, *, seq, fc, rows):
    j = pl.program_id(0)
    nblk = 2 * HYENA_ORDER
    nrc = seq // rows

    @pl.when(j == 0)
    def _():
        fr = fr_ref[...]
        w3_hi, w3_lo = _split_bf16(w3_ref[...])

        def taps(r, acc):
            sl = pl.ds(pl.multiple_of(r * rows, rows), rows)
            h = jnp.sin(fr * (_dot_f32(z_ref[sl, :], w1_ref[...]) + b1_ref[...]))
            h = jnp.sin(fr * (_dot_f32(h, w2_ref[...]) + b2_ref[...]))
            h = _dot_3pass(h, w3_hi, w3_lo)
            decay = jnp.exp(-t_ref[sl, :] * dl_ref[...])
            h = h * jnp.concatenate([decay] * nblk, axis=-1)
            h_ref[sl, :] = h
            return acc + jnp.sum(jnp.abs(h), axis=0, keepdims=True)

        tot = lax.fori_loop(0, nrc, taps, jnp.zeros((1, nblk * HY_W), F32))
        inv = 1.0 / (tot + EPS)
        sgn = (1 - 2 * (lax.broadcasted_iota(jnp.int32, (rows, 1), 0) % 2)).astype(F32)

        def fold(r, acc):
            sl = pl.ds(pl.multiple_of(r * rows, rows), rows)
            h = h_ref[sl, :] * inv
            pos = r * rows + lax.broadcasted_iota(jnp.int32, (rows, 1), 0)
            sums = []
            for o in range(HYENA_ORDER):
                fw = h[:, (2 * o) * HY_W:(2 * o + 1) * HY_W]
                bw = jnp.where(pos == 0, 0.0, h[:, (2 * o + 1) * HY_W:(2 * o + 2) * HY_W])
                sums.append(fw + bw)
                sd_ref[0, sl, o * HY_W:(o + 1) * HY_W] = (fw + bw).astype(BF16)
                sd_ref[1, sl, o * HY_W:(o + 1) * HY_W] = (fw - bw).astype(BF16)
            return acc + jnp.sum(jnp.concatenate(sums, axis=-1) * sgn, axis=0, keepdims=True)

        nyq_ref[...] = lax.fori_loop(0, nrc, fold, jnp.zeros((1, HYENA_ORDER * HY_W), F32))

    re = _dot(fwd_ref[0:fc, :], sd_ref[0])
    sec = _dot(fwd_ref[fc:2 * fc, :], sd_ref[1])
    first = (lax.broadcasted_iota(jnp.int32, (fc, 1), 0) == 0) & (j == 0)
    sec = jnp.where(first, nyq_ref[...], sec)
    weight = jnp.where(first, 0.5 / seq, 1.0 / seq)
    o_ref[0:fc, :] = re * weight
    o_ref[fc:2 * fc, :] = sec * weight


def _filter(feats, fwd, w1p, b1, fr, w2, b2, w3, l, seq):
    z, t, deltas = feats
    nch, fc2, _ = fwd.shape
    fc = fc2 // 2
    width = HYENA_ORDER * HY_W
    rows = min(256, seq)
    full = lambda a: pl.BlockSpec(a.shape, lambda j: (0,) * a.ndim)
    lay = lambda a: pl.BlockSpec((None,) + a.shape[1:], lambda j: (l,) + (0,) * (a.ndim - 1))
    return pl.pallas_call(
        functools.partial(_filter_kernel, seq=seq, fc=fc, rows=rows),
        grid=(nch,),
        in_specs=[full(z), full(t), full(deltas), lay(w1p), lay(b1), lay(fr), lay(w2), lay(b2), lay(w3),
                  pl.BlockSpec((None, 2 * fc, seq), lambda j: (j, 0, 0))],
        out_specs=pl.BlockSpec((None, 2 * fc, width), lambda j: (j, 0, 0)),
        out_shape=jax.ShapeDtypeStruct((nch, 2 * fc, width), F32),
        scratch_shapes=[pltpu.VMEM((seq, 2 * width), F32),
                        pltpu.VMEM((2, seq, width), BF16),
                        pltpu.VMEM((1, width), F32)],
        compiler_params=_params(("arbitrary",), 48), name="hyena_filter",
    )(z, t, deltas, w1p, b1, fr, w2, b2, w3, fwd)


def _hyena_kernel(v_ref, x1_ref, x2_ref, gc_ref, cw_ref, cb_ref, hb_ref, fwd_ref, inv_ref, kf_ref,
                  o_ref, z_ref, zb_ref, y_ref, *, seq, fc, sb, tc):
    o = pl.program_id(2)
    j = pl.program_id(3)
    last_j = pl.num_programs(3) - 1
    pos = lax.broadcasted_iota(jnp.int32, (seq, 1), 0)

    def short_conv(ref, k, s):
        x = ref[s * seq:(s + 1) * seq, :]
        prev = jnp.where(pos == 0, 0.0, pltpu.roll(x, 1, 0))
        nxt = jnp.where(pos == seq - 1, 0.0, pltpu.roll(x, seq - 1, 0))
        return (prev * cw_ref[0, k:k + 1, :] + x * cw_ref[1, k:k + 1, :]
                + nxt * cw_ref[2, k:k + 1, :] + cb_ref[k:k + 1, :])

    @pl.when((o == 0) & (j == 0))
    def _():
        for s in range(sb):
            lanes = slice(s * tc, (s + 1) * tc)
            z = short_conv(v_ref, 0, s)
            z_ref[:, lanes] = z
            zb_ref[:, lanes] = z.astype(BF16)
        y_ref[...] = jnp.zeros_like(y_ref)

    @pl.when((o == 1) & (j == 0))
    def _():
        for s in range(sb):
            lanes = slice(s * tc, (s + 1) * tc)
            z = short_conv(x1_ref, 1, s) * (y_ref[:, lanes] + hb_ref[0:1, :] * z_ref[:, lanes])
            z_ref[:, lanes] = z
            zb_ref[:, lanes] = z.astype(BF16)
        y_ref[...] = jnp.zeros_like(y_ref)

    zf = _dot(fwd_ref[...], zb_ref[...])
    kr, ks = kf_ref[0:fc, :], kf_ref[fc:2 * fc, :]
    nyq = (lax.broadcasted_iota(jnp.int32, (fc, 1), 0) == 0) & (j == 0)
    parts = []
    for s in range(sb):
        lanes = slice(s * tc, (s + 1) * tc)
        zr, zs = zf[0:fc, lanes], zf[fc:2 * fc, lanes]
        ss = zs * ks
        yr = zr * kr - jnp.where(nyq, 0.0, ss)
        ys = jnp.where(nyq, ss, zr * ks + zs * kr)
        parts.append(jnp.concatenate([yr, ys], axis=0).astype(BF16))
    y_ref[...] += _dot(inv_ref[...], jnp.concatenate(parts, axis=-1))

    @pl.when((o == 1) & (j == last_j))
    def _():
        for s in range(sb):
            lanes = slice(s * tc, (s + 1) * tc)
            z = short_conv(x2_ref, 2, s) * (y_ref[:, lanes] + hb_ref[1:2, :] * z_ref[:, lanes])
            o_ref[s * seq:(s + 1) * seq, :] = (z * _silu(gc_ref[s * seq:(s + 1) * seq, :])).astype(BF16)


def _hyena(proj, cw, cb, hb, fwd, inv, kf, seq):
    m = proj.shape[0]
    nch, fc2, _ = fwd.shape
    fc = fc2 // 2
    _, tc, sb = _hyena_tiling(seq)
    sb = min(sb, m // seq)
    nct = HY_W // tc
    hc0 = OFF_HC // tc
    once = pl.Buffered(1) if seq > 1024 else None
    col = lambda k: pl.BlockSpec((sb * seq, tc), lambda b, c, o, j: (b, hc0 + k * nct + c),
                                 pipeline_mode=once)
    return pl.pallas_call(
        functools.partial(_hyena_kernel, seq=seq, fc=fc, sb=sb, tc=tc),
        grid=(m // (sb * seq), nct, HYENA_ORDER, nch),
        in_specs=[
            col(0), col(1), col(2),
            pl.BlockSpec((sb * seq, tc), lambda b, c, o, j: (b, OFF_GC // tc + c), pipeline_mode=once),
            pl.BlockSpec((3, 3, tc), lambda b, c, o, j: (0, 0, c)),
            pl.BlockSpec((3, tc), lambda b, c, o, j: (0, c)),
            pl.BlockSpec((HYENA_ORDER, tc), lambda b, c, o, j: (0, c)),
            pl.BlockSpec((None, 2 * fc, seq), lambda b, c, o, j: (j, 0, 0)),
            pl.BlockSpec((None, seq, 2 * fc), lambda b, c, o, j: (j, 0, 0)),
            pl.BlockSpec((None, 2 * fc, tc), lambda b, c, o, j: (j, 0, o * nct + c)),
        ],
        out_specs=pl.BlockSpec((sb * seq, tc), lambda b, c, o, j: (b, c)),
        out_shape=jax.ShapeDtypeStruct((m, HY_W), BF16),
        scratch_shapes=[pltpu.VMEM((seq, sb * tc), F32), pltpu.VMEM((seq, sb * tc), BF16),
                        pltpu.VMEM((seq, sb * tc), F32)],
        compiler_params=_params(("arbitrary",) * 4, 56), name="hyena",
    )(proj, proj, proj, proj, cw, cb, hb, fwd, inv, kf)


def _merge_kernel(ya_ref, yb_ref, yc_ref, ga_ref, gb_ref, gc_ref, wa_ref, wb_ref, wc_ref, *refs):
    o_ref = _out_ref(refs)
    acc = ga_ref[...].astype(F32) * _dot(ya_ref[...], wa_ref[...])
    acc += gb_ref[...].astype(F32) * _dot(yb_ref[...], wb_ref[...])
    acc += gc_ref[...].astype(F32) * _dot(yc_ref[...], wc_ref[...])
    o_ref[...] = acc.astype(BF16)


def _merge(ya, yb, yc, gates, w_bf, l, seq, w32=None):
    m = ya.shape[0]
    tm = min(ROW_TILE, m)
    tn = COL_TILE
    d = D_MODEL
    row = lambda w: pl.BlockSpec((tm, w), lambda i, j: (i, 0))
    gate = lambda k: pl.BlockSpec((tm, tn), lambda i, j: (i, k * d // tn + j))
    nxt_args, nxt_in, nxt_out, nxt_shape = _next_weight_specs(w32, l, m // tm, tn)
    return pl.pallas_call(
        _merge_kernel, grid=(m // tm, d // tn),
        in_specs=[row(ATT_W), row(FNET_W), row(HY_W), gate(0), gate(1), gate(2),
                  pl.BlockSpec((ATT_W, tn), lambda i, j: (0, j)),
                  pl.BlockSpec((FNET_W, tn), lambda i, j: (ATT_W // FNET_W, j)),
                  pl.BlockSpec((HY_W, tn), lambda i, j: ((ATT_W + FNET_W) // HY_W, j))] + nxt_in,
        out_specs=[pl.BlockSpec((tm, tn), lambda i, j: (i, j))] + nxt_out,
        out_shape=[jax.ShapeDtypeStruct((m, d), BF16)] + nxt_shape,
        compiler_params=_params(("arbitrary", "arbitrary"), 40), name="merge",
    )(ya, yb, yc, gates, gates, gates, w_bf, w_bf, w_bf, *nxt_args)


def _outproj_kernel(mg_ref, w_ref, x_ref, ada_ref, *refs):
    o_ref = _out_ref(refs)
    o_ref[...] = x_ref[...] + ada_ref[...] * _dot(mg_ref[...], w_ref[...])


def _outproj(merged, x2d, ada4, ada_row, w_bf, l, seq, w32=None):
    m, d = x2d.shape
    tm = min(ROW_TILE, m)
    tn = COL_TILE
    nxt_args, nxt_in, nxt_out, nxt_shape = _next_weight_specs(w32, l, m // tm, tn)
    return pl.pallas_call(
        _outproj_kernel, grid=(m // tm, d // tn),
        in_specs=[pl.BlockSpec((tm, d), lambda i, j: (i, 0)),
                  pl.BlockSpec((d, tn), lambda i, j: (0, j)),
                  pl.BlockSpec((tm, tn), lambda i, j: (i, j)),
                  pl.BlockSpec((None, None, 1, tn),
                               lambda i, j: (l, ada_row(i * tm), 0, 2 * d // tn + j))] + nxt_in,
        out_specs=[pl.BlockSpec((tm, tn), lambda i, j: (i, j))] + nxt_out,
        out_shape=[jax.ShapeDtypeStruct((m, d), F32)] + nxt_shape,
        compiler_params=_params(("arbitrary", "arbitrary"), 44), name="outproj",
    )(merged, w_bf, x2d, ada4, *nxt_args)


def kernel(x_prompt, x_sample, cache_k, cache_v, c, c_ctx, norm_g, w_ada, b_ada, w_in, q_norm_g, k_norm_g, rpb, conv_w, conv_b, f_w1, f_b1, f_freq, f_w2, f_b2, f_w3, hy_bias, w_br, w_out):
    batch, seq, d = x_prompt.shape
    dec_batch, dec_seq, _ = x_sample.shape
    depth = norm_g.shape[0]
    past = cache_k.shape[2]

    ada4 = _ada(jnp.concatenate([c_ctx[None, :], c], axis=0), w_ada, b_ada)
    ada4 = ada4.reshape(depth, ADA_ROWS, 1, 3 * d)
    ctx_row = lambda tok: 0
    lat_row = lambda tok: 1 + tok // dec_seq

    w_in_bf, w_br_bf, w_out_bf = (w[0].astype(BF16) for w in (w_in, w_br, w_out))
    norm_g3 = norm_g.reshape(depth, 1, d)
    qg3 = q_norm_g.reshape(depth, 1, HEAD_DIM)
    kg3 = k_norm_g.reshape(depth, 1, HEAD_DIM)
    cache_k4 = cache_k.reshape(dec_batch, depth, past, ATT_W)
    cache_v4 = cache_v.reshape(dec_batch, depth, past, ATT_W)
    w1p = jnp.pad(f_w1, ((0, 0), (0, 128 - FILTER_EMB), (0, 0)))
    b1 = f_b1.reshape(depth, 1, FILTER_FF)
    fr = f_freq.reshape(depth, 1, FILTER_FF)
    b2 = f_b2.reshape(depth, 1, FILTER_FF)
    win_bias = _window_bias(rpb)

    groups = []
    for x, s, row in ((x_prompt, seq, ctx_row), (x_sample, dec_seq, lat_row)):
        fwd, inv = _hyena_tables(s)
        groups.append(dict(x=x.reshape(-1, d), seq=s, row=row, fnet=_fnet_tables(s),
                           fwd=fwd, inv=inv, feats=_filter_features(s)))

    new_kv = ()
    for l in range(depth):
        cw = conv_w[l].reshape(3, 3, HY_W)
        cb = conv_b[l].reshape(3, HY_W)
        hb = hy_bias[l]
        for gi, g in enumerate(groups):
            s = g["seq"]
            side = gi == 0
            proj, gates, *nxt_in = _inproj(g["x"], norm_g3, ada4, g["row"], w_in_bf, l, s,
                                           w_in if side else None)
            if gi == 0:
                ya, *new_kv = _ctx_attn(proj, qg3, kg3, new_kv, l, s, depth)
            else:
                qn, kn, vb = _prep_lat(proj, qg3, kg3, l)
                ya = _lat_attn(qn, kn, vb, cache_k4, cache_v4, win_bias, proj, l, s)
            yb = _fnet(proj, g["fnet"], s)
            kf = _filter(g["feats"], g["fwd"], w1p, b1, fr, f_w2, b2, f_w3, l, s)
            yc = _hyena(proj, cw, cb, hb, g["fwd"], g["inv"], kf, s)
            merged, *nxt_br = _merge(ya, yb, yc, gates, w_br_bf, l, s, w_br if side else None)
            g["x"], *nxt_out = _outproj(merged, g["x"], ada4, g["row"], w_out_bf, l, s,
                                        w_out if side else None)
            if side:
                nxt = (nxt_in, nxt_br, nxt_out)
        if l + 1 < depth:
            (w_in_bf,), (w_br_bf,), (w_out_bf,) = nxt

    nk, nv = new_kv
    return (groups[0]["x"].reshape(batch, seq, d),
            groups[1]["x"].reshape(dec_batch, dec_seq, d),
            nk.reshape(batch, depth, seq, N_HEADS, HEAD_DIM),
            nv.reshape(batch, depth, seq, N_HEADS, HEAD_DIM))
```

```python
import functools
import math

import jax
import jax.numpy as jnp
import numpy as np
from jax import lax
from jax.experimental import pallas as pl
from jax.experimental.pallas import tpu as pltpu

F32 = jnp.float32
BF16 = jnp.bfloat16

D_MODEL = 2048
GRID_W = 64
N_HEADS = 8
HEAD_DIM = 128
ATT_W = N_HEADS * HEAD_DIM
MAX_KH = 8
KW = 16
FNET_GROUPS = 4
FNET_GDIM = 128
FNET_W = FNET_GROUPS * FNET_GDIM
HY_W = 512
HYENA_ORDER = 2
FILTER_EMB = 33
FILTER_FF = 64
MIN_DECAY = math.log(1e-2) / 1.5
MAX_DECAY = math.log(1e-2) / 0.3
OFF_Q = 0
OFF_K = OFF_Q + ATT_W
OFF_V = OFF_K + ATT_W
OFF_GA = OFF_V + ATT_W
OFF_UB = OFF_GA + ATT_W
OFF_GB = OFF_UB + FNET_W
OFF_HC = OFF_GB + FNET_W
OFF_GC = OFF_HC + 3 * HY_W
OFF_MG = OFF_GC + HY_W
N_IN = OFF_MG + 3 * D_MODEL
EPS = 1e-6
NEG = -1e30
ATT_SCALE = HEAD_DIM ** -0.5

LANES = 128
SUBLANES = 8
ADA_ROWS = 8
FREQ_CHUNK = 512
LAT_ROW_BLOCK = 4
ROW_TILE = 1024
COL_TILE = 1024
MIB = 1024 * 1024


def _params(semantics, vmem_mib):
    return pltpu.CompilerParams(dimension_semantics=semantics,
                                vmem_limit_bytes=vmem_mib * MIB)


def _silu(x):
    return x * jax.nn.sigmoid(x)


def _dot(a, b):
    return jnp.dot(a, b, preferred_element_type=F32)


def _dot_nt(a, b):
    return lax.dot_general(a, b, (((1,), (1,)), ((), ())), preferred_element_type=F32)


def _ada_kernel(cv_ref, w_ref, b_ref, o_ref, s_ref):
    n_rows, d, _ = cv_ref.shape
    tn = w_ref.shape[1]

    @pl.when((pl.program_id(0) == 0) & (pl.program_id(1) == 0))
    def _():
        s_ref[...] = _silu(cv_ref[...])

    def body(i, acc):
        rows = pl.ds(pl.multiple_of(i * SUBLANES, SUBLANES), SUBLANES)
        w = w_ref[rows, :]
        return tuple(acc[r] + w * jnp.concatenate([s_ref[r, rows, :]] * (tn // LANES), axis=-1)
                     for r in range(n_rows))

    zero = jnp.zeros((SUBLANES, tn), F32)
    acc = lax.fori_loop(0, d // SUBLANES, body, (zero,) * n_rows, unroll=4)
    o_ref[...] = jnp.zeros(o_ref.shape, F32)
    for r in range(n_rows):
        o_ref[r:r + 1, :] = jnp.sum(acc[r], axis=0, keepdims=True) + b_ref[...]


def _ada(conds, w_ada, b_ada):
    depth, d, n = w_ada.shape
    rows = conds.shape[0]
    assert rows <= ADA_ROWS
    tn = COL_TILE
    cv = jnp.broadcast_to(conds[:, :, None], (rows, d, LANES))
    return pl.pallas_call(
        _ada_kernel,
        grid=(depth, n // tn),
        in_specs=[
            pl.BlockSpec((rows, d, LANES), lambda l, j: (0, 0, 0)),
            pl.BlockSpec((None, d, tn), lambda l, j: (l, 0, j)),
            pl.BlockSpec((None, 1, tn), lambda l, j: (l, 0, j)),
        ],
        out_specs=pl.BlockSpec((None, ADA_ROWS, tn), lambda l, j: (l, 0, j)),
        out_shape=jax.ShapeDtypeStruct((depth, ADA_ROWS, n), F32),
        scratch_shapes=[pltpu.VMEM((rows, d, LANES), F32)],
        compiler_params=_params(("arbitrary", "arbitrary"), 32),
        name="ada",
    )(cv, w_ada, b_ada.reshape(depth, 1, n))


def _cast_next(src_ref, dst_ref):
    dst_ref[...] = src_ref[...].astype(BF16)


def _out_ref(refs):
    if len(refs) == 1:
        return refs[0]
    src_ref, o_ref, dst_ref = refs
    _cast_next(src_ref, dst_ref)
    return o_ref


def _next_weight_specs(w32, l, n_i, tn):
    if w32 is None or l + 1 >= w32.shape[0]:
        return [], [], [], []
    _, rows, cols = w32.shape
    rb = rows // n_i
    return ([w32], [pl.BlockSpec((None, rb, tn), lambda i, j: (l + 1, i, j))],
            [pl.BlockSpec((rb, tn), lambda i, j: (i, j))], [jax.ShapeDtypeStruct((rows, cols), BF16)])


def _inproj_kernel(x_ref, g_ref, ada_ref, w_ref, *refs, rows, n_plain):
    if len(refs) == 5:
        nxt32_ref, o_ref, gate_ref, nxt_ref, h_ref = refs
        _cast_next(nxt32_ref, nxt_ref)
    else:
        o_ref, gate_ref, h_ref = refs
    j = pl.program_id(1)

    @pl.when(j == 0)
    def _():
        g = g_ref[...]
        shift = ada_ref[:, 0:D_MODEL]
        scale1 = 1.0 + ada_ref[:, D_MODEL:2 * D_MODEL]

        def body(r, carry):
            sl = pl.ds(pl.multiple_of(r * rows, rows), rows)
            x = x_ref[sl, :]
            ms = jnp.mean(x * x, axis=-1, keepdims=True)
            y = x * lax.rsqrt(ms + EPS) * g
            h_ref[sl, :] = (y * scale1 + shift).astype(BF16)
            return carry

        lax.fori_loop(0, x_ref.shape[0] // rows, body, 0)

    @pl.when(j < n_plain)
    def _():
        o_ref[...] = _dot(h_ref[...], w_ref[...])

    @pl.when(j >= n_plain)
    def _():
        gate_ref[...] = jax.nn.sigmoid(_dot(h_ref[...], w_ref[...])).astype(BF16)


def _inproj(x2d, norm_g3, ada4, ada_row, w_bf, l, seq, w32=None):
    m, d = x2d.shape
    tm = min(ROW_TILE, m)
    tn = COL_TILE
    n_plain = OFF_MG // tn
    nxt_args, nxt_in, nxt_out, nxt_shape = _next_weight_specs(w32, l, m // tm, tn)
    return pl.pallas_call(
        functools.partial(_inproj_kernel, rows=128, n_plain=n_plain),
        grid=(m // tm, N_IN // tn),
        in_specs=[
            pl.BlockSpec((tm, d), lambda i, j: (i, 0)),
            pl.BlockSpec((None, 1, d), lambda i, j: (l, 0, 0)),
            pl.BlockSpec((None, None, 1, 3 * d), lambda i, j: (l, ada_row(i * tm), 0, 0)),
            pl.BlockSpec((d, tn), lambda i, j: (0, j)),
        ] + nxt_in,
        out_specs=[pl.BlockSpec((tm, tn), lambda i, j: (i, jnp.minimum(j, n_plain - 1))),
                   pl.BlockSpec((tm, tn), lambda i, j: (i, jnp.maximum(j - n_plain, 0)))] + nxt_out,
        out_shape=[jax.ShapeDtypeStruct((m, OFF_MG), F32),
                   jax.ShapeDtypeStruct((m, N_IN - OFF_MG), BF16)] + nxt_shape,
        scratch_shapes=[pltpu.VMEM((tm, d), BF16)],
        compiler_params=_params(("arbitrary", "arbitrary"), 52),
        name="inproj",
    )(x2d, norm_g3, ada4, w_bf, *nxt_args)


def _head_norm(x, g):
    ms = jnp.mean(x * x, axis=-1, keepdims=True)
    return x * lax.rsqrt(ms + EPS) * g


def _prep_lat_kernel(q_ref, k_ref, v_ref, qg_ref, kg_ref, qn_ref, kn_ref, vb_ref):
    for h in range(N_HEADS):
        sl = slice(h * HEAD_DIM, (h + 1) * HEAD_DIM)
        qn_ref[:, sl] = _head_norm(q_ref[:, sl], qg_ref[...]).astype(BF16)
        kn_ref[:, sl] = _head_norm(k_ref[:, sl], kg_ref[...]).astype(BF16)
    vb_ref[...] = v_ref[...].astype(BF16)


def _prep_lat(proj, qg3, kg3, l):
    m = proj.shape[0]
    tm = min(256, m)
    col = lambda c: pl.BlockSpec((tm, ATT_W), lambda i: (i, c))
    gspec = pl.BlockSpec((None, 1, HEAD_DIM), lambda i: (l, 0, 0))
    return pl.pallas_call(
        _prep_lat_kernel, grid=(m // tm,),
        in_specs=[col(OFF_Q // ATT_W), col(OFF_K // ATT_W), col(OFF_V // ATT_W), gspec, gspec],
        out_specs=[col(0)] * 3, out_shape=[jax.ShapeDtypeStruct((m, ATT_W), BF16)] * 3,
        compiler_params=_params(("arbitrary",), 32), name="prep_lat",
    )(proj, proj, proj, qg3, kg3)


def _ctx_attn_kernel(q_ref, k_ref, v_ref, ga_ref, qg_ref, kg_ref, *refs, layer):
    o_ref, nk_ref, nv_ref = refs[-3:]
    if len(refs) == 3:
        for other in range(nk_ref.shape[0]):
            if other != layer:
                nk_ref[other] = jnp.zeros(nk_ref.shape[1:], F32)
                nv_ref[other] = jnp.zeros(nv_ref.shape[1:], F32)
        nk_ref, nv_ref = nk_ref.at[layer], nv_ref.at[layer]
    for h in range(N_HEADS):
        sl = slice(h * HEAD_DIM, (h + 1) * HEAD_DIM)
        q = _head_norm(q_ref[:, sl], qg_ref[...])
        k = _head_norm(k_ref[:, sl], kg_ref[...])
        v = v_ref[:, sl]
        nk_ref[:, sl] = k
        nv_ref[:, sl] = v
        s = _dot_nt(q.astype(BF16), k.astype(BF16)) * ATT_SCALE
        p = jnp.exp(s - jnp.max(s, axis=-1, keepdims=True))
        inv = 1.0 / jnp.sum(p, axis=-1, keepdims=True)
        o = _dot(p.astype(BF16), v.astype(BF16)) * inv
        o_ref[:, sl] = (o * _silu(ga_ref[:, sl])).astype(BF16)


def _ctx_attn(proj, qg3, kg3, new_kv, l, seq, depth):
    m = proj.shape[0]
    kv_shape = jax.ShapeDtypeStruct((m // seq, depth, seq, ATT_W), F32)
    col = lambda c: pl.BlockSpec((seq, ATT_W), lambda b: (b, c))
    gspec = pl.BlockSpec((None, 1, HEAD_DIM), lambda b: (l, 0, 0))
    if new_kv:
        kv_spec = pl.BlockSpec((None, None, seq, ATT_W), lambda b: (b, l, 0, 0))
    else:
        kv_spec = pl.BlockSpec((None, depth, seq, ATT_W), lambda b: (b, 0, 0, 0))
    return pl.pallas_call(
        functools.partial(_ctx_attn_kernel, layer=l), grid=(m // seq,),
        in_specs=[col(OFF_Q // ATT_W), col(OFF_K // ATT_W), col(OFF_V // ATT_W), col(OFF_GA // ATT_W),
                  gspec, gspec] + [pl.BlockSpec(memory_space=pl.ANY)] * len(new_kv),
        out_specs=[col(0), kv_spec, kv_spec],
        out_shape=[jax.ShapeDtypeStruct((m, ATT_W), BF16), kv_shape, kv_shape],
        input_output_aliases={6: 1, 7: 2} if new_kv else {},
        compiler_params=_params(("arbitrary",), 32), name="ctx_attn",
    )(proj, proj, proj, proj, qg3, kg3, *new_kv)


def _window_bias(rpb):
    depth, heads, n_dr, n_dc = rpb.shape
    n_e = n_dr + 1
    c = np.arange(GRID_W)[:, None, None]
    p = np.arange(2)[None, :, None]
    kc = np.arange(GRID_W)[None, None, :]
    cs = np.clip(c - KW // 2, 0, GRID_W - KW)
    in_cols = np.broadcast_to((kc >= cs) & (kc < cs + KW), (GRID_W, 2, GRID_W)).reshape(-1)
    dc = np.broadcast_to(np.clip(kc - c + KW - 1, 0, n_dc - 1), (GRID_W, 2, GRID_W)).reshape(-1)
    member = np.broadcast_to(p, (GRID_W, 2, GRID_W)).reshape(-1)
    row_member = np.repeat(np.arange(2), n_dc)[:, None]
    row_dc = np.tile(np.arange(n_dc), 2)[:, None]
    onehot = ((jnp.asarray(row_member) == jnp.asarray(member)[None, :])
              & (jnp.asarray(row_dc) == jnp.asarray(dc)[None, :])).astype(F32)
    padded = jnp.pad(rpb, ((0, 0), (0, 0), (1, 1), (0, 0)))
    pairs = jnp.concatenate([padded[:, :, :-1], padded[:, :, 1:]], axis=-1)
    cb = jnp.einsum("lhek,kx->lhex", pairs, onehot, precision=lax.Precision.HIGHEST)
    row_ok = (np.arange(n_e)[:, None] + member[None, :] >= 1) & (np.arange(n_e)[:, None] + member[None, :] <= n_dr)
    cb = jnp.where(jnp.asarray(row_ok & in_cols[None, :]), cb, NEG)
    return cb.reshape(depth, heads, n_e, GRID_W, 2 * GRID_W)


def _lat_attn_kernel(q_ref, k_ref, v_ref, ck_ref, cv_ref, bias_ref, ga_ref, o_ref,
                     ckb_ref, cvb_ref, *, rows, kh, rb, kr):
    g = pl.program_id(1)

    @pl.when(g == 0)
    def _():
        for h in range(N_HEADS):
            sl = slice(h * HEAD_DIM, (h + 1) * HEAD_DIM)
            ckb_ref[:, sl] = ck_ref[:, h, :].astype(BF16)
            cvb_ref[:, sl] = cv_ref[:, h, :].astype(BF16)

    r0 = g * rb
    ws = jnp.clip(r0 - kh // 2, 0, rows - kr)
    span = pl.ds(pl.multiple_of(ws * GRID_W, GRID_W), kr * GRID_W)
    q_row = r0 + lax.broadcasted_iota(jnp.int32, (rb * GRID_W, 1), 0) // GRID_W
    k_row = ws + lax.broadcasted_iota(jnp.int32, (1, kr * GRID_W), 1) // GRID_W
    q_rs = jnp.clip(q_row - kh // 2, 0, rows - kh)
    in_window = (k_row >= q_rs) & (k_row < q_rs + kh)
    n_pairs = bias_ref.shape[1]
    for h in range(N_HEADS):
        sl = slice(h * HEAD_DIM, (h + 1) * HEAD_DIM)
        q = q_ref[:, sl]
        bias = jnp.concatenate(
            [jnp.concatenate(
                [bias_ref[h, jnp.clip(ws + 2 * j - (r0 + qi) + MAX_KH, 0, n_pairs - 1)]
                 for j in range(kr // 2)], axis=-1)
             for qi in range(rb)], axis=0)
        s_win = jnp.where(in_window, _dot_nt(q, k_ref[span, sl]) * ATT_SCALE + bias, NEG)
        s_ctx = _dot_nt(q, ckb_ref[:, sl]) * ATT_SCALE
        mx = jnp.maximum(jnp.max(s_win, axis=-1, keepdims=True),
                         jnp.max(s_ctx, axis=-1, keepdims=True))
        p_win = jnp.exp(s_win - mx)
        p_ctx = jnp.exp(s_ctx - mx)
        inv = 1.0 / (jnp.sum(p_win, axis=-1, keepdims=True) + jnp.sum(p_ctx, axis=-1, keepdims=True))
        o = (_dot(p_win.astype(BF16), v_ref[span, sl]) + _dot(p_ctx.astype(BF16), cvb_ref[:, sl])) * inv
        o_ref[:, sl] = (o * _silu(ga_ref[:, sl])).astype(BF16)


def _lat_attn(qn, kn, vb, cache_k, cache_v, bias, proj, l, seq):
    m = qn.shape[0]
    batch = m // seq
    rows = seq // GRID_W
    kh = min(MAX_KH, rows)
    rb = min(LAT_ROW_BLOCK, rows)
    kr = min(rows, kh + rb)
    assert rows % rb == 0 and kr % 2 == 0
    past = cache_k.shape[2]
    qspec = pl.BlockSpec((rb * GRID_W, ATT_W), lambda b, g: (b * (rows // rb) + g, 0))
    kvspec = pl.BlockSpec((seq, ATT_W), lambda b, g: (b, 0))
    cspec = pl.BlockSpec((None, None, past, N_HEADS, HEAD_DIM), lambda b, g: (b, l, 0, 0, 0))
    return pl.pallas_call(
        functools.partial(_lat_attn_kernel, rows=rows, kh=kh, rb=rb, kr=kr),
        grid=(batch, rows // rb),
        in_specs=[qspec, kvspec, kvspec, cspec, cspec,
                  pl.BlockSpec((None,) + bias.shape[1:], lambda b, g: (l, 0, 0, 0, 0)),
                  pl.BlockSpec((rb * GRID_W, ATT_W),
                               lambda b, g: (b * (rows // rb) + g, OFF_GA // ATT_W))],
        out_specs=qspec,
        out_shape=jax.ShapeDtypeStruct((m, ATT_W), BF16),
        scratch_shapes=[pltpu.VMEM((past, ATT_W), BF16), pltpu.VMEM((past, ATT_W), BF16)],
        compiler_params=_params(("arbitrary", "arbitrary"), 48), name="lat_attn",
    )(qn, kn, vb, cache_k, cache_v, bias, proj)


def _phase_cos_sin(phase, period):
    ang = (phase % period).astype(F32) * (2.0 * math.pi / period)
    return jnp.cos(ang), jnp.sin(ang)


def _trig_kernel(ca_ref, sa_ref, cb_ref, sb_ref, *refs, patch_row):
    o_ref = refs[-1]
    cb, sb = cb_ref[...], sb_ref[...]
    for a in range(ca_ref.shape[0]):
        blk = ca_ref[a:a + 1, :] * cb - sa_ref[a:a + 1, :] * sb
        if patch_row is not None:
            row = (pl.program_id(1) * ca_ref.shape[0] + a) * LANES + lax.broadcasted_iota(jnp.int32, (LANES, 1), 0)
            blk = jnp.where(row == patch_row, refs[0][...], blk)
        o_ref[a * LANES:(a + 1) * LANES, :] = blk.astype(o_ref.dtype)


def _trig_table(blk_freq, blk_off, col_pos, col_off, period, patch=None):
    n_j, n_col = col_pos.shape
    n_blk = blk_freq.shape[0]
    ca, sa = _phase_cos_sin(col_pos[:, None, :] * blk_freq[None, :, None] + blk_off[None, :, None]
                            + col_off[:, None, :], period)
    cb, sb = _phase_cos_sin(col_pos[:, None, :] * jnp.arange(LANES, dtype=jnp.int32)[None, :, None], period)
    ab = min(SUBLANES, n_blk)
    blk_spec = pl.BlockSpec((None, ab, n_col), lambda j, i: (j, i, 0))
    tab_spec = pl.BlockSpec((None, LANES, n_col), lambda j, i: (j, 0, 0))
    extra, extra_specs = ([], []) if patch is None else ([patch[1]], [pl.BlockSpec((1, n_col), lambda j, i: (0, 0))])
    return pl.pallas_call(
        functools.partial(_trig_kernel, patch_row=None if patch is None else patch[0]),
        grid=(n_j, n_blk // ab),
        in_specs=[blk_spec, blk_spec, tab_spec, tab_spec] + extra_specs,
        out_specs=pl.BlockSpec((None, ab * LANES, n_col), lambda j, i: (j, i, 0)),
        out_shape=jax.ShapeDtypeStruct((n_j, n_blk * LANES, n_col), BF16),
        compiler_params=_params(("arbitrary", "arbitrary"), 32), name="trig_table",
    )(ca, sa, cb, sb, *extra)


def _fnet_tables(seq):
    col = jnp.arange(2 * seq, dtype=jnp.int32)[None, :]
    blk = jnp.arange(seq // LANES, dtype=jnp.int32) * LANES
    tab = _trig_table(blk, jnp.zeros_like(blk), col % seq, (col // seq) * (seq // 4), seq)[0]
    g = jnp.arange(FNET_GDIM, dtype=jnp.int32)
    cc, sc = _phase_cos_sin(g[:, None] * g[None, :], FNET_GDIM)
    return tab, cc.astype(BF16), sc.astype(BF16)


def _fnet_kernel(u_ref, gb_ref, tl_ref, cc_ref, sc_ref, o_ref, t_ref, *, seq, tl, sb):
    i = pl.program_id(1)

    @pl.when(i == 0)
    def _():
        for s in range(sb):
            for g in range(FNET_GROUPS):
                ug = u_ref[s * seq:(s + 1) * seq, g * FNET_GDIM:(g + 1) * FNET_GDIM].astype(BF16)
                lanes = slice(s * FNET_W + g * FNET_GDIM, s * FNET_W + (g + 1) * FNET_GDIM)
                t_ref[0:seq, lanes] = _dot(ug, cc_ref[...]).astype(BF16)
                t_ref[seq:2 * seq, lanes] = _dot(ug, sc_ref[...]).astype(BF16)

    y = _dot(tl_ref[...], t_ref[...]) * ((seq * FNET_GDIM) ** -0.5)
    for s in range(sb):
        rows = slice(s * tl, (s + 1) * tl)
        o_ref[rows, :] = (y[:, s * FNET_W:(s + 1) * FNET_W] * _silu(gb_ref[rows, :])).astype(BF16)


def _fnet(proj, tables, seq):
    m = proj.shape[0]
    tab_l, cc, sc = tables
    tl = min(512, seq)
    nt = seq // tl
    sb = min(4, m // seq) if nt == 1 else 1
    return pl.pallas_call(
        functools.partial(_fnet_kernel, seq=seq, tl=tl, sb=sb),
        grid=(m // (sb * seq), nt),
        in_specs=[
            pl.BlockSpec((sb * seq, FNET_W), lambda b, i: (b, OFF_UB // FNET_W)),
            pl.BlockSpec((sb * tl, FNET_W), lambda b, i: (b * nt + i, OFF_GB // FNET_W)),
            pl.BlockSpec((tl, 2 * seq), lambda b, i: (i, 0)),
            pl.BlockSpec((FNET_GDIM, FNET_GDIM), lambda b, i: (0, 0)),
            pl.BlockSpec((FNET_GDIM, FNET_GDIM), lambda b, i: (0, 0)),
        ],
        out_specs=pl.BlockSpec((sb * tl, FNET_W), lambda b, i: (b * nt + i, 0)),
        out_shape=jax.ShapeDtypeStruct((m, FNET_W), BF16),
        scratch_shapes=[pltpu.VMEM((2 * seq, sb * FNET_W), BF16)],
        compiler_params=_params(("arbitrary", "arbitrary"), 40), name="fnet",
    )(proj, proj, tab_l, cc, sc)


def _hyena_tiling(seq):
    if seq > 1024:
        return min(FREQ_CHUNK, seq), HY_W // 2, 2
    return min(FREQ_CHUNK, seq), HY_W, 4


def _hyena_tables(seq):
    fc = _hyena_tiling(seq)[0]
    nch = seq // fc
    period = 2 * seq
    pos = jnp.arange(seq, dtype=jnp.int32)
    zeros = jnp.zeros_like(pos)
    blk = jnp.arange(2 * seq // LANES, dtype=jnp.int32) * LANES
    k0 = blk % (2 * fc)
    blk_freq = (blk // (2 * fc)) * fc + k0 % fc
    blk_off = (k0 // fc) * (period // 4)
    nyquist = (1 - 2 * (pos % 2)).astype(F32)[None, :]
    fwd = _trig_table(blk_freq, blk_off, pos[None, :], zeros[None, :], period, patch=(fc, nyquist))
    k = jnp.arange(2 * fc, dtype=jnp.int32)
    f = jnp.arange(nch, dtype=jnp.int32)[:, None] * fc + (k % fc)[None, :]
    is_sec = jnp.broadcast_to((k >= fc)[None, :], f.shape)
    nyq = is_sec & (f == 0)
    row_blk = jnp.arange(seq // LANES, dtype=jnp.int32) * LANES
    inv = _trig_table(row_blk, jnp.zeros_like(row_blk), jnp.where(nyq, seq, f),
                      jnp.where(is_sec & ~nyq, period // 4, 0), period)
    return fwd.reshape(nch, 2 * fc, seq), inv


def _filter_features(seq):
    t = jnp.linspace(0.0, 1.0, seq, dtype=F32)[:, None]
    bands = (FILTER_EMB - 1) // 2
    w = (2.0 * math.pi / seq) * jnp.arange(seq, dtype=F32)[:, None]
    f = jnp.linspace(1e-4, bands - 1, bands, dtype=F32)[None, :]
    z = jnp.concatenate([t, jnp.cos(w * f), -jnp.sin(w * f)], axis=-1)
    z = jnp.pad(z, ((0, 0), (0, 128 - FILTER_EMB)))
    deltas = jnp.abs(jnp.linspace(MIN_DECAY, MAX_DECAY, HY_W, dtype=F32))[None, :]
    return z, t, deltas


def _dot_f32(a, b):
    return jnp.dot(a, b, preferred_element_type=F32, precision=lax.Precision.HIGHEST)


def _split_bf16(x):
    hi = x.astype(BF16)
    return hi, (x - hi.astype(F32)).astype(BF16)


def _dot_3pass(a, b_hi, b_lo):
    a_hi, a_lo = _split_bf16(a)
    return _dot(a_hi, b_hi) + (_dot(a_hi, b_lo) + _dot(a_lo, b_hi))


def _filter_kernel(z_ref, t_ref, dl_ref, w1_ref, b1_ref, fr_ref, w2_ref, b2_ref, w3_ref,
                   fwd_ref, o_ref, h_ref, sd_ref, nyq_ref, *, seq, fc, rows):
    j = pl.program_id(0)
    nblk = 2 * HYENA_ORDER
    nrc = seq // rows

    @pl.when(j == 0)
    def _():
        fr = fr_ref[...]
        w3_hi, w3_lo = _split_bf16(w3_ref[...])

        def taps(r, acc):
            sl = pl.ds(pl.multiple_of(r * rows, rows), rows)
            h = jnp.sin(fr * (_dot_f32(z_ref[sl, :], w1_ref[...]) + b1_ref[...]))
            h = jnp.sin(fr * (_dot_f32(h, w2_ref[...]) + b2_ref[...]))
            h = _dot_3pass(h, w3_hi, w3_lo)
            decay = jnp.exp(-t_ref[sl, :] * dl_ref[...])
            h = h * jnp.concatenate([decay] * nblk, axis=-1)
            h_ref[sl, :] = h
            return acc + jnp.sum(jnp.abs(h), axis=0, keepdims=True)

        tot = lax.fori_loop(0, nrc, taps, jnp.zeros((1, nblk * HY_W), F32))
        inv = 1.0 / (tot + EPS)
        sgn = (1 - 2 * (lax.broadcasted_iota(jnp.int32, (rows, 1), 0) % 2)).astype(F32)

        def fold(r, acc):
            sl = pl.ds(pl.multiple_of(r * rows, rows), rows)
            h = h_ref[sl, :] * inv
            pos = r * rows + lax.broadcasted_iota(jnp.int32, (rows, 1), 0)
            sums = []
            for o in range(HYENA_ORDER):
                fw = h[:, (2 * o) * HY_W:(2 * o + 1) * HY_W]
                bw = jnp.where(pos == 0, 0.0, h[:, (2 * o + 1) * HY_W:(2 * o + 2) * HY_W])
                sums.append(fw + bw)
                sd_ref[0, sl, o * HY_W:(o + 1) * HY_W] = (fw + bw).astype(BF16)
                sd_ref[1, sl, o * HY_W:(o + 1) * HY_W] = (fw - bw).astype(BF16)
            return acc + jnp.sum(jnp.concatenate(sums, axis=-1) * sgn, axis=0, keepdims=True)

        nyq_ref[...] = lax.fori_loop(0, nrc, fold, jnp.zeros((1, HYENA_ORDER * HY_W), F32))

    re = _dot(fwd_ref[0:fc, :], sd_ref[0])
    sec = _dot(fwd_ref[fc:2 * fc, :], sd_ref[1])
    first = (lax.broadcasted_iota(jnp.int32, (fc, 1), 0) == 0) & (j == 0)
    sec = jnp.where(first, nyq_ref[...], sec)
    weight = jnp.where(first, 0.5 / seq, 1.0 / seq)
    o_ref[0:fc, :] = re * weight
    o_ref[fc:2 * fc, :] = sec * weight


def _filter(feats, fwd, w1p, b1, fr, w2, b2, w3, l, seq):
    z, t, deltas = feats
    nch, fc2, _ = fwd.shape
    fc = fc2 // 2
    width = HYENA_ORDER * HY_W
    rows = min(256, seq)
    full = lambda a: pl.BlockSpec(a.shape, lambda j: (0,) * a.ndim)
    lay = lambda a: pl.BlockSpec((None,) + a.shape[1:], lambda j: (l,) + (0,) * (a.ndim - 1))
    return pl.pallas_call(
        functools.partial(_filter_kernel, seq=seq, fc=fc, rows=rows),
        grid=(nch,),
        in_specs=[full(z), full(t), full(deltas), lay(w1p), lay(b1), lay(fr), lay(w2), lay(b2), lay(w3),
                  pl.BlockSpec((None, 2 * fc, seq), lambda j: (j, 0, 0))],
        out_specs=pl.BlockSpec((None, 2 * fc, width), lambda j: (j, 0, 0)),
        out_shape=jax.ShapeDtypeStruct((nch, 2 * fc, width), F32),
        scratch_shapes=[pltpu.VMEM((seq, 2 * width), F32),
                        pltpu.VMEM((2, seq, width), BF16),
                        pltpu.VMEM((1, width), F32)],
        compiler_params=_params(("arbitrary",), 48), name="hyena_filter",
    )(z, t, deltas, w1p, b1, fr, w2, b2, w3, fwd)


def _hyena_kernel(v_ref, x1_ref, x2_ref, gc_ref, cw_ref, cb_ref, hb_ref, fwd_ref, inv_ref, kf_ref,
                  o_ref, z_ref, zb_ref, y_ref, *, seq, fc, sb, tc):
    o = pl.program_id(2)
    j = pl.program_id(3)
    last_j = pl.num_programs(3) - 1
    pos = lax.broadcasted_iota(jnp.int32, (seq, 1), 0)

    def short_conv(ref, k, s):
        x = ref[s * seq:(s + 1) * seq, :]
        prev = jnp.where(pos == 0, 0.0, pltpu.roll(x, 1, 0))
        nxt = jnp.where(pos == seq - 1, 0.0, pltpu.roll(x, seq - 1, 0))
        return (prev * cw_ref[0, k:k + 1, :] + x * cw_ref[1, k:k + 1, :]
                + nxt * cw_ref[2, k:k + 1, :] + cb_ref[k:k + 1, :])

    @pl.when((o == 0) & (j == 0))
    def _():
        for s in range(sb):
            lanes = slice(s * tc, (s + 1) * tc)
            z = short_conv(v_ref, 0, s)
            z_ref[:, lanes] = z
            zb_ref[:, lanes] = z.astype(BF16)
        y_ref[...] = jnp.zeros_like(y_ref)

    @pl.when((o == 1) & (j == 0))
    def _():
        for s in range(sb):
            lanes = slice(s * tc, (s + 1) * tc)
            z = short_conv(x1_ref, 1, s) * (y_ref[:, lanes] + hb_ref[0:1, :] * z_ref[:, lanes])
            z_ref[:, lanes] = z
            zb_ref[:, lanes] = z.astype(BF16)
        y_ref[...] = jnp.zeros_like(y_ref)

    zf = _dot(fwd_ref[...], zb_ref[...])
    kr, ks = kf_ref[0:fc, :], kf_ref[fc:2 * fc, :]
    nyq = (lax.broadcasted_iota(jnp.int32, (fc, 1), 0) == 0) & (j == 0)
    parts = []
    for s in range(sb):
        lanes = slice(s * tc, (s + 1) * tc)
        zr, zs = zf[0:fc, lanes], zf[fc:2 * fc, lanes]
        ss = zs * ks
        yr = zr * kr - jnp.where(nyq, 0.0, ss)
        ys = jnp.where(nyq, ss, zr * ks + zs * kr)
        parts.append(jnp.concatenate([yr, ys], axis=0).astype(BF16))
    y_ref[...] += _dot(inv_ref[...], jnp.concatenate(parts, axis=-1))

    @pl.when((o == 1) & (j == last_j))
    def _():
        for s in range(sb):
            lanes = slice(s * tc, (s + 1) * tc)
            z = short_conv(x2_ref, 2, s) * (y_ref[:, lanes] + hb_ref[1:2, :] * z_ref[:, lanes])
            o_ref[s * seq:(s + 1) * seq, :] = (z * _silu(gc_ref[s * seq:(s + 1) * seq, :])).astype(BF16)


def _hyena(proj, cw, cb, hb, fwd, inv, kf, seq):
    m = proj.shape[0]
    nch, fc2, _ = fwd.shape
    fc = fc2 // 2
    _, tc, sb = _hyena_tiling(seq)
    sb = min(sb, m // seq)
    nct = HY_W // tc
    hc0 = OFF_HC // tc
    once = pl.Buffered(1) if seq > 1024 else None
    col = lambda k: pl.BlockSpec((sb * seq, tc), lambda b, c, o, j: (b, hc0 + k * nct + c),
                                 pipeline_mode=once)
    return pl.pallas_call(
        functools.partial(_hyena_kernel, seq=seq, fc=fc, sb=sb, tc=tc),
        grid=(m // (sb * seq), nct, HYENA_ORDER, nch),
        in_specs=[
            col(0), col(1), col(2),
            pl.BlockSpec((sb * seq, tc), lambda b, c, o, j: (b, OFF_GC // tc + c), pipeline_mode=once),
            pl.BlockSpec((3, 3, tc), lambda b, c, o, j: (0, 0, c)),
            pl.BlockSpec((3, tc), lambda b, c, o, j: (0, c)),
            pl.BlockSpec((HYENA_ORDER, tc), lambda b, c, o, j: (0, c)),
            pl.BlockSpec((None, 2 * fc, seq), lambda b, c, o, j: (j, 0, 0)),
            pl.BlockSpec((None, seq, 2 * fc), lambda b, c, o, j: (j, 0, 0)),
            pl.BlockSpec((None, 2 * fc, tc), lambda b, c, o, j: (j, 0, o * nct + c)),
        ],
        out_specs=pl.BlockSpec((sb * seq, tc), lambda b, c, o, j: (b, c)),
        out_shape=jax.ShapeDtypeStruct((m, HY_W), BF16),
        scratch_shapes=[pltpu.VMEM((seq, sb * tc), F32), pltpu.VMEM((seq, sb * tc), BF16),
                        pltpu.VMEM((seq, sb * tc), F32)],
        compiler_params=_params(("arbitrary",) * 4, 56), name="hyena",
    )(proj, proj, proj, proj, cw, cb, hb, fwd, inv, kf)


def _merge_kernel(ya_ref, yb_ref, yc_ref, ga_ref, gb_ref, gc_ref, wa_ref, wb_ref, wc_ref, *refs):
    o_ref = _out_ref(refs)
    acc = ga_ref[...].astype(F32) * _dot(ya_ref[...], wa_ref[...])
    acc += gb_ref[...].astype(F32) * _dot(yb_ref[...], wb_ref[...])
    acc += gc_ref[...].astype(F32) * _dot(yc_ref[...], wc_ref[...])
    o_ref[...] = acc.astype(BF16)


def _merge(ya, yb, yc, gates, w_bf, l, seq, w32=None):
    m = ya.shape[0]
    tm = min(ROW_TILE, m)
    tn = COL_TILE
    d = D_MODEL
    row = lambda w: pl.BlockSpec((tm, w), lambda i, j: (i, 0))
    gate = lambda k: pl.BlockSpec((tm, tn), lambda i, j: (i, k * d // tn + j))
    nxt_args, nxt_in, nxt_out, nxt_shape = _next_weight_specs(w32, l, m // tm, tn)
    return pl.pallas_call(
        _merge_kernel, grid=(m // tm, d // tn),
        in_specs=[row(ATT_W), row(FNET_W), row(HY_W), gate(0), gate(1), gate(2),
                  pl.BlockSpec((ATT_W, tn), lambda i, j: (0, j)),
                  pl.BlockSpec((FNET_W, tn), lambda i, j: (ATT_W // FNET_W, j)),
                  pl.BlockSpec((HY_W, tn), lambda i, j: ((ATT_W + FNET_W) // HY_W, j))] + nxt_in,
        out_specs=[pl.BlockSpec((tm, tn), lambda i, j: (i, j))] + nxt_out,
        out_shape=[jax.ShapeDtypeStruct((m, d), BF16)] + nxt_shape,
        compiler_params=_params(("arbitrary", "arbitrary"), 40), name="merge",
    )(ya, yb, yc, gates, gates, gates, w_bf, w_bf, w_bf, *nxt_args)


def _outproj_kernel(mg_ref, w_ref, x_ref, ada_ref, *refs):
    o_ref = _out_ref(refs)
    o_ref[...] = x_ref[...] + ada_ref[...] * _dot(mg_ref[...], w_ref[...])


def _outproj(merged, x2d, ada4, ada_row, w_bf, l, seq, w32=None):
    m, d = x2d.shape
    tm = min(ROW_TILE, m)
    tn = COL_TILE
    nxt_args, nxt_in, nxt_out, nxt_shape = _next_weight_specs(w32, l, m // tm, tn)
    return pl.pallas_call(
        _outproj_kernel, grid=(m // tm, d // tn),
        in_specs=[pl.BlockSpec((tm, d), lambda i, j: (i, 0)),
                  pl.BlockSpec((d, tn), lambda i, j: (0, j)),
                  pl.BlockSpec((tm, tn), lambda i, j: (i, j)),
                  pl.BlockSpec((None, None, 1, tn),
                               lambda i, j: (l, ada_row(i * tm), 0, 2 * d // tn + j))] + nxt_in,
        out_specs=[pl.BlockSpec((tm, tn), lambda i, j: (i, j))] + nxt_out,
        out_shape=[jax.ShapeDtypeStruct((m, d), F32)] + nxt_shape,
        compiler_params=_params(("arbitrary", "arbitrary"), 44), name="outproj",
    )(merged, w_bf, x2d, ada4, *nxt_args)


def kernel(x_prompt, x_sample, cache_k, cache_v, c, c_ctx, norm_g, w_ada, b_ada, w_in, q_norm_g, k_norm_g, rpb, conv_w, conv_b, f_w1, f_b1, f_freq, f_w2, f_b2, f_w3, hy_bias, w_br, w_out):
    batch, seq, d = x_prompt.shape
    dec_batch, dec_seq, _ = x_sample.shape
    depth = norm_g.shape[0]

    ada4 = _ada(jnp.concatenate([c_ctx[None, :], c], axis=0), w_ada, b_ada)
    ada4 = ada4.reshape(depth, ADA_ROWS, 1, 3 * d)
    ctx_row = lambda tok: 0
    lat_row = lambda tok: 1 + tok // dec_seq

    w_in_bf, w_br_bf, w_out_bf = (w[0].astype(BF16) for w in (w_in, w_br, w_out))
    norm_g3 = norm_g.reshape(depth, 1, d)
    qg3 = q_norm_g.reshape(depth, 1, HEAD_DIM)
    kg3 = k_norm_g.reshape(depth, 1, HEAD_DIM)
    w1p = jnp.pad(f_w1, ((0, 0), (0, 128 - FILTER_EMB), (0, 0)))
    b1 = f_b1.reshape(depth, 1, FILTER_FF)
    fr = f_freq.reshape(depth, 1, FILTER_FF)
    b2 = f_b2.reshape(depth, 1, FILTER_FF)
    win_bias = _window_bias(rpb)

    groups = []
    for x, s, row in ((x_prompt, seq, ctx_row), (x_sample, dec_seq, lat_row)):
        fwd, inv = _hyena_tables(s)
        groups.append(dict(x=x.reshape(-1, d), seq=s, row=row, fnet=_fnet_tables(s),
                           fwd=fwd, inv=inv, feats=_filter_features(s)))

    new_kv = ()
    for l in range(depth):
        cw = conv_w[l].reshape(3, 3, HY_W)
        cb = conv_b[l].reshape(3, HY_W)
        hb = hy_bias[l]
        for gi, g in enumerate(groups):
            s = g["seq"]
            side = gi == 0
            proj, gates, *nxt_in = _inproj(g["x"], norm_g3, ada4, g["row"], w_in_bf, l, s,
                                           w_in if side else None)
            if gi == 0:
                ya, *new_kv = _ctx_attn(proj, qg3, kg3, new_kv, l, s, depth)
            else:
                qn, kn, vb = _prep_lat(proj, qg3, kg3, l)
                ya = _lat_attn(qn, kn, vb, cache_k, cache_v, win_bias, proj, l, s)
            yb = _fnet(proj, g["fnet"], s)
            kf = _filter(g["feats"], g["fwd"], w1p, b1, fr, f_w2, b2, f_w3, l, s)
            yc = _hyena(proj, cw, cb, hb, g["fwd"], g["inv"], kf, s)
            merged, *nxt_br = _merge(ya, yb, yc, gates, w_br_bf, l, s, w_br if side else None)
            g["x"], *nxt_out = _outproj(merged, g["x"], ada4, g["row"], w_out_bf, l, s,
                                        w_out if side else None)
            if side:
                nxt = (nxt_in, nxt_br, nxt_out)
        if l + 1 < depth:
            (w_in_bf,), (w_br_bf,), (w_out_bf,) = nxt

    nk, nv = new_kv
    return (groups[0]["x"].reshape(batch, seq, d),
            groups[1]["x"].reshape(dec_batch, dec_seq, d),
            nk.reshape(batch, depth, seq, N_HEADS, HEAD_DIM),
            nv.reshape(batch, depth, seq, N_HEADS, HEAD_DIM))
```

```python
import functools
import math

import jax
import jax.numpy as jnp
import numpy as np
from jax import lax
from jax.experimental import pallas as pl
from jax.experimental.pallas import tpu as pltpu

F32 = jnp.float32
BF16 = jnp.bfloat16

D_MODEL = 2048
GRID_W = 64
N_HEADS = 8
HEAD_DIM = 128
ATT_W = N_HEADS * HEAD_DIM
MAX_KH = 8
KW = 16
FNET_GROUPS = 4
FNET_GDIM = 128
FNET_W = FNET_GROUPS * FNET_GDIM
HY_W = 512
HYENA_ORDER = 2
FILTER_EMB = 33
FILTER_FF = 64
MIN_DECAY = math.log(1e-2) / 1.5
MAX_DECAY = math.log(1e-2) / 0.3
OFF_Q = 0
OFF_K = OFF_Q + ATT_W
OFF_V = OFF_K + ATT_W
OFF_GA = OFF_V + ATT_W
OFF_UB = OFF_GA + ATT_W
OFF_GB = OFF_UB + FNET_W
OFF_HC = OFF_GB + FNET_W
OFF_GC = OFF_HC + 3 * HY_W
OFF_MG = OFF_GC + HY_W
N_IN = OFF_MG + 3 * D_MODEL
EPS = 1e-6
NEG = -1e30
ATT_SCALE = HEAD_DIM ** -0.5

LANES = 128
SUBLANES = 8
ADA_ROWS = 8
FREQ_CHUNK = 512
LAT_ROW_BLOCK = 4
ROW_TILE = 1024
COL_TILE = 1024
MIB = 1024 * 1024


def _params(semantics, vmem_mib):
    return pltpu.CompilerParams(dimension_semantics=semantics,
                                vmem_limit_bytes=vmem_mib * MIB)


def _sigmoid(x):
    return 0.5 * jnp.tanh(0.5 * x) + 0.5


def _silu(x):
    return x * _sigmoid(x)


def _dot(a, b):
    return jnp.dot(a, b, preferred_element_type=F32)


def _dot_nt(a, b):
    return lax.dot_general(a, b, (((1,), (1,)), ((), ())), preferred_element_type=F32)


def _ada_kernel(cv_ref, w_ref, b_ref, o_ref, s_ref):
    n_rows, d, _ = cv_ref.shape
    tn = w_ref.shape[1]

    @pl.when((pl.program_id(0) == 0) & (pl.program_id(1) == 0))
    def _():
        s_ref[...] = _silu(cv_ref[...])

    def body(i, acc):
        rows = pl.ds(pl.multiple_of(i * SUBLANES, SUBLANES), SUBLANES)
        w = w_ref[rows, :]
        return tuple(acc[r] + w * jnp.concatenate([s_ref[r, rows, :]] * (tn // LANES), axis=-1)
                     for r in range(n_rows))

    zero = jnp.zeros((SUBLANES, tn), F32)
    acc = lax.fori_loop(0, d // SUBLANES, body, (zero,) * n_rows, unroll=4)
    o_ref[...] = jnp.zeros(o_ref.shape, F32)
    for r in range(n_rows):
        o_ref[r:r + 1, :] = jnp.sum(acc[r], axis=0, keepdims=True) + b_ref[...]


def _ada(conds, w_ada, b_ada):
    depth, d, n = w_ada.shape
    rows = conds.shape[0]
    assert rows <= ADA_ROWS
    tn = COL_TILE
    cv = jnp.broadcast_to(conds[:, :, None], (rows, d, LANES))
    return pl.pallas_call(
        _ada_kernel,
        grid=(depth, n // tn),
        in_specs=[
            pl.BlockSpec((rows, d, LANES), lambda l, j: (0, 0, 0)),
            pl.BlockSpec((None, d, tn), lambda l, j: (l, 0, j)),
            pl.BlockSpec((None, 1, tn), lambda l, j: (l, 0, j)),
        ],
        out_specs=pl.BlockSpec((None, ADA_ROWS, tn), lambda l, j: (l, 0, j)),
        out_shape=jax.ShapeDtypeStruct((depth, ADA_ROWS, n), F32),
        scratch_shapes=[pltpu.VMEM((rows, d, LANES), F32)],
        compiler_params=_params(("arbitrary", "arbitrary"), 32),
        name="ada",
    )(cv, w_ada, b_ada.reshape(depth, 1, n))


def _cast_next(src_ref, dst_ref):
    dst_ref[...] = src_ref[...].astype(BF16)


def _out_ref(refs):
    if len(refs) == 1:
        return refs[0]
    src_ref, o_ref, dst_ref = refs
    _cast_next(src_ref, dst_ref)
    return o_ref


def _next_weight_specs(w32, l, n_i, tn):
    if w32 is None or l + 1 >= w32.shape[0]:
        return [], [], [], []
    _, rows, cols = w32.shape
    rb = rows // n_i
    return ([w32], [pl.BlockSpec((None, rb, tn), lambda i, j: (l + 1, i, j))],
            [pl.BlockSpec((rb, tn), lambda i, j: (i, j))], [jax.ShapeDtypeStruct((rows, cols), BF16)])


def _inproj_kernel(x_ref, g_ref, ada_ref, w_ref, *refs, rows, n_plain):
    if len(refs) == 5:
        nxt32_ref, o_ref, gate_ref, nxt_ref, h_ref = refs
        _cast_next(nxt32_ref, nxt_ref)
    else:
        o_ref, gate_ref, h_ref = refs
    j = pl.program_id(1)

    @pl.when(j == 0)
    def _():
        g = g_ref[...]
        shift = ada_ref[:, 0:D_MODEL]
        scale1 = 1.0 + ada_ref[:, D_MODEL:2 * D_MODEL]

        def body(r, carry):
            sl = pl.ds(pl.multiple_of(r * rows, rows), rows)
            x = x_ref[sl, :]
            ms = jnp.mean(x * x, axis=-1, keepdims=True)
            y = x * lax.rsqrt(ms + EPS) * g
            h_ref[sl, :] = (y * scale1 + shift).astype(BF16)
            return carry

        lax.fori_loop(0, x_ref.shape[0] // rows, body, 0)

    @pl.when(j < n_plain)
    def _():
        o_ref[...] = _dot(h_ref[...], w_ref[...])

    @pl.when(j >= n_plain)
    def _():
        gate_ref[...] = _sigmoid(_dot(h_ref[...], w_ref[...])).astype(BF16)


def _inproj(x2d, norm_g3, ada4, ada_row, w_bf, l, w32=None):
    m, d = x2d.shape
    tm = min(ROW_TILE, m)
    tn = COL_TILE
    n_plain = OFF_MG // tn
    nxt_args, nxt_in, nxt_out, nxt_shape = _next_weight_specs(w32, l, m // tm, tn)
    return pl.pallas_call(
        functools.partial(_inproj_kernel, rows=128, n_plain=n_plain),
        grid=(m // tm, N_IN // tn),
        in_specs=[
            pl.BlockSpec((tm, d), lambda i, j: (i, 0)),
            pl.BlockSpec((None, 1, d), lambda i, j: (l, 0, 0)),
            pl.BlockSpec((None, None, 1, 3 * d), lambda i, j: (l, ada_row(i * tm), 0, 0)),
            pl.BlockSpec((d, tn), lambda i, j: (0, j)),
        ] + nxt_in,
        out_specs=[pl.BlockSpec((tm, tn), lambda i, j: (i, jnp.minimum(j, n_plain - 1))),
                   pl.BlockSpec((tm, tn), lambda i, j: (i, jnp.maximum(j - n_plain, 0)))] + nxt_out,
        out_shape=[jax.ShapeDtypeStruct((m, OFF_MG), F32),
                   jax.ShapeDtypeStruct((m, N_IN - OFF_MG), BF16)] + nxt_shape,
        scratch_shapes=[pltpu.VMEM((tm, d), BF16)],
        compiler_params=_params(("arbitrary", "arbitrary"), 52),
        name="inproj",
    )(x2d, norm_g3, ada4, w_bf, *nxt_args)


def _head_norm(x, g):
    ms = jnp.mean(x * x, axis=-1, keepdims=True)
    return x * lax.rsqrt(ms + EPS) * g


def _prep_lat_kernel(q_ref, k_ref, v_ref, qg_ref, kg_ref, qn_ref, kn_ref, vb_ref):
    for h in range(N_HEADS):
        sl = slice(h * HEAD_DIM, (h + 1) * HEAD_DIM)
        qn_ref[:, sl] = _head_norm(q_ref[:, sl], qg_ref[...]).astype(BF16)
        kn_ref[:, sl] = _head_norm(k_ref[:, sl], kg_ref[...]).astype(BF16)
    vb_ref[...] = v_ref[...].astype(BF16)


def _prep_lat(proj, qg3, kg3, l):
    m = proj.shape[0]
    tm = min(256, m)
    col = lambda c: pl.BlockSpec((tm, ATT_W), lambda i: (i, c))
    gspec = pl.BlockSpec((None, 1, HEAD_DIM), lambda i: (l, 0, 0))
    return pl.pallas_call(
        _prep_lat_kernel, grid=(m // tm,),
        in_specs=[col(OFF_Q // ATT_W), col(OFF_K // ATT_W), col(OFF_V // ATT_W), gspec, gspec],
        out_specs=[col(0)] * 3, out_shape=[jax.ShapeDtypeStruct((m, ATT_W), BF16)] * 3,
        compiler_params=_params(("arbitrary",), 32), name="prep_lat",
    )(proj, proj, proj, qg3, kg3)


def _ctx_attn_kernel(q_ref, k_ref, v_ref, ga_ref, qg_ref, kg_ref, *refs, layer):
    o_ref, nk_ref, nv_ref = refs[-3:]
    if len(refs) == 3:
        for other in range(nk_ref.shape[0]):
            if other != layer:
                nk_ref[other] = jnp.zeros(nk_ref.shape[1:], F32)
                nv_ref[other] = jnp.zeros(nv_ref.shape[1:], F32)
        nk_ref, nv_ref = nk_ref.at[layer], nv_ref.at[layer]
    for h in range(N_HEADS):
        sl = slice(h * HEAD_DIM, (h + 1) * HEAD_DIM)
        q = _head_norm(q_ref[:, sl], qg_ref[...])
        k = _head_norm(k_ref[:, sl], kg_ref[...])
        v = v_ref[:, sl]
        nk_ref[:, sl] = k
        nv_ref[:, sl] = v
        s = _dot_nt(q.astype(BF16), k.astype(BF16)) * ATT_SCALE
        p = jnp.exp(s - jnp.max(s, axis=-1, keepdims=True))
        inv = 1.0 / jnp.sum(p, axis=-1, keepdims=True)
        o = _dot(p.astype(BF16), v.astype(BF16)) * inv
        o_ref[:, sl] = (o * _silu(ga_ref[:, sl])).astype(BF16)


def _ctx_attn(proj, qg3, kg3, new_kv, l, seq, depth):
    m = proj.shape[0]
    kv_shape = jax.ShapeDtypeStruct((m // seq, depth, seq, ATT_W), F32)
    col = lambda c: pl.BlockSpec((seq, ATT_W), lambda b: (b, c))
    gspec = pl.BlockSpec((None, 1, HEAD_DIM), lambda b: (l, 0, 0))
    if new_kv:
        kv_spec = pl.BlockSpec((None, None, seq, ATT_W), lambda b: (b, l, 0, 0))
    else:
        kv_spec = pl.BlockSpec((None, depth, seq, ATT_W), lambda b: (b, 0, 0, 0))
    return pl.pallas_call(
        functools.partial(_ctx_attn_kernel, layer=l), grid=(m // seq,),
        in_specs=[col(OFF_Q // ATT_W), col(OFF_K // ATT_W), col(OFF_V // ATT_W), col(OFF_GA // ATT_W),
                  gspec, gspec] + [pl.BlockSpec(memory_space=pl.ANY)] * len(new_kv),
        out_specs=[col(0), kv_spec, kv_spec],
        out_shape=[jax.ShapeDtypeStruct((m, ATT_W), BF16), kv_shape, kv_shape],
        input_output_aliases={6: 1, 7: 2} if new_kv else {},
        compiler_params=_params(("arbitrary",), 32), name="ctx_attn",
    )(proj, proj, proj, proj, qg3, kg3, *new_kv)


def _window_bias(rpb):
    depth, heads, n_dr, n_dc = rpb.shape
    n_e = n_dr + 1
    c = np.arange(GRID_W)[:, None, None]
    p = np.arange(2)[None, :, None]
    kc = np.arange(GRID_W)[None, None, :]
    cs = np.clip(c - KW // 2, 0, GRID_W - KW)
    in_cols = np.broadcast_to((kc >= cs) & (kc < cs + KW), (GRID_W, 2, GRID_W)).reshape(-1)
    dc = np.broadcast_to(np.clip(kc - c + KW - 1, 0, n_dc - 1), (GRID_W, 2, GRID_W)).reshape(-1)
    member = np.broadcast_to(p, (GRID_W, 2, GRID_W)).reshape(-1)
    row_member = np.repeat(np.arange(2), n_dc)[:, None]
    row_dc = np.tile(np.arange(n_dc), 2)[:, None]
    onehot = ((jnp.asarray(row_member) == jnp.asarray(member)[None, :])
              & (jnp.asarray(row_dc) == jnp.asarray(dc)[None, :])).astype(F32)
    padded = jnp.pad(rpb, ((0, 0), (0, 0), (1, 1), (0, 0)))
    pairs = jnp.concatenate([padded[:, :, :-1], padded[:, :, 1:]], axis=-1)
    cb = jnp.einsum("lhek,kx->lhex", pairs, onehot, precision=lax.Precision.HIGHEST)
    row_ok = (np.arange(n_e)[:, None] + member[None, :] >= 1) & (np.arange(n_e)[:, None] + member[None, :] <= n_dr)
    cb = jnp.where(jnp.asarray(row_ok & in_cols[None, :]), cb, NEG)
    return cb.reshape(depth, heads, n_e, GRID_W, 2 * GRID_W)


def _lat_attn_kernel(q_ref, k_ref, v_ref, ck_ref, cv_ref, bias_ref, ga_ref, o_ref,
                     ckb_ref, cvb_ref, *, rows, kh, rb, kr):
    g = pl.program_id(1)

    @pl.when(g == 0)
    def _():
        for h in range(N_HEADS):
            sl = slice(h * HEAD_DIM, (h + 1) * HEAD_DIM)
            ckb_ref[:, sl] = ck_ref[:, h, :].astype(BF16)
            cvb_ref[:, sl] = cv_ref[:, h, :].astype(BF16)

    r0 = g * rb
    ws = jnp.clip(r0 - kh // 2, 0, rows - kr)
    span = pl.ds(pl.multiple_of(ws * GRID_W, GRID_W), kr * GRID_W)
    q_row = r0 + lax.broadcasted_iota(jnp.int32, (rb * GRID_W, 1), 0) // GRID_W
    k_row = ws + lax.broadcasted_iota(jnp.int32, (1, kr * GRID_W), 1) // GRID_W
    q_rs = jnp.clip(q_row - kh // 2, 0, rows - kh)
    in_window = (k_row >= q_rs) & (k_row < q_rs + kh)
    n_pairs = bias_ref.shape[1]
    for h in range(N_HEADS):
        sl = slice(h * HEAD_DIM, (h + 1) * HEAD_DIM)
        q = q_ref[:, sl]
        bias = jnp.concatenate(
            [jnp.concatenate(
                [bias_ref[h, jnp.clip(ws + 2 * j - (r0 + qi) + MAX_KH, 0, n_pairs - 1)]
                 for j in range(kr // 2)], axis=-1)
             for qi in range(rb)], axis=0)
        s_win = jnp.where(in_window, _dot_nt(q, k_ref[span, sl]) * ATT_SCALE + bias, NEG)
        s_ctx = _dot_nt(q, ckb_ref[:, sl]) * ATT_SCALE
        mx = jnp.maximum(jnp.max(s_win, axis=-1, keepdims=True),
                         jnp.max(s_ctx, axis=-1, keepdims=True))
        p_win = jnp.exp(s_win - mx)
        p_ctx = jnp.exp(s_ctx - mx)
        inv = 1.0 / (jnp.sum(p_win, axis=-1, keepdims=True) + jnp.sum(p_ctx, axis=-1, keepdims=True))
        o = (_dot(p_win.astype(BF16), v_ref[span, sl]) + _dot(p_ctx.astype(BF16), cvb_ref[:, sl])) * inv
        o_ref[:, sl] = (o * _silu(ga_ref[:, sl])).astype(BF16)


def _lat_attn(qn, kn, vb, cache_k, cache_v, bias, proj, l, seq):
    m = qn.shape[0]
    batch = m // seq
    rows = seq // GRID_W
    kh = min(MAX_KH, rows)
    rb = min(LAT_ROW_BLOCK, rows)
    kr = min(rows, kh + rb)
    assert rows % rb == 0 and kr % 2 == 0
    past = cache_k.shape[2]
    qspec = pl.BlockSpec((rb * GRID_W, ATT_W), lambda b, g: (b * (rows // rb) + g, 0))
    kvspec = pl.BlockSpec((seq, ATT_W), lambda b, g: (b, 0))
    cspec = pl.BlockSpec((None, None, past, N_HEADS, HEAD_DIM), lambda b, g: (b, l, 0, 0, 0))
    return pl.pallas_call(
        functools.partial(_lat_attn_kernel, rows=rows, kh=kh, rb=rb, kr=kr),
        grid=(batch, rows // rb),
        in_specs=[qspec, kvspec, kvspec, cspec, cspec,
                  pl.BlockSpec((None,) + bias.shape[1:], lambda b, g: (l, 0, 0, 0, 0)),
                  pl.BlockSpec((rb * GRID_W, ATT_W),
                               lambda b, g: (b * (rows // rb) + g, OFF_GA // ATT_W))],
        out_specs=qspec,
        out_shape=jax.ShapeDtypeStruct((m, ATT_W), BF16),
        scratch_shapes=[pltpu.VMEM((past, ATT_W), BF16), pltpu.VMEM((past, ATT_W), BF16)],
        compiler_params=_params(("arbitrary", "arbitrary"), 48), name="lat_attn",
    )(qn, kn, vb, cache_k, cache_v, bias, proj)


def _phase_cos_sin(phase, period):
    ang = (phase % period).astype(F32) * (2.0 * math.pi / period)
    return jnp.cos(ang), jnp.sin(ang)


def _trig_kernel(ca_ref, sa_ref, cb_ref, sb_ref, *refs, patch_row):
    o_ref = refs[-1]
    cb, sb = cb_ref[...], sb_ref[...]
    for a in range(ca_ref.shape[0]):
        blk = ca_ref[a:a + 1, :] * cb - sa_ref[a:a + 1, :] * sb
        if patch_row is not None:
            row = (pl.program_id(1) * ca_ref.shape[0] + a) * LANES + lax.broadcasted_iota(jnp.int32, (LANES, 1), 0)
            blk = jnp.where(row == patch_row, refs[0][...], blk)
        o_ref[a * LANES:(a + 1) * LANES, :] = blk.astype(o_ref.dtype)


def _trig_table(blk_freq, blk_off, col_pos, col_off, period, patch=None):
    n_j, n_col = col_pos.shape
    n_blk = blk_freq.shape[0]
    ca, sa = _phase_cos_sin(col_pos[:, None, :] * blk_freq[None, :, None] + blk_off[None, :, None]
                            + col_off[:, None, :], period)
    cb, sb = _phase_cos_sin(col_pos[:, None, :] * jnp.arange(LANES, dtype=jnp.int32)[None, :, None], period)
    ab = min(SUBLANES, n_blk)
    blk_spec = pl.BlockSpec((None, ab, n_col), lambda j, i: (j, i, 0))
    tab_spec = pl.BlockSpec((None, LANES, n_col), lambda j, i: (j, 0, 0))
    extra, extra_specs = ([], []) if patch is None else ([patch[1]], [pl.BlockSpec((1, n_col), lambda j, i: (0, 0))])
    return pl.pallas_call(
        functools.partial(_trig_kernel, patch_row=None if patch is None else patch[0]),
        grid=(n_j, n_blk // ab),
        in_specs=[blk_spec, blk_spec, tab_spec, tab_spec] + extra_specs,
        out_specs=pl.BlockSpec((None, ab * LANES, n_col), lambda j, i: (j, i, 0)),
        out_shape=jax.ShapeDtypeStruct((n_j, n_blk * LANES, n_col), BF16),
        compiler_params=_params(("arbitrary", "arbitrary"), 32), name="trig_table",
    )(ca, sa, cb, sb, *extra)


def _fnet_tables(seq):
    col = jnp.arange(2 * seq, dtype=jnp.int32)[None, :]
    blk = jnp.arange(seq // LANES, dtype=jnp.int32) * LANES
    tab = _trig_table(blk, jnp.zeros_like(blk), col % seq, (col // seq) * (seq // 4), seq)[0]
    g = jnp.arange(FNET_GDIM, dtype=jnp.int32)
    cc, sc = _phase_cos_sin(g[:, None] * g[None, :], FNET_GDIM)
    return tab, cc.astype(BF16), sc.astype(BF16)


def _fnet_kernel(u_ref, gb_ref, tl_ref, cc_ref, sc_ref, o_ref, t_ref, *, seq, tl, sb):
    i = pl.program_id(1)

    @pl.when(i == 0)
    def _():
        for s in range(sb):
            for g in range(FNET_GROUPS):
                ug = u_ref[s * seq:(s + 1) * seq, g * FNET_GDIM:(g + 1) * FNET_GDIM].astype(BF16)
                lanes = slice(s * FNET_W + g * FNET_GDIM, s * FNET_W + (g + 1) * FNET_GDIM)
                t_ref[0:seq, lanes] = _dot(ug, cc_ref[...]).astype(BF16)
                t_ref[seq:2 * seq, lanes] = _dot(ug, sc_ref[...]).astype(BF16)

    y = _dot(tl_ref[...], t_ref[...]) * ((seq * FNET_GDIM) ** -0.5)
    for s in range(sb):
        rows = slice(s * tl, (s + 1) * tl)
        o_ref[rows, :] = (y[:, s * FNET_W:(s + 1) * FNET_W] * _silu(gb_ref[rows, :])).astype(BF16)


def _fnet(proj, tables, seq):
    m = proj.shape[0]
    tab_l, cc, sc = tables
    tl = min(512, seq)
    nt = seq // tl
    sb = min(4, m // seq) if nt == 1 else 1
    return pl.pallas_call(
        functools.partial(_fnet_kernel, seq=seq, tl=tl, sb=sb),
        grid=(m // (sb * seq), nt),
        in_specs=[
            pl.BlockSpec((sb * seq, FNET_W), lambda b, i: (b, OFF_UB // FNET_W)),
            pl.BlockSpec((sb * tl, FNET_W), lambda b, i: (b * nt + i, OFF_GB // FNET_W)),
            pl.BlockSpec((tl, 2 * seq), lambda b, i: (i, 0)),
            pl.BlockSpec((FNET_GDIM, FNET_GDIM), lambda b, i: (0, 0)),
            pl.BlockSpec((FNET_GDIM, FNET_GDIM), lambda b, i: (0, 0)),
        ],
        out_specs=pl.BlockSpec((sb * tl, FNET_W), lambda b, i: (b * nt + i, 0)),
        out_shape=jax.ShapeDtypeStruct((m, FNET_W), BF16),
        scratch_shapes=[pltpu.VMEM((2 * seq, sb * FNET_W), BF16)],
        compiler_params=_params(("arbitrary", "arbitrary"), 40), name="fnet",
    )(proj, proj, tab_l, cc, sc)


def _hyena_tiling(seq):
    if seq > 1024:
        return min(FREQ_CHUNK, seq), HY_W // 2, 2
    return min(FREQ_CHUNK, seq), HY_W, 4


def _hyena_tables(seq):
    fc = _hyena_tiling(seq)[0]
    nch = seq // fc
    period = 2 * seq
    pos = jnp.arange(seq, dtype=jnp.int32)
    zeros = jnp.zeros_like(pos)
    blk = jnp.arange(2 * seq // LANES, dtype=jnp.int32) * LANES
    k0 = blk % (2 * fc)
    blk_freq = (blk // (2 * fc)) * fc + k0 % fc
    blk_off = (k0 // fc) * (period // 4)
    nyquist = (1 - 2 * (pos % 2)).astype(F32)[None, :]
    fwd = _trig_table(blk_freq, blk_off, pos[None, :], zeros[None, :], period, patch=(fc, nyquist))
    k = jnp.arange(2 * fc, dtype=jnp.int32)
    f = jnp.arange(nch, dtype=jnp.int32)[:, None] * fc + (k % fc)[None, :]
    is_sec = jnp.broadcast_to((k >= fc)[None, :], f.shape)
    nyq = is_sec & (f == 0)
    row_blk = jnp.arange(seq // LANES, dtype=jnp.int32) * LANES
    inv = _trig_table(row_blk, jnp.zeros_like(row_blk), jnp.where(nyq, seq, f),
                      jnp.where(is_sec & ~nyq, period // 4, 0), period)
    return fwd.reshape(nch, 2 * fc, seq), inv


def _filter_features(seq):
    t = jnp.linspace(0.0, 1.0, seq, dtype=F32)[:, None]
    bands = (FILTER_EMB - 1) // 2
    w = (2.0 * math.pi / seq) * jnp.arange(seq, dtype=F32)[:, None]
    f = jnp.linspace(1e-4, bands - 1, bands, dtype=F32)[None, :]
    z = jnp.concatenate([t, jnp.cos(w * f), -jnp.sin(w * f)], axis=-1)
    z = jnp.pad(z, ((0, 0), (0, 128 - FILTER_EMB)))
    deltas = jnp.abs(jnp.linspace(MIN_DECAY, MAX_DECAY, HY_W, dtype=F32))[None, :]
    return z, t, deltas


def _dot_f32(a, b):
    return jnp.dot(a, b, preferred_element_type=F32, precision=lax.Precision.HIGHEST)


def _split_bf16(x):
    hi = x.astype(BF16)
    return hi, (x - hi.astype(F32)).astype(BF16)


def _dot_3pass(a, b_hi, b_lo):
    a_hi, a_lo = _split_bf16(a)
    return _dot(a_hi, b_hi) + (_dot(a_hi, b_lo) + _dot(a_lo, b_hi))


def _filter_kernel(z_ref, t_ref, dl_ref, w1_ref, b1_ref, fr_ref, w2_ref, b2_ref, w3_ref,
                   fwd_ref, o_ref, h_ref, sd_ref, nyq_ref, *, seq, fc, rows):
    j = pl.program_id(0)
    nblk = 2 * HYENA_ORDER
    nrc = seq // rows

    @pl.when(j == 0)
    def _():
        fr = fr_ref[...]
        w3_hi, w3_lo = _split_bf16(w3_ref[...])

        def taps(r, acc):
            sl = pl.ds(pl.multiple_of(r * rows, rows), rows)
            h = jnp.sin(fr * (_dot_f32(z_ref[sl, :], w1_ref[...]) + b1_ref[...]))
            h = jnp.sin(fr * (_dot_f32(h, w2_ref[...]) + b2_ref[...]))
            h = _dot_3pass(h, w3_hi, w3_lo)
            decay = jnp.exp(-t_ref[sl, :] * dl_ref[...])
            h = h * jnp.concatenate([decay] * nblk, axis=-1)
            h_ref[sl, :] = h
            return acc + jnp.sum(jnp.abs(h), axis=0, keepdims=True)

        tot = lax.fori_loop(0, nrc, taps, jnp.zeros((1, nblk * HY_W), F32))
        inv = 1.0 / (tot + EPS)
        sgn = (1 - 2 * (lax.broadcasted_iota(jnp.int32, (rows, 1), 0) % 2)).astype(F32)

        def fold(r, acc):
            sl = pl.ds(pl.multiple_of(r * rows, rows), rows)
            h = h_ref[sl, :] * inv
            pos = r * rows + lax.broadcasted_iota(jnp.int32, (rows, 1), 0)
            sums = []
            for o in range(HYENA_ORDER):
                fw = h[:, (2 * o) * HY_W:(2 * o + 1) * HY_W]
                bw = jnp.where(pos == 0, 0.0, h[:, (2 * o + 1) * HY_W:(2 * o + 2) * HY_W])
                sums.append(fw + bw)
                sd_ref[0, sl, o * HY_W:(o + 1) * HY_W] = (fw + bw).astype(BF16)
                sd_ref[1, sl, o * HY_W:(o + 1) * HY_W] = (fw - bw).astype(BF16)
            return acc + jnp.sum(jnp.concatenate(sums, axis=-1) * sgn, axis=0, keepdims=True)

        nyq_ref[...] = lax.fori_loop(0, nrc, fold, jnp.zeros((1, HYENA_ORDER * HY_W), F32))

    re = _dot(fwd_ref[0:fc, :], sd_ref[0])
    sec = _dot(fwd_ref[fc:2 * fc, :], sd_ref[1])
    first = (lax.broadcasted_iota(jnp.int32, (fc, 1), 0) == 0) & (j == 0)
    sec = jnp.where(first, nyq_ref[...], sec)
    weight = jnp.where(first, 0.5 / seq, 1.0 / seq)
    o_ref[0:fc, :] = re * weight
    o_ref[fc:2 * fc, :] = sec * weight


def _filter(feats, fwd, w1p, b1, fr, w2, b2, w3, l, seq):
    z, t, deltas = feats
    nch, fc2, _ = fwd.shape
    fc = fc2 // 2
    width = HYENA_ORDER * HY_W
    rows = min(256, seq)
    full = lambda a: pl.BlockSpec(a.shape, lambda j: (0,) * a.ndim)
    lay = lambda a: pl.BlockSpec((None,) + a.shape[1:], lambda j: (l,) + (0,) * (a.ndim - 1))
    return pl.pallas_call(
        functools.partial(_filter_kernel, seq=seq, fc=fc, rows=rows),
        grid=(nch,),
        in_specs=[full(z), full(t), full(deltas), lay(w1p), lay(b1), lay(fr), lay(w2), lay(b2), lay(w3),
                  pl.BlockSpec((None, 2 * fc, seq), lambda j: (j, 0, 0))],
        out_specs=pl.BlockSpec((None, 2 * fc, width), lambda j: (j, 0, 0)),
        out_shape=jax.ShapeDtypeStruct((nch, 2 * fc, width), F32),
        scratch_shapes=[pltpu.VMEM((seq, 2 * width), F32),
                        pltpu.VMEM((2, seq, width), BF16),
                        pltpu.VMEM((1, width), F32)],
        compiler_params=_params(("arbitrary",), 48), name="hyena_filter",
    )(z, t, deltas, w1p, b1, fr, w2, b2, w3, fwd)


def _hyena_kernel(v_ref, x1_ref, x2_ref, gc_ref, cw_ref, cb_ref, hb_ref, fwd_ref, inv_ref, kf_ref,
                  o_ref, z_ref, zb_ref, y_ref, *, seq, fc, sb, tc):
    o = pl.program_id(2)
    j = pl.program_id(3)
    last_j = pl.num_programs(3) - 1
    pos = lax.broadcasted_iota(jnp.int32, (seq, 1), 0)

    def short_conv(ref, k, s):
        x = ref[s * seq:(s + 1) * seq, :]
        prev = jnp.where(pos == 0, 0.0, pltpu.roll(x, 1, 0))
        nxt = jnp.where(pos == seq - 1, 0.0, pltpu.roll(x, seq - 1, 0))
        return (prev * cw_ref[0, k:k + 1, :] + x * cw_ref[1, k:k + 1, :]
                + nxt * cw_ref[2, k:k + 1, :] + cb_ref[k:k + 1, :])

    @pl.when((o == 0) & (j == 0))
    def _():
        for s in range(sb):
            lanes = slice(s * tc, (s + 1) * tc)
            z = short_conv(v_ref, 0, s)
            z_ref[:, lanes] = z
            zb_ref[:, lanes] = z.astype(BF16)
        y_ref[...] = jnp.zeros_like(y_ref)

    @pl.when((o == 1) & (j == 0))
    def _():
        for s in range(sb):
            lanes = slice(s * tc, (s + 1) * tc)
            z = short_conv(x1_ref, 1, s) * (y_ref[:, lanes] + hb_ref[0:1, :] * z_ref[:, lanes])
            z_ref[:, lanes] = z
            zb_ref[:, lanes] = z.astype(BF16)
        y_ref[...] = jnp.zeros_like(y_ref)

    zf = _dot(fwd_ref[...], zb_ref[...])
    kr, ks = kf_ref[0:fc, :], kf_ref[fc:2 * fc, :]
    nyq = (lax.broadcasted_iota(jnp.int32, (fc, 1), 0) == 0) & (j == 0)
    parts = []
    for s in range(sb):
        lanes = slice(s * tc, (s + 1) * tc)
        zr, zs = zf[0:fc, lanes], zf[fc:2 * fc, lanes]
        ss = zs * ks
        yr = zr * kr - jnp.where(nyq, 0.0, ss)
        ys = jnp.where(nyq, ss, zr * ks + zs * kr)
        parts.append(jnp.concatenate([yr, ys], axis=0).astype(BF16))
    y_ref[...] += _dot(inv_ref[...], jnp.concatenate(parts, axis=-1))

    @pl.when((o == 1) & (j == last_j))
    def _():
        for s in range(sb):
            lanes = slice(s * tc, (s + 1) * tc)
            z = short_conv(x2_ref, 2, s) * (y_ref[:, lanes] + hb_ref[1:2, :] * z_ref[:, lanes])
            o_ref[s * seq:(s + 1) * seq, :] = (z * _silu(gc_ref[s * seq:(s + 1) * seq, :])).astype(BF16)


def _hyena(proj, cw, cb, hb, fwd, inv, kf, seq):
    m = proj.shape[0]
    nch, fc2, _ = fwd.shape
    fc = fc2 // 2
    _, tc, sb = _hyena_tiling(seq)
    sb = min(sb, m // seq)
    nct = HY_W // tc
    hc0 = OFF_HC // tc
    once = pl.Buffered(1) if seq > 1024 else None
    col = lambda k: pl.BlockSpec((sb * seq, tc), lambda b, c, o, j: (b, hc0 + k * nct + c),
                                 pipeline_mode=once)
    return pl.pallas_call(
        functools.partial(_hyena_kernel, seq=seq, fc=fc, sb=sb, tc=tc),
        grid=(m // (sb * seq), nct, HYENA_ORDER, nch),
        in_specs=[
            col(0), col(1), col(2),
            pl.BlockSpec((sb * seq, tc), lambda b, c, o, j: (b, OFF_GC // tc + c), pipeline_mode=once),
            pl.BlockSpec((3, 3, tc), lambda b, c, o, j: (0, 0, c)),
            pl.BlockSpec((3, tc), lambda b, c, o, j: (0, c)),
            pl.BlockSpec((HYENA_ORDER, tc), lambda b, c, o, j: (0, c)),
            pl.BlockSpec((None, 2 * fc, seq), lambda b, c, o, j: (j, 0, 0)),
            pl.BlockSpec((None, seq, 2 * fc), lambda b, c, o, j: (j, 0, 0)),
            pl.BlockSpec((None, 2 * fc, tc), lambda b, c, o, j: (j, 0, o * nct + c)),
        ],
        out_specs=pl.BlockSpec((sb * seq, tc), lambda b, c, o, j: (b, c)),
        out_shape=jax.ShapeDtypeStruct((m, HY_W), BF16),
        scratch_shapes=[pltpu.VMEM((seq, sb * tc), F32), pltpu.VMEM((seq, sb * tc), BF16),
                        pltpu.VMEM((seq, sb * tc), F32)],
        compiler_params=_params(("arbitrary",) * 4, 56), name="hyena",
    )(proj, proj, proj, proj, cw, cb, hb, fwd, inv, kf)


def _merge_kernel(ya_ref, yb_ref, yc_ref, ga_ref, gb_ref, gc_ref, wa_ref, wb_ref, wc_ref, *refs):
    o_ref = _out_ref(refs)
    acc = ga_ref[...].astype(F32) * _dot(ya_ref[...], wa_ref[...])
    acc += gb_ref[...].astype(F32) * _dot(yb_ref[...], wb_ref[...])
    acc += gc_ref[...].astype(F32) * _dot(yc_ref[...], wc_ref[...])
    o_ref[...] = acc.astype(BF16)


def _merge(ya, yb, yc, gates, w_bf, l, w32=None):
    m = ya.shape[0]
    tm = min(ROW_TILE, m)
    tn = COL_TILE
    d = D_MODEL
    row = lambda w: pl.BlockSpec((tm, w), lambda i, j: (i, 0))
    gate = lambda k: pl.BlockSpec((tm, tn), lambda i, j: (i, k * d // tn + j))
    nxt_args, nxt_in, nxt_out, nxt_shape = _next_weight_specs(w32, l, m // tm, tn)
    return pl.pallas_call(
        _merge_kernel, grid=(m // tm, d // tn),
        in_specs=[row(ATT_W), row(FNET_W), row(HY_W), gate(0), gate(1), gate(2),
                  pl.BlockSpec((ATT_W, tn), lambda i, j: (0, j)),
                  pl.BlockSpec((FNET_W, tn), lambda i, j: (ATT_W // FNET_W, j)),
                  pl.BlockSpec((HY_W, tn), lambda i, j: ((ATT_W + FNET_W) // HY_W, j))] + nxt_in,
        out_specs=[pl.BlockSpec((tm, tn), lambda i, j: (i, j))] + nxt_out,
        out_shape=[jax.ShapeDtypeStruct((m, d), BF16)] + nxt_shape,
        compiler_params=_params(("arbitrary", "arbitrary"), 40), name="merge",
    )(ya, yb, yc, gates, gates, gates, w_bf, w_bf, w_bf, *nxt_args)


def _outproj_kernel(mg_ref, w_ref, x_ref, ada_ref, *refs):
    o_ref = _out_ref(refs)
    o_ref[...] = x_ref[...] + ada_ref[...] * _dot(mg_ref[...], w_ref[...])


def _outproj(merged, x2d, ada4, ada_row, w_bf, l, w32=None):
    m, d = x2d.shape
    tm = min(ROW_TILE, m)
    tn = COL_TILE
    nxt_args, nxt_in, nxt_out, nxt_shape = _next_weight_specs(w32, l, m // tm, tn)
    return pl.pallas_call(
        _outproj_kernel, grid=(m // tm, d // tn),
        in_specs=[pl.BlockSpec((tm, d), lambda i, j: (i, 0)),
                  pl.BlockSpec((d, tn), lambda i, j: (0, j)),
                  pl.BlockSpec((tm, tn), lambda i, j: (i, j)),
                  pl.BlockSpec((None, None, 1, tn),
                               lambda i, j: (l, ada_row(i * tm), 0, 2 * d // tn + j))] + nxt_in,
        out_specs=[pl.BlockSpec((tm, tn), lambda i, j: (i, j))] + nxt_out,
        out_shape=[jax.ShapeDtypeStruct((m, d), F32)] + nxt_shape,
        compiler_params=_params(("arbitrary", "arbitrary"), 44), name="outproj",
    )(merged, w_bf, x2d, ada4, *nxt_args)


def kernel(x_prompt, x_sample, cache_k, cache_v, c, c_ctx, norm_g, w_ada, b_ada, w_in, q_norm_g, k_norm_g, rpb, conv_w, conv_b, f_w1, f_b1, f_freq, f_w2, f_b2, f_w3, hy_bias, w_br, w_out):
    batch, seq, d = x_prompt.shape
    dec_batch, dec_seq, _ = x_sample.shape
    depth = norm_g.shape[0]

    ada4 = _ada(jnp.concatenate([c_ctx[None, :], c], axis=0), w_ada, b_ada)
    ada4 = ada4.reshape(depth, ADA_ROWS, 1, 3 * d)
    ctx_row = lambda tok: 0
    lat_row = lambda tok: 1 + tok // dec_seq

    w_in_bf, w_br_bf, w_out_bf = (w[0].astype(BF16) for w in (w_in, w_br, w_out))
    norm_g3 = norm_g.reshape(depth, 1, d)
    qg3 = q_norm_g.reshape(depth, 1, HEAD_DIM)
    kg3 = k_norm_g.reshape(depth, 1, HEAD_DIM)
    w1p = jnp.pad(f_w1, ((0, 0), (0, 128 - FILTER_EMB), (0, 0)))
    b1 = f_b1.reshape(depth, 1, FILTER_FF)
    fr = f_freq.reshape(depth, 1, FILTER_FF)
    b2 = f_b2.reshape(depth, 1, FILTER_FF)
    win_bias = _window_bias(rpb)

    groups = []
    for x, s, row in ((x_prompt, seq, ctx_row), (x_sample, dec_seq, lat_row)):
        fwd, inv = _hyena_tables(s)
        groups.append(dict(x=x.reshape(-1, d), seq=s, row=row, fnet=_fnet_tables(s),
                           fwd=fwd, inv=inv, feats=_filter_features(s)))

    new_kv = ()
    for l in range(depth):
        cw = conv_w[l].reshape(3, 3, HY_W)
        cb = conv_b[l].reshape(3, HY_W)
        hb = hy_bias[l]
        for gi, g in enumerate(groups):
            s = g["seq"]
            side = gi == 0
            proj, gates, *nxt_in = _inproj(g["x"], norm_g3, ada4, g["row"], w_in_bf, l,
                                           w_in if side else None)
            if gi == 0:
                ya, *new_kv = _ctx_attn(proj, qg3, kg3, new_kv, l, s, depth)
            else:
                qn, kn, vb = _prep_lat(proj, qg3, kg3, l)
                ya = _lat_attn(qn, kn, vb, cache_k, cache_v, win_bias, proj, l, s)
            yb = _fnet(proj, g["fnet"], s)
            kf = _filter(g["feats"], g["fwd"], w1p, b1, fr, f_w2, b2, f_w3, l, s)
            yc = _hyena(proj, cw, cb, hb, g["fwd"], g["inv"], kf, s)
            merged, *nxt_br = _merge(ya, yb, yc, gates, w_br_bf, l, w_br if side else None)
            g["x"], *nxt_out = _outproj(merged, g["x"], ada4, g["row"], w_out_bf, l,
                                        w_out if side else None)
            if side:
                nxt = (nxt_in, nxt_br, nxt_out)
        if l + 1 < depth:
            (w_in_bf,), (w_br_bf,), (w_out_bf,) = nxt

    nk, nv = new_kv
    return (groups[0]["x"].reshape(batch, seq, d),
            groups[1]["x"].reshape(dec_batch, dec_seq, d),
            nk.reshape(batch, depth, seq, N_HEADS, HEAD_DIM),
            nv.reshape(batch, depth, seq, N_HEADS, HEAD_DIM))
```

```python
import functools
import math

import jax
import jax.numpy as jnp
import numpy as np
from jax import lax
from jax.experimental import pallas as pl
from jax.experimental.pallas import tpu as pltpu

F32 = jnp.float32
BF16 = jnp.bfloat16

D_MODEL = 2048
GRID_W = 64
N_HEADS = 8
HEAD_DIM = 128
ATT_W = N_HEADS * HEAD_DIM
MAX_KH = 8
KW = 16
FNET_GROUPS = 4
FNET_GDIM = 128
FNET_W = FNET_GROUPS * FNET_GDIM
HY_W = 512
HYENA_ORDER = 2
FILTER_EMB = 33
FILTER_FF = 64
MIN_DECAY = math.log(1e-2) / 1.5
MAX_DECAY = math.log(1e-2) / 0.3
OFF_Q = 0
OFF_K = OFF_Q + ATT_W
OFF_V = OFF_K + ATT_W
OFF_GA = OFF_V + ATT_W
OFF_UB = OFF_GA + ATT_W
OFF_GB = OFF_UB + FNET_W
OFF_HC = OFF_GB + FNET_W
OFF_GC = OFF_HC + 3 * HY_W
OFF_MG = OFF_GC + HY_W
N_IN = OFF_MG + 3 * D_MODEL
EPS = 1e-6
NEG = -1e30
ATT_SCALE = HEAD_DIM ** -0.5

LANES = 128
SUBLANES = 8
ADA_ROWS = 8
FREQ_CHUNK = 512
LAT_ROW_BLOCK = 4
ROW_TILE = 1024
COL_TILE = 1024
MIB = 1024 * 1024


def _params(semantics, vmem_mib):
    return pltpu.CompilerParams(dimension_semantics=semantics,
                                vmem_limit_bytes=vmem_mib * MIB)


def _sigmoid(x):
    return 0.5 * jnp.tanh(0.5 * x) + 0.5


def _silu(x):
    return x * _sigmoid(x)


def _dot(a, b):
    return jnp.dot(a, b, preferred_element_type=F32)


def _dot_nt(a, b):
    return lax.dot_general(a, b, (((1,), (1,)), ((), ())), preferred_element_type=F32)


def _ada_kernel(cv_ref, w_ref, b_ref, o_ref, s_ref):
    n_rows, d, _ = cv_ref.shape
    tn = w_ref.shape[1]

    @pl.when((pl.program_id(0) == 0) & (pl.program_id(1) == 0))
    def _():
        s_ref[...] = _silu(cv_ref[...])

    def body(i, acc):
        rows = pl.ds(pl.multiple_of(i * SUBLANES, SUBLANES), SUBLANES)
        w = w_ref[rows, :]
        return tuple(acc[r] + w * jnp.concatenate([s_ref[r, rows, :]] * (tn // LANES), axis=-1)
                     for r in range(n_rows))

    zero = jnp.zeros((SUBLANES, tn), F32)
    acc = lax.fori_loop(0, d // SUBLANES, body, (zero,) * n_rows, unroll=4)
    o_ref[...] = jnp.zeros(o_ref.shape, F32)
    for r in range(n_rows):
        o_ref[r:r + 1, :] = jnp.sum(acc[r], axis=0, keepdims=True) + b_ref[...]


def _ada(conds, w_ada, b_ada):
    depth, d, n = w_ada.shape
    rows = conds.shape[0]
    assert rows <= ADA_ROWS
    tn = COL_TILE
    cv = jnp.broadcast_to(conds[:, :, None], (rows, d, LANES))
    return pl.pallas_call(
        _ada_kernel,
        grid=(depth, n // tn),
        in_specs=[
            pl.BlockSpec((rows, d, LANES), lambda l, j: (0, 0, 0)),
            pl.BlockSpec((None, d, tn), lambda l, j: (l, 0, j)),
            pl.BlockSpec((None, 1, tn), lambda l, j: (l, 0, j)),
        ],
        out_specs=pl.BlockSpec((None, ADA_ROWS, tn), lambda l, j: (l, 0, j)),
        out_shape=jax.ShapeDtypeStruct((depth, ADA_ROWS, n), F32),
        scratch_shapes=[pltpu.VMEM((rows, d, LANES), F32)],
        compiler_params=_params(("arbitrary", "arbitrary"), 32),
        name="ada",
    )(cv, w_ada, b_ada.reshape(depth, 1, n))


def _cast_job_specs(jobs, n_i, n_j):
    args, in_specs, out_specs, out_shapes = [], [], [], []
    for w32, layer in jobs:
        _, rows, cols = w32.shape
        rb = rows // n_i
        n_ct = max(t for t in range(1, n_j + 1) if cols % (t * LANES) == 0)
        tw = cols // n_ct
        args.append(w32)
        in_specs.append(pl.BlockSpec((None, rb, tw),
                                     lambda i, j, layer=layer, n_ct=n_ct: (layer, i, jnp.minimum(j, n_ct - 1))))
        out_specs.append(pl.BlockSpec((rb, tw), lambda i, j, n_ct=n_ct: (i, jnp.minimum(j, n_ct - 1))))
        out_shapes.append(jax.ShapeDtypeStruct((rows, cols), BF16))
    return args, in_specs, out_specs, out_shapes


def _inproj_kernel(x_ref, g_ref, ada_ref, w_ref, *refs, rows, n_plain, n_jobs):
    o_ref, gate_ref = refs[n_jobs:n_jobs + 2]
    h_ref = refs[-1]
    for src_ref, dst_ref in zip(refs[:n_jobs], refs[n_jobs + 2:-1]):
        dst_ref[...] = src_ref[...].astype(BF16)
    j = pl.program_id(1)

    @pl.when(j == 0)
    def _():
        g = g_ref[...]
        shift = ada_ref[:, 0:D_MODEL]
        scale1 = 1.0 + ada_ref[:, D_MODEL:2 * D_MODEL]

        def body(r, carry):
            sl = pl.ds(pl.multiple_of(r * rows, rows), rows)
            x = x_ref[sl, :]
            ms = jnp.mean(x * x, axis=-1, keepdims=True)
            y = x * lax.rsqrt(ms + EPS) * g
            h_ref[sl, :] = (y * scale1 + shift).astype(BF16)
            return carry

        lax.fori_loop(0, x_ref.shape[0] // rows, body, 0)

    @pl.when(j < n_plain)
    def _():
        o_ref[...] = _dot(h_ref[...], w_ref[...])

    @pl.when(j >= n_plain)
    def _():
        gate_ref[...] = _sigmoid(_dot(h_ref[...], w_ref[...])).astype(BF16)


def _inproj(x2d, norm_g3, ada4, ada_row, w_bf, l, cast_jobs=()):
    m, d = x2d.shape
    tm = min(ROW_TILE, m)
    tn = COL_TILE
    n_plain = OFF_MG // tn
    nxt_args, nxt_in, nxt_out, nxt_shape = _cast_job_specs(cast_jobs, m // tm, N_IN // tn)
    return pl.pallas_call(
        functools.partial(_inproj_kernel, rows=128, n_plain=n_plain, n_jobs=len(cast_jobs)),
        grid=(m // tm, N_IN // tn),
        in_specs=[
            pl.BlockSpec((tm, d), lambda i, j: (i, 0)),
            pl.BlockSpec((None, 1, d), lambda i, j: (l, 0, 0)),
            pl.BlockSpec((None, None, 1, 3 * d), lambda i, j: (l, ada_row(i * tm), 0, 0)),
            pl.BlockSpec((d, tn), lambda i, j: (0, j)),
        ] + nxt_in,
        out_specs=[pl.BlockSpec((tm, tn), lambda i, j: (i, jnp.minimum(j, n_plain - 1))),
                   pl.BlockSpec((tm, tn), lambda i, j: (i, jnp.maximum(j - n_plain, 0)))] + nxt_out,
        out_shape=[jax.ShapeDtypeStruct((m, OFF_MG), F32),
                   jax.ShapeDtypeStruct((m, N_IN - OFF_MG), BF16)] + nxt_shape,
        scratch_shapes=[pltpu.VMEM((tm, d), BF16)],
        compiler_params=_params(("arbitrary", "arbitrary"), 52),
        name="inproj",
    )(x2d, norm_g3, ada4, w_bf, *nxt_args)


def _head_norm(x, g):
    ms = jnp.mean(x * x, axis=-1, keepdims=True)
    return x * lax.rsqrt(ms + EPS) * g


def _prep_lat_kernel(q_ref, k_ref, v_ref, qg_ref, kg_ref, qn_ref, kn_ref, vb_ref):
    for h in range(N_HEADS):
        sl = slice(h * HEAD_DIM, (h + 1) * HEAD_DIM)
        qn_ref[:, sl] = _head_norm(q_ref[:, sl], qg_ref[...]).astype(BF16)
        kn_ref[:, sl] = _head_norm(k_ref[:, sl], kg_ref[...]).astype(BF16)
    vb_ref[...] = v_ref[...].astype(BF16)


def _prep_lat(proj, qg3, kg3, l):
    m = proj.shape[0]
    tm = min(256, m)
    col = lambda c: pl.BlockSpec((tm, ATT_W), lambda i: (i, c))
    gspec = pl.BlockSpec((None, 1, HEAD_DIM), lambda i: (l, 0, 0))
    return pl.pallas_call(
        _prep_lat_kernel, grid=(m // tm,),
        in_specs=[col(OFF_Q // ATT_W), col(OFF_K // ATT_W), col(OFF_V // ATT_W), gspec, gspec],
        out_specs=[col(0)] * 3, out_shape=[jax.ShapeDtypeStruct((m, ATT_W), BF16)] * 3,
        compiler_params=_params(("arbitrary",), 32), name="prep_lat",
    )(proj, proj, proj, qg3, kg3)


def _ctx_attn_kernel(q_ref, k_ref, v_ref, ga_ref, qg_ref, kg_ref, *refs, layer):
    o_ref, nk_ref, nv_ref = refs[-3:]
    if len(refs) == 3:
        for other in range(nk_ref.shape[0]):
            if other != layer:
                nk_ref[other] = jnp.zeros(nk_ref.shape[1:], F32)
                nv_ref[other] = jnp.zeros(nv_ref.shape[1:], F32)
        nk_ref, nv_ref = nk_ref.at[layer], nv_ref.at[layer]
    for h in range(N_HEADS):
        sl = slice(h * HEAD_DIM, (h + 1) * HEAD_DIM)
        q = _head_norm(q_ref[:, sl], qg_ref[...])
        k = _head_norm(k_ref[:, sl], kg_ref[...])
        v = v_ref[:, sl]
        nk_ref[:, sl] = k
        nv_ref[:, sl] = v
        s = _dot_nt(q.astype(BF16), k.astype(BF16)) * ATT_SCALE
        p = jnp.exp(s - jnp.max(s, axis=-1, keepdims=True))
        inv = 1.0 / jnp.sum(p, axis=-1, keepdims=True)
        o = _dot(p.astype(BF16), v.astype(BF16)) * inv
        o_ref[:, sl] = (o * _silu(ga_ref[:, sl])).astype(BF16)


def _ctx_attn(proj, qg3, kg3, new_kv, l, seq, depth):
    m = proj.shape[0]
    kv_shape = jax.ShapeDtypeStruct((m // seq, depth, seq, ATT_W), F32)
    col = lambda c: pl.BlockSpec((seq, ATT_W), lambda b: (b, c))
    gspec = pl.BlockSpec((None, 1, HEAD_DIM), lambda b: (l, 0, 0))
    if new_kv:
        kv_spec = pl.BlockSpec((None, None, seq, ATT_W), lambda b: (b, l, 0, 0))
    else:
        kv_spec = pl.BlockSpec((None, depth, seq, ATT_W), lambda b: (b, 0, 0, 0))
    return pl.pallas_call(
        functools.partial(_ctx_attn_kernel, layer=l), grid=(m // seq,),
        in_specs=[col(OFF_Q // ATT_W), col(OFF_K // ATT_W), col(OFF_V // ATT_W), col(OFF_GA // ATT_W),
                  gspec, gspec] + [pl.BlockSpec(memory_space=pl.ANY)] * len(new_kv),
        out_specs=[col(0), kv_spec, kv_spec],
        out_shape=[jax.ShapeDtypeStruct((m, ATT_W), BF16), kv_shape, kv_shape],
        input_output_aliases={6: 1, 7: 2} if new_kv else {},
        compiler_params=_params(("arbitrary",), 32), name="ctx_attn",
    )(proj, proj, proj, proj, qg3, kg3, *new_kv)


def _window_bias(rpb):
    depth, heads, n_dr, n_dc = rpb.shape
    n_e = n_dr + 1
    c = np.arange(GRID_W)[:, None, None]
    p = np.arange(2)[None, :, None]
    kc = np.arange(GRID_W)[None, None, :]
    cs = np.clip(c - KW // 2, 0, GRID_W - KW)
    in_cols = np.broadcast_to((kc >= cs) & (kc < cs + KW), (GRID_W, 2, GRID_W)).reshape(-1)
    dc = np.broadcast_to(np.clip(kc - c + KW - 1, 0, n_dc - 1), (GRID_W, 2, GRID_W)).reshape(-1)
    member = np.broadcast_to(p, (GRID_W, 2, GRID_W)).reshape(-1)
    row_member = np.repeat(np.arange(2), n_dc)[:, None]
    row_dc = np.tile(np.arange(n_dc), 2)[:, None]
    onehot = ((jnp.asarray(row_member) == jnp.asarray(member)[None, :])
              & (jnp.asarray(row_dc) == jnp.asarray(dc)[None, :])).astype(F32)
    padded = jnp.pad(rpb, ((0, 0), (0, 0), (1, 1), (0, 0)))
    pairs = jnp.concatenate([padded[:, :, :-1], padded[:, :, 1:]], axis=-1)
    cb = jnp.einsum("lhek,kx->lhex", pairs, onehot, precision=lax.Precision.HIGHEST)
    row_ok = (np.arange(n_e)[:, None] + member[None, :] >= 1) & (np.arange(n_e)[:, None] + member[None, :] <= n_dr)
    cb = jnp.where(jnp.asarray(row_ok & in_cols[None, :]), cb, NEG)
    return cb.reshape(depth, heads, n_e, GRID_W, 2 * GRID_W)


def _lat_attn_kernel(q_ref, k_ref, v_ref, ck_ref, cv_ref, bias_ref, ga_ref, o_ref,
                     ckb_ref, cvb_ref, *, rows, kh, rb, kr):
    g = pl.program_id(1)

    @pl.when(g == 0)
    def _():
        for h in range(N_HEADS):
            sl = slice(h * HEAD_DIM, (h + 1) * HEAD_DIM)
            ckb_ref[:, sl] = ck_ref[:, h, :].astype(BF16)
            cvb_ref[:, sl] = cv_ref[:, h, :].astype(BF16)

    r0 = g * rb
    ws = jnp.clip(r0 - kh // 2, 0, rows - kr)
    span = pl.ds(pl.multiple_of(ws * GRID_W, GRID_W), kr * GRID_W)
    q_row = r0 + lax.broadcasted_iota(jnp.int32, (rb * GRID_W, 1), 0) // GRID_W
    k_row = ws + lax.broadcasted_iota(jnp.int32, (1, kr * GRID_W), 1) // GRID_W
    q_rs = jnp.clip(q_row - kh // 2, 0, rows - kh)
    in_window = (k_row >= q_rs) & (k_row < q_rs + kh)
    n_pairs = bias_ref.shape[1]
    for h in range(N_HEADS):
        sl = slice(h * HEAD_DIM, (h + 1) * HEAD_DIM)
        q = q_ref[:, sl]
        bias = jnp.concatenate(
            [jnp.concatenate(
                [bias_ref[h, jnp.clip(ws + 2 * j - (r0 + qi) + MAX_KH, 0, n_pairs - 1)]
                 for j in range(kr // 2)], axis=-1)
             for qi in range(rb)], axis=0)
        s_win = jnp.where(in_window, _dot_nt(q, k_ref[span, sl]) * ATT_SCALE + bias, NEG)
        s_ctx = _dot_nt(q, ckb_ref[:, sl]) * ATT_SCALE
        mx = jnp.maximum(jnp.max(s_win, axis=-1, keepdims=True),
                         jnp.max(s_ctx, axis=-1, keepdims=True))
        p_win = jnp.exp(s_win - mx)
        p_ctx = jnp.exp(s_ctx - mx)
        inv = 1.0 / (jnp.sum(p_win, axis=-1, keepdims=True) + jnp.sum(p_ctx, axis=-1, keepdims=True))
        o = (_dot(p_win.astype(BF16), v_ref[span, sl]) + _dot(p_ctx.astype(BF16), cvb_ref[:, sl])) * inv
        o_ref[:, sl] = (o * _silu(ga_ref[:, sl])).astype(BF16)


def _lat_attn(qn, kn, vb, cache_k, cache_v, bias, proj, l, seq):
    m = qn.shape[0]
    batch = m // seq
    rows = seq // GRID_W
    kh = min(MAX_KH, rows)
    rb = min(LAT_ROW_BLOCK, rows)
    kr = min(rows, kh + rb)
    assert rows % rb == 0 and kr % 2 == 0
    past = cache_k.shape[2]
    qspec = pl.BlockSpec((rb * GRID_W, ATT_W), lambda b, g: (b * (rows // rb) + g, 0))
    kvspec = pl.BlockSpec((seq, ATT_W), lambda b, g: (b, 0))
    cspec = pl.BlockSpec((None, None, past, N_HEADS, HEAD_DIM), lambda b, g: (b, l, 0, 0, 0))
    return pl.pallas_call(
        functools.partial(_lat_attn_kernel, rows=rows, kh=kh, rb=rb, kr=kr),
        grid=(batch, rows // rb),
        in_specs=[qspec, kvspec, kvspec, cspec, cspec,
                  pl.BlockSpec((None,) + bias.shape[1:], lambda b, g: (l, 0, 0, 0, 0)),
                  pl.BlockSpec((rb * GRID_W, ATT_W),
                               lambda b, g: (b * (rows // rb) + g, OFF_GA // ATT_W))],
        out_specs=qspec,
        out_shape=jax.ShapeDtypeStruct((m, ATT_W), BF16),
        scratch_shapes=[pltpu.VMEM((past, ATT_W), BF16), pltpu.VMEM((past, ATT_W), BF16)],
        compiler_params=_params(("arbitrary", "arbitrary"), 48), name="lat_attn",
    )(qn, kn, vb, cache_k, cache_v, bias, proj)


def _phase_cos_sin(phase, period):
    ang = (phase % period).astype(F32) * (2.0 * math.pi / period)
    return jnp.cos(ang), jnp.sin(ang)


def _trig_kernel(ca_ref, sa_ref, cb_ref, sb_ref, *refs, patch_row):
    o_ref = refs[-1]
    cb, sb = cb_ref[...], sb_ref[...]
    for a in range(ca_ref.shape[0]):
        blk = ca_ref[a:a + 1, :] * cb - sa_ref[a:a + 1, :] * sb
        if patch_row is not None:
            row = (pl.program_id(1) * ca_ref.shape[0] + a) * LANES + lax.broadcasted_iota(jnp.int32, (LANES, 1), 0)
            blk = jnp.where(row == patch_row, refs[0][...], blk)
        o_ref[a * LANES:(a + 1) * LANES, :] = blk.astype(o_ref.dtype)


def _trig_table(blk_freq, blk_off, col_pos, col_off, period, patch=None):
    n_j, n_col = col_pos.shape
    n_blk = blk_freq.shape[0]
    ca, sa = _phase_cos_sin(col_pos[:, None, :] * blk_freq[None, :, None] + blk_off[None, :, None]
                            + col_off[:, None, :], period)
    cb, sb = _phase_cos_sin(col_pos[:, None, :] * jnp.arange(LANES, dtype=jnp.int32)[None, :, None], period)
    ab = min(SUBLANES, n_blk)
    blk_spec = pl.BlockSpec((None, ab, n_col), lambda j, i: (j, i, 0))
    tab_spec = pl.BlockSpec((None, LANES, n_col), lambda j, i: (j, 0, 0))
    extra, extra_specs = ([], []) if patch is None else ([patch[1]], [pl.BlockSpec((1, n_col), lambda j, i: (0, 0))])
    return pl.pallas_call(
        functools.partial(_trig_kernel, patch_row=None if patch is None else patch[0]),
        grid=(n_j, n_blk // ab),
        in_specs=[blk_spec, blk_spec, tab_spec, tab_spec] + extra_specs,
        out_specs=pl.BlockSpec((None, ab * LANES, n_col), lambda j, i: (j, i, 0)),
        out_shape=jax.ShapeDtypeStruct((n_j, n_blk * LANES, n_col), BF16),
        compiler_params=_params(("arbitrary", "arbitrary"), 32), name="trig_table",
    )(ca, sa, cb, sb, *extra)


def _fnet_tables(seq):
    col = jnp.arange(2 * seq, dtype=jnp.int32)[None, :]
    blk = jnp.arange(seq // LANES, dtype=jnp.int32) * LANES
    tab = _trig_table(blk, jnp.zeros_like(blk), col % seq, (col // seq) * (seq // 4), seq)[0]
    g = jnp.arange(FNET_GDIM, dtype=jnp.int32)
    cc, sc = _phase_cos_sin(g[:, None] * g[None, :], FNET_GDIM)
    return tab, cc.astype(BF16), sc.astype(BF16)


def _fnet_kernel(u_ref, gb_ref, tl_ref, cc_ref, sc_ref, o_ref, t_ref, *, seq, tl, sb):
    i = pl.program_id(1)

    @pl.when(i == 0)
    def _():
        for s in range(sb):
            for g in range(FNET_GROUPS):
                ug = u_ref[s * seq:(s + 1) * seq, g * FNET_GDIM:(g + 1) * FNET_GDIM].astype(BF16)
                lanes = slice(s * FNET_W + g * FNET_GDIM, s * FNET_W + (g + 1) * FNET_GDIM)
                t_ref[0:seq, lanes] = _dot(ug, cc_ref[...]).astype(BF16)
                t_ref[seq:2 * seq, lanes] = _dot(ug, sc_ref[...]).astype(BF16)

    y = _dot(tl_ref[...], t_ref[...]) * ((seq * FNET_GDIM) ** -0.5)
    for s in range(sb):
        rows = slice(s * tl, (s + 1) * tl)
        o_ref[rows, :] = (y[:, s * FNET_W:(s + 1) * FNET_W] * _silu(gb_ref[rows, :])).astype(BF16)


def _fnet(proj, tables, seq):
    m = proj.shape[0]
    tab_l, cc, sc = tables
    tl = min(512, seq)
    nt = seq // tl
    sb = min(4, m // seq) if nt == 1 else 1
    return pl.pallas_call(
        functools.partial(_fnet_kernel, seq=seq, tl=tl, sb=sb),
        grid=(m // (sb * seq), nt),
        in_specs=[
            pl.BlockSpec((sb * seq, FNET_W), lambda b, i: (b, OFF_UB // FNET_W)),
            pl.BlockSpec((sb * tl, FNET_W), lambda b, i: (b * nt + i, OFF_GB // FNET_W)),
            pl.BlockSpec((tl, 2 * seq), lambda b, i: (i, 0)),
            pl.BlockSpec((FNET_GDIM, FNET_GDIM), lambda b, i: (0, 0)),
            pl.BlockSpec((FNET_GDIM, FNET_GDIM), lambda b, i: (0, 0)),
        ],
        out_specs=pl.BlockSpec((sb * tl, FNET_W), lambda b, i: (b * nt + i, 0)),
        out_shape=jax.ShapeDtypeStruct((m, FNET_W), BF16),
        scratch_shapes=[pltpu.VMEM((2 * seq, sb * FNET_W), BF16)],
        compiler_params=_params(("arbitrary", "arbitrary"), 40), name="fnet",
    )(proj, proj, tab_l, cc, sc)


def _hyena_tiling(seq):
    if seq > 1024:
        return min(FREQ_CHUNK, seq), HY_W // 2, 2
    return min(FREQ_CHUNK, seq), HY_W, 4


def _hyena_tables(seq):
    fc = _hyena_tiling(seq)[0]
    nch = seq // fc
    period = 2 * seq
    pos = jnp.arange(seq, dtype=jnp.int32)
    zeros = jnp.zeros_like(pos)
    blk = jnp.arange(2 * seq // LANES, dtype=jnp.int32) * LANES
    k0 = blk % (2 * fc)
    blk_freq = (blk // (2 * fc)) * fc + k0 % fc
    blk_off = (k0 // fc) * (period // 4)
    nyquist = (1 - 2 * (pos % 2)).astype(F32)[None, :]
    fwd = _trig_table(blk_freq, blk_off, pos[None, :], zeros[None, :], period, patch=(fc, nyquist))
    k = jnp.arange(2 * fc, dtype=jnp.int32)
    f = jnp.arange(nch, dtype=jnp.int32)[:, None] * fc + (k % fc)[None, :]
    is_sec = jnp.broadcast_to((k >= fc)[None, :], f.shape)
    nyq = is_sec & (f == 0)
    row_blk = jnp.arange(seq // LANES, dtype=jnp.int32) * LANES
    inv = _trig_table(row_blk, jnp.zeros_like(row_blk), jnp.where(nyq, seq, f),
                      jnp.where(is_sec & ~nyq, period // 4, 0), period)
    return fwd.reshape(nch, 2 * fc, seq), inv


def _filter_features(seq):
    t = jnp.linspace(0.0, 1.0, seq, dtype=F32)[:, None]
    bands = (FILTER_EMB - 1) // 2
    w = (2.0 * math.pi / seq) * jnp.arange(seq, dtype=F32)[:, None]
    f = jnp.linspace(1e-4, bands - 1, bands, dtype=F32)[None, :]
    z = jnp.concatenate([t, jnp.cos(w * f), -jnp.sin(w * f)], axis=-1)
    z = jnp.pad(z, ((0, 0), (0, 128 - FILTER_EMB)))
    deltas = jnp.abs(jnp.linspace(MIN_DECAY, MAX_DECAY, HY_W, dtype=F32))[None, :]
    return z, t, deltas


def _dot_f32(a, b):
    return jnp.dot(a, b, preferred_element_type=F32, precision=lax.Precision.HIGHEST)


def _split_bf16(x):
    hi = x.astype(BF16)
    return hi, (x - hi.astype(F32)).astype(BF16)


def _dot_3pass(a, b_hi, b_lo):
    a_hi, a_lo = _split_bf16(a)
    return _dot(a_hi, b_hi) + (_dot(a_hi, b_lo) + _dot(a_lo, b_hi))


def _filter_kernel(z_ref, t_ref, dl_ref, w1_ref, b1_ref, fr_ref, w2_ref, b2_ref, w3_ref,
                   fwd_ref, o_ref, h_ref, sd_ref, nyq_ref, *, seq, fc, rows):
    j = pl.program_id(0)
    nblk = 2 * HYENA_ORDER
    nrc = seq // rows

    @pl.when(j == 0)
    def _():
        fr = fr_ref[...]
        w3_hi, w3_lo = _split_bf16(w3_ref[...])

        def taps(r, acc):
            sl = pl.ds(pl.multiple_of(r * rows, rows), rows)
            h = jnp.sin(fr * (_dot_f32(z_ref[sl, :], w1_ref[...]) + b1_ref[...]))
            h = jnp.sin(fr * (_dot_f32(h, w2_ref[...]) + b2_ref[...]))
            h = _dot_3pass(h, w3_hi, w3_lo)
            decay = jnp.exp(-t_ref[sl, :] * dl_ref[...])
            h = h * jnp.concatenate([decay] * nblk, axis=-1)
            h_ref[sl, :] = h
            return acc + jnp.sum(jnp.abs(h), axis=0, keepdims=True)

        tot = lax.fori_loop(0, nrc, taps, jnp.zeros((1, nblk * HY_W), F32))
        inv = 1.0 / (tot + EPS)
        sgn = (1 - 2 * (lax.broadcasted_iota(jnp.int32, (rows, 1), 0) % 2)).astype(F32)

        def fold(r, acc):
            sl = pl.ds(pl.multiple_of(r * rows, rows), rows)
            h = h_ref[sl, :] * inv
            pos = r * rows + lax.broadcasted_iota(jnp.int32, (rows, 1), 0)
            sums = []
            for o in range(HYENA_ORDER):
                fw = h[:, (2 * o) * HY_W:(2 * o + 1) * HY_W]
                bw = jnp.where(pos == 0, 0.0, h[:, (2 * o + 1) * HY_W:(2 * o + 2) * HY_W])
                sums.append(fw + bw)
                sd_ref[0, sl, o * HY_W:(o + 1) * HY_W] = (fw + bw).astype(BF16)
                sd_ref[1, sl, o * HY_W:(o + 1) * HY_W] = (fw - bw).astype(BF16)
            return acc + jnp.sum(jnp.concatenate(sums, axis=-1) * sgn, axis=0, keepdims=True)

        nyq_ref[...] = lax.fori_loop(0, nrc, fold, jnp.zeros((1, HYENA_ORDER * HY_W), F32))

    re = _dot(fwd_ref[0:fc, :], sd_ref[0])
    sec = _dot(fwd_ref[fc:2 * fc, :], sd_ref[1])
    first = (lax.broadcasted_iota(jnp.int32, (fc, 1), 0) == 0) & (j == 0)
    sec = jnp.where(first, nyq_ref[...], sec)
    weight = jnp.where(first, 0.5 / seq, 1.0 / seq)
    o_ref[0:fc, :] = re * weight
    o_ref[fc:2 * fc, :] = sec * weight


def _filter(feats, fwd, w1p, b1, fr, w2, b2, w3, l, seq):
    z, t, deltas = feats
    nch, fc2, _ = fwd.shape
    fc = fc2 // 2
    width = HYENA_ORDER * HY_W
    rows = min(256, seq)
    full = lambda a: pl.BlockSpec(a.shape, lambda j: (0,) * a.ndim)
    lay = lambda a: pl.BlockSpec((None,) + a.shape[1:], lambda j: (l,) + (0,) * (a.ndim - 1))
    return pl.pallas_call(
        functools.partial(_filter_kernel, seq=seq, fc=fc, rows=rows),
        grid=(nch,),
        in_specs=[full(z), full(t), full(deltas), lay(w1p), lay(b1), lay(fr), lay(w2), lay(b2), lay(w3),
                  pl.BlockSpec((None, 2 * fc, seq), lambda j: (j, 0, 0))],
        out_specs=pl.BlockSpec((None, 2 * fc, width), lambda j: (j, 0, 0)),
        out_shape=jax.ShapeDtypeStruct((nch, 2 * fc, width), F32),
        scratch_shapes=[pltpu.VMEM((seq, 2 * width), F32),
                        pltpu.VMEM((2, seq, width), BF16),
                        pltpu.VMEM((1, width), F32)],
        compiler_params=_params(("arbitrary",), 48), name="hyena_filter",
    )(z, t, deltas, w1p, b1, fr, w2, b2, w3, fwd)


def _hyena_kernel(v_ref, x1_ref, x2_ref, gc_ref, cw_ref, cb_ref, hb_ref, fwd_ref, inv_ref, kf_ref,
                  o_ref, z_ref, zb_ref, y_ref, *, seq, fc, sb, tc):
    o = pl.program_id(2)
    j = pl.program_id(3)
    last_j = pl.num_programs(3) - 1
    pos = lax.broadcasted_iota(jnp.int32, (seq, 1), 0)

    def short_conv(ref, k, s):
        x = ref[s * seq:(s + 1) * seq, :]
        prev = jnp.where(pos == 0, 0.0, pltpu.roll(x, 1, 0))
        nxt = jnp.where(pos == seq - 1, 0.0, pltpu.roll(x, seq - 1, 0))
        return (prev * cw_ref[0, k:k + 1, :] + x * cw_ref[1, k:k + 1, :]
                + nxt * cw_ref[2, k:k + 1, :] + cb_ref[k:k + 1, :])

    @pl.when((o == 0) & (j == 0))
    def _():
        for s in range(sb):
            lanes = slice(s * tc, (s + 1) * tc)
            z = short_conv(v_ref, 0, s)
            z_ref[:, lanes] = z
            zb_ref[:, lanes] = z.astype(BF16)
        y_ref[...] = jnp.zeros_like(y_ref)

    @pl.when((o == 1) & (j == 0))
    def _():
        for s in range(sb):
            lanes = slice(s * tc, (s + 1) * tc)
            z = short_conv(x1_ref, 1, s) * (y_ref[:, lanes] + hb_ref[0:1, :] * z_ref[:, lanes])
            z_ref[:, lanes] = z
            zb_ref[:, lanes] = z.astype(BF16)
        y_ref[...] = jnp.zeros_like(y_ref)

    zf = _dot(fwd_ref[...], zb_ref[...])
    kr, ks = kf_ref[0:fc, :], kf_ref[fc:2 * fc, :]
    nyq = (lax.broadcasted_iota(jnp.int32, (fc, 1), 0) == 0) & (j == 0)
    parts = []
    for s in range(sb):
        lanes = slice(s * tc, (s + 1) * tc)
        zr, zs = zf[0:fc, lanes], zf[fc:2 * fc, lanes]
        ss = zs * ks
        yr = zr * kr - jnp.where(nyq, 0.0, ss)
        ys = jnp.where(nyq, ss, zr * ks + zs * kr)
        parts.append(jnp.concatenate([yr, ys], axis=0).astype(BF16))
    y_ref[...] += _dot(inv_ref[...], jnp.concatenate(parts, axis=-1))

    @pl.when((o == 1) & (j == last_j))
    def _():
        for s in range(sb):
            lanes = slice(s * tc, (s + 1) * tc)
            z = short_conv(x2_ref, 2, s) * (y_ref[:, lanes] + hb_ref[1:2, :] * z_ref[:, lanes])
            o_ref[s * seq:(s + 1) * seq, :] = (z * _silu(gc_ref[s * seq:(s + 1) * seq, :])).astype(BF16)


def _hyena(proj, cw, cb, hb, fwd, inv, kf, seq):
    m = proj.shape[0]
    nch, fc2, _ = fwd.shape
    fc = fc2 // 2
    _, tc, sb = _hyena_tiling(seq)
    sb = min(sb, m // seq)
    nct = HY_W // tc
    hc0 = OFF_HC // tc
    once = pl.Buffered(1) if seq > 1024 else None
    col = lambda k: pl.BlockSpec((sb * seq, tc), lambda b, c, o, j: (b, hc0 + k * nct + c),
                                 pipeline_mode=once)
    return pl.pallas_call(
        functools.partial(_hyena_kernel, seq=seq, fc=fc, sb=sb, tc=tc),
        grid=(m // (sb * seq), nct, HYENA_ORDER, nch),
        in_specs=[
            col(0), col(1), col(2),
            pl.BlockSpec((sb * seq, tc), lambda b, c, o, j: (b, OFF_GC // tc + c), pipeline_mode=once),
            pl.BlockSpec((3, 3, tc), lambda b, c, o, j: (0, 0, c)),
            pl.BlockSpec((3, tc), lambda b, c, o, j: (0, c)),
            pl.BlockSpec((HYENA_ORDER, tc), lambda b, c, o, j: (0, c)),
            pl.BlockSpec((None, 2 * fc, seq), lambda b, c, o, j: (j, 0, 0)),
            pl.BlockSpec((None, seq, 2 * fc), lambda b, c, o, j: (j, 0, 0)),
            pl.BlockSpec((None, 2 * fc, tc), lambda b, c, o, j: (j, 0, o * nct + c)),
        ],
        out_specs=pl.BlockSpec((sb * seq, tc), lambda b, c, o, j: (b, c)),
        out_shape=jax.ShapeDtypeStruct((m, HY_W), BF16),
        scratch_shapes=[pltpu.VMEM((seq, sb * tc), F32), pltpu.VMEM((seq, sb * tc), BF16),
                        pltpu.VMEM((seq, sb * tc), F32)],
        compiler_params=_params(("arbitrary",) * 4, 56), name="hyena",
    )(proj, proj, proj, proj, cw, cb, hb, fwd, inv, kf)


def _merge_kernel(ya_ref, yb_ref, yc_ref, ga_ref, gb_ref, gc_ref, wa_ref, wb_ref, wc_ref, o_ref):
    acc = ga_ref[...].astype(F32) * _dot(ya_ref[...], wa_ref[...])
    acc += gb_ref[...].astype(F32) * _dot(yb_ref[...], wb_ref[...])
    acc += gc_ref[...].astype(F32) * _dot(yc_ref[...], wc_ref[...])
    o_ref[...] = acc.astype(BF16)


def _merge(ya, yb, yc, gates, w_bf):
    m = ya.shape[0]
    tm = min(ROW_TILE, m)
    tn = COL_TILE
    d = D_MODEL
    row = lambda w: pl.BlockSpec((tm, w), lambda i, j: (i, 0))
    gate = lambda k: pl.BlockSpec((tm, tn), lambda i, j: (i, k * d // tn + j))
    return pl.pallas_call(
        _merge_kernel, grid=(m // tm, d // tn),
        in_specs=[row(ATT_W), row(FNET_W), row(HY_W), gate(0), gate(1), gate(2),
                  pl.BlockSpec((ATT_W, tn), lambda i, j: (0, j)),
                  pl.BlockSpec((FNET_W, tn), lambda i, j: (ATT_W // FNET_W, j)),
                  pl.BlockSpec((HY_W, tn), lambda i, j: ((ATT_W + FNET_W) // HY_W, j))],
        out_specs=pl.BlockSpec((tm, tn), lambda i, j: (i, j)),
        out_shape=jax.ShapeDtypeStruct((m, d), BF16),
        compiler_params=_params(("arbitrary", "arbitrary"), 40), name="merge",
    )(ya, yb, yc, gates, gates, gates, w_bf, w_bf, w_bf)


def _outproj_kernel(mg_ref, w_ref, x_ref, ada_ref, o_ref):
    o_ref[...] = x_ref[...] + ada_ref[...] * _dot(mg_ref[...], w_ref[...])


def _outproj(merged, x2d, ada4, ada_row, w_bf, l):
    m, d = x2d.shape
    tm = min(ROW_TILE, m)
    tn = COL_TILE
    return pl.pallas_call(
        _outproj_kernel, grid=(m // tm, d // tn),
        in_specs=[pl.BlockSpec((tm, d), lambda i, j: (i, 0)),
                  pl.BlockSpec((d, tn), lambda i, j: (0, j)),
                  pl.BlockSpec((tm, tn), lambda i, j: (i, j)),
                  pl.BlockSpec((None, None, 1, tn),
                               lambda i, j: (l, ada_row(i * tm), 0, 2 * d // tn + j))],
        out_specs=pl.BlockSpec((tm, tn), lambda i, j: (i, j)),
        out_shape=jax.ShapeDtypeStruct((m, d), F32),
        compiler_params=_params(("arbitrary", "arbitrary"), 40), name="outproj",
    )(merged, w_bf, x2d, ada4)


def kernel(x_prompt, x_sample, cache_k, cache_v, c, c_ctx, norm_g, w_ada, b_ada, w_in, q_norm_g, k_norm_g, rpb, conv_w, conv_b, f_w1, f_b1, f_freq, f_w2, f_b2, f_w3, hy_bias, w_br, w_out):
    batch, seq, d = x_prompt.shape
    dec_batch, dec_seq, _ = x_sample.shape
    depth = norm_g.shape[0]

    ada4 = _ada(jnp.concatenate([c_ctx[None, :], c], axis=0), w_ada, b_ada)
    ada4 = ada4.reshape(depth, ADA_ROWS, 1, 3 * d)
    ctx_row = lambda tok: 0
    lat_row = lambda tok: 1 + tok // dec_seq

    w_in_bf = w_in[0].astype(BF16)
    norm_g3 = norm_g.reshape(depth, 1, d)
    qg3 = q_norm_g.reshape(depth, 1, HEAD_DIM)
    kg3 = k_norm_g.reshape(depth, 1, HEAD_DIM)
    w1p = jnp.pad(f_w1, ((0, 0), (0, 128 - FILTER_EMB), (0, 0)))
    b1 = f_b1.reshape(depth, 1, FILTER_FF)
    fr = f_freq.reshape(depth, 1, FILTER_FF)
    b2 = f_b2.reshape(depth, 1, FILTER_FF)
    win_bias = _window_bias(rpb)

    groups = []
    for x, s, row in ((x_prompt, seq, ctx_row), (x_sample, dec_seq, lat_row)):
        fwd, inv = _hyena_tables(s)
        groups.append(dict(x=x.reshape(-1, d), seq=s, row=row, fnet=_fnet_tables(s),
                           fwd=fwd, inv=inv, feats=_filter_features(s)))

    new_kv = ()
    for l in range(depth):
        cw = conv_w[l].reshape(3, 3, HY_W)
        cb = conv_b[l].reshape(3, HY_W)
        hb = hy_bias[l]
        for gi, g in enumerate(groups):
            s = g["seq"]
            if gi == 0:
                jobs = [(w_br, l), (w_out, l)] + ([(w_in, l + 1)] if l + 1 < depth else [])
                proj, gates, w_br_bf, w_out_bf, *w_in_next = _inproj(
                    g["x"], norm_g3, ada4, g["row"], w_in_bf, l, jobs)
            else:
                proj, gates = _inproj(g["x"], norm_g3, ada4, g["row"], w_in_bf, l)
            if gi == 0:
                ya, *new_kv = _ctx_attn(proj, qg3, kg3, new_kv, l, s, depth)
            else:
                qn, kn, vb = _prep_lat(proj, qg3, kg3, l)
                ya = _lat_attn(qn, kn, vb, cache_k, cache_v, win_bias, proj, l, s)
            yb = _fnet(proj, g["fnet"], s)
            kf = _filter(g["feats"], g["fwd"], w1p, b1, fr, f_w2, b2, f_w3, l, s)
            yc = _hyena(proj, cw, cb, hb, g["fwd"], g["inv"], kf, s)
            merged = _merge(ya, yb, yc, gates, w_br_bf)
            g["x"] = _outproj(merged, g["x"], ada4, g["row"], w_out_bf, l)
        if w_in_next:
            (w_in_bf,) = w_in_next

    nk, nv = new_kv
    return (groups[0]["x"].reshape(batch, seq, d),
            groups[1]["x"].reshape(dec_batch, dec_seq, d),
            nk.reshape(batch, depth, seq, N_HEADS, HEAD_DIM),
            nv.reshape(batch, depth, seq, N_HEADS, HEAD_DIM))
```

```python
import functools
import math

import jax
import jax.numpy as jnp
import numpy as np
from jax import lax
from jax.experimental import pallas as pl
from jax.experimental.pallas import tpu as pltpu

F32 = jnp.float32
BF16 = jnp.bfloat16

D_MODEL = 2048
GRID_W = 64
N_HEADS = 8
HEAD_DIM = 128
ATT_W = N_HEADS * HEAD_DIM
MAX_KH = 8
KW = 16
FNET_GROUPS = 4
FNET_GDIM = 128
FNET_W = FNET_GROUPS * FNET_GDIM
HY_W = 512
HYENA_ORDER = 2
FILTER_EMB = 33
FILTER_FF = 64
MIN_DECAY = math.log(1e-2) / 1.5
MAX_DECAY = math.log(1e-2) / 0.3
OFF_Q = 0
OFF_K = OFF_Q + ATT_W
OFF_V = OFF_K + ATT_W
OFF_GA = OFF_V + ATT_W
OFF_UB = OFF_GA + ATT_W
OFF_GB = OFF_UB + FNET_W
OFF_HC = OFF_GB + FNET_W
OFF_GC = OFF_HC + 3 * HY_W
OFF_MG = OFF_GC + HY_W
N_IN = OFF_MG + 3 * D_MODEL
EPS = 1e-6
NEG = -1e30
ATT_SCALE = HEAD_DIM ** -0.5
LOG2_E = math.log2(math.e)

LANES = 128
SUBLANES = 8
ADA_ROWS = 8
FREQ_CHUNK = 512
LAT_ROW_BLOCK = 4
ROW_TILE = 1024
COL_TILE = 1024
MIB = 1024 * 1024


def _params(semantics, vmem_mib):
    return pltpu.CompilerParams(dimension_semantics=semantics,
                                vmem_limit_bytes=vmem_mib * MIB)


def _sigmoid(x):
    return 0.5 * jnp.tanh(0.5 * x) + 0.5


def _silu(x):
    return x * _sigmoid(x)


def _dot(a, b):
    return jnp.dot(a, b, preferred_element_type=F32)


def _dot_nt(a, b):
    return lax.dot_general(a, b, (((1,), (1,)), ((), ())), preferred_element_type=F32)


def _ada_kernel(cv_ref, w_ref, b_ref, o_ref, s_ref):
    n_rows, d, _ = cv_ref.shape
    tn = w_ref.shape[1]

    @pl.when((pl.program_id(0) == 0) & (pl.program_id(1) == 0))
    def _():
        s_ref[...] = _silu(cv_ref[...])

    def body(i, acc):
        rows = pl.ds(pl.multiple_of(i * SUBLANES, SUBLANES), SUBLANES)
        w = w_ref[rows, :]
        return tuple(acc[r] + w * jnp.concatenate([s_ref[r, rows, :]] * (tn // LANES), axis=-1)
                     for r in range(n_rows))

    zero = jnp.zeros((SUBLANES, tn), F32)
    acc = lax.fori_loop(0, d // SUBLANES, body, (zero,) * n_rows, unroll=4)
    o_ref[...] = jnp.zeros(o_ref.shape, F32)
    for r in range(n_rows):
        o_ref[r:r + 1, :] = jnp.sum(acc[r], axis=0, keepdims=True) + b_ref[...]


def _ada(conds, w_ada, b_ada):
    depth, d, n = w_ada.shape
    rows = conds.shape[0]
    assert rows <= ADA_ROWS
    tn = COL_TILE
    cv = jnp.broadcast_to(conds[:, :, None], (rows, d, LANES))
    return pl.pallas_call(
        _ada_kernel,
        grid=(depth, n // tn),
        in_specs=[
            pl.BlockSpec((rows, d, LANES), lambda l, j: (0, 0, 0)),
            pl.BlockSpec((None, d, tn), lambda l, j: (l, 0, j)),
            pl.BlockSpec((None, 1, tn), lambda l, j: (l, 0, j)),
        ],
        out_specs=pl.BlockSpec((None, ADA_ROWS, tn), lambda l, j: (l, 0, j)),
        out_shape=jax.ShapeDtypeStruct((depth, ADA_ROWS, n), F32),
        scratch_shapes=[pltpu.VMEM((rows, d, LANES), F32)],
        compiler_params=_params(("arbitrary", "arbitrary"), 32),
        name="ada",
    )(cv, w_ada, b_ada.reshape(depth, 1, n))


def _cast_job_specs(jobs, n_i, n_j):
    args, in_specs, out_specs, out_shapes = [], [], [], []
    for w32, layer in jobs:
        _, rows, cols = w32.shape
        rb = rows // n_i
        n_ct = max(t for t in range(1, n_j + 1) if cols % (t * LANES) == 0)
        tw = cols // n_ct
        args.append(w32)
        in_specs.append(pl.BlockSpec((None, rb, tw),
                                     lambda i, j, layer=layer, n_ct=n_ct: (layer, i, jnp.minimum(j, n_ct - 1))))
        out_specs.append(pl.BlockSpec((rb, tw), lambda i, j, n_ct=n_ct: (i, jnp.minimum(j, n_ct - 1))))
        out_shapes.append(jax.ShapeDtypeStruct((rows, cols), BF16))
    return args, in_specs, out_specs, out_shapes


def _inproj_kernel(x_ref, g_ref, ada_ref, w_ref, *refs, rows, n_plain, n_jobs):
    o_ref, gate_ref = refs[n_jobs:n_jobs + 2]
    h_ref = refs[-1]
    for src_ref, dst_ref in zip(refs[:n_jobs], refs[n_jobs + 2:-1]):
        dst_ref[...] = src_ref[...].astype(BF16)
    j = pl.program_id(1)

    @pl.when(j == 0)
    def _():
        g = g_ref[...]
        shift = ada_ref[:, 0:D_MODEL]
        scale1 = 1.0 + ada_ref[:, D_MODEL:2 * D_MODEL]

        def body(r, carry):
            sl = pl.ds(pl.multiple_of(r * rows, rows), rows)
            x = x_ref[sl, :]
            ms = jnp.mean(x * x, axis=-1, keepdims=True)
            y = x * lax.rsqrt(ms + EPS) * g
            h_ref[sl, :] = (y * scale1 + shift).astype(BF16)
            return carry

        lax.fori_loop(0, x_ref.shape[0] // rows, body, 0, unroll=8)

    @pl.when(j < n_plain)
    def _():
        o_ref[...] = _dot(h_ref[...], w_ref[...])

    @pl.when(j >= n_plain)
    def _():
        gate_ref[...] = _sigmoid(_dot(h_ref[...], w_ref[...])).astype(BF16)


def _inproj(x2d, norm_g3, ada4, ada_row, w_bf, l, cast_jobs=()):
    m, d = x2d.shape
    tm = min(ROW_TILE, m)
    tn = COL_TILE
    n_plain = OFF_MG // tn
    nxt_args, nxt_in, nxt_out, nxt_shape = _cast_job_specs(cast_jobs, m // tm, N_IN // tn)
    return pl.pallas_call(
        functools.partial(_inproj_kernel, rows=16, n_plain=n_plain, n_jobs=len(cast_jobs)),
        grid=(m // tm, N_IN // tn),
        in_specs=[
            pl.BlockSpec((tm, d), lambda i, j: (i, 0)),
            pl.BlockSpec((None, 1, d), lambda i, j: (l, 0, 0)),
            pl.BlockSpec((None, None, 1, 3 * d), lambda i, j: (l, ada_row(i * tm), 0, 0)),
            pl.BlockSpec((d, tn), lambda i, j: (0, j)),
        ] + nxt_in,
        out_specs=[pl.BlockSpec((tm, tn), lambda i, j: (i, jnp.minimum(j, n_plain - 1))),
                   pl.BlockSpec((tm, tn), lambda i, j: (i, jnp.maximum(j - n_plain, 0)))] + nxt_out,
        out_shape=[jax.ShapeDtypeStruct((m, OFF_MG), F32),
                   jax.ShapeDtypeStruct((m, N_IN - OFF_MG), BF16)] + nxt_shape,
        scratch_shapes=[pltpu.VMEM((tm, d), BF16)],
        compiler_params=_params(("arbitrary", "arbitrary"), 52),
        name="inproj",
    )(x2d, norm_g3, ada4, w_bf, *nxt_args)


def _head_norm(x, g):
    ms = jnp.mean(x * x, axis=-1, keepdims=True)
    return x * lax.rsqrt(ms + EPS) * g


def _prep_lat_kernel(q_ref, k_ref, v_ref, qg_ref, kg_ref, qn_ref, kn_ref, vb_ref):
    for h in range(N_HEADS):
        sl = slice(h * HEAD_DIM, (h + 1) * HEAD_DIM)
        qn_ref[:, sl] = _head_norm(q_ref[:, sl], qg_ref[...]).astype(BF16)
        kn_ref[:, sl] = _head_norm(k_ref[:, sl], kg_ref[...]).astype(BF16)
    vb_ref[...] = v_ref[...].astype(BF16)


def _prep_lat(proj, qg3, kg3, l):
    m = proj.shape[0]
    tm = min(256, m)
    col = lambda c: pl.BlockSpec((tm, ATT_W), lambda i: (i, c))
    gspec = pl.BlockSpec((None, 1, HEAD_DIM), lambda i: (l, 0, 0))
    return pl.pallas_call(
        _prep_lat_kernel, grid=(m // tm,),
        in_specs=[col(OFF_Q // ATT_W), col(OFF_K // ATT_W), col(OFF_V // ATT_W), gspec, gspec],
        out_specs=[col(0)] * 3, out_shape=[jax.ShapeDtypeStruct((m, ATT_W), BF16)] * 3,
        compiler_params=_params(("arbitrary",), 32), name="prep_lat",
    )(proj, proj, proj, qg3, kg3)


def _ctx_attn_kernel(q_ref, k_ref, v_ref, ga_ref, qg_ref, kg_ref, *refs, layer):
    o_ref, nk_ref, nv_ref = refs[-3:]
    if len(refs) == 3:
        for other in range(nk_ref.shape[0]):
            if other != layer:
                nk_ref[other] = jnp.zeros(nk_ref.shape[1:], F32)
                nv_ref[other] = jnp.zeros(nv_ref.shape[1:], F32)
        nk_ref, nv_ref = nk_ref.at[layer], nv_ref.at[layer]
    for h in range(N_HEADS):
        sl = slice(h * HEAD_DIM, (h + 1) * HEAD_DIM)
        q = _head_norm(q_ref[:, sl], qg_ref[...])
        k = _head_norm(k_ref[:, sl], kg_ref[...])
        v = v_ref[:, sl]
        nk_ref[:, sl] = k
        nv_ref[:, sl] = v
        s = _dot_nt(q.astype(BF16), k.astype(BF16)) * (ATT_SCALE * LOG2_E)
        p = jnp.exp2(s - jnp.max(s, axis=-1, keepdims=True))
        inv = 1.0 / jnp.sum(p, axis=-1, keepdims=True)
        o = _dot(p.astype(BF16), v.astype(BF16)) * inv
        o_ref[:, sl] = (o * _silu(ga_ref[:, sl])).astype(BF16)


def _ctx_attn(proj, qg3, kg3, new_kv, l, seq, depth):
    m = proj.shape[0]
    kv_shape = jax.ShapeDtypeStruct((m // seq, depth, seq, ATT_W), F32)
    col = lambda c: pl.BlockSpec((seq, ATT_W), lambda b: (b, c))
    gspec = pl.BlockSpec((None, 1, HEAD_DIM), lambda b: (l, 0, 0))
    if new_kv:
        kv_spec = pl.BlockSpec((None, None, seq, ATT_W), lambda b: (b, l, 0, 0))
    else:
        kv_spec = pl.BlockSpec((None, depth, seq, ATT_W), lambda b: (b, 0, 0, 0))
    return pl.pallas_call(
        functools.partial(_ctx_attn_kernel, layer=l), grid=(m // seq,),
        in_specs=[col(OFF_Q // ATT_W), col(OFF_K // ATT_W), col(OFF_V // ATT_W), col(OFF_GA // ATT_W),
                  gspec, gspec] + [pl.BlockSpec(memory_space=pl.ANY)] * len(new_kv),
        out_specs=[col(0), kv_spec, kv_spec],
        out_shape=[jax.ShapeDtypeStruct((m, ATT_W), BF16), kv_shape, kv_shape],
        input_output_aliases={6: 1, 7: 2} if new_kv else {},
        compiler_params=_params(("arbitrary",), 32), name="ctx_attn",
    )(proj, proj, proj, proj, qg3, kg3, *new_kv)


def _window_bias(rpb):
    depth, heads, n_dr, n_dc = rpb.shape
    n_e = n_dr + 1
    c = np.arange(GRID_W)[:, None, None]
    p = np.arange(2)[None, :, None]
    kc = np.arange(GRID_W)[None, None, :]
    cs = np.clip(c - KW // 2, 0, GRID_W - KW)
    in_cols = np.broadcast_to((kc >= cs) & (kc < cs + KW), (GRID_W, 2, GRID_W)).reshape(-1)
    dc = np.broadcast_to(np.clip(kc - c + KW - 1, 0, n_dc - 1), (GRID_W, 2, GRID_W)).reshape(-1)
    member = np.broadcast_to(p, (GRID_W, 2, GRID_W)).reshape(-1)
    row_member = np.repeat(np.arange(2), n_dc)[:, None]
    row_dc = np.tile(np.arange(n_dc), 2)[:, None]
    onehot = ((jnp.asarray(row_member) == jnp.asarray(member)[None, :])
              & (jnp.asarray(row_dc) == jnp.asarray(dc)[None, :])).astype(F32)
    padded = jnp.pad(rpb, ((0, 0), (0, 0), (1, 1), (0, 0)))
    pairs = jnp.concatenate([padded[:, :, :-1], padded[:, :, 1:]], axis=-1)
    cb = jnp.einsum("lhek,kx->lhex", pairs, onehot, precision=lax.Precision.HIGHEST)
    row_ok = (np.arange(n_e)[:, None] + member[None, :] >= 1) & (np.arange(n_e)[:, None] + member[None, :] <= n_dr)
    cb = jnp.where(jnp.asarray(row_ok & in_cols[None, :]), cb * LOG2_E, NEG)
    return cb.reshape(depth, heads, n_e, GRID_W, 2 * GRID_W)


def _lat_attn_kernel(q_ref, k_ref, v_ref, ck_ref, cv_ref, bias_ref, ga_ref, o_ref,
                     ckb_ref, cvb_ref, *, rows, kh, rb, kr):
    g = pl.program_id(1)

    @pl.when(g == 0)
    def _():
        for h in range(N_HEADS):
            sl = slice(h * HEAD_DIM, (h + 1) * HEAD_DIM)
            ckb_ref[:, sl] = ck_ref[:, h, :].astype(BF16)
            cvb_ref[:, sl] = cv_ref[:, h, :].astype(BF16)

    r0 = g * rb
    ws = jnp.clip(r0 - kh // 2, 0, rows - kr)
    span = pl.ds(pl.multiple_of(ws * GRID_W, GRID_W), kr * GRID_W)
    q_row = r0 + lax.broadcasted_iota(jnp.int32, (rb * GRID_W, 1), 0) // GRID_W
    k_row = ws + lax.broadcasted_iota(jnp.int32, (1, kr * GRID_W), 1) // GRID_W
    q_rs = jnp.clip(q_row - kh // 2, 0, rows - kh)
    in_window = (k_row >= q_rs) & (k_row < q_rs + kh)
    n_pairs = bias_ref.shape[1]
    for h in range(N_HEADS):
        sl = slice(h * HEAD_DIM, (h + 1) * HEAD_DIM)
        q = q_ref[:, sl]
        bias = jnp.concatenate(
            [jnp.concatenate(
                [bias_ref[h, jnp.clip(ws + 2 * j - (r0 + qi) + MAX_KH, 0, n_pairs - 1)]
                 for j in range(kr // 2)], axis=-1)
             for qi in range(rb)], axis=0)
        s_win = jnp.where(in_window, _dot_nt(q, k_ref[span, sl]) * (ATT_SCALE * LOG2_E) + bias, NEG)
        s_ctx = _dot_nt(q, ckb_ref[:, sl]) * (ATT_SCALE * LOG2_E)
        mx = jnp.maximum(jnp.max(s_win, axis=-1, keepdims=True),
                         jnp.max(s_ctx, axis=-1, keepdims=True))
        p_win = jnp.exp2(s_win - mx)
        p_ctx = jnp.exp2(s_ctx - mx)
        inv = 1.0 / (jnp.sum(p_win, axis=-1, keepdims=True) + jnp.sum(p_ctx, axis=-1, keepdims=True))
        o = (_dot(p_win.astype(BF16), v_ref[span, sl]) + _dot(p_ctx.astype(BF16), cvb_ref[:, sl])) * inv
        o_ref[:, sl] = (o * _silu(ga_ref[:, sl])).astype(BF16)


def _lat_attn(qn, kn, vb, cache_k, cache_v, bias, proj, l, seq):
    m = qn.shape[0]
    batch = m // seq
    rows = seq // GRID_W
    kh = min(MAX_KH, rows)
    rb = min(LAT_ROW_BLOCK, rows)
    kr = min(rows, kh + rb)
    assert rows % rb == 0 and kr % 2 == 0
    past = cache_k.shape[2]
    qspec = pl.BlockSpec((rb * GRID_W, ATT_W), lambda b, g: (b * (rows // rb) + g, 0))
    kvspec = pl.BlockSpec((seq, ATT_W), lambda b, g: (b, 0))
    cspec = pl.BlockSpec((None, None, past, N_HEADS, HEAD_DIM), lambda b, g: (b, l, 0, 0, 0))
    return pl.pallas_call(
        functools.partial(_lat_attn_kernel, rows=rows, kh=kh, rb=rb, kr=kr),
        grid=(batch, rows // rb),
        in_specs=[qspec, kvspec, kvspec, cspec, cspec,
                  pl.BlockSpec((None,) + bias.shape[1:], lambda b, g: (l, 0, 0, 0, 0)),
                  pl.BlockSpec((rb * GRID_W, ATT_W),
                               lambda b, g: (b * (rows // rb) + g, OFF_GA // ATT_W))],
        out_specs=qspec,
        out_shape=jax.ShapeDtypeStruct((m, ATT_W), BF16),
        scratch_shapes=[pltpu.VMEM((past, ATT_W), BF16), pltpu.VMEM((past, ATT_W), BF16)],
        compiler_params=_params(("arbitrary", "arbitrary"), 48), name="lat_attn",
    )(qn, kn, vb, cache_k, cache_v, bias, proj)


def _phase_cos_sin(phase, period):
    ang = (phase % period).astype(F32) * (2.0 * math.pi / period)
    return jnp.cos(ang), jnp.sin(ang)


def _trig_kernel(ca_ref, sa_ref, cb_ref, sb_ref, *refs, patch_row):
    o_ref = refs[-1]
    cb, sb = cb_ref[...], sb_ref[...]
    for a in range(ca_ref.shape[0]):
        blk = ca_ref[a:a + 1, :] * cb - sa_ref[a:a + 1, :] * sb
        if patch_row is not None:
            row = (pl.program_id(1) * ca_ref.shape[0] + a) * LANES + lax.broadcasted_iota(jnp.int32, (LANES, 1), 0)
            blk = jnp.where(row == patch_row, refs[0][...], blk)
        o_ref[a * LANES:(a + 1) * LANES, :] = blk.astype(o_ref.dtype)


def _trig_table(blk_freq, blk_off, col_pos, col_off, period, patch=None):
    n_j, n_col = col_pos.shape
    n_blk = blk_freq.shape[0]
    ca, sa = _phase_cos_sin(col_pos[:, None, :] * blk_freq[None, :, None] + blk_off[None, :, None]
                            + col_off[:, None, :], period)
    cb, sb = _phase_cos_sin(col_pos[:, None, :] * jnp.arange(LANES, dtype=jnp.int32)[None, :, None], period)
    ab = min(SUBLANES, n_blk)
    blk_spec = pl.BlockSpec((None, ab, n_col), lambda j, i: (j, i, 0))
    tab_spec = pl.BlockSpec((None, LANES, n_col), lambda j, i: (j, 0, 0))
    extra, extra_specs = ([], []) if patch is None else ([patch[1]], [pl.BlockSpec((1, n_col), lambda j, i: (0, 0))])
    return pl.pallas_call(
        functools.partial(_trig_kernel, patch_row=None if patch is None else patch[0]),
        grid=(n_j, n_blk // ab),
        in_specs=[blk_spec, blk_spec, tab_spec, tab_spec] + extra_specs,
        out_specs=pl.BlockSpec((None, ab * LANES, n_col), lambda j, i: (j, i, 0)),
        out_shape=jax.ShapeDtypeStruct((n_j, n_blk * LANES, n_col), BF16),
        compiler_params=_params(("arbitrary", "arbitrary"), 32), name="trig_table",
    )(ca, sa, cb, sb, *extra)


def _fnet_tables(seq):
    col = jnp.arange(2 * seq, dtype=jnp.int32)[None, :]
    blk = jnp.arange(seq // LANES, dtype=jnp.int32) * LANES
    tab = _trig_table(blk, jnp.zeros_like(blk), col % seq, (col // seq) * (seq // 4), seq)[0]
    g = jnp.arange(FNET_GDIM, dtype=jnp.int32)
    cc, sc = _phase_cos_sin(g[:, None] * g[None, :], FNET_GDIM)
    return tab, cc.astype(BF16), sc.astype(BF16)


def _fnet_kernel(u_ref, gb_ref, tl_ref, cc_ref, sc_ref, o_ref, t_ref, *, seq, tl, sb):
    i = pl.program_id(1)

    @pl.when(i == 0)
    def _():
        for s in range(sb):
            for g in range(FNET_GROUPS):
                ug = u_ref[s * seq:(s + 1) * seq, g * FNET_GDIM:(g + 1) * FNET_GDIM].astype(BF16)
                lanes = slice(s * FNET_W + g * FNET_GDIM, s * FNET_W + (g + 1) * FNET_GDIM)
                t_ref[0:seq, lanes] = _dot(ug, cc_ref[...]).astype(BF16)
                t_ref[seq:2 * seq, lanes] = _dot(ug, sc_ref[...]).astype(BF16)

    y = _dot(tl_ref[...], t_ref[...]) * ((seq * FNET_GDIM) ** -0.5)
    for s in range(sb):
        rows = slice(s * tl, (s + 1) * tl)
        o_ref[rows, :] = (y[:, s * FNET_W:(s + 1) * FNET_W] * _silu(gb_ref[rows, :])).astype(BF16)


def _fnet(proj, tables, seq):
    m = proj.shape[0]
    tab_l, cc, sc = tables
    tl = min(512, seq)
    nt = seq // tl
    sb = min(4, m // seq) if nt == 1 else 1
    return pl.pallas_call(
        functools.partial(_fnet_kernel, seq=seq, tl=tl, sb=sb),
        grid=(m // (sb * seq), nt),
        in_specs=[
            pl.BlockSpec((sb * seq, FNET_W), lambda b, i: (b, OFF_UB // FNET_W)),
            pl.BlockSpec((sb * tl, FNET_W), lambda b, i: (b * nt + i, OFF_GB // FNET_W)),
            pl.BlockSpec((tl, 2 * seq), lambda b, i: (i, 0)),
            pl.BlockSpec((FNET_GDIM, FNET_GDIM), lambda b, i: (0, 0)),
            pl.BlockSpec((FNET_GDIM, FNET_GDIM), lambda b, i: (0, 0)),
        ],
        out_specs=pl.BlockSpec((sb * tl, FNET_W), lambda b, i: (b * nt + i, 0)),
        out_shape=jax.ShapeDtypeStruct((m, FNET_W), BF16),
        scratch_shapes=[pltpu.VMEM((2 * seq, sb * FNET_W), BF16)],
        compiler_params=_params(("arbitrary", "arbitrary"), 40), name="fnet",
    )(proj, proj, tab_l, cc, sc)


def _hyena_tiling(seq):
    if seq > 1024:
        return min(FREQ_CHUNK, seq), HY_W // 2, 2
    return min(FREQ_CHUNK, seq), HY_W, 4


def _hyena_tables(seq):
    fc = _hyena_tiling(seq)[0]
    nch = seq // fc
    period = 2 * seq
    pos = jnp.arange(seq, dtype=jnp.int32)
    zeros = jnp.zeros_like(pos)
    blk = jnp.arange(2 * seq // LANES, dtype=jnp.int32) * LANES
    k0 = blk % (2 * fc)
    blk_freq = (blk // (2 * fc)) * fc + k0 % fc
    blk_off = (k0 // fc) * (period // 4)
    nyquist = (1 - 2 * (pos % 2)).astype(F32)[None, :]
    fwd = _trig_table(blk_freq, blk_off, pos[None, :], zeros[None, :], period, patch=(fc, nyquist))
    k = jnp.arange(2 * fc, dtype=jnp.int32)
    f = jnp.arange(nch, dtype=jnp.int32)[:, None] * fc + (k % fc)[None, :]
    is_sec = jnp.broadcast_to((k >= fc)[None, :], f.shape)
    nyq = is_sec & (f == 0)
    row_blk = jnp.arange(seq // LANES, dtype=jnp.int32) * LANES
    inv = _trig_table(row_blk, jnp.zeros_like(row_blk), jnp.where(nyq, seq, f),
                      jnp.where(is_sec & ~nyq, period // 4, 0), period)
    return fwd.reshape(nch, 2 * fc, seq), inv


def _filter_features(seq):
    t = jnp.linspace(0.0, 1.0, seq, dtype=F32)[:, None]
    bands = (FILTER_EMB - 1) // 2
    w = (2.0 * math.pi / seq) * jnp.arange(seq, dtype=F32)[:, None]
    f = jnp.linspace(1e-4, bands - 1, bands, dtype=F32)[None, :]
    z = jnp.concatenate([t, jnp.cos(w * f), -jnp.sin(w * f)], axis=-1)
    z = jnp.pad(z, ((0, 0), (0, 128 - FILTER_EMB)))
    deltas = jnp.abs(jnp.linspace(MIN_DECAY, MAX_DECAY, HY_W, dtype=F32))[None, :]
    return z, t, deltas


def _dot_f32(a, b):
    return jnp.dot(a, b, preferred_element_type=F32, precision=lax.Precision.HIGHEST)


def _split_bf16(x):
    hi = x.astype(BF16)
    return hi, (x - hi.astype(F32)).astype(BF16)


def _dot_3pass(a, b_hi, b_lo):
    a_hi, a_lo = _split_bf16(a)
    return _dot(a_hi, b_hi) + (_dot(a_hi, b_lo) + _dot(a_lo, b_hi))


def _filter_kernel(z_ref, t_ref, dl_ref, w1_ref, b1_ref, fr_ref, w2_ref, b2_ref, w3_ref,
                   fwd_ref, o_ref, h_ref, sd_ref, nyq_ref, *, seq, fc, rows):
    j = pl.program_id(0)
    nblk = 2 * HYENA_ORDER
    nrc = seq // rows

    @pl.when(j == 0)
    def _():
        fr = fr_ref[...]
        w3_hi, w3_lo = _split_bf16(w3_ref[...])

        def taps(r, acc):
            sl = pl.ds(pl.multiple_of(r * rows, rows), rows)
            h = jnp.sin(fr * (_dot_f32(z_ref[sl, :], w1_ref[...]) + b1_ref[...]))
            h = jnp.sin(fr * (_dot_f32(h, w2_ref[...]) + b2_ref[...]))
            h = _dot_3pass(h, w3_hi, w3_lo)
            decay = jnp.exp(-t_ref[sl, :] * dl_ref[...])
            h = h * jnp.concatenate([decay] * nblk, axis=-1)
            h_ref[sl, :] = h
            return acc + jnp.sum(jnp.abs(h), axis=0, keepdims=True)

        tot = lax.fori_loop(0, nrc, taps, jnp.zeros((1, nblk * HY_W), F32))
        inv = 1.0 / (tot + EPS)
        sgn = (1 - 2 * (lax.broadcasted_iota(jnp.int32, (rows, 1), 0) % 2)).astype(F32)

        def fold(r, acc):
            sl = pl.ds(pl.multiple_of(r * rows, rows), rows)
            h = h_ref[sl, :] * inv
            pos = r * rows + lax.broadcasted_iota(jnp.int32, (rows, 1), 0)
            sums = []
            for o in range(HYENA_ORDER):
                fw = h[:, (2 * o) * HY_W:(2 * o + 1) * HY_W]
                bw = jnp.where(pos == 0, 0.0, h[:, (2 * o + 1) * HY_W:(2 * o + 2) * HY_W])
                sums.append(fw + bw)
                sd_ref[0, sl, o * HY_W:(o + 1) * HY_W] = (fw + bw).astype(BF16)
                sd_ref[1, sl, o * HY_W:(o + 1) * HY_W] = (fw - bw).astype(BF16)
            return acc + jnp.sum(jnp.concatenate(sums, axis=-1) * sgn, axis=0, keepdims=True)

        nyq_ref[...] = lax.fori_loop(0, nrc, fold, jnp.zeros((1, HYENA_ORDER * HY_W), F32))

    re = _dot(fwd_ref[0:fc, :], sd_ref[0])
    sec = _dot(fwd_ref[fc:2 * fc, :], sd_ref[1])
    first = (lax.broadcasted_iota(jnp.int32, (fc, 1), 0) == 0) & (j == 0)
    sec = jnp.where(first, nyq_ref[...], sec)
    weight = jnp.where(first, 0.5 / seq, 1.0 / seq)
    o_ref[0:fc, :] = re * weight
    o_ref[fc:2 * fc, :] = sec * weight


def _filter(feats, fwd, w1p, b1, fr, w2, b2, w3, l, seq):
    z, t, deltas = feats
    nch, fc2, _ = fwd.shape
    fc = fc2 // 2
    width = HYENA_ORDER * HY_W
    rows = min(256, seq)
    full = lambda a: pl.BlockSpec(a.shape, lambda j: (0,) * a.ndim)
    lay = lambda a: pl.BlockSpec((None,) + a.shape[1:], lambda j: (l,) + (0,) * (a.ndim - 1))
    return pl.pallas_call(
        functools.partial(_filter_kernel, seq=seq, fc=fc, rows=rows),
        grid=(nch,),
        in_specs=[full(z), full(t), full(deltas), lay(w1p), lay(b1), lay(fr), lay(w2), lay(b2), lay(w3),
                  pl.BlockSpec((None, 2 * fc, seq), lambda j: (j, 0, 0))],
        out_specs=pl.BlockSpec((None, 2 * fc, width), lambda j: (j, 0, 0)),
        out_shape=jax.ShapeDtypeStruct((nch, 2 * fc, width), F32),
        scratch_shapes=[pltpu.VMEM((seq, 2 * width), F32),
                        pltpu.VMEM((2, seq, width), BF16),
                        pltpu.VMEM((1, width), F32)],
        compiler_params=_params(("arbitrary",), 48), name="hyena_filter",
    )(z, t, deltas, w1p, b1, fr, w2, b2, w3, fwd)


def _hyena_kernel(v_ref, x1_ref, x2_ref, gc_ref, cw_ref, cb_ref, hb_ref, fwd_ref, inv_ref, kf_ref,
                  o_ref, z_ref, zb_ref, y_ref, *, seq, fc, sb, tc):
    o = pl.program_id(2)
    j = pl.program_id(3)
    last_j = pl.num_programs(3) - 1
    pos = lax.broadcasted_iota(jnp.int32, (seq, 1), 0)

    def short_conv(ref, k, s):
        x = ref[s * seq:(s + 1) * seq, :]
        prev = jnp.where(pos == 0, 0.0, pltpu.roll(x, 1, 0))
        nxt = jnp.where(pos == seq - 1, 0.0, pltpu.roll(x, seq - 1, 0))
        return (prev * cw_ref[0, k:k + 1, :] + x * cw_ref[1, k:k + 1, :]
                + nxt * cw_ref[2, k:k + 1, :] + cb_ref[k:k + 1, :])

    @pl.when((o == 0) & (j == 0))
    def _():
        for s in range(sb):
            lanes = slice(s * tc, (s + 1) * tc)
            z = short_conv(v_ref, 0, s)
            z_ref[:, lanes] = z
            zb_ref[:, lanes] = z.astype(BF16)
        y_ref[...] = jnp.zeros_like(y_ref)

    @pl.when((o == 1) & (j == 0))
    def _():
        for s in range(sb):
            lanes = slice(s * tc, (s + 1) * tc)
            z = short_conv(x1_ref, 1, s) * (y_ref[:, lanes] + hb_ref[0:1, :] * z_ref[:, lanes])
            z_ref[:, lanes] = z
            zb_ref[:, lanes] = z.astype(BF16)
        y_ref[...] = jnp.zeros_like(y_ref)

    zf = _dot(fwd_ref[...], zb_ref[...])
    kr, ks = kf_ref[0:fc, :], kf_ref[fc:2 * fc, :]
    nyq = (lax.broadcasted_iota(jnp.int32, (fc, 1), 0) == 0) & (j == 0)
    parts = []
    for s in range(sb):
        lanes = slice(s * tc, (s + 1) * tc)
        zr, zs = zf[0:fc, lanes], zf[fc:2 * fc, lanes]
        ss = zs * ks
        yr = zr * kr - jnp.where(nyq, 0.0, ss)
        ys = jnp.where(nyq, ss, zr * ks + zs * kr)
        parts.append(jnp.concatenate([yr, ys], axis=0).astype(BF16))
    y_ref[...] += _dot(inv_ref[...], jnp.concatenate(parts, axis=-1))

    @pl.when((o == 1) & (j == last_j))
    def _():
        for s in range(sb):
            lanes = slice(s * tc, (s + 1) * tc)
            z = short_conv(x2_ref, 2, s) * (y_ref[:, lanes] + hb_ref[1:2, :] * z_ref[:, lanes])
            o_ref[s * seq:(s + 1) * seq, :] = (z * _silu(gc_ref[s * seq:(s + 1) * seq, :])).astype(BF16)


def _hyena(proj, cw, cb, hb, fwd, inv, kf, seq):
    m = proj.shape[0]
    nch, fc2, _ = fwd.shape
    fc = fc2 // 2
    _, tc, sb = _hyena_tiling(seq)
    sb = min(sb, m // seq)
    nct = HY_W // tc
    hc0 = OFF_HC // tc
    once = pl.Buffered(1) if seq > 1024 else None
    col = lambda k: pl.BlockSpec((sb * seq, tc), lambda b, c, o, j: (b, hc0 + k * nct + c),
                                 pipeline_mode=once)
    return pl.pallas_call(
        functools.partial(_hyena_kernel, seq=seq, fc=fc, sb=sb, tc=tc),
        grid=(m // (sb * seq), nct, HYENA_ORDER, nch),
        in_specs=[
            col(0), col(1), col(2),
            pl.BlockSpec((sb * seq, tc), lambda b, c, o, j: (b, OFF_GC // tc + c), pipeline_mode=once),
            pl.BlockSpec((3, 3, tc), lambda b, c, o, j: (0, 0, c)),
            pl.BlockSpec((3, tc), lambda b, c, o, j: (0, c)),
            pl.BlockSpec((HYENA_ORDER, tc), lambda b, c, o, j: (0, c)),
            pl.BlockSpec((None, 2 * fc, seq), lambda b, c, o, j: (j, 0, 0)),
            pl.BlockSpec((None, seq, 2 * fc), lambda b, c, o, j: (j, 0, 0)),
            pl.BlockSpec((None, 2 * fc, tc), lambda b, c, o, j: (j, 0, o * nct + c)),
        ],
        out_specs=pl.BlockSpec((sb * seq, tc), lambda b, c, o, j: (b, c)),
        out_shape=jax.ShapeDtypeStruct((m, HY_W), BF16),
        scratch_shapes=[pltpu.VMEM((seq, sb * tc), F32), pltpu.VMEM((seq, sb * tc), BF16),
                        pltpu.VMEM((seq, sb * tc), F32)],
        compiler_params=_params(("arbitrary",) * 4, 56), name="hyena",
    )(proj, proj, proj, proj, cw, cb, hb, fwd, inv, kf)


def _merge_kernel(ya_ref, yb_ref, yc_ref, ga_ref, gb_ref, gc_ref, wa_ref, wb_ref, wc_ref, o_ref):
    acc = ga_ref[...].astype(F32) * _dot(ya_ref[...], wa_ref[...])
    acc += gb_ref[...].astype(F32) * _dot(yb_ref[...], wb_ref[...])
    acc += gc_ref[...].astype(F32) * _dot(yc_ref[...], wc_ref[...])
    o_ref[...] = acc.astype(BF16)


def _merge(ya, yb, yc, gates, w_bf):
    m = ya.shape[0]
    tm = min(ROW_TILE, m)
    tn = COL_TILE
    d = D_MODEL
    row = lambda w: pl.BlockSpec((tm, w), lambda i, j: (i, 0))
    gate = lambda k: pl.BlockSpec((tm, tn), lambda i, j: (i, k * d // tn + j))
    return pl.pallas_call(
        _merge_kernel, grid=(m // tm, d // tn),
        in_specs=[row(ATT_W), row(FNET_W), row(HY_W), gate(0), gate(1), gate(2),
                  pl.BlockSpec((ATT_W, tn), lambda i, j: (0, j)),
                  pl.BlockSpec((FNET_W, tn), lambda i, j: (ATT_W // FNET_W, j)),
                  pl.BlockSpec((HY_W, tn), lambda i, j: ((ATT_W + FNET_W) // HY_W, j))],
        out_specs=pl.BlockSpec((tm, tn), lambda i, j: (i, j)),
        out_shape=jax.ShapeDtypeStruct((m, d), BF16),
        compiler_params=_params(("arbitrary", "arbitrary"), 40), name="merge",
    )(ya, yb, yc, gates, gates, gates, w_bf, w_bf, w_bf)


def _outproj_kernel(mg_ref, w_ref, x_ref, ada_ref, o_ref):
    o_ref[...] = x_ref[...] + ada_ref[...] * _dot(mg_ref[...], w_ref[...])


def _outproj(merged, x2d, ada4, ada_row, w_bf, l):
    m, d = x2d.shape
    tm = min(ROW_TILE, m)
    tn = COL_TILE
    return pl.pallas_call(
        _outproj_kernel, grid=(m // tm, d // tn),
        in_specs=[pl.BlockSpec((tm, d), lambda i, j: (i, 0)),
                  pl.BlockSpec((d, tn), lambda i, j: (0, j)),
                  pl.BlockSpec((tm, tn), lambda i, j: (i, j)),
                  pl.BlockSpec((None, None, 1, tn),
                               lambda i, j: (l, ada_row(i * tm), 0, 2 * d // tn + j))],
        out_specs=pl.BlockSpec((tm, tn), lambda i, j: (i, j)),
        out_shape=jax.ShapeDtypeStruct((m, d), F32),
        compiler_params=_params(("arbitrary", "arbitrary"), 40), name="outproj",
    )(merged, w_bf, x2d, ada4)


def kernel(x_prompt, x_sample, cache_k, cache_v, c, c_ctx, norm_g, w_ada, b_ada, w_in, q_norm_g, k_norm_g, rpb, conv_w, conv_b, f_w1, f_b1, f_freq, f_w2, f_b2, f_w3, hy_bias, w_br, w_out):
    batch, seq, d = x_prompt.shape
    dec_batch, dec_seq, _ = x_sample.shape
    depth = norm_g.shape[0]

    ada4 = _ada(jnp.concatenate([c_ctx[None, :], c], axis=0), w_ada, b_ada)
    ada4 = ada4.reshape(depth, ADA_ROWS, 1, 3 * d)
    ctx_row = lambda tok: 0
    lat_row = lambda tok: 1 + tok // dec_seq

    w_in_bf = w_in[0].astype(BF16)
    norm_g3 = norm_g.reshape(depth, 1, d)
    qg3 = q_norm_g.reshape(depth, 1, HEAD_DIM)
    kg3 = k_norm_g.reshape(depth, 1, HEAD_DIM)
    w1p = jnp.pad(f_w1, ((0, 0), (0, 128 - FILTER_EMB), (0, 0)))
    b1 = f_b1.reshape(depth, 1, FILTER_FF)
    fr = f_freq.reshape(depth, 1, FILTER_FF)
    b2 = f_b2.reshape(depth, 1, FILTER_FF)
    win_bias = _window_bias(rpb)

    groups = []
    for x, s, row in ((x_prompt, seq, ctx_row), (x_sample, dec_seq, lat_row)):
        fwd, inv = _hyena_tables(s)
        groups.append(dict(x=x.reshape(-1, d), seq=s, row=row, fnet=_fnet_tables(s),
                           fwd=fwd, inv=inv, feats=_filter_features(s)))

    new_kv = ()
    for l in range(depth):
        cw = conv_w[l].reshape(3, 3, HY_W)
        cb = conv_b[l].reshape(3, HY_W)
        hb = hy_bias[l]
        for gi, g in enumerate(groups):
            s = g["seq"]
            if gi == 0:
                jobs = [(w_br, l), (w_out, l)] + ([(w_in, l + 1)] if l + 1 < depth else [])
                proj, gates, w_br_bf, w_out_bf, *w_in_next = _inproj(
                    g["x"], norm_g3, ada4, g["row"], w_in_bf, l, jobs)
            else:
                proj, gates = _inproj(g["x"], norm_g3, ada4, g["row"], w_in_bf, l)
            if gi == 0:
                ya, *new_kv = _ctx_attn(proj, qg3, kg3, new_kv, l, s, depth)
            else:
                qn, kn, vb = _prep_lat(proj, qg3, kg3, l)
                ya = _lat_attn(qn, kn, vb, cache_k, cache_v, win_bias, proj, l, s)
            yb = _fnet(proj, g["fnet"], s)
            kf = _filter(g["feats"], g["fwd"], w1p, b1, fr, f_w2, b2, f_w3, l, s)
            yc = _hyena(proj, cw, cb, hb, g["fwd"], g["inv"], kf, s)
            merged = _merge(ya, yb, yc, gates, w_br_bf)
            g["x"] = _outproj(merged, g["x"], ada4, g["row"], w_out_bf, l)
        if w_in_next:
            (w_in_bf,) = w_in_next

    nk, nv = new_kv
    return (groups[0]["x"].reshape(batch, seq, d),
            groups[1]["x"].reshape(dec_batch, dec_seq, d),
            nk.reshape(batch, depth, seq, N_HEADS, HEAD_DIM),
            nv.reshape(batch, depth, seq, N_HEADS, HEAD_DIM))
```

```python
import functools
import math

import jax
import jax.numpy as jnp
import numpy as np
from jax import lax
from jax.experimental import pallas as pl
from jax.experimental.pallas import tpu as pltpu

F32 = jnp.float32
BF16 = jnp.bfloat16

D_MODEL = 2048
GRID_W = 64
N_HEADS = 8
HEAD_DIM = 128
ATT_W = N_HEADS * HEAD_DIM
MAX_KH = 8
KW = 16
FNET_GROUPS = 4
FNET_GDIM = 128
FNET_W = FNET_GROUPS * FNET_GDIM
HY_W = 512
HYENA_ORDER = 2
FILTER_EMB = 33
FILTER_FF = 64
MIN_DECAY = math.log(1e-2) / 1.5
MAX_DECAY = math.log(1e-2) / 0.3
OFF_Q = 0
OFF_K = OFF_Q + ATT_W
OFF_V = OFF_K + ATT_W
OFF_GA = OFF_V + ATT_W
OFF_UB = OFF_GA + ATT_W
OFF_GB = OFF_UB + FNET_W
OFF_HC = OFF_GB + FNET_W
OFF_GC = OFF_HC + 3 * HY_W
OFF_MG = OFF_GC + HY_W
N_IN = OFF_MG + 3 * D_MODEL
EPS = 1e-6
NEG = -1e30
ATT_SCALE = HEAD_DIM ** -0.5
LOG2_E = math.log2(math.e)

LANES = 128
SUBLANES = 8
ADA_ROWS = 8
FREQ_CHUNK = 512
LAT_ROW_BLOCK = 4
ROW_TILE = 1024
COL_TILE = 1024
MIB = 1024 * 1024


def _params(semantics, vmem_mib):
    return pltpu.CompilerParams(dimension_semantics=semantics,
                                vmem_limit_bytes=vmem_mib * MIB)


def _sigmoid(x):
    return 0.5 * jnp.tanh(0.5 * x) + 0.5


def _silu(x):
    return x * _sigmoid(x)


def _dot(a, b):
    return jnp.dot(a, b, preferred_element_type=F32)


def _dot_nt(a, b):
    return lax.dot_general(a, b, (((1,), (1,)), ((), ())), preferred_element_type=F32)


def _ada_kernel(cv_ref, w_ref, b_ref, o_ref, s_ref):
    n_rows, d, _ = cv_ref.shape
    tn = w_ref.shape[1]

    @pl.when((pl.program_id(0) == 0) & (pl.program_id(1) == 0))
    def _():
        s_ref[...] = _silu(cv_ref[...])

    def body(i, acc):
        rows = pl.ds(pl.multiple_of(i * SUBLANES, SUBLANES), SUBLANES)
        w = w_ref[rows, :]
        return tuple(acc[r] + w * jnp.concatenate([s_ref[r, rows, :]] * (tn // LANES), axis=-1)
                     for r in range(n_rows))

    zero = jnp.zeros((SUBLANES, tn), F32)
    acc = lax.fori_loop(0, d // SUBLANES, body, (zero,) * n_rows, unroll=4)
    o_ref[...] = jnp.zeros(o_ref.shape, F32)
    for r in range(n_rows):
        o_ref[r:r + 1, :] = jnp.sum(acc[r], axis=0, keepdims=True) + b_ref[...]


def _ada(conds, w_ada, b_ada):
    depth, d, n = w_ada.shape
    rows = conds.shape[0]
    assert rows <= ADA_ROWS
    tn = COL_TILE
    cv = jnp.broadcast_to(conds[:, :, None], (rows, d, LANES))
    return pl.pallas_call(
        _ada_kernel,
        grid=(depth, n // tn),
        in_specs=[
            pl.BlockSpec((rows, d, LANES), lambda l, j: (0, 0, 0)),
            pl.BlockSpec((None, d, tn), lambda l, j: (l, 0, j)),
            pl.BlockSpec((None, 1, tn), lambda l, j: (l, 0, j)),
        ],
        out_specs=pl.BlockSpec((None, ADA_ROWS, tn), lambda l, j: (l, 0, j)),
        out_shape=jax.ShapeDtypeStruct((depth, ADA_ROWS, n), F32),
        scratch_shapes=[pltpu.VMEM((rows, d, LANES), F32)],
        compiler_params=_params(("arbitrary", "arbitrary"), 32),
        name="ada",
    )(cv, w_ada, b_ada.reshape(depth, 1, n))


def _cast_job_specs(jobs, n_i, n_j):
    args, in_specs, out_specs, out_shapes = [], [], [], []
    for w32, layer in jobs:
        _, rows, cols = w32.shape
        rb = rows // n_i
        n_ct = max(t for t in range(1, n_j + 1) if cols % (t * LANES) == 0)
        tw = cols // n_ct
        args.append(w32)
        in_specs.append(pl.BlockSpec((None, rb, tw),
                                     lambda i, j, layer=layer, n_ct=n_ct: (layer, i, jnp.minimum(j, n_ct - 1))))
        out_specs.append(pl.BlockSpec((rb, tw), lambda i, j, n_ct=n_ct: (i, jnp.minimum(j, n_ct - 1))))
        out_shapes.append(jax.ShapeDtypeStruct((rows, cols), BF16))
    return args, in_specs, out_specs, out_shapes


def _inproj_kernel(x_ref, g_ref, ada_ref, w_ref, *refs, rows, n_plain, n_jobs):
    o_ref, gate_ref = refs[n_jobs:n_jobs + 2]
    h_ref = refs[-1]
    for src_ref, dst_ref in zip(refs[:n_jobs], refs[n_jobs + 2:-1]):
        dst_ref[...] = src_ref[...].astype(BF16)
    j = pl.program_id(1)

    @pl.when(j == 0)
    def _():
        g = g_ref[...]
        shift = ada_ref[:, 0:D_MODEL]
        scale1 = 1.0 + ada_ref[:, D_MODEL:2 * D_MODEL]

        def body(r, carry):
            sl = pl.ds(pl.multiple_of(r * rows, rows), rows)
            x = x_ref[sl, :]
            ms = jnp.mean(x * x, axis=-1, keepdims=True)
            y = x * lax.rsqrt(ms + EPS) * g
            h_ref[sl, :] = (y * scale1 + shift).astype(BF16)
            return carry

        lax.fori_loop(0, x_ref.shape[0] // rows, body, 0, unroll=8)

    @pl.when(j < n_plain)
    def _():
        o_ref[...] = _dot(h_ref[...], w_ref[...]).astype(o_ref.dtype)

    @pl.when(j >= n_plain)
    def _():
        gate_ref[...] = _sigmoid(_dot(h_ref[...], w_ref[...])).astype(BF16)


def _inproj(x2d, norm_g3, ada4, ada_row, w_bf, l, cast_jobs=()):
    m, d = x2d.shape
    tm = min(ROW_TILE, m)
    tn = COL_TILE
    n_plain = OFF_MG // tn
    nxt_args, nxt_in, nxt_out, nxt_shape = _cast_job_specs(cast_jobs, m // tm, N_IN // tn)
    return pl.pallas_call(
        functools.partial(_inproj_kernel, rows=16, n_plain=n_plain, n_jobs=len(cast_jobs)),
        grid=(m // tm, N_IN // tn),
        in_specs=[
            pl.BlockSpec((tm, d), lambda i, j: (i, 0)),
            pl.BlockSpec((None, 1, d), lambda i, j: (l, 0, 0)),
            pl.BlockSpec((None, None, 1, 3 * d), lambda i, j: (l, ada_row(i * tm), 0, 0)),
            pl.BlockSpec((d, tn), lambda i, j: (0, j)),
        ] + nxt_in,
        out_specs=[pl.BlockSpec((tm, tn), lambda i, j: (i, jnp.minimum(j, n_plain - 1))),
                   pl.BlockSpec((tm, tn), lambda i, j: (i, jnp.maximum(j - n_plain, 0)))] + nxt_out,
        out_shape=[jax.ShapeDtypeStruct((m, OFF_MG), BF16),
                   jax.ShapeDtypeStruct((m, N_IN - OFF_MG), BF16)] + nxt_shape,
        scratch_shapes=[pltpu.VMEM((tm, d), BF16)],
        compiler_params=_params(("arbitrary", "arbitrary"), 52),
        name="inproj",
    )(x2d, norm_g3, ada4, w_bf, *nxt_args)


def _head_norm(x, g):
    ms = jnp.mean(x * x, axis=-1, keepdims=True)
    return x * lax.rsqrt(ms + EPS) * g


def _prep_lat_kernel(q_ref, k_ref, qg_ref, kg_ref, qn_ref, kn_ref):
    for h in range(N_HEADS):
        sl = slice(h * HEAD_DIM, (h + 1) * HEAD_DIM)
        qn_ref[:, sl] = _head_norm(q_ref[:, sl].astype(F32), qg_ref[...]).astype(BF16)
        kn_ref[:, sl] = _head_norm(k_ref[:, sl].astype(F32), kg_ref[...]).astype(BF16)


def _prep_lat(proj, qg3, kg3, l):
    m = proj.shape[0]
    tm = min(256, m)
    col = lambda c: pl.BlockSpec((tm, ATT_W), lambda i: (i, c))
    gspec = pl.BlockSpec((None, 1, HEAD_DIM), lambda i: (l, 0, 0))
    return pl.pallas_call(
        _prep_lat_kernel, grid=(m // tm,),
        in_specs=[col(OFF_Q // ATT_W), col(OFF_K // ATT_W), gspec, gspec],
        out_specs=[col(0)] * 2, out_shape=[jax.ShapeDtypeStruct((m, ATT_W), BF16)] * 2,
        compiler_params=_params(("arbitrary",), 32), name="prep_lat",
    )(proj, proj, qg3, kg3)


def _ctx_attn_kernel(q_ref, k_ref, v_ref, ga_ref, qg_ref, kg_ref, *refs, layer):
    o_ref, nk_ref, nv_ref = refs[-3:]
    if len(refs) == 3:
        for other in range(nk_ref.shape[0]):
            if other != layer:
                nk_ref[other] = jnp.zeros(nk_ref.shape[1:], F32)
                nv_ref[other] = jnp.zeros(nv_ref.shape[1:], F32)
        nk_ref, nv_ref = nk_ref.at[layer], nv_ref.at[layer]
    for h in range(N_HEADS):
        sl = slice(h * HEAD_DIM, (h + 1) * HEAD_DIM)
        q = _head_norm(q_ref[:, sl].astype(F32), qg_ref[...])
        k = _head_norm(k_ref[:, sl].astype(F32), kg_ref[...])
        v = v_ref[:, sl]
        nk_ref[:, sl] = k
        nv_ref[:, sl] = v.astype(F32)
        s = _dot_nt(q.astype(BF16), k.astype(BF16)) * (ATT_SCALE * LOG2_E)
        p = jnp.exp2(s - jnp.max(s, axis=-1, keepdims=True))
        inv = 1.0 / jnp.sum(p, axis=-1, keepdims=True)
        o = _dot(p.astype(BF16), v) * inv
        o_ref[:, sl] = (o * _silu(ga_ref[:, sl].astype(F32))).astype(BF16)


def _ctx_attn(proj, qg3, kg3, new_kv, l, seq, depth):
    m = proj.shape[0]
    kv_shape = jax.ShapeDtypeStruct((m // seq, depth, seq, ATT_W), F32)
    col = lambda c: pl.BlockSpec((seq, ATT_W), lambda b: (b, c))
    gspec = pl.BlockSpec((None, 1, HEAD_DIM), lambda b: (l, 0, 0))
    if new_kv:
        kv_spec = pl.BlockSpec((None, None, seq, ATT_W), lambda b: (b, l, 0, 0))
    else:
        kv_spec = pl.BlockSpec((None, depth, seq, ATT_W), lambda b: (b, 0, 0, 0))
    return pl.pallas_call(
        functools.partial(_ctx_attn_kernel, layer=l), grid=(m // seq,),
        in_specs=[col(OFF_Q // ATT_W), col(OFF_K // ATT_W), col(OFF_V // ATT_W), col(OFF_GA // ATT_W),
                  gspec, gspec] + [pl.BlockSpec(memory_space=pl.ANY)] * len(new_kv),
        out_specs=[col(0), kv_spec, kv_spec],
        out_shape=[jax.ShapeDtypeStruct((m, ATT_W), BF16), kv_shape, kv_shape],
        input_output_aliases={6: 1, 7: 2} if new_kv else {},
        compiler_params=_params(("arbitrary",), 32), name="ctx_attn",
    )(proj, proj, proj, proj, qg3, kg3, *new_kv)


def _window_bias(rpb):
    depth, heads, n_dr, n_dc = rpb.shape
    n_e = n_dr + 1
    c = np.arange(GRID_W)[:, None, None]
    p = np.arange(2)[None, :, None]
    kc = np.arange(GRID_W)[None, None, :]
    cs = np.clip(c - KW // 2, 0, GRID_W - KW)
    in_cols = np.broadcast_to((kc >= cs) & (kc < cs + KW), (GRID_W, 2, GRID_W)).reshape(-1)
    dc = np.broadcast_to(np.clip(kc - c + KW - 1, 0, n_dc - 1), (GRID_W, 2, GRID_W)).reshape(-1)
    member = np.broadcast_to(p, (GRID_W, 2, GRID_W)).reshape(-1)
    row_member = np.repeat(np.arange(2), n_dc)[:, None]
    row_dc = np.tile(np.arange(n_dc), 2)[:, None]
    onehot = ((jnp.asarray(row_member) == jnp.asarray(member)[None, :])
              & (jnp.asarray(row_dc) == jnp.asarray(dc)[None, :])).astype(F32)
    padded = jnp.pad(rpb, ((0, 0), (0, 0), (1, 1), (0, 0)))
    pairs = jnp.concatenate([padded[:, :, :-1], padded[:, :, 1:]], axis=-1)
    cb = jnp.einsum("lhek,kx->lhex", pairs, onehot, precision=lax.Precision.HIGHEST)
    row_ok = (np.arange(n_e)[:, None] + member[None, :] >= 1) & (np.arange(n_e)[:, None] + member[None, :] <= n_dr)
    cb = jnp.where(jnp.asarray(row_ok & in_cols[None, :]), cb * LOG2_E, NEG)
    return cb.reshape(depth, heads, n_e, GRID_W, 2 * GRID_W)


def _lat_attn_kernel(q_ref, k_ref, v_ref, ck_ref, cv_ref, bias_ref, ga_ref, o_ref,
                     ckb_ref, cvb_ref, *, rows, kh, rb, kr):
    g = pl.program_id(1)

    @pl.when(g == 0)
    def _():
        for h in range(N_HEADS):
            sl = slice(h * HEAD_DIM, (h + 1) * HEAD_DIM)
            ckb_ref[:, sl] = ck_ref[:, h, :].astype(BF16)
            cvb_ref[:, sl] = cv_ref[:, h, :].astype(BF16)

    r0 = g * rb
    ws = jnp.clip(r0 - kh // 2, 0, rows - kr)
    span = pl.ds(pl.multiple_of(ws * GRID_W, GRID_W), kr * GRID_W)
    q_row = r0 + lax.broadcasted_iota(jnp.int32, (rb * GRID_W, 1), 0) // GRID_W
    k_row = ws + lax.broadcasted_iota(jnp.int32, (1, kr * GRID_W), 1) // GRID_W
    q_rs = jnp.clip(q_row - kh // 2, 0, rows - kh)
    in_window = (k_row >= q_rs) & (k_row < q_rs + kh)
    n_pairs = bias_ref.shape[1]
    for h in range(N_HEADS):
        sl = slice(h * HEAD_DIM, (h + 1) * HEAD_DIM)
        q = q_ref[:, sl]
        bias = jnp.concatenate(
            [jnp.concatenate(
                [bias_ref[h, jnp.clip(ws + 2 * j - (r0 + qi) + MAX_KH, 0, n_pairs - 1)]
                 for j in range(kr // 2)], axis=-1)
             for qi in range(rb)], axis=0)
        s_win = jnp.where(in_window, _dot_nt(q, k_ref[span, sl]) * (ATT_SCALE * LOG2_E) + bias, NEG)
        s_ctx = _dot_nt(q, ckb_ref[:, sl]) * (ATT_SCALE * LOG2_E)
        mx = jnp.maximum(jnp.max(s_win, axis=-1, keepdims=True),
                         jnp.max(s_ctx, axis=-1, keepdims=True))
        p_win = jnp.exp2(s_win - mx)
        p_ctx = jnp.exp2(s_ctx - mx)
        inv = 1.0 / (jnp.sum(p_win, axis=-1, keepdims=True) + jnp.sum(p_ctx, axis=-1, keepdims=True))
        o = (_dot(p_win.astype(BF16), v_ref[span, sl]) + _dot(p_ctx.astype(BF16), cvb_ref[:, sl])) * inv
        o_ref[:, sl] = (o * _silu(ga_ref[:, sl].astype(F32))).astype(BF16)


def _lat_attn(qn, kn, cache_k, cache_v, bias, proj, l, seq):
    m = qn.shape[0]
    batch = m // seq
    rows = seq // GRID_W
    kh = min(MAX_KH, rows)
    rb = min(LAT_ROW_BLOCK, rows)
    kr = min(rows, kh + rb)
    assert rows % rb == 0 and kr % 2 == 0
    past = cache_k.shape[2]
    qspec = pl.BlockSpec((rb * GRID_W, ATT_W), lambda b, g: (b * (rows // rb) + g, 0))
    kspec = pl.BlockSpec((seq, ATT_W), lambda b, g: (b, 0))
    vspec = pl.BlockSpec((seq, ATT_W), lambda b, g: (b, OFF_V // ATT_W))
    cspec = pl.BlockSpec((None, None, past, N_HEADS, HEAD_DIM), lambda b, g: (b, l, 0, 0, 0))
    return pl.pallas_call(
        functools.partial(_lat_attn_kernel, rows=rows, kh=kh, rb=rb, kr=kr),
        grid=(batch, rows // rb),
        in_specs=[qspec, kspec, vspec, cspec, cspec,
                  pl.BlockSpec((None,) + bias.shape[1:], lambda b, g: (l, 0, 0, 0, 0)),
                  pl.BlockSpec((rb * GRID_W, ATT_W),
                               lambda b, g: (b * (rows // rb) + g, OFF_GA // ATT_W))],
        out_specs=qspec,
        out_shape=jax.ShapeDtypeStruct((m, ATT_W), BF16),
        scratch_shapes=[pltpu.VMEM((past, ATT_W), BF16), pltpu.VMEM((past, ATT_W), BF16)],
        compiler_params=_params(("arbitrary", "arbitrary"), 48), name="lat_attn",
    )(qn, kn, proj, cache_k, cache_v, bias, proj)


def _phase_cos_sin(phase, period):
    ang = (phase % period).astype(F32) * (2.0 * math.pi / period)
    return jnp.cos(ang), jnp.sin(ang)


def _trig_kernel(ca_ref, sa_ref, cb_ref, sb_ref, *refs, patch_row):
    o_ref = refs[-1]
    cb, sb = cb_ref[...], sb_ref[...]
    for a in range(ca_ref.shape[0]):
        blk = ca_ref[a:a + 1, :] * cb - sa_ref[a:a + 1, :] * sb
        if patch_row is not None:
            row = (pl.program_id(1) * ca_ref.shape[0] + a) * LANES + lax.broadcasted_iota(jnp.int32, (LANES, 1), 0)
            blk = jnp.where(row == patch_row, refs[0][...], blk)
        o_ref[a * LANES:(a + 1) * LANES, :] = blk.astype(o_ref.dtype)


def _trig_table(blk_freq, blk_off, col_pos, col_off, period, patch=None):
    n_j, n_col = col_pos.shape
    n_blk = blk_freq.shape[0]
    ca, sa = _phase_cos_sin(col_pos[:, None, :] * blk_freq[None, :, None] + blk_off[None, :, None]
                            + col_off[:, None, :], period)
    cb, sb = _phase_cos_sin(col_pos[:, None, :] * jnp.arange(LANES, dtype=jnp.int32)[None, :, None], period)
    ab = min(SUBLANES, n_blk)
    blk_spec = pl.BlockSpec((None, ab, n_col), lambda j, i: (j, i, 0))
    tab_spec = pl.BlockSpec((None, LANES, n_col), lambda j, i: (j, 0, 0))
    extra, extra_specs = ([], []) if patch is None else ([patch[1]], [pl.BlockSpec((1, n_col), lambda j, i: (0, 0))])
    return pl.pallas_call(
        functools.partial(_trig_kernel, patch_row=None if patch is None else patch[0]),
        grid=(n_j, n_blk // ab),
        in_specs=[blk_spec, blk_spec, tab_spec, tab_spec] + extra_specs,
        out_specs=pl.BlockSpec((None, ab * LANES, n_col), lambda j, i: (j, i, 0)),
        out_shape=jax.ShapeDtypeStruct((n_j, n_blk * LANES, n_col), BF16),
        compiler_params=_params(("arbitrary", "arbitrary"), 32), name="trig_table",
    )(ca, sa, cb, sb, *extra)


def _fnet_tables(seq):
    col = jnp.arange(2 * seq, dtype=jnp.int32)[None, :]
    blk = jnp.arange(seq // LANES, dtype=jnp.int32) * LANES
    tab = _trig_table(blk, jnp.zeros_like(blk), col % seq, (col // seq) * (seq // 4), seq)[0]
    g = jnp.arange(FNET_GDIM, dtype=jnp.int32)
    cc, sc = _phase_cos_sin(g[:, None] * g[None, :], FNET_GDIM)
    return tab, cc.astype(BF16), sc.astype(BF16)


def _fnet_kernel(u_ref, gb_ref, tl_ref, cc_ref, sc_ref, o_ref, t_ref, *, seq, tl, sb):
    i = pl.program_id(1)

    @pl.when(i == 0)
    def _():
        for s in range(sb):
            for g in range(FNET_GROUPS):
                ug = u_ref[s * seq:(s + 1) * seq, g * FNET_GDIM:(g + 1) * FNET_GDIM]
                lanes = slice(s * FNET_W + g * FNET_GDIM, s * FNET_W + (g + 1) * FNET_GDIM)
                t_ref[0:seq, lanes] = _dot(ug, cc_ref[...]).astype(BF16)
                t_ref[seq:2 * seq, lanes] = _dot(ug, sc_ref[...]).astype(BF16)

    y = _dot(tl_ref[...], t_ref[...]) * ((seq * FNET_GDIM) ** -0.5)
    for s in range(sb):
        rows = slice(s * tl, (s + 1) * tl)
        o_ref[rows, :] = (y[:, s * FNET_W:(s + 1) * FNET_W] * _silu(gb_ref[rows, :].astype(F32))).astype(BF16)


def _fnet(proj, tables, seq):
    m = proj.shape[0]
    tab_l, cc, sc = tables
    tl = min(512, seq)
    nt = seq // tl
    sb = min(4, m // seq) if nt == 1 else 1
    return pl.pallas_call(
        functools.partial(_fnet_kernel, seq=seq, tl=tl, sb=sb),
        grid=(m // (sb * seq), nt),
        in_specs=[
            pl.BlockSpec((sb * seq, FNET_W), lambda b, i: (b, OFF_UB // FNET_W)),
            pl.BlockSpec((sb * tl, FNET_W), lambda b, i: (b * nt + i, OFF_GB // FNET_W)),
            pl.BlockSpec((tl, 2 * seq), lambda b, i: (i, 0)),
            pl.BlockSpec((FNET_GDIM, FNET_GDIM), lambda b, i: (0, 0)),
            pl.BlockSpec((FNET_GDIM, FNET_GDIM), lambda b, i: (0, 0)),
        ],
        out_specs=pl.BlockSpec((sb * tl, FNET_W), lambda b, i: (b * nt + i, 0)),
        out_shape=jax.ShapeDtypeStruct((m, FNET_W), BF16),
        scratch_shapes=[pltpu.VMEM((2 * seq, sb * FNET_W), BF16)],
        compiler_params=_params(("arbitrary", "arbitrary"), 40), name="fnet",
    )(proj, proj, tab_l, cc, sc)


def _hyena_tiling(seq):
    if seq > 1024:
        return min(FREQ_CHUNK, seq), HY_W // 2, 2
    return min(FREQ_CHUNK, seq), HY_W, 4


def _hyena_tables(seq):
    fc = _hyena_tiling(seq)[0]
    nch = seq // fc
    period = 2 * seq
    pos = jnp.arange(seq, dtype=jnp.int32)
    zeros = jnp.zeros_like(pos)
    blk = jnp.arange(2 * seq // LANES, dtype=jnp.int32) * LANES
    k0 = blk % (2 * fc)
    blk_freq = (blk // (2 * fc)) * fc + k0 % fc
    blk_off = (k0 // fc) * (period // 4)
    nyquist = (1 - 2 * (pos % 2)).astype(F32)[None, :]
    fwd = _trig_table(blk_freq, blk_off, pos[None, :], zeros[None, :], period, patch=(fc, nyquist))
    k = jnp.arange(2 * fc, dtype=jnp.int32)
    f = jnp.arange(nch, dtype=jnp.int32)[:, None] * fc + (k % fc)[None, :]
    is_sec = jnp.broadcast_to((k >= fc)[None, :], f.shape)
    nyq = is_sec & (f == 0)
    row_blk = jnp.arange(seq // LANES, dtype=jnp.int32) * LANES
    inv = _trig_table(row_blk, jnp.zeros_like(row_blk), jnp.where(nyq, seq, f),
                      jnp.where(is_sec & ~nyq, period // 4, 0), period)
    return fwd.reshape(nch, 2 * fc, seq), inv


def _filter_features(seq):
    t = jnp.linspace(0.0, 1.0, seq, dtype=F32)[:, None]
    bands = (FILTER_EMB - 1) // 2
    w = (2.0 * math.pi / seq) * jnp.arange(seq, dtype=F32)[:, None]
    f = jnp.linspace(1e-4, bands - 1, bands, dtype=F32)[None, :]
    z = jnp.concatenate([t, jnp.cos(w * f), -jnp.sin(w * f)], axis=-1)
    z = jnp.pad(z, ((0, 0), (0, 128 - FILTER_EMB)))
    deltas = jnp.abs(jnp.linspace(MIN_DECAY, MAX_DECAY, HY_W, dtype=F32))[None, :]
    return z, t, deltas


def _dot_f32(a, b):
    return jnp.dot(a, b, preferred_element_type=F32, precision=lax.Precision.HIGHEST)


def _split_bf16(x):
    hi = x.astype(BF16)
    return hi, (x - hi.astype(F32)).astype(BF16)


def _dot_3pass(a, b_hi, b_lo):
    a_hi, a_lo = _split_bf16(a)
    return _dot(a_hi, b_hi) + (_dot(a_hi, b_lo) + _dot(a_lo, b_hi))


def _filter_kernel(z_ref, t_ref, dl_ref, w1_ref, b1_ref, fr_ref, w2_ref, b2_ref, w3_ref,
                   fwd_ref, o_ref, h_ref, sd_ref, nyq_ref, *, seq, fc, rows):
    j = pl.program_id(0)
    nblk = 2 * HYENA_ORDER
    nrc = seq // rows

    @pl.when(j == 0)
    def _():
        fr = fr_ref[...]
        w3_hi, w3_lo = _split_bf16(w3_ref[...])

        def taps(r, acc):
            sl = pl.ds(pl.multiple_of(r * rows, rows), rows)
            h = jnp.sin(fr * (_dot_f32(z_ref[sl, :], w1_ref[...]) + b1_ref[...]))
            h = jnp.sin(fr * (_dot_f32(h, w2_ref[...]) + b2_ref[...]))
            h = _dot_3pass(h, w3_hi, w3_lo)
            decay = jnp.exp(-t_ref[sl, :] * dl_ref[...])
            h = h * jnp.concatenate([decay] * nblk, axis=-1)
            h_ref[sl, :] = h
            return acc + jnp.sum(jnp.abs(h), axis=0, keepdims=True)

        tot = lax.fori_loop(0, nrc, taps, jnp.zeros((1, nblk * HY_W), F32))
        inv = 1.0 / (tot + EPS)
        sgn = (1 - 2 * (lax.broadcasted_iota(jnp.int32, (rows, 1), 0) % 2)).astype(F32)

        def fold(r, acc):
            sl = pl.ds(pl.multiple_of(r * rows, rows), rows)
            h = h_ref[sl, :] * inv
            pos = r * rows + lax.broadcasted_iota(jnp.int32, (rows, 1), 0)
            sums = []
            for o in range(HYENA_ORDER):
                fw = h[:, (2 * o) * HY_W:(2 * o + 1) * HY_W]
                bw = jnp.where(pos == 0, 0.0, h[:, (2 * o + 1) * HY_W:(2 * o + 2) * HY_W])
                sums.append(fw + bw)
                sd_ref[0, sl, o * HY_W:(o + 1) * HY_W] = (fw + bw).astype(BF16)
                sd_ref[1, sl, o * HY_W:(o + 1) * HY_W] = (fw - bw).astype(BF16)
            return acc + jnp.sum(jnp.concatenate(sums, axis=-1) * sgn, axis=0, keepdims=True)

        nyq_ref[...] = lax.fori_loop(0, nrc, fold, jnp.zeros((1, HYENA_ORDER * HY_W), F32))

    re = _dot(fwd_ref[0:fc, :], sd_ref[0])
    sec = _dot(fwd_ref[fc:2 * fc, :], sd_ref[1])
    first = (lax.broadcasted_iota(jnp.int32, (fc, 1), 0) == 0) & (j == 0)
    sec = jnp.where(first, nyq_ref[...], sec)
    weight = jnp.where(first, 0.5 / seq, 1.0 / seq)
    o_ref[0:fc, :] = re * weight
    o_ref[fc:2 * fc, :] = sec * weight


def _filter(feats, fwd, w1p, b1, fr, w2, b2, w3, l, seq):
    z, t, deltas = feats
    nch, fc2, _ = fwd.shape
    fc = fc2 // 2
    width = HYENA_ORDER * HY_W
    rows = min(256, seq)
    full = lambda a: pl.BlockSpec(a.shape, lambda j: (0,) * a.ndim)
    lay = lambda a: pl.BlockSpec((None,) + a.shape[1:], lambda j: (l,) + (0,) * (a.ndim - 1))
    return pl.pallas_call(
        functools.partial(_filter_kernel, seq=seq, fc=fc, rows=rows),
        grid=(nch,),
        in_specs=[full(z), full(t), full(deltas), lay(w1p), lay(b1), lay(fr), lay(w2), lay(b2), lay(w3),
                  pl.BlockSpec((None, 2 * fc, seq), lambda j: (j, 0, 0))],
        out_specs=pl.BlockSpec((None, 2 * fc, width), lambda j: (j, 0, 0)),
        out_shape=jax.ShapeDtypeStruct((nch, 2 * fc, width), F32),
        scratch_shapes=[pltpu.VMEM((seq, 2 * width), F32),
                        pltpu.VMEM((2, seq, width), BF16),
                        pltpu.VMEM((1, width), F32)],
        compiler_params=_params(("arbitrary",), 48), name="hyena_filter",
    )(z, t, deltas, w1p, b1, fr, w2, b2, w3, fwd)


def _hyena_kernel(v_ref, x1_ref, x2_ref, gc_ref, cw_ref, cb_ref, hb_ref, fwd_ref, inv_ref, kf_ref,
                  o_ref, z_ref, zb_ref, y_ref, *, seq, fc, sb, tc):
    o = pl.program_id(2)
    j = pl.program_id(3)
    last_j = pl.num_programs(3) - 1
    pos = lax.broadcasted_iota(jnp.int32, (seq, 1), 0)

    def short_conv(ref, k, s):
        x = ref[s * seq:(s + 1) * seq, :].astype(F32)
        prev = jnp.where(pos == 0, 0.0, pltpu.roll(x, 1, 0))
        nxt = jnp.where(pos == seq - 1, 0.0, pltpu.roll(x, seq - 1, 0))
        return (prev * cw_ref[0, k:k + 1, :] + x * cw_ref[1, k:k + 1, :]
                + nxt * cw_ref[2, k:k + 1, :] + cb_ref[k:k + 1, :])

    @pl.when((o == 0) & (j == 0))
    def _():
        for s in range(sb):
            lanes = slice(s * tc, (s + 1) * tc)
            z = short_conv(v_ref, 0, s)
            z_ref[:, lanes] = z
            zb_ref[:, lanes] = z.astype(BF16)
        y_ref[...] = jnp.zeros_like(y_ref)

    @pl.when((o == 1) & (j == 0))
    def _():
        for s in range(sb):
            lanes = slice(s * tc, (s + 1) * tc)
            z = short_conv(x1_ref, 1, s) * (y_ref[:, lanes] + hb_ref[0:1, :] * z_ref[:, lanes])
            z_ref[:, lanes] = z
            zb_ref[:, lanes] = z.astype(BF16)
        y_ref[...] = jnp.zeros_like(y_ref)

    zf = _dot(fwd_ref[...], zb_ref[...])
    kr, ks = kf_ref[0:fc, :], kf_ref[fc:2 * fc, :]
    nyq = (lax.broadcasted_iota(jnp.int32, (fc, 1), 0) == 0) & (j == 0)
    parts = []
    for s in range(sb):
        lanes = slice(s * tc, (s + 1) * tc)
        zr, zs = zf[0:fc, lanes], zf[fc:2 * fc, lanes]
        ss = zs * ks
        yr = zr * kr - jnp.where(nyq, 0.0, ss)
        ys = jnp.where(nyq, ss, zr * ks + zs * kr)
        parts.append(jnp.concatenate([yr, ys], axis=0).astype(BF16))
    y_ref[...] += _dot(inv_ref[...], jnp.concatenate(parts, axis=-1))

    @pl.when((o == 1) & (j == last_j))
    def _():
        for s in range(sb):
            lanes = slice(s * tc, (s + 1) * tc)
            z = short_conv(x2_ref, 2, s) * (y_ref[:, lanes] + hb_ref[1:2, :] * z_ref[:, lanes])
            o_ref[s * seq:(s + 1) * seq, :] = (z * _silu(gc_ref[s * seq:(s + 1) * seq, :].astype(F32))).astype(BF16)


def _hyena(proj, cw, cb, hb, fwd, inv, kf, seq):
    m = proj.shape[0]
    nch, fc2, _ = fwd.shape
    fc = fc2 // 2
    _, tc, sb = _hyena_tiling(seq)
    sb = min(sb, m // seq)
    nct = HY_W // tc
    hc0 = OFF_HC // tc
    col = lambda k: pl.BlockSpec((sb * seq, tc), lambda b, c, o, j: (b, hc0 + k * nct + c))
    return pl.pallas_call(
        functools.partial(_hyena_kernel, seq=seq, fc=fc, sb=sb, tc=tc),
        grid=(m // (sb * seq), nct, HYENA_ORDER, nch),
        in_specs=[
            col(0), col(1), col(2),
            pl.BlockSpec((sb * seq, tc), lambda b, c, o, j: (b, OFF_GC // tc + c)),
            pl.BlockSpec((3, 3, tc), lambda b, c, o, j: (0, 0, c)),
            pl.BlockSpec((3, tc), lambda b, c, o, j: (0, c)),
            pl.BlockSpec((HYENA_ORDER, tc), lambda b, c, o, j: (0, c)),
            pl.BlockSpec((None, 2 * fc, seq), lambda b, c, o, j: (j, 0, 0)),
            pl.BlockSpec((None, seq, 2 * fc), lambda b, c, o, j: (j, 0, 0)),
            pl.BlockSpec((None, 2 * fc, tc), lambda b, c, o, j: (j, 0, o * nct + c)),
        ],
        out_specs=pl.BlockSpec((sb * seq, tc), lambda b, c, o, j: (b, c)),
        out_shape=jax.ShapeDtypeStruct((m, HY_W), BF16),
        scratch_shapes=[pltpu.VMEM((seq, sb * tc), F32), pltpu.VMEM((seq, sb * tc), BF16),
                        pltpu.VMEM((seq, sb * tc), F32)],
        compiler_params=_params(("arbitrary",) * 4, 56), name="hyena",
    )(proj, proj, proj, proj, cw, cb, hb, fwd, inv, kf)


def _merge_kernel(ya_ref, yb_ref, yc_ref, ga_ref, gb_ref, gc_ref, wa_ref, wb_ref, wc_ref, o_ref):
    acc = ga_ref[...].astype(F32) * _dot(ya_ref[...], wa_ref[...])
    acc += gb_ref[...].astype(F32) * _dot(yb_ref[...], wb_ref[...])
    acc += gc_ref[...].astype(F32) * _dot(yc_ref[...], wc_ref[...])
    o_ref[...] = acc.astype(BF16)


def _merge(ya, yb, yc, gates, w_bf):
    m = ya.shape[0]
    tm = min(ROW_TILE, m)
    tn = COL_TILE
    d = D_MODEL
    row = lambda w: pl.BlockSpec((tm, w), lambda i, j: (i, 0))
    gate = lambda k: pl.BlockSpec((tm, tn), lambda i, j: (i, k * d // tn + j))
    return pl.pallas_call(
        _merge_kernel, grid=(m // tm, d // tn),
        in_specs=[row(ATT_W), row(FNET_W), row(HY_W), gate(0), gate(1), gate(2),
                  pl.BlockSpec((ATT_W, tn), lambda i, j: (0, j)),
                  pl.BlockSpec((FNET_W, tn), lambda i, j: (ATT_W // FNET_W, j)),
                  pl.BlockSpec((HY_W, tn), lambda i, j: ((ATT_W + FNET_W) // HY_W, j))],
        out_specs=pl.BlockSpec((tm, tn), lambda i, j: (i, j)),
        out_shape=jax.ShapeDtypeStruct((m, d), BF16),
        compiler_params=_params(("arbitrary", "arbitrary"), 40), name="merge",
    )(ya, yb, yc, gates, gates, gates, w_bf, w_bf, w_bf)


def _outproj_kernel(mg_ref, w_ref, x_ref, ada_ref, o_ref):
    o_ref[...] = x_ref[...] + ada_ref[...] * _dot(mg_ref[...], w_ref[...])


def _outproj(merged, x2d, ada4, ada_row, w_bf, l):
    m, d = x2d.shape
    tm = min(ROW_TILE, m)
    tn = COL_TILE
    return pl.pallas_call(
        _outproj_kernel, grid=(m // tm, d // tn),
        in_specs=[pl.BlockSpec((tm, d), lambda i, j: (i, 0)),
                  pl.BlockSpec((d, tn), lambda i, j: (0, j)),
                  pl.BlockSpec((tm, tn), lambda i, j: (i, j)),
                  pl.BlockSpec((None, None, 1, tn),
                               lambda i, j: (l, ada_row(i * tm), 0, 2 * d // tn + j))],
        out_specs=pl.BlockSpec((tm, tn), lambda i, j: (i, j)),
        out_shape=jax.ShapeDtypeStruct((m, d), F32),
        compiler_params=_params(("arbitrary", "arbitrary"), 40), name="outproj",
    )(merged, w_bf, x2d, ada4)


def kernel(x_prompt, x_sample, cache_k, cache_v, c, c_ctx, norm_g, w_ada, b_ada, w_in, q_norm_g, k_norm_g, rpb, conv_w, conv_b, f_w1, f_b1, f_freq, f_w2, f_b2, f_w3, hy_bias, w_br, w_out):
    batch, seq, d = x_prompt.shape
    dec_batch, dec_seq, _ = x_sample.shape
    depth = norm_g.shape[0]

    ada4 = _ada(jnp.concatenate([c_ctx[None, :], c], axis=0), w_ada, b_ada)
    ada4 = ada4.reshape(depth, ADA_ROWS, 1, 3 * d)
    ctx_row = lambda tok: 0
    lat_row = lambda tok: 1 + tok // dec_seq

    w_in_bf = w_in[0].astype(BF16)
    norm_g3 = norm_g.reshape(depth, 1, d)
    qg3 = q_norm_g.reshape(depth, 1, HEAD_DIM)
    kg3 = k_norm_g.reshape(depth, 1, HEAD_DIM)
    w1p = jnp.pad(f_w1, ((0, 0), (0, 128 - FILTER_EMB), (0, 0)))
    b1 = f_b1.reshape(depth, 1, FILTER_FF)
    fr = f_freq.reshape(depth, 1, FILTER_FF)
    b2 = f_b2.reshape(depth, 1, FILTER_FF)
    win_bias = _window_bias(rpb)

    groups = []
    for x, s, row in ((x_prompt, seq, ctx_row), (x_sample, dec_seq, lat_row)):
        fwd, inv = _hyena_tables(s)
        groups.append(dict(x=x.reshape(-1, d), seq=s, row=row, fnet=_fnet_tables(s),
                           fwd=fwd, inv=inv, feats=_filter_features(s)))

    new_kv = ()
    for l in range(depth):
        cw = conv_w[l].reshape(3, 3, HY_W)
        cb = conv_b[l].reshape(3, HY_W)
        hb = hy_bias[l]
        for gi, g in enumerate(groups):
            s = g["seq"]
            if gi == 0:
                jobs = [(w_br, l), (w_out, l)] + ([(w_in, l + 1)] if l + 1 < depth else [])
                proj, gates, w_br_bf, w_out_bf, *w_in_next = _inproj(
                    g["x"], norm_g3, ada4, g["row"], w_in_bf, l, jobs)
            else:
                proj, gates = _inproj(g["x"], norm_g3, ada4, g["row"], w_in_bf, l)
            if gi == 0:
                ya, *new_kv = _ctx_attn(proj, qg3, kg3, new_kv, l, s, depth)
            else:
                qn, kn = _prep_lat(proj, qg3, kg3, l)
                ya = _lat_attn(qn, kn, cache_k, cache_v, win_bias, proj, l, s)
            yb = _fnet(proj, g["fnet"], s)
            kf = _filter(g["feats"], g["fwd"], w1p, b1, fr, f_w2, b2, f_w3, l, s)
            yc = _hyena(proj, cw, cb, hb, g["fwd"], g["inv"], kf, s)
            merged = _merge(ya, yb, yc, gates, w_br_bf)
            g["x"] = _outproj(merged, g["x"], ada4, g["row"], w_out_bf, l)
        if w_in_next:
            (w_in_bf,) = w_in_next

    nk, nv = new_kv
    return (groups[0]["x"].reshape(batch, seq, d),
            groups[1]["x"].reshape(dec_batch, dec_seq, d),
            nk.reshape(batch, depth, seq, N_HEADS, HEAD_DIM),
            nv.reshape(batch, depth, seq, N_HEADS, HEAD_DIM))
```

```python
import functools
import math

import jax
import jax.numpy as jnp
import numpy as np
from jax import lax
from jax.experimental import pallas as pl
from jax.experimental.pallas import tpu as pltpu

F32 = jnp.float32
BF16 = jnp.bfloat16

D_MODEL = 2048
GRID_W = 64
N_HEADS = 8
HEAD_DIM = 128
ATT_W = N_HEADS * HEAD_DIM
MAX_KH = 8
KW = 16
FNET_GROUPS = 4
FNET_GDIM = 128
FNET_W = FNET_GROUPS * FNET_GDIM
HY_W = 512
HYENA_ORDER = 2
FILTER_EMB = 33
FILTER_FF = 64
MIN_DECAY = math.log(1e-2) / 1.5
MAX_DECAY = math.log(1e-2) / 0.3
OFF_Q = 0
OFF_K = OFF_Q + ATT_W
OFF_V = OFF_K + ATT_W
OFF_GA = OFF_V + ATT_W
OFF_UB = OFF_GA + ATT_W
OFF_GB = OFF_UB + FNET_W
OFF_HC = OFF_GB + FNET_W
OFF_GC = OFF_HC + 3 * HY_W
OFF_MG = OFF_GC + HY_W
N_IN = OFF_MG + 3 * D_MODEL
EPS = 1e-6
NEG = -1e30
ATT_SCALE = HEAD_DIM ** -0.5
LOG2_E = math.log2(math.e)

LANES = 128
SUBLANES = 8
ADA_ROWS = 8
FREQ_CHUNK = 512
LAT_ROW_BLOCK = 4
ROW_TILE = 1024
COL_TILE = 1024
MIB = 1024 * 1024
SCOPED_VMEM_MIB = 56


def _params(semantics, vmem_mib):
    return pltpu.CompilerParams(dimension_semantics=semantics,
                                vmem_limit_bytes=max(vmem_mib, SCOPED_VMEM_MIB) * MIB)


def _sigmoid(x):
    return 0.5 * jnp.tanh(0.5 * x) + 0.5


def _silu(x):
    return x * _sigmoid(x)


def _dot(a, b):
    return jnp.dot(a, b, preferred_element_type=F32)


def _dot_nt(a, b):
    return lax.dot_general(a, b, (((1,), (1,)), ((), ())), preferred_element_type=F32)


def _ada_kernel(cv_ref, w_ref, b_ref, o_ref, s_ref):
    n_rows, d, _ = cv_ref.shape
    tn = w_ref.shape[1]

    @pl.when((pl.program_id(0) == 0) & (pl.program_id(1) == 0))
    def _():
        s_ref[...] = _silu(cv_ref[...])

    def body(i, acc):
        rows = pl.ds(pl.multiple_of(i * SUBLANES, SUBLANES), SUBLANES)
        w = w_ref[rows, :]
        return tuple(acc[r] + w * jnp.concatenate([s_ref[r, rows, :]] * (tn // LANES), axis=-1)
                     for r in range(n_rows))

    zero = jnp.zeros((SUBLANES, tn), F32)
    acc = lax.fori_loop(0, d // SUBLANES, body, (zero,) * n_rows, unroll=4)
    o_ref[...] = jnp.zeros(o_ref.shape, F32)
    for r in range(n_rows):
        o_ref[r:r + 1, :] = jnp.sum(acc[r], axis=0, keepdims=True) + b_ref[...]


def _ada(conds, w_ada, b_ada):
    depth, d, n = w_ada.shape
    rows = conds.shape[0]
    assert rows <= ADA_ROWS
    tn = COL_TILE
    cv = jnp.broadcast_to(conds[:, :, None], (rows, d, LANES))
    return pl.pallas_call(
        _ada_kernel,
        grid=(depth, n // tn),
        in_specs=[
            pl.BlockSpec((rows, d, LANES), lambda l, j: (0, 0, 0)),
            pl.BlockSpec((None, d, tn), lambda l, j: (l, 0, j)),
            pl.BlockSpec((None, 1, tn), lambda l, j: (l, 0, j)),
        ],
        out_specs=pl.BlockSpec((None, ADA_ROWS, tn), lambda l, j: (l, 0, j)),
        out_shape=jax.ShapeDtypeStruct((depth, ADA_ROWS, n), F32),
        scratch_shapes=[pltpu.VMEM((rows, d, LANES), F32)],
        compiler_params=_params(("arbitrary", "arbitrary"), 32),
        name="ada",
    )(cv, w_ada, b_ada.reshape(depth, 1, n))


def _cast_job_specs(jobs, n_i, n_j):
    args, in_specs, out_specs, out_shapes = [], [], [], []
    for w32, layer in jobs:
        _, rows, cols = w32.shape
        rb = rows // n_i
        n_ct = max(t for t in range(1, n_j + 1) if cols % (t * LANES) == 0)
        tw = cols // n_ct
        args.append(w32)
        in_specs.append(pl.BlockSpec((None, rb, tw),
                                     lambda i, j, layer=layer, n_ct=n_ct: (layer, i, jnp.minimum(j, n_ct - 1))))
        out_specs.append(pl.BlockSpec((rb, tw), lambda i, j, n_ct=n_ct: (i, jnp.minimum(j, n_ct - 1))))
        out_shapes.append(jax.ShapeDtypeStruct((rows, cols), BF16))
    return args, in_specs, out_specs, out_shapes


def _inproj_kernel(x_ref, g_ref, ada_ref, w_ref, *refs, rows, n_plain, n_jobs):
    o_ref, gate_ref = refs[n_jobs:n_jobs + 2]
    h_ref = refs[-1]
    for src_ref, dst_ref in zip(refs[:n_jobs], refs[n_jobs + 2:-1]):
        dst_ref[...] = src_ref[...].astype(BF16)
    j = pl.program_id(1)

    @pl.when(j == 0)
    def _():
        g = g_ref[...]
        shift = ada_ref[:, 0:D_MODEL]
        scale1 = 1.0 + ada_ref[:, D_MODEL:2 * D_MODEL]

        def body(r, carry):
            sl = pl.ds(pl.multiple_of(r * rows, rows), rows)
            x = x_ref[sl, :]
            ms = jnp.mean(x * x, axis=-1, keepdims=True)
            y = x * lax.rsqrt(ms + EPS) * g
            h_ref[sl, :] = (y * scale1 + shift).astype(BF16)
            return carry

        lax.fori_loop(0, x_ref.shape[0] // rows, body, 0, unroll=8)

    @pl.when(j < n_plain)
    def _():
        o_ref[...] = _dot(h_ref[...], w_ref[...]).astype(o_ref.dtype)

    @pl.when(j >= n_plain)
    def _():
        gate_ref[...] = _sigmoid(_dot(h_ref[...], w_ref[...])).astype(BF16)


def _inproj(x2d, norm_g3, ada4, ada_row, w_bf, l, cast_jobs=()):
    m, d = x2d.shape
    tm = min(ROW_TILE, m)
    tn = COL_TILE
    n_plain = OFF_MG // tn
    nxt_args, nxt_in, nxt_out, nxt_shape = _cast_job_specs(cast_jobs, m // tm, N_IN // tn)
    return pl.pallas_call(
        functools.partial(_inproj_kernel, rows=16, n_plain=n_plain, n_jobs=len(cast_jobs)),
        grid=(m // tm, N_IN // tn),
        in_specs=[
            pl.BlockSpec((tm, d), lambda i, j: (i, 0)),
            pl.BlockSpec((None, 1, d), lambda i, j: (l, 0, 0)),
            pl.BlockSpec((None, None, 1, 3 * d), lambda i, j: (l, ada_row(i * tm), 0, 0)),
            pl.BlockSpec((d, tn), lambda i, j: (0, j)),
        ] + nxt_in,
        out_specs=[pl.BlockSpec((tm, tn), lambda i, j: (i, jnp.minimum(j, n_plain - 1))),
                   pl.BlockSpec((tm, tn), lambda i, j: (i, jnp.maximum(j - n_plain, 0)))] + nxt_out,
        out_shape=[jax.ShapeDtypeStruct((m, OFF_MG), BF16),
                   jax.ShapeDtypeStruct((m, N_IN - OFF_MG), BF16)] + nxt_shape,
        scratch_shapes=[pltpu.VMEM((tm, d), BF16)],
        compiler_params=_params(("arbitrary", "arbitrary"), 52),
        name="inproj",
    )(x2d, norm_g3, ada4, w_bf, *nxt_args)


def _head_norm(x, g):
    ms = jnp.mean(x * x, axis=-1, keepdims=True)
    return x * lax.rsqrt(ms + EPS) * g


def _prep_lat_kernel(q_ref, k_ref, qg_ref, kg_ref, qn_ref, kn_ref):
    for h in range(N_HEADS):
        sl = slice(h * HEAD_DIM, (h + 1) * HEAD_DIM)
        qn_ref[:, sl] = _head_norm(q_ref[:, sl].astype(F32), qg_ref[...]).astype(BF16)
        kn_ref[:, sl] = _head_norm(k_ref[:, sl].astype(F32), kg_ref[...]).astype(BF16)


def _prep_lat(proj, qg3, kg3, l):
    m = proj.shape[0]
    tm = min(256, m)
    col = lambda c: pl.BlockSpec((tm, ATT_W), lambda i: (i, c))
    gspec = pl.BlockSpec((None, 1, HEAD_DIM), lambda i: (l, 0, 0))
    return pl.pallas_call(
        _prep_lat_kernel, grid=(m // tm,),
        in_specs=[col(OFF_Q // ATT_W), col(OFF_K // ATT_W), gspec, gspec],
        out_specs=[col(0)] * 2, out_shape=[jax.ShapeDtypeStruct((m, ATT_W), BF16)] * 2,
        compiler_params=_params(("arbitrary",), 32), name="prep_lat",
    )(proj, proj, qg3, kg3)


def _ctx_attn_kernel(q_ref, k_ref, v_ref, ga_ref, qg_ref, kg_ref, *refs, layer):
    o_ref, nk_ref, nv_ref = refs[-3:]
    if len(refs) == 3:
        for other in range(nk_ref.shape[0]):
            if other != layer:
                nk_ref[other] = jnp.zeros(nk_ref.shape[1:], F32)
                nv_ref[other] = jnp.zeros(nv_ref.shape[1:], F32)
        nk_ref, nv_ref = nk_ref.at[layer], nv_ref.at[layer]
    for h in range(N_HEADS):
        sl = slice(h * HEAD_DIM, (h + 1) * HEAD_DIM)
        q = _head_norm(q_ref[:, sl].astype(F32), qg_ref[...])
        k = _head_norm(k_ref[:, sl].astype(F32), kg_ref[...])
        v = v_ref[:, sl]
        nk_ref[:, sl] = k
        nv_ref[:, sl] = v.astype(F32)
        s = _dot_nt(q.astype(BF16), k.astype(BF16)) * (ATT_SCALE * LOG2_E)
        p = jnp.exp2(s - jnp.max(s, axis=-1, keepdims=True))
        inv = 1.0 / jnp.sum(p, axis=-1, keepdims=True)
        o = _dot(p.astype(BF16), v) * inv
        o_ref[:, sl] = (o * _silu(ga_ref[:, sl].astype(F32))).astype(BF16)


def _ctx_attn(proj, qg3, kg3, new_kv, l, seq, depth):
    m = proj.shape[0]
    kv_shape = jax.ShapeDtypeStruct((m // seq, depth, seq, ATT_W), F32)
    col = lambda c: pl.BlockSpec((seq, ATT_W), lambda b: (b, c))
    gspec = pl.BlockSpec((None, 1, HEAD_DIM), lambda b: (l, 0, 0))
    if new_kv:
        kv_spec = pl.BlockSpec((None, None, seq, ATT_W), lambda b: (b, l, 0, 0))
    else:
        kv_spec = pl.BlockSpec((None, depth, seq, ATT_W), lambda b: (b, 0, 0, 0))
    return pl.pallas_call(
        functools.partial(_ctx_attn_kernel, layer=l), grid=(m // seq,),
        in_specs=[col(OFF_Q // ATT_W), col(OFF_K // ATT_W), col(OFF_V // ATT_W), col(OFF_GA // ATT_W),
                  gspec, gspec] + [pl.BlockSpec(memory_space=pl.ANY)] * len(new_kv),
        out_specs=[col(0), kv_spec, kv_spec],
        out_shape=[jax.ShapeDtypeStruct((m, ATT_W), BF16), kv_shape, kv_shape],
        input_output_aliases={6: 1, 7: 2} if new_kv else {},
        compiler_params=_params(("arbitrary",), 32), name="ctx_attn",
    )(proj, proj, proj, proj, qg3, kg3, *new_kv)


def _window_bias(rpb):
    depth, heads, n_dr, n_dc = rpb.shape
    n_e = n_dr + 1
    c = np.arange(GRID_W)[:, None, None]
    p = np.arange(2)[None, :, None]
    kc = np.arange(GRID_W)[None, None, :]
    cs = np.clip(c - KW // 2, 0, GRID_W - KW)
    in_cols = np.broadcast_to((kc >= cs) & (kc < cs + KW), (GRID_W, 2, GRID_W)).reshape(-1)
    dc = np.broadcast_to(np.clip(kc - c + KW - 1, 0, n_dc - 1), (GRID_W, 2, GRID_W)).reshape(-1)
    member = np.broadcast_to(p, (GRID_W, 2, GRID_W)).reshape(-1)
    row_member = np.repeat(np.arange(2), n_dc)[:, None]
    row_dc = np.tile(np.arange(n_dc), 2)[:, None]
    onehot = ((jnp.asarray(row_member) == jnp.asarray(member)[None, :])
              & (jnp.asarray(row_dc) == jnp.asarray(dc)[None, :])).astype(F32)
    padded = jnp.pad(rpb, ((0, 0), (0, 0), (1, 1), (0, 0)))
    pairs = jnp.concatenate([padded[:, :, :-1], padded[:, :, 1:]], axis=-1)
    cb = jnp.einsum("lhek,kx->lhex", pairs, onehot, precision=lax.Precision.HIGHEST)
    row_ok = (np.arange(n_e)[:, None] + member[None, :] >= 1) & (np.arange(n_e)[:, None] + member[None, :] <= n_dr)
    cb = jnp.where(jnp.asarray(row_ok & in_cols[None, :]), cb * LOG2_E, NEG)
    return cb.reshape(depth, heads, n_e, GRID_W, 2 * GRID_W)


def _lat_attn_kernel(q_ref, k_ref, v_ref, ck_ref, cv_ref, bias_ref, ga_ref, o_ref,
                     ckb_ref, cvb_ref, *, rows, kh, rb, kr):
    g = pl.program_id(1)

    @pl.when(g == 0)
    def _():
        for h in range(N_HEADS):
            sl = slice(h * HEAD_DIM, (h + 1) * HEAD_DIM)
            ckb_ref[:, sl] = ck_ref[:, h, :].astype(BF16)
            cvb_ref[:, sl] = cv_ref[:, h, :].astype(BF16)

    r0 = g * rb
    ws = jnp.clip(r0 - kh // 2, 0, rows - kr)
    span = pl.ds(pl.multiple_of(ws * GRID_W, GRID_W), kr * GRID_W)
    q_row = r0 + lax.broadcasted_iota(jnp.int32, (rb * GRID_W, 1), 0) // GRID_W
    k_row = ws + lax.broadcasted_iota(jnp.int32, (1, kr * GRID_W), 1) // GRID_W
    q_rs = jnp.clip(q_row - kh // 2, 0, rows - kh)
    in_window = (k_row >= q_rs) & (k_row < q_rs + kh)
    n_pairs = bias_ref.shape[1]
    for h in range(N_HEADS):
        sl = slice(h * HEAD_DIM, (h + 1) * HEAD_DIM)
        q = q_ref[:, sl]
        bias = jnp.concatenate(
            [jnp.concatenate(
                [bias_ref[h, jnp.clip(ws + 2 * j - (r0 + qi) + MAX_KH, 0, n_pairs - 1)]
                 for j in range(kr // 2)], axis=-1)
             for qi in range(rb)], axis=0)
        s_win = jnp.where(in_window, _dot_nt(q, k_ref[span, sl]) * (ATT_SCALE * LOG2_E) + bias, NEG)
        s_ctx = _dot_nt(q, ckb_ref[:, sl]) * (ATT_SCALE * LOG2_E)
        mx = jnp.maximum(jnp.max(s_win, axis=-1, keepdims=True),
                         jnp.max(s_ctx, axis=-1, keepdims=True))
        p_win = jnp.exp2(s_win - mx)
        p_ctx = jnp.exp2(s_ctx - mx)
        inv = 1.0 / (jnp.sum(p_win, axis=-1, keepdims=True) + jnp.sum(p_ctx, axis=-1, keepdims=True))
        o = (_dot(p_win.astype(BF16), v_ref[span, sl]) + _dot(p_ctx.astype(BF16), cvb_ref[:, sl])) * inv
        o_ref[:, sl] = (o * _silu(ga_ref[:, sl].astype(F32))).astype(BF16)


def _lat_attn(qn, kn, cache_k, cache_v, bias, proj, l, seq):
    m = qn.shape[0]
    batch = m // seq
    rows = seq // GRID_W
    kh = min(MAX_KH, rows)
    rb = min(LAT_ROW_BLOCK, rows)
    kr = min(rows, kh + rb)
    assert rows % rb == 0 and kr % 2 == 0
    past = cache_k.shape[2]
    qspec = pl.BlockSpec((rb * GRID_W, ATT_W), lambda b, g: (b * (rows // rb) + g, 0))
    kspec = pl.BlockSpec((seq, ATT_W), lambda b, g: (b, 0))
    vspec = pl.BlockSpec((seq, ATT_W), lambda b, g: (b, OFF_V // ATT_W))
    cspec = pl.BlockSpec((None, None, past, N_HEADS, HEAD_DIM), lambda b, g: (b, l, 0, 0, 0))
    return pl.pallas_call(
        functools.partial(_lat_attn_kernel, rows=rows, kh=kh, rb=rb, kr=kr),
        grid=(batch, rows // rb),
        in_specs=[qspec, kspec, vspec, cspec, cspec,
                  pl.BlockSpec((None,) + bias.shape[1:], lambda b, g: (l, 0, 0, 0, 0)),
                  pl.BlockSpec((rb * GRID_W, ATT_W),
                               lambda b, g: (b * (rows // rb) + g, OFF_GA // ATT_W))],
        out_specs=qspec,
        out_shape=jax.ShapeDtypeStruct((m, ATT_W), BF16),
        scratch_shapes=[pltpu.VMEM((past, ATT_W), BF16), pltpu.VMEM((past, ATT_W), BF16)],
        compiler_params=_params(("arbitrary", "arbitrary"), 48), name="lat_attn",
    )(qn, kn, proj, cache_k, cache_v, bias, proj)


def _phase_cos_sin(phase, period):
    ang = (phase % period).astype(F32) * (2.0 * math.pi / period)
    return jnp.cos(ang), jnp.sin(ang)


def _trig_kernel(ca_ref, sa_ref, cb_ref, sb_ref, *refs, patch_row):
    o_ref = refs[-1]
    cb, sb = cb_ref[...], sb_ref[...]
    for a in range(ca_ref.shape[0]):
        blk = ca_ref[a:a + 1, :] * cb - sa_ref[a:a + 1, :] * sb
        if patch_row is not None:
            row = (pl.program_id(1) * ca_ref.shape[0] + a) * LANES + lax.broadcasted_iota(jnp.int32, (LANES, 1), 0)
            blk = jnp.where(row == patch_row, refs[0][...], blk)
        o_ref[a * LANES:(a + 1) * LANES, :] = blk.astype(o_ref.dtype)


def _trig_table(blk_freq, blk_off, col_pos, col_off, period, patch=None):
    n_j, n_col = col_pos.shape
    n_blk = blk_freq.shape[0]
    ca, sa = _phase_cos_sin(col_pos[:, None, :] * blk_freq[None, :, None] + blk_off[None, :, None]
                            + col_off[:, None, :], period)
    cb, sb = _phase_cos_sin(col_pos[:, None, :] * jnp.arange(LANES, dtype=jnp.int32)[None, :, None], period)
    ab = min(SUBLANES, n_blk)
    blk_spec = pl.BlockSpec((None, ab, n_col), lambda j, i: (j, i, 0))
    tab_spec = pl.BlockSpec((None, LANES, n_col), lambda j, i: (j, 0, 0))
    extra, extra_specs = ([], []) if patch is None else ([patch[1]], [pl.BlockSpec((1, n_col), lambda j, i: (0, 0))])
    return pl.pallas_call(
        functools.partial(_trig_kernel, patch_row=None if patch is None else patch[0]),
        grid=(n_j, n_blk // ab),
        in_specs=[blk_spec, blk_spec, tab_spec, tab_spec] + extra_specs,
        out_specs=pl.BlockSpec((None, ab * LANES, n_col), lambda j, i: (j, i, 0)),
        out_shape=jax.ShapeDtypeStruct((n_j, n_blk * LANES, n_col), BF16),
        compiler_params=_params(("arbitrary", "arbitrary"), 32), name="trig_table",
    )(ca, sa, cb, sb, *extra)


def _fnet_tables(seq):
    col = jnp.arange(2 * seq, dtype=jnp.int32)[None, :]
    blk = jnp.arange(seq // LANES, dtype=jnp.int32) * LANES
    tab = _trig_table(blk, jnp.zeros_like(blk), col % seq, (col // seq) * (seq // 4), seq)[0]
    g = jnp.arange(FNET_GDIM, dtype=jnp.int32)
    cc, sc = _phase_cos_sin(g[:, None] * g[None, :], FNET_GDIM)
    return tab, cc.astype(BF16), sc.astype(BF16)


def _fnet_kernel(u_ref, gb_ref, tl_ref, cc_ref, sc_ref, o_ref, t_ref, *, seq, tl, sb):
    i = pl.program_id(1)

    @pl.when(i == 0)
    def _():
        for s in range(sb):
            for g in range(FNET_GROUPS):
                ug = u_ref[s * seq:(s + 1) * seq, g * FNET_GDIM:(g + 1) * FNET_GDIM]
                lanes = slice(s * FNET_W + g * FNET_GDIM, s * FNET_W + (g + 1) * FNET_GDIM)
                t_ref[0:seq, lanes] = _dot(ug, cc_ref[...]).astype(BF16)
                t_ref[seq:2 * seq, lanes] = _dot(ug, sc_ref[...]).astype(BF16)

    y = _dot(tl_ref[...], t_ref[...]) * ((seq * FNET_GDIM) ** -0.5)
    for s in range(sb):
        rows = slice(s * tl, (s + 1) * tl)
        o_ref[rows, :] = (y[:, s * FNET_W:(s + 1) * FNET_W] * _silu(gb_ref[rows, :].astype(F32))).astype(BF16)


def _fnet(proj, tables, seq):
    m = proj.shape[0]
    tab_l, cc, sc = tables
    tl = min(512, seq)
    nt = seq // tl
    sb = min(4, m // seq) if nt == 1 else 1
    return pl.pallas_call(
        functools.partial(_fnet_kernel, seq=seq, tl=tl, sb=sb),
        grid=(m // (sb * seq), nt),
        in_specs=[
            pl.BlockSpec((sb * seq, FNET_W), lambda b, i: (b, OFF_UB // FNET_W)),
            pl.BlockSpec((sb * tl, FNET_W), lambda b, i: (b * nt + i, OFF_GB // FNET_W)),
            pl.BlockSpec((tl, 2 * seq), lambda b, i: (i, 0)),
            pl.BlockSpec((FNET_GDIM, FNET_GDIM), lambda b, i: (0, 0)),
            pl.BlockSpec((FNET_GDIM, FNET_GDIM), lambda b, i: (0, 0)),
        ],
        out_specs=pl.BlockSpec((sb * tl, FNET_W), lambda b, i: (b * nt + i, 0)),
        out_shape=jax.ShapeDtypeStruct((m, FNET_W), BF16),
        scratch_shapes=[pltpu.VMEM((2 * seq, sb * FNET_W), BF16)],
        compiler_params=_params(("arbitrary", "arbitrary"), 40), name="fnet",
    )(proj, proj, tab_l, cc, sc)


def _hyena_tiling(seq):
    if seq > 1024:
        return min(FREQ_CHUNK, seq), HY_W // 2, 2
    return min(FREQ_CHUNK, seq), HY_W, 4


def _hyena_tables(seq):
    fc = _hyena_tiling(seq)[0]
    nch = seq // fc
    period = 2 * seq
    pos = jnp.arange(seq, dtype=jnp.int32)
    zeros = jnp.zeros_like(pos)
    blk = jnp.arange(2 * seq // LANES, dtype=jnp.int32) * LANES
    k0 = blk % (2 * fc)
    blk_freq = (blk // (2 * fc)) * fc + k0 % fc
    blk_off = (k0 // fc) * (period // 4)
    nyquist = (1 - 2 * (pos % 2)).astype(F32)[None, :]
    fwd = _trig_table(blk_freq, blk_off, pos[None, :], zeros[None, :], period, patch=(fc, nyquist))
    k = jnp.arange(2 * fc, dtype=jnp.int32)
    f = jnp.arange(nch, dtype=jnp.int32)[:, None] * fc + (k % fc)[None, :]
    is_sec = jnp.broadcast_to((k >= fc)[None, :], f.shape)
    nyq = is_sec & (f == 0)
    row_blk = jnp.arange(seq // LANES, dtype=jnp.int32) * LANES
    inv = _trig_table(row_blk, jnp.zeros_like(row_blk), jnp.where(nyq, seq, f),
                      jnp.where(is_sec & ~nyq, period // 4, 0), period)
    return fwd.reshape(nch, 2 * fc, seq), inv


def _filter_features(seq):
    t = jnp.linspace(0.0, 1.0, seq, dtype=F32)[:, None]
    bands = (FILTER_EMB - 1) // 2
    w = (2.0 * math.pi / seq) * jnp.arange(seq, dtype=F32)[:, None]
    f = jnp.linspace(1e-4, bands - 1, bands, dtype=F32)[None, :]
    z = jnp.concatenate([t, jnp.cos(w * f), -jnp.sin(w * f)], axis=-1)
    z = jnp.pad(z, ((0, 0), (0, 128 - FILTER_EMB)))
    deltas = jnp.abs(jnp.linspace(MIN_DECAY, MAX_DECAY, HY_W, dtype=F32))[None, :]
    return z, t, deltas


def _dot_f32(a, b):
    return jnp.dot(a, b, preferred_element_type=F32, precision=lax.Precision.HIGHEST)


def _split_bf16(x):
    hi = x.astype(BF16)
    return hi, (x - hi.astype(F32)).astype(BF16)


def _dot_3pass(a, b_hi, b_lo):
    a_hi, a_lo = _split_bf16(a)
    return _dot(a_hi, b_hi) + (_dot(a_hi, b_lo) + _dot(a_lo, b_hi))


def _filter_kernel(z_ref, t_ref, dl_ref, w1_ref, b1_ref, fr_ref, w2_ref, b2_ref, w3_ref,
                   fwd_ref, o_ref, h_ref, sd_ref, nyq_ref, *, seq, fc, rows):
    j = pl.program_id(0)
    nblk = 2 * HYENA_ORDER
    nrc = seq // rows

    @pl.when(j == 0)
    def _():
        fr = fr_ref[...]
        w3_hi, w3_lo = _split_bf16(w3_ref[...])

        def taps(r, acc):
            sl = pl.ds(pl.multiple_of(r * rows, rows), rows)
            h = jnp.sin(fr * (_dot_f32(z_ref[sl, :], w1_ref[...]) + b1_ref[...]))
            h = jnp.sin(fr * (_dot_f32(h, w2_ref[...]) + b2_ref[...]))
            h = _dot_3pass(h, w3_hi, w3_lo)
            decay = jnp.exp(-t_ref[sl, :] * dl_ref[...])
            h = h * jnp.concatenate([decay] * nblk, axis=-1)
            h_ref[sl, :] = h
            return acc + jnp.sum(jnp.abs(h), axis=0, keepdims=True)

        tot = lax.fori_loop(0, nrc, taps, jnp.zeros((1, nblk * HY_W), F32))
        inv = 1.0 / (tot + EPS)
        sgn = (1 - 2 * (lax.broadcasted_iota(jnp.int32, (rows, 1), 0) % 2)).astype(F32)

        def fold(r, acc):
            sl = pl.ds(pl.multiple_of(r * rows, rows), rows)
            h = h_ref[sl, :] * inv
            pos = r * rows + lax.broadcasted_iota(jnp.int32, (rows, 1), 0)
            sums = []
            for o in range(HYENA_ORDER):
                fw = h[:, (2 * o) * HY_W:(2 * o + 1) * HY_W]
                bw = jnp.where(pos == 0, 0.0, h[:, (2 * o + 1) * HY_W:(2 * o + 2) * HY_W])
                sums.append(fw + bw)
                sd_ref[0, sl, o * HY_W:(o + 1) * HY_W] = (fw + bw).astype(BF16)
                sd_ref[1, sl, o * HY_W:(o + 1) * HY_W] = (fw - bw).astype(BF16)
            return acc + jnp.sum(jnp.concatenate(sums, axis=-1) * sgn, axis=0, keepdims=True)

        nyq_ref[...] = lax.fori_loop(0, nrc, fold, jnp.zeros((1, HYENA_ORDER * HY_W), F32))

    re = _dot(fwd_ref[0:fc, :], sd_ref[0])
    sec = _dot(fwd_ref[fc:2 * fc, :], sd_ref[1])
    first = (lax.broadcasted_iota(jnp.int32, (fc, 1), 0) == 0) & (j == 0)
    sec = jnp.where(first, nyq_ref[...], sec)
    weight = jnp.where(first, 0.5 / seq, 1.0 / seq)
    o_ref[0:fc, :] = re * weight
    o_ref[fc:2 * fc, :] = sec * weight


def _filter(feats, fwd, w1p, b1, fr, w2, b2, w3, l, seq):
    z, t, deltas = feats
    nch, fc2, _ = fwd.shape
    fc = fc2 // 2
    width = HYENA_ORDER * HY_W
    rows = min(256, seq)
    full = lambda a: pl.BlockSpec(a.shape, lambda j: (0,) * a.ndim)
    lay = lambda a: pl.BlockSpec((None,) + a.shape[1:], lambda j: (l,) + (0,) * (a.ndim - 1))
    return pl.pallas_call(
        functools.partial(_filter_kernel, seq=seq, fc=fc, rows=rows),
        grid=(nch,),
        in_specs=[full(z), full(t), full(deltas), lay(w1p), lay(b1), lay(fr), lay(w2), lay(b2), lay(w3),
                  pl.BlockSpec((None, 2 * fc, seq), lambda j: (j, 0, 0))],
        out_specs=pl.BlockSpec((None, 2 * fc, width), lambda j: (j, 0, 0)),
        out_shape=jax.ShapeDtypeStruct((nch, 2 * fc, width), F32),
        scratch_shapes=[pltpu.VMEM((seq, 2 * width), F32),
                        pltpu.VMEM((2, seq, width), BF16),
                        pltpu.VMEM((1, width), F32)],
        compiler_params=_params(("arbitrary",), 48), name="hyena_filter",
    )(z, t, deltas, w1p, b1, fr, w2, b2, w3, fwd)


def _hyena_kernel(v_ref, x1_ref, x2_ref, gc_ref, cw_ref, cb_ref, hb_ref, fwd_ref, inv_ref, kf_ref,
                  o_ref, z_ref, zb_ref, y_ref, *, seq, fc, sb, tc):
    o = pl.program_id(2)
    j = pl.program_id(3)
    last_j = pl.num_programs(3) - 1
    pos = lax.broadcasted_iota(jnp.int32, (seq, 1), 0)

    def short_conv(ref, k, s):
        x = ref[s * seq:(s + 1) * seq, :].astype(F32)
        prev = jnp.where(pos == 0, 0.0, pltpu.roll(x, 1, 0))
        nxt = jnp.where(pos == seq - 1, 0.0, pltpu.roll(x, seq - 1, 0))
        return (prev * cw_ref[0, k:k + 1, :] + x * cw_ref[1, k:k + 1, :]
                + nxt * cw_ref[2, k:k + 1, :] + cb_ref[k:k + 1, :])

    @pl.when((o == 0) & (j == 0))
    def _():
        for s in range(sb):
            lanes = slice(s * tc, (s + 1) * tc)
            z = short_conv(v_ref, 0, s)
            z_ref[:, lanes] = z
            zb_ref[:, lanes] = z.astype(BF16)
        y_ref[...] = jnp.zeros_like(y_ref)

    @pl.when((o == 1) & (j == 0))
    def _():
        for s in range(sb):
            lanes = slice(s * tc, (s + 1) * tc)
            z = short_conv(x1_ref, 1, s) * (y_ref[:, lanes] + hb_ref[0:1, :] * z_ref[:, lanes])
            z_ref[:, lanes] = z
            zb_ref[:, lanes] = z.astype(BF16)
        y_ref[...] = jnp.zeros_like(y_ref)

    zf = _dot(fwd_ref[...], zb_ref[...])
    kr, ks = kf_ref[0:fc, :], kf_ref[fc:2 * fc, :]
    nyq = (lax.broadcasted_iota(jnp.int32, (fc, 1), 0) == 0) & (j == 0)
    parts = []
    for s in range(sb):
        lanes = slice(s * tc, (s + 1) * tc)
        zr, zs = zf[0:fc, lanes], zf[fc:2 * fc, lanes]
        ss = zs * ks
        yr = zr * kr - jnp.where(nyq, 0.0, ss)
        ys = jnp.where(nyq, ss, zr * ks + zs * kr)
        parts.append(jnp.concatenate([yr, ys], axis=0).astype(BF16))
    y_ref[...] += _dot(inv_ref[...], jnp.concatenate(parts, axis=-1))

    @pl.when((o == 1) & (j == last_j))
    def _():
        for s in range(sb):
            lanes = slice(s * tc, (s + 1) * tc)
            z = short_conv(x2_ref, 2, s) * (y_ref[:, lanes] + hb_ref[1:2, :] * z_ref[:, lanes])
            o_ref[s * seq:(s + 1) * seq, :] = (z * _silu(gc_ref[s * seq:(s + 1) * seq, :].astype(F32))).astype(BF16)


def _hyena(proj, cw, cb, hb, fwd, inv, kf, seq):
    m = proj.shape[0]
    nch, fc2, _ = fwd.shape
    fc = fc2 // 2
    _, tc, sb = _hyena_tiling(seq)
    sb = min(sb, m // seq)
    nct = HY_W // tc
    hc0 = OFF_HC // tc
    col = lambda k: pl.BlockSpec((sb * seq, tc), lambda b, c, o, j: (b, hc0 + k * nct + c))
    return pl.pallas_call(
        functools.partial(_hyena_kernel, seq=seq, fc=fc, sb=sb, tc=tc),
        grid=(m // (sb * seq), nct, HYENA_ORDER, nch),
        in_specs=[
            col(0), col(1), col(2),
            pl.BlockSpec((sb * seq, tc), lambda b, c, o, j: (b, OFF_GC // tc + c)),
            pl.BlockSpec((3, 3, tc), lambda b, c, o, j: (0, 0, c)),
            pl.BlockSpec((3, tc), lambda b, c, o, j: (0, c)),
            pl.BlockSpec((HYENA_ORDER, tc), lambda b, c, o, j: (0, c)),
            pl.BlockSpec((None, 2 * fc, seq), lambda b, c, o, j: (j, 0, 0)),
            pl.BlockSpec((None, seq, 2 * fc), lambda b, c, o, j: (j, 0, 0)),
            pl.BlockSpec((None, 2 * fc, tc), lambda b, c, o, j: (j, 0, o * nct + c)),
        ],
        out_specs=pl.BlockSpec((sb * seq, tc), lambda b, c, o, j: (b, c)),
        out_shape=jax.ShapeDtypeStruct((m, HY_W), BF16),
        scratch_shapes=[pltpu.VMEM((seq, sb * tc), F32), pltpu.VMEM((seq, sb * tc), BF16),
                        pltpu.VMEM((seq, sb * tc), F32)],
        compiler_params=_params(("arbitrary",) * 4, 56), name="hyena",
    )(proj, proj, proj, proj, cw, cb, hb, fwd, inv, kf)


def _merge_kernel(ya_ref, yb_ref, yc_ref, ga_ref, gb_ref, gc_ref, wa_ref, wb_ref, wc_ref, o_ref):
    acc = ga_ref[...].astype(F32) * _dot(ya_ref[...], wa_ref[...])
    acc += gb_ref[...].astype(F32) * _dot(yb_ref[...], wb_ref[...])
    acc += gc_ref[...].astype(F32) * _dot(yc_ref[...], wc_ref[...])
    o_ref[...] = acc.astype(BF16)


def _merge(ya, yb, yc, gates, w_bf):
    m = ya.shape[0]
    tm = min(ROW_TILE, m)
    tn = COL_TILE
    d = D_MODEL
    row = lambda w: pl.BlockSpec((tm, w), lambda i, j: (i, 0))
    gate = lambda k: pl.BlockSpec((tm, tn), lambda i, j: (i, k * d // tn + j))
    return pl.pallas_call(
        _merge_kernel, grid=(m // tm, d // tn),
        in_specs=[row(ATT_W), row(FNET_W), row(HY_W), gate(0), gate(1), gate(2),
                  pl.BlockSpec((ATT_W, tn), lambda i, j: (0, j)),
                  pl.BlockSpec((FNET_W, tn), lambda i, j: (ATT_W // FNET_W, j)),
                  pl.BlockSpec((HY_W, tn), lambda i, j: ((ATT_W + FNET_W) // HY_W, j))],
        out_specs=pl.BlockSpec((tm, tn), lambda i, j: (i, j)),
        out_shape=jax.ShapeDtypeStruct((m, d), BF16),
        compiler_params=_params(("arbitrary", "arbitrary"), 40), name="merge",
    )(ya, yb, yc, gates, gates, gates, w_bf, w_bf, w_bf)


def _outproj_kernel(mg_ref, w_ref, x_ref, ada_ref, o_ref):
    o_ref[...] = x_ref[...] + ada_ref[...] * _dot(mg_ref[...], w_ref[...])


def _outproj(merged, x2d, ada4, ada_row, w_bf, l):
    m, d = x2d.shape
    tm = min(ROW_TILE, m)
    tn = COL_TILE
    return pl.pallas_call(
        _outproj_kernel, grid=(m // tm, d // tn),
        in_specs=[pl.BlockSpec((tm, d), lambda i, j: (i, 0)),
                  pl.BlockSpec((d, tn), lambda i, j: (0, j)),
                  pl.BlockSpec((tm, tn), lambda i, j: (i, j)),
                  pl.BlockSpec((None, None, 1, tn),
                               lambda i, j: (l, ada_row(i * tm), 0, 2 * d // tn + j))],
        out_specs=pl.BlockSpec((tm, tn), lambda i, j: (i, j)),
        out_shape=jax.ShapeDtypeStruct((m, d), F32),
        compiler_params=_params(("arbitrary", "arbitrary"), 40), name="outproj",
    )(merged, w_bf, x2d, ada4)


def kernel(x_prompt, x_sample, cache_k, cache_v, c, c_ctx, norm_g, w_ada, b_ada, w_in, q_norm_g, k_norm_g, rpb, conv_w, conv_b, f_w1, f_b1, f_freq, f_w2, f_b2, f_w3, hy_bias, w_br, w_out):
    batch, seq, d = x_prompt.shape
    dec_batch, dec_seq, _ = x_sample.shape
    depth = norm_g.shape[0]

    ada4 = _ada(jnp.concatenate([c_ctx[None, :], c], axis=0), w_ada, b_ada)
    ada4 = ada4.reshape(depth, ADA_ROWS, 1, 3 * d)
    ctx_row = lambda tok: 0
    lat_row = lambda tok: 1 + tok // dec_seq

    w_in_bf = w_in[0].astype(BF16)
    norm_g3 = norm_g.reshape(depth, 1, d)
    qg3 = q_norm_g.reshape(depth, 1, HEAD_DIM)
    kg3 = k_norm_g.reshape(depth, 1, HEAD_DIM)
    w1p = jnp.pad(f_w1, ((0, 0), (0, 128 - FILTER_EMB), (0, 0)))
    b1 = f_b1.reshape(depth, 1, FILTER_FF)
    fr = f_freq.reshape(depth, 1, FILTER_FF)
    b2 = f_b2.reshape(depth, 1, FILTER_FF)
    win_bias = _window_bias(rpb)

    groups = []
    for x, s, row in ((x_prompt, seq, ctx_row), (x_sample, dec_seq, lat_row)):
        fwd, inv = _hyena_tables(s)
        groups.append(dict(x=x.reshape(-1, d), seq=s, row=row, fnet=_fnet_tables(s),
                           fwd=fwd, inv=inv, feats=_filter_features(s)))

    new_kv = ()
    for l in range(depth):
        cw = conv_w[l].reshape(3, 3, HY_W)
        cb = conv_b[l].reshape(3, HY_W)
        hb = hy_bias[l]
        for gi, g in enumerate(groups):
            s = g["seq"]
            if gi == 0:
                jobs = [(w_br, l), (w_out, l)] + ([(w_in, l + 1)] if l + 1 < depth else [])
                proj, gates, w_br_bf, w_out_bf, *w_in_next = _inproj(
                    g["x"], norm_g3, ada4, g["row"], w_in_bf, l, jobs)
            else:
                proj, gates = _inproj(g["x"], norm_g3, ada4, g["row"], w_in_bf, l)
            if gi == 0:
                ya, *new_kv = _ctx_attn(proj, qg3, kg3, new_kv, l, s, depth)
            else:
                qn, kn = _prep_lat(proj, qg3, kg3, l)
                ya = _lat_attn(qn, kn, cache_k, cache_v, win_bias, proj, l, s)
            yb = _fnet(proj, g["fnet"], s)
            kf = _filter(g["feats"], g["fwd"], w1p, b1, fr, f_w2, b2, f_w3, l, s)
            yc = _hyena(proj, cw, cb, hb, g["fwd"], g["inv"], kf, s)
            merged = _merge(ya, yb, yc, gates, w_br_bf)
            g["x"] = _outproj(merged, g["x"], ada4, g["row"], w_out_bf, l)
        if w_in_next:
            (w_in_bf,) = w_in_next

    nk, nv = new_kv
    return (groups[0]["x"].reshape(batch, seq, d),
            groups[1]["x"].reshape(dec_batch, dec_seq, d),
            nk.reshape(batch, depth, seq, N_HEADS, HEAD_DIM),
            nv.reshape(batch, depth, seq, N_HEADS, HEAD_DIM))
```

```python
import functools
import math

import jax
import jax.numpy as jnp
import numpy as np
from jax import lax
from jax.experimental import pallas as pl
from jax.experimental.pallas import tpu as pltpu

F32 = jnp.float32
BF16 = jnp.bfloat16

D_MODEL = 2048
GRID_W = 64
N_HEADS = 8
HEAD_DIM = 128
ATT_W = N_HEADS * HEAD_DIM
MAX_KH = 8
KW = 16
FNET_GROUPS = 4
FNET_GDIM = 128
FNET_W = FNET_GROUPS * FNET_GDIM
HY_W = 512
HYENA_ORDER = 2
FILTER_EMB = 33
FILTER_FF = 64
MIN_DECAY = math.log(1e-2) / 1.5
MAX_DECAY = math.log(1e-2) / 0.3
OFF_Q = 0
OFF_K = OFF_Q + ATT_W
OFF_V = OFF_K + ATT_W
OFF_GA = OFF_V + ATT_W
OFF_UB = OFF_GA + ATT_W
OFF_GB = OFF_UB + FNET_W
OFF_HC = OFF_GB + FNET_W
OFF_GC = OFF_HC + 3 * HY_W
OFF_MG = OFF_GC + HY_W
N_IN = OFF_MG + 3 * D_MODEL
EPS = 1e-6
NEG = -1e30
ATT_SCALE = HEAD_DIM ** -0.5
LOG2_E = math.log2(math.e)

LANES = 128
SUBLANES = 8
ADA_ROWS = 8
FREQ_CHUNK = 512
LAT_ROW_BLOCK = 4
ROW_TILE = 1024
COL_TILE = 1024
MIB = 1024 * 1024


def _params(semantics, vmem_mib):
    return pltpu.CompilerParams(dimension_semantics=semantics,
                                vmem_limit_bytes=vmem_mib * MIB)


def _sigmoid(x):
    return 0.5 * jnp.tanh(0.5 * x) + 0.5


def _silu(x):
    return x * _sigmoid(x)


def _dot(a, b):
    return jnp.dot(a, b, preferred_element_type=F32)


def _dot_nt(a, b):
    return lax.dot_general(a, b, (((1,), (1,)), ((), ())), preferred_element_type=F32)


def _ada_kernel(cv_ref, w_ref, b_ref, o_ref, s_ref):
    n_rows, d, _ = cv_ref.shape
    tn = w_ref.shape[1]

    @pl.when((pl.program_id(0) == 0) & (pl.program_id(1) == 0))
    def _():
        s_ref[...] = _silu(cv_ref[...])

    def body(i, acc):
        rows = pl.ds(pl.multiple_of(i * SUBLANES, SUBLANES), SUBLANES)
        w = w_ref[rows, :]
        return tuple(acc[r] + w * jnp.concatenate([s_ref[r, rows, :]] * (tn // LANES), axis=-1)
                     for r in range(n_rows))

    zero = jnp.zeros((SUBLANES, tn), F32)
    acc = lax.fori_loop(0, d // SUBLANES, body, (zero,) * n_rows, unroll=4)
    o_ref[...] = jnp.zeros(o_ref.shape, F32)
    for r in range(n_rows):
        o_ref[r:r + 1, :] = jnp.sum(acc[r], axis=0, keepdims=True) + b_ref[...]


def _ada(conds, w_ada, b_ada):
    depth, d, n = w_ada.shape
    rows = conds.shape[0]
    assert rows <= ADA_ROWS
    tn = COL_TILE
    cv = jnp.broadcast_to(conds[:, :, None], (rows, d, LANES))
    return pl.pallas_call(
        _ada_kernel,
        grid=(depth, n // tn),
        in_specs=[
            pl.BlockSpec((rows, d, LANES), lambda l, j: (0, 0, 0)),
            pl.BlockSpec((None, d, tn), lambda l, j: (l, 0, j)),
            pl.BlockSpec((None, 1, tn), lambda l, j: (l, 0, j)),
        ],
        out_specs=pl.BlockSpec((None, ADA_ROWS, tn), lambda l, j: (l, 0, j)),
        out_shape=jax.ShapeDtypeStruct((depth, ADA_ROWS, n), F32),
        scratch_shapes=[pltpu.VMEM((rows, d, LANES), F32)],
        compiler_params=_params(("arbitrary", "arbitrary"), 24),
        name="ada",
    )(cv, w_ada, b_ada.reshape(depth, 1, n))


def _cast_job_specs(jobs, n_i, n_j):
    args, in_specs, out_specs, out_shapes = [], [], [], []
    for w32, layer in jobs:
        _, rows, cols = w32.shape
        rb = rows // n_i
        n_ct = max(t for t in range(1, n_j + 1) if cols % (t * LANES) == 0)
        tw = cols // n_ct
        args.append(w32)
        in_specs.append(pl.BlockSpec((None, rb, tw),
                                     lambda i, j, layer=layer, n_ct=n_ct: (layer, i, jnp.minimum(j, n_ct - 1))))
        out_specs.append(pl.BlockSpec((rb, tw), lambda i, j, n_ct=n_ct: (i, jnp.minimum(j, n_ct - 1))))
        out_shapes.append(jax.ShapeDtypeStruct((rows, cols), BF16))
    return args, in_specs, out_specs, out_shapes


def _inproj_kernel(x_ref, g_ref, ada_ref, w_ref, *refs, rows, n_plain, n_jobs):
    o_ref, gate_ref = refs[n_jobs:n_jobs + 2]
    h_ref = refs[-1]
    for src_ref, dst_ref in zip(refs[:n_jobs], refs[n_jobs + 2:-1]):
        dst_ref[...] = src_ref[...].astype(BF16)
    j = pl.program_id(1)

    @pl.when(j == 0)
    def _():
        g = g_ref[...]
        shift = ada_ref[:, 0:D_MODEL]
        scale1 = 1.0 + ada_ref[:, D_MODEL:2 * D_MODEL]

        def body(r, carry):
            sl = pl.ds(pl.multiple_of(r * rows, rows), rows)
            x = x_ref[sl, :]
            ms = jnp.mean(x * x, axis=-1, keepdims=True)
            y = x * lax.rsqrt(ms + EPS) * g
            h_ref[sl, :] = (y * scale1 + shift).astype(BF16)
            return carry

        lax.fori_loop(0, x_ref.shape[0] // rows, body, 0, unroll=8)

    @pl.when(j < n_plain)
    def _():
        o_ref[...] = _dot(h_ref[...], w_ref[...]).astype(o_ref.dtype)

    @pl.when(j >= n_plain)
    def _():
        gate_ref[...] = _sigmoid(_dot(h_ref[...], w_ref[...])).astype(BF16)


def _inproj(x2d, norm_g3, ada4, ada_row, w_bf, l, cast_jobs=()):
    m, d = x2d.shape
    tm = min(ROW_TILE, m)
    tn = COL_TILE
    n_plain = OFF_MG // tn
    nxt_args, nxt_in, nxt_out, nxt_shape = _cast_job_specs(cast_jobs, m // tm, N_IN // tn)
    return pl.pallas_call(
        functools.partial(_inproj_kernel, rows=16, n_plain=n_plain, n_jobs=len(cast_jobs)),
        grid=(m // tm, N_IN // tn),
        in_specs=[
            pl.BlockSpec((tm, d), lambda i, j: (i, 0)),
            pl.BlockSpec((None, 1, d), lambda i, j: (l, 0, 0)),
            pl.BlockSpec((None, None, 1, 3 * d), lambda i, j: (l, ada_row(i * tm), 0, 0)),
            pl.BlockSpec((d, tn), lambda i, j: (0, j)),
        ] + nxt_in,
        out_specs=[pl.BlockSpec((tm, tn), lambda i, j: (i, jnp.minimum(j, n_plain - 1))),
                   pl.BlockSpec((tm, tn), lambda i, j: (i, jnp.maximum(j - n_plain, 0)))] + nxt_out,
        out_shape=[jax.ShapeDtypeStruct((m, OFF_MG), BF16),
                   jax.ShapeDtypeStruct((m, N_IN - OFF_MG), BF16)] + nxt_shape,
        scratch_shapes=[pltpu.VMEM((tm, d), BF16)],
        compiler_params=_params(("arbitrary", "arbitrary"), 48),
        name="inproj",
    )(x2d, norm_g3, ada4, w_bf, *nxt_args)


def _head_norm(x, g):
    ms = jnp.mean(x * x, axis=-1, keepdims=True)
    return x * lax.rsqrt(ms + EPS) * g


def _prep_lat_kernel(q_ref, k_ref, qg_ref, kg_ref, qn_ref, kn_ref):
    for h in range(N_HEADS):
        sl = slice(h * HEAD_DIM, (h + 1) * HEAD_DIM)
        qn_ref[:, sl] = _head_norm(q_ref[:, sl].astype(F32), qg_ref[...]).astype(BF16)
        kn_ref[:, sl] = _head_norm(k_ref[:, sl].astype(F32), kg_ref[...]).astype(BF16)


def _prep_lat(proj, qg3, kg3, l):
    m = proj.shape[0]
    tm = min(256, m)
    col = lambda c: pl.BlockSpec((tm, ATT_W), lambda i: (i, c))
    gspec = pl.BlockSpec((None, 1, HEAD_DIM), lambda i: (l, 0, 0))
    return pl.pallas_call(
        _prep_lat_kernel, grid=(m // tm,),
        in_specs=[col(OFF_Q // ATT_W), col(OFF_K // ATT_W), gspec, gspec],
        out_specs=[col(0)] * 2, out_shape=[jax.ShapeDtypeStruct((m, ATT_W), BF16)] * 2,
        compiler_params=_params(("arbitrary",), 8), name="prep_lat",
    )(proj, proj, qg3, kg3)


def _ctx_attn_kernel(q_ref, k_ref, v_ref, ga_ref, qg_ref, kg_ref, *refs, layer):
    o_ref, nk_ref, nv_ref = refs[-3:]
    if len(refs) == 3:
        for other in range(nk_ref.shape[0]):
            if other != layer:
                nk_ref[other] = jnp.zeros(nk_ref.shape[1:], F32)
                nv_ref[other] = jnp.zeros(nv_ref.shape[1:], F32)
        nk_ref, nv_ref = nk_ref.at[layer], nv_ref.at[layer]
    for h in range(N_HEADS):
        sl = slice(h * HEAD_DIM, (h + 1) * HEAD_DIM)
        q = _head_norm(q_ref[:, sl].astype(F32), qg_ref[...])
        k = _head_norm(k_ref[:, sl].astype(F32), kg_ref[...])
        v = v_ref[:, sl]
        nk_ref[:, sl] = k
        nv_ref[:, sl] = v.astype(F32)
        s = _dot_nt(q.astype(BF16), k.astype(BF16)) * (ATT_SCALE * LOG2_E)
        p = jnp.exp2(s - jnp.max(s, axis=-1, keepdims=True))
        inv = 1.0 / jnp.sum(p, axis=-1, keepdims=True)
        o = _dot(p.astype(BF16), v) * inv
        o_ref[:, sl] = (o * _silu(ga_ref[:, sl].astype(F32))).astype(BF16)


def _ctx_attn(proj, qg3, kg3, new_kv, l, seq, depth):
    m = proj.shape[0]
    kv_shape = jax.ShapeDtypeStruct((m // seq, depth, seq, ATT_W), F32)
    col = lambda c: pl.BlockSpec((seq, ATT_W), lambda b: (b, c))
    gspec = pl.BlockSpec((None, 1, HEAD_DIM), lambda b: (l, 0, 0))
    if new_kv:
        kv_spec = pl.BlockSpec((None, None, seq, ATT_W), lambda b: (b, l, 0, 0))
    else:
        kv_spec = pl.BlockSpec((None, depth, seq, ATT_W), lambda b: (b, 0, 0, 0))
    return pl.pallas_call(
        functools.partial(_ctx_attn_kernel, layer=l), grid=(m // seq,),
        in_specs=[col(OFF_Q // ATT_W), col(OFF_K // ATT_W), col(OFF_V // ATT_W), col(OFF_GA // ATT_W),
                  gspec, gspec] + [pl.BlockSpec(memory_space=pl.ANY)] * len(new_kv),
        out_specs=[col(0), kv_spec, kv_spec],
        out_shape=[jax.ShapeDtypeStruct((m, ATT_W), BF16), kv_shape, kv_shape],
        input_output_aliases={6: 1, 7: 2} if new_kv else {},
        compiler_params=_params(("arbitrary",), 16), name="ctx_attn",
    )(proj, proj, proj, proj, qg3, kg3, *new_kv)


def _window_bias(rpb):
    depth, heads, n_dr, n_dc = rpb.shape
    n_e = n_dr + 1
    c = np.arange(GRID_W)[:, None, None]
    p = np.arange(2)[None, :, None]
    kc = np.arange(GRID_W)[None, None, :]
    cs = np.clip(c - KW // 2, 0, GRID_W - KW)
    in_cols = np.broadcast_to((kc >= cs) & (kc < cs + KW), (GRID_W, 2, GRID_W)).reshape(-1)
    dc = np.broadcast_to(np.clip(kc - c + KW - 1, 0, n_dc - 1), (GRID_W, 2, GRID_W)).reshape(-1)
    member = np.broadcast_to(p, (GRID_W, 2, GRID_W)).reshape(-1)
    row_member = np.repeat(np.arange(2), n_dc)[:, None]
    row_dc = np.tile(np.arange(n_dc), 2)[:, None]
    onehot = ((jnp.asarray(row_member) == jnp.asarray(member)[None, :])
              & (jnp.asarray(row_dc) == jnp.asarray(dc)[None, :])).astype(F32)
    padded = jnp.pad(rpb, ((0, 0), (0, 0), (1, 1), (0, 0)))
    pairs = jnp.concatenate([padded[:, :, :-1], padded[:, :, 1:]], axis=-1)
    cb = jnp.einsum("lhek,kx->lhex", pairs, onehot, precision=lax.Precision.HIGHEST)
    row_ok = (np.arange(n_e)[:, None] + member[None, :] >= 1) & (np.arange(n_e)[:, None] + member[None, :] <= n_dr)
    cb = jnp.where(jnp.asarray(row_ok & in_cols[None, :]), cb * LOG2_E, NEG)
    return cb.reshape(depth, heads, n_e, GRID_W, 2 * GRID_W)


def _lat_attn_kernel(q_ref, k_ref, v_ref, ck_ref, cv_ref, bias_ref, ga_ref, o_ref,
                     ckb_ref, cvb_ref, *, rows, kh, rb, kr):
    g = pl.program_id(1)

    @pl.when(g == 0)
    def _():
        for h in range(N_HEADS):
            sl = slice(h * HEAD_DIM, (h + 1) * HEAD_DIM)
            ckb_ref[:, sl] = ck_ref[:, h, :].astype(BF16)
            cvb_ref[:, sl] = cv_ref[:, h, :].astype(BF16)

    r0 = g * rb
    ws = jnp.clip(r0 - kh // 2, 0, rows - kr)
    span = pl.ds(pl.multiple_of(ws * GRID_W, GRID_W), kr * GRID_W)
    q_row = r0 + lax.broadcasted_iota(jnp.int32, (rb * GRID_W, 1), 0) // GRID_W
    k_row = ws + lax.broadcasted_iota(jnp.int32, (1, kr * GRID_W), 1) // GRID_W
    q_rs = jnp.clip(q_row - kh // 2, 0, rows - kh)
    in_window = (k_row >= q_rs) & (k_row < q_rs + kh)
    n_pairs = bias_ref.shape[1]
    for h in range(N_HEADS):
        sl = slice(h * HEAD_DIM, (h + 1) * HEAD_DIM)
        q = q_ref[:, sl]
        bias = jnp.concatenate(
            [jnp.concatenate(
                [bias_ref[h, jnp.clip(ws + 2 * j - (r0 + qi) + MAX_KH, 0, n_pairs - 1)]
                 for j in range(kr // 2)], axis=-1)
             for qi in range(rb)], axis=0)
        s_win = jnp.where(in_window, _dot_nt(q, k_ref[span, sl]) * (ATT_SCALE * LOG2_E) + bias, NEG)
        s_ctx = _dot_nt(q, ckb_ref[:, sl]) * (ATT_SCALE * LOG2_E)
        mx = jnp.maximum(jnp.max(s_win, axis=-1, keepdims=True),
                         jnp.max(s_ctx, axis=-1, keepdims=True))
        p_win = jnp.exp2(s_win - mx)
        p_ctx = jnp.exp2(s_ctx - mx)
        inv = 1.0 / (jnp.sum(p_win, axis=-1, keepdims=True) + jnp.sum(p_ctx, axis=-1, keepdims=True))
        o = (_dot(p_win.astype(BF16), v_ref[span, sl]) + _dot(p_ctx.astype(BF16), cvb_ref[:, sl])) * inv
        o_ref[:, sl] = (o * _silu(ga_ref[:, sl].astype(F32))).astype(BF16)


def _lat_attn(qn, kn, cache_k, cache_v, bias, proj, l, seq):
    m = qn.shape[0]
    batch = m // seq
    rows = seq // GRID_W
    kh = min(MAX_KH, rows)
    rb = min(LAT_ROW_BLOCK, rows)
    kr = min(rows, kh + rb)
    assert rows % rb == 0 and kr % 2 == 0
    past = cache_k.shape[2]
    qspec = pl.BlockSpec((rb * GRID_W, ATT_W), lambda b, g: (b * (rows // rb) + g, 0))
    kspec = pl.BlockSpec((seq, ATT_W), lambda b, g: (b, 0))
    vspec = pl.BlockSpec((seq, ATT_W), lambda b, g: (b, OFF_V // ATT_W))
    cspec = pl.BlockSpec((None, None, past, N_HEADS, HEAD_DIM), lambda b, g: (b, l, 0, 0, 0))
    return pl.pallas_call(
        functools.partial(_lat_attn_kernel, rows=rows, kh=kh, rb=rb, kr=kr),
        grid=(batch, rows // rb),
        in_specs=[qspec, kspec, vspec, cspec, cspec,
                  pl.BlockSpec((None,) + bias.shape[1:], lambda b, g: (l, 0, 0, 0, 0)),
                  pl.BlockSpec((rb * GRID_W, ATT_W),
                               lambda b, g: (b * (rows // rb) + g, OFF_GA // ATT_W))],
        out_specs=qspec,
        out_shape=jax.ShapeDtypeStruct((m, ATT_W), BF16),
        scratch_shapes=[pltpu.VMEM((past, ATT_W), BF16), pltpu.VMEM((past, ATT_W), BF16)],
        compiler_params=_params(("arbitrary", "arbitrary"), 48), name="lat_attn",
    )(qn, kn, proj, cache_k, cache_v, bias, proj)


def _phase_cos_sin(phase, period):
    ang = (phase % period).astype(F32) * (2.0 * math.pi / period)
    return jnp.cos(ang), jnp.sin(ang)


def _trig_kernel(ca_ref, sa_ref, cb_ref, sb_ref, *refs, patch_row):
    o_ref = refs[-1]
    cb, sb = cb_ref[...], sb_ref[...]
    for a in range(ca_ref.shape[0]):
        blk = ca_ref[a:a + 1, :] * cb - sa_ref[a:a + 1, :] * sb
        if patch_row is not None:
            row = (pl.program_id(1) * ca_ref.shape[0] + a) * LANES + lax.broadcasted_iota(jnp.int32, (LANES, 1), 0)
            blk = jnp.where(row == patch_row, refs[0][...], blk)
        o_ref[a * LANES:(a + 1) * LANES, :] = blk.astype(o_ref.dtype)


def _trig_table(blk_freq, blk_off, col_pos, col_off, period, patch=None):
    n_j, n_col = col_pos.shape
    n_blk = blk_freq.shape[0]
    ca, sa = _phase_cos_sin(col_pos[:, None, :] * blk_freq[None, :, None] + blk_off[None, :, None]
                            + col_off[:, None, :], period)
    cb, sb = _phase_cos_sin(col_pos[:, None, :] * jnp.arange(LANES, dtype=jnp.int32)[None, :, None], period)
    ab = min(SUBLANES, n_blk)
    blk_spec = pl.BlockSpec((None, ab, n_col), lambda j, i: (j, i, 0))
    tab_spec = pl.BlockSpec((None, LANES, n_col), lambda j, i: (j, 0, 0))
    extra, extra_specs = ([], []) if patch is None else ([patch[1]], [pl.BlockSpec((1, n_col), lambda j, i: (0, 0))])
    return pl.pallas_call(
        functools.partial(_trig_kernel, patch_row=None if patch is None else patch[0]),
        grid=(n_j, n_blk // ab),
        in_specs=[blk_spec, blk_spec, tab_spec, tab_spec] + extra_specs,
        out_specs=pl.BlockSpec((None, ab * LANES, n_col), lambda j, i: (j, i, 0)),
        out_shape=jax.ShapeDtypeStruct((n_j, n_blk * LANES, n_col), BF16),
        compiler_params=_params(("arbitrary", "arbitrary"), 20), name="trig_table",
    )(ca, sa, cb, sb, *extra)


def _fnet_tables(seq):
    col = jnp.arange(2 * seq, dtype=jnp.int32)[None, :]
    blk = jnp.arange(seq // LANES, dtype=jnp.int32) * LANES
    tab = _trig_table(blk, jnp.zeros_like(blk), col % seq, (col // seq) * (seq // 4), seq)[0]
    g = jnp.arange(FNET_GDIM, dtype=jnp.int32)
    cc, sc = _phase_cos_sin(g[:, None] * g[None, :], FNET_GDIM)
    return tab, cc.astype(BF16), sc.astype(BF16)


def _fnet_kernel(u_ref, gb_ref, tl_ref, cc_ref, sc_ref, o_ref, t_ref, *, seq, tl, sb):
    i = pl.program_id(1)

    @pl.when(i == 0)
    def _():
        for s in range(sb):
            for g in range(FNET_GROUPS):
                ug = u_ref[s * seq:(s + 1) * seq, g * FNET_GDIM:(g + 1) * FNET_GDIM]
                lanes = slice(s * FNET_W + g * FNET_GDIM, s * FNET_W + (g + 1) * FNET_GDIM)
                t_ref[0:seq, lanes] = _dot(ug, cc_ref[...]).astype(BF16)
                t_ref[seq:2 * seq, lanes] = _dot(ug, sc_ref[...]).astype(BF16)

    y = _dot(tl_ref[...], t_ref[...]) * ((seq * FNET_GDIM) ** -0.5)
    for s in range(sb):
        rows = slice(s * tl, (s + 1) * tl)
        o_ref[rows, :] = (y[:, s * FNET_W:(s + 1) * FNET_W] * _silu(gb_ref[rows, :].astype(F32))).astype(BF16)


def _fnet(proj, tables, seq):
    m = proj.shape[0]
    tab_l, cc, sc = tables
    tl = min(512, seq)
    nt = seq // tl
    sb = min(4, m // seq) if nt == 1 else 1
    return pl.pallas_call(
        functools.partial(_fnet_kernel, seq=seq, tl=tl, sb=sb),
        grid=(m // (sb * seq), nt),
        in_specs=[
            pl.BlockSpec((sb * seq, FNET_W), lambda b, i: (b, OFF_UB // FNET_W)),
            pl.BlockSpec((sb * tl, FNET_W), lambda b, i: (b * nt + i, OFF_GB // FNET_W)),
            pl.BlockSpec((tl, 2 * seq), lambda b, i: (i, 0)),
            pl.BlockSpec((FNET_GDIM, FNET_GDIM), lambda b, i: (0, 0)),
            pl.BlockSpec((FNET_GDIM, FNET_GDIM), lambda b, i: (0, 0)),
        ],
        out_specs=pl.BlockSpec((sb * tl, FNET_W), lambda b, i: (b * nt + i, 0)),
        out_shape=jax.ShapeDtypeStruct((m, FNET_W), BF16),
        scratch_shapes=[pltpu.VMEM((2 * seq, sb * FNET_W), BF16)],
        compiler_params=_params(("arbitrary", "arbitrary"), 24), name="fnet",
    )(proj, proj, tab_l, cc, sc)


def _hyena_tiling(seq):
    if seq > 1024:
        return min(FREQ_CHUNK, seq), HY_W // 2, 2
    return min(FREQ_CHUNK, seq), HY_W, 4


def _hyena_tables(seq):
    fc = _hyena_tiling(seq)[0]
    nch = seq // fc
    period = 2 * seq
    pos = jnp.arange(seq, dtype=jnp.int32)
    zeros = jnp.zeros_like(pos)
    blk = jnp.arange(2 * seq // LANES, dtype=jnp.int32) * LANES
    k0 = blk % (2 * fc)
    blk_freq = (blk // (2 * fc)) * fc + k0 % fc
    blk_off = (k0 // fc) * (period // 4)
    nyquist = (1 - 2 * (pos % 2)).astype(F32)[None, :]
    fwd = _trig_table(blk_freq, blk_off, pos[None, :], zeros[None, :], period, patch=(fc, nyquist))
    k = jnp.arange(2 * fc, dtype=jnp.int32)
    f = jnp.arange(nch, dtype=jnp.int32)[:, None] * fc + (k % fc)[None, :]
    is_sec = jnp.broadcast_to((k >= fc)[None, :], f.shape)
    nyq = is_sec & (f == 0)
    row_blk = jnp.arange(seq // LANES, dtype=jnp.int32) * LANES
    inv = _trig_table(row_blk, jnp.zeros_like(row_blk), jnp.where(nyq, seq, f),
                      jnp.where(is_sec & ~nyq, period // 4, 0), period)
    return fwd.reshape(nch, 2 * fc, seq), inv


def _filter_features(seq):
    t = jnp.linspace(0.0, 1.0, seq, dtype=F32)[:, None]
    bands = (FILTER_EMB - 1) // 2
    w = (2.0 * math.pi / seq) * jnp.arange(seq, dtype=F32)[:, None]
    f = jnp.linspace(1e-4, bands - 1, bands, dtype=F32)[None, :]
    z = jnp.concatenate([t, jnp.cos(w * f), -jnp.sin(w * f)], axis=-1)
    z = jnp.pad(z, ((0, 0), (0, 128 - FILTER_EMB)))
    deltas = jnp.abs(jnp.linspace(MIN_DECAY, MAX_DECAY, HY_W, dtype=F32))[None, :]
    return z, t, deltas


def _dot_f32(a, b):
    return jnp.dot(a, b, preferred_element_type=F32, precision=lax.Precision.HIGHEST)


def _split_bf16(x):
    hi = x.astype(BF16)
    return hi, (x - hi.astype(F32)).astype(BF16)


def _dot_3pass(a, b_hi, b_lo):
    a_hi, a_lo = _split_bf16(a)
    return _dot(a_hi, b_hi) + (_dot(a_hi, b_lo) + _dot(a_lo, b_hi))


def _filter_kernel(z_ref, t_ref, dl_ref, w1_ref, b1_ref, fr_ref, w2_ref, b2_ref, w3_ref,
                   fwd_ref, o_ref, h_ref, sd_ref, nyq_ref, *, seq, fc, rows):
    j = pl.program_id(0)
    nblk = 2 * HYENA_ORDER
    nrc = seq // rows

    @pl.when(j == 0)
    def _():
        fr = fr_ref[...]
        w3_hi, w3_lo = _split_bf16(w3_ref[...])

        def taps(r, acc):
            sl = pl.ds(pl.multiple_of(r * rows, rows), rows)
            h = jnp.sin(fr * (_dot_f32(z_ref[sl, :], w1_ref[...]) + b1_ref[...]))
            h = jnp.sin(fr * (_dot_f32(h, w2_ref[...]) + b2_ref[...]))
            h = _dot_3pass(h, w3_hi, w3_lo)
            decay = jnp.exp(-t_ref[sl, :] * dl_ref[...])
            h = h * jnp.concatenate([decay] * nblk, axis=-1)
            h_ref[sl, :] = h
            return acc + jnp.sum(jnp.abs(h), axis=0, keepdims=True)

        tot = lax.fori_loop(0, nrc, taps, jnp.zeros((1, nblk * HY_W), F32))
        inv = 1.0 / (tot + EPS)
        sgn = (1 - 2 * (lax.broadcasted_iota(jnp.int32, (rows, 1), 0) % 2)).astype(F32)

        def fold(r, acc):
            sl = pl.ds(pl.multiple_of(r * rows, rows), rows)
            h = h_ref[sl, :] * inv
            pos = r * rows + lax.broadcasted_iota(jnp.int32, (rows, 1), 0)
            sums = []
            for o in range(HYENA_ORDER):
                fw = h[:, (2 * o) * HY_W:(2 * o + 1) * HY_W]
                bw = jnp.where(pos == 0, 0.0, h[:, (2 * o + 1) * HY_W:(2 * o + 2) * HY_W])
                sums.append(fw + bw)
                sd_ref[0, sl, o * HY_W:(o + 1) * HY_W] = (fw + bw).astype(BF16)
                sd_ref[1, sl, o * HY_W:(o + 1) * HY_W] = (fw - bw).astype(BF16)
            return acc + jnp.sum(jnp.concatenate(sums, axis=-1) * sgn, axis=0, keepdims=True)

        nyq_ref[...] = lax.fori_loop(0, nrc, fold, jnp.zeros((1, HYENA_ORDER * HY_W), F32))

    re = _dot(fwd_ref[0:fc, :], sd_ref[0])
    sec = _dot(fwd_ref[fc:2 * fc, :], sd_ref[1])
    first = (lax.broadcasted_iota(jnp.int32, (fc, 1), 0) == 0) & (j == 0)
    sec = jnp.where(first, nyq_ref[...], sec)
    weight = jnp.where(first, 0.5 / seq, 1.0 / seq)
    o_ref[0:fc, :] = re * weight
    o_ref[fc:2 * fc, :] = sec * weight


def _filter(feats, fwd, w1p, b1, fr, w2, b2, w3, l, seq):
    z, t, deltas = feats
    nch, fc2, _ = fwd.shape
    fc = fc2 // 2
    width = HYENA_ORDER * HY_W
    rows = min(256, seq)
    full = lambda a: pl.BlockSpec(a.shape, lambda j: (0,) * a.ndim)
    lay = lambda a: pl.BlockSpec((None,) + a.shape[1:], lambda j: (l,) + (0,) * (a.ndim - 1))
    return pl.pallas_call(
        functools.partial(_filter_kernel, seq=seq, fc=fc, rows=rows),
        grid=(nch,),
        in_specs=[full(z), full(t), full(deltas), lay(w1p), lay(b1), lay(fr), lay(w2), lay(b2), lay(w3),
                  pl.BlockSpec((None, 2 * fc, seq), lambda j: (j, 0, 0))],
        out_specs=pl.BlockSpec((None, 2 * fc, width), lambda j: (j, 0, 0)),
        out_shape=jax.ShapeDtypeStruct((nch, 2 * fc, width), F32),
        scratch_shapes=[pltpu.VMEM((seq, 2 * width), F32),
                        pltpu.VMEM((2, seq, width), BF16),
                        pltpu.VMEM((1, width), F32)],
        compiler_params=_params(("arbitrary",), 48), name="hyena_filter",
    )(z, t, deltas, w1p, b1, fr, w2, b2, w3, fwd)


def _hyena_kernel(v_ref, x1_ref, x2_ref, gc_ref, cw_ref, cb_ref, hb_ref, fwd_ref, inv_ref, kf_ref,
                  o_ref, z_ref, zb_ref, y_ref, *, seq, fc, sb, tc):
    o = pl.program_id(2)
    j = pl.program_id(3)
    last_j = pl.num_programs(3) - 1
    pos = lax.broadcasted_iota(jnp.int32, (seq, 1), 0)

    def short_conv(ref, k, s):
        x = ref[s * seq:(s + 1) * seq, :].astype(F32)
        prev = jnp.where(pos == 0, 0.0, pltpu.roll(x, 1, 0))
        nxt = jnp.where(pos == seq - 1, 0.0, pltpu.roll(x, seq - 1, 0))
        return (prev * cw_ref[0, k:k + 1, :] + x * cw_ref[1, k:k + 1, :]
                + nxt * cw_ref[2, k:k + 1, :] + cb_ref[k:k + 1, :])

    @pl.when((o == 0) & (j == 0))
    def _():
        for s in range(sb):
            lanes = slice(s * tc, (s + 1) * tc)
            z = short_conv(v_ref, 0, s)
            z_ref[:, lanes] = z
            zb_ref[:, lanes] = z.astype(BF16)
        y_ref[...] = jnp.zeros_like(y_ref)

    @pl.when((o == 1) & (j == 0))
    def _():
        for s in range(sb):
            lanes = slice(s * tc, (s + 1) * tc)
            z = short_conv(x1_ref, 1, s) * (y_ref[:, lanes] + hb_ref[0:1, :] * z_ref[:, lanes])
            z_ref[:, lanes] = z
            zb_ref[:, lanes] = z.astype(BF16)
        y_ref[...] = jnp.zeros_like(y_ref)

    zf = _dot(fwd_ref[...], zb_ref[...])
    kr, ks = kf_ref[0:fc, :], kf_ref[fc:2 * fc, :]
    nyq = (lax.broadcasted_iota(jnp.int32, (fc, 1), 0) == 0) & (j == 0)
    parts = []
    for s in range(sb):
        lanes = slice(s * tc, (s + 1) * tc)
        zr, zs = zf[0:fc, lanes], zf[fc:2 * fc, lanes]
        ss = zs * ks
        yr = zr * kr - jnp.where(nyq, 0.0, ss)
        ys = jnp.where(nyq, ss, zr * ks + zs * kr)
        parts.append(jnp.concatenate([yr, ys], axis=0).astype(BF16))
    y_ref[...] += _dot(inv_ref[...], jnp.concatenate(parts, axis=-1))

    @pl.when((o == 1) & (j == last_j))
    def _():
        for s in range(sb):
            lanes = slice(s * tc, (s + 1) * tc)
            z = short_conv(x2_ref, 2, s) * (y_ref[:, lanes] + hb_ref[1:2, :] * z_ref[:, lanes])
            o_ref[s * seq:(s + 1) * seq, :] = (z * _silu(gc_ref[s * seq:(s + 1) * seq, :].astype(F32))).astype(BF16)


def _hyena(proj, cw, cb, hb, fwd, inv, kf, seq):
    m = proj.shape[0]
    nch, fc2, _ = fwd.shape
    fc = fc2 // 2
    _, tc, sb = _hyena_tiling(seq)
    sb = min(sb, m // seq)
    nct = HY_W // tc
    hc0 = OFF_HC // tc
    col = lambda k: pl.BlockSpec((sb * seq, tc), lambda b, c, o, j: (b, hc0 + k * nct + c))
    return pl.pallas_call(
        functools.partial(_hyena_kernel, seq=seq, fc=fc, sb=sb, tc=tc),
        grid=(m // (sb * seq), nct, HYENA_ORDER, nch),
        in_specs=[
            col(0), col(1), col(2),
            pl.BlockSpec((sb * seq, tc), lambda b, c, o, j: (b, OFF_GC // tc + c)),
            pl.BlockSpec((3, 3, tc), lambda b, c, o, j: (0, 0, c)),
            pl.BlockSpec((3, tc), lambda b, c, o, j: (0, c)),
            pl.BlockSpec((HYENA_ORDER, tc), lambda b, c, o, j: (0, c)),
            pl.BlockSpec((None, 2 * fc, seq), lambda b, c, o, j: (j, 0, 0)),
            pl.BlockSpec((None, seq, 2 * fc), lambda b, c, o, j: (j, 0, 0)),
            pl.BlockSpec((None, 2 * fc, tc), lambda b, c, o, j: (j, 0, o * nct + c)),
        ],
        out_specs=pl.BlockSpec((sb * seq, tc), lambda b, c, o, j: (b, c)),
        out_shape=jax.ShapeDtypeStruct((m, HY_W), BF16),
        scratch_shapes=[pltpu.VMEM((seq, sb * tc), F32), pltpu.VMEM((seq, sb * tc), BF16),
                        pltpu.VMEM((seq, sb * tc), F32)],
        compiler_params=_params(("arbitrary",) * 4, 56 if seq > 1024 else 28), name="hyena",
    )(proj, proj, proj, proj, cw, cb, hb, fwd, inv, kf)


def _merge_kernel(ya_ref, yb_ref, yc_ref, ga_ref, gb_ref, gc_ref, wa_ref, wb_ref, wc_ref, o_ref):
    acc = ga_ref[...].astype(F32) * _dot(ya_ref[...], wa_ref[...])
    acc += gb_ref[...].astype(F32) * _dot(yb_ref[...], wb_ref[...])
    acc += gc_ref[...].astype(F32) * _dot(yc_ref[...], wc_ref[...])
    o_ref[...] = acc.astype(BF16)


def _merge(ya, yb, yc, gates, w_bf):
    m = ya.shape[0]
    tm = min(ROW_TILE, m)
    tn = COL_TILE
    d = D_MODEL
    row = lambda w: pl.BlockSpec((tm, w), lambda i, j: (i, 0))
    gate = lambda k: pl.BlockSpec((tm, tn), lambda i, j: (i, k * d // tn + j))
    return pl.pallas_call(
        _merge_kernel, grid=(m // tm, d // tn),
        in_specs=[row(ATT_W), row(FNET_W), row(HY_W), gate(0), gate(1), gate(2),
                  pl.BlockSpec((ATT_W, tn), lambda i, j: (0, j)),
                  pl.BlockSpec((FNET_W, tn), lambda i, j: (ATT_W // FNET_W, j)),
                  pl.BlockSpec((HY_W, tn), lambda i, j: ((ATT_W + FNET_W) // HY_W, j))],
        out_specs=pl.BlockSpec((tm, tn), lambda i, j: (i, j)),
        out_shape=jax.ShapeDtypeStruct((m, d), BF16),
        compiler_params=_params(("arbitrary", "arbitrary"), 36), name="merge",
    )(ya, yb, yc, gates, gates, gates, w_bf, w_bf, w_bf)


def _outproj_kernel(mg_ref, w_ref, x_ref, ada_ref, o_ref):
    o_ref[...] = x_ref[...] + ada_ref[...] * _dot(mg_ref[...], w_ref[...])


def _outproj(merged, x2d, ada4, ada_row, w_bf, l):
    m, d = x2d.shape
    tm = min(ROW_TILE, m)
    tn = COL_TILE
    return pl.pallas_call(
        _outproj_kernel, grid=(m // tm, d // tn),
        in_specs=[pl.BlockSpec((tm, d), lambda i, j: (i, 0)),
                  pl.BlockSpec((d, tn), lambda i, j: (0, j)),
                  pl.BlockSpec((tm, tn), lambda i, j: (i, j)),
                  pl.BlockSpec((None, None, 1, tn),
                               lambda i, j: (l, ada_row(i * tm), 0, 2 * d // tn + j))],
        out_specs=pl.BlockSpec((tm, tn), lambda i, j: (i, j)),
        out_shape=jax.ShapeDtypeStruct((m, d), F32),
        compiler_params=_params(("arbitrary", "arbitrary"), 36), name="outproj",
    )(merged, w_bf, x2d, ada4)


def kernel(x_prompt, x_sample, cache_k, cache_v, c, c_ctx, norm_g, w_ada, b_ada, w_in, q_norm_g, k_norm_g, rpb, conv_w, conv_b, f_w1, f_b1, f_freq, f_w2, f_b2, f_w3, hy_bias, w_br, w_out):
    batch, seq, d = x_prompt.shape
    dec_batch, dec_seq, _ = x_sample.shape
    depth = norm_g.shape[0]

    ada4 = _ada(jnp.concatenate([c_ctx[None, :], c], axis=0), w_ada, b_ada)
    ada4 = ada4.reshape(depth, ADA_ROWS, 1, 3 * d)
    ctx_row = lambda tok: 0
    lat_row = lambda tok: 1 + tok // dec_seq

    w_in_bf = w_in[0].astype(BF16)
    norm_g3 = norm_g.reshape(depth, 1, d)
    qg3 = q_norm_g.reshape(depth, 1, HEAD_DIM)
    kg3 = k_norm_g.reshape(depth, 1, HEAD_DIM)
    w1p = jnp.pad(f_w1, ((0, 0), (0, 128 - FILTER_EMB), (0, 0)))
    b1 = f_b1.reshape(depth, 1, FILTER_FF)
    fr = f_freq.reshape(depth, 1, FILTER_FF)
    b2 = f_b2.reshape(depth, 1, FILTER_FF)
    win_bias = _window_bias(rpb)

    groups = []
    for x, s, row in ((x_prompt, seq, ctx_row), (x_sample, dec_seq, lat_row)):
        fwd, inv = _hyena_tables(s)
        groups.append(dict(x=x.reshape(-1, d), seq=s, row=row, fnet=_fnet_tables(s),
                           fwd=fwd, inv=inv, feats=_filter_features(s)))

    new_kv = ()
    for l in range(depth):
        cw = conv_w[l].reshape(3, 3, HY_W)
        cb = conv_b[l].reshape(3, HY_W)
        hb = hy_bias[l]
        for gi, g in enumerate(groups):
            s = g["seq"]
            if gi == 0:
                jobs = [(w_br, l), (w_out, l)] + ([(w_in, l + 1)] if l + 1 < depth else [])
                proj, gates, w_br_bf, w_out_bf, *w_in_next = _inproj(
                    g["x"], norm_g3, ada4, g["row"], w_in_bf, l, jobs)
            else:
                proj, gates = _inproj(g["x"], norm_g3, ada4, g["row"], w_in_bf, l)
            if gi == 0:
                ya, *new_kv = _ctx_attn(proj, qg3, kg3, new_kv, l, s, depth)
            else:
                qn, kn = _prep_lat(proj, qg3, kg3, l)
                ya = _lat_attn(qn, kn, cache_k, cache_v, win_bias, proj, l, s)
            yb = _fnet(proj, g["fnet"], s)
            kf = _filter(g["feats"], g["fwd"], w1p, b1, fr, f_w2, b2, f_w3, l, s)
            yc = _hyena(proj, cw, cb, hb, g["fwd"], g["inv"], kf, s)
            merged = _merge(ya, yb, yc, gates, w_br_bf)
            g["x"] = _outproj(merged, g["x"], ada4, g["row"], w_out_bf, l)
        if w_in_next:
            (w_in_bf,) = w_in_next

    nk, nv = new_kv
    return (groups[0]["x"].reshape(batch, seq, d),
            groups[1]["x"].reshape(dec_batch, dec_seq, d),
            nk.reshape(batch, depth, seq, N_HEADS, HEAD_DIM),
            nv.reshape(batch, depth, seq, N_HEADS, HEAD_DIM))
```

```python
import functools
import math

import jax
import jax.numpy as jnp
import numpy as np
from jax import lax
from jax.experimental import pallas as pl
from jax.experimental.pallas import tpu as pltpu

F32 = jnp.float32
BF16 = jnp.bfloat16

D_MODEL = 2048
GRID_W = 64
N_HEADS = 8
HEAD_DIM = 128
ATT_W = N_HEADS * HEAD_DIM
MAX_KH = 8
KW = 16
FNET_GROUPS = 4
FNET_GDIM = 128
FNET_W = FNET_GROUPS * FNET_GDIM
HY_W = 512
HYENA_ORDER = 2
FILTER_EMB = 33
FILTER_FF = 64
MIN_DECAY = math.log(1e-2) / 1.5
MAX_DECAY = math.log(1e-2) / 0.3
OFF_Q = 0
OFF_K = OFF_Q + ATT_W
OFF_V = OFF_K + ATT_W
OFF_GA = OFF_V + ATT_W
OFF_UB = OFF_GA + ATT_W
OFF_GB = OFF_UB + FNET_W
OFF_HC = OFF_GB + FNET_W
OFF_GC = OFF_HC + 3 * HY_W
OFF_MG = OFF_GC + HY_W
N_IN = OFF_MG + 3 * D_MODEL
EPS = 1e-6
NEG = -1e30
ATT_SCALE = HEAD_DIM ** -0.5
LOG2_E = math.log2(math.e)

LANES = 128
SUBLANES = 8
ADA_ROWS = 8
FREQ_CHUNK = 512
LAT_ROW_BLOCK = 4
ROW_TILE = 1024
COL_TILE = 1024
MIB = 1024 * 1024


def _params(semantics, vmem_mib):
    return pltpu.CompilerParams(dimension_semantics=semantics,
                                vmem_limit_bytes=vmem_mib * MIB)


def _sigmoid(x):
    return 0.5 * jnp.tanh(0.5 * x) + 0.5


def _silu(x):
    return x * _sigmoid(x)


def _dot(a, b):
    return jnp.dot(a, b, preferred_element_type=F32)


def _dot_nt(a, b):
    return lax.dot_general(a, b, (((1,), (1,)), ((), ())), preferred_element_type=F32)


def _ada_kernel(cv_ref, w_ref, b_ref, o_ref, s_ref):
    n_rows, d, _ = cv_ref.shape
    tn = w_ref.shape[1]

    @pl.when((pl.program_id(0) == 0) & (pl.program_id(1) == 0))
    def _():
        s_ref[...] = _silu(cv_ref[...])

    def body(i, acc):
        rows = pl.ds(pl.multiple_of(i * SUBLANES, SUBLANES), SUBLANES)
        w = w_ref[rows, :]
        return tuple(acc[r] + w * jnp.concatenate([s_ref[r, rows, :]] * (tn // LANES), axis=-1)
                     for r in range(n_rows))

    zero = jnp.zeros((SUBLANES, tn), F32)
    acc = lax.fori_loop(0, d // SUBLANES, body, (zero,) * n_rows, unroll=4)
    o_ref[...] = jnp.zeros(o_ref.shape, F32)
    for r in range(n_rows):
        o_ref[r:r + 1, :] = jnp.sum(acc[r], axis=0, keepdims=True) + b_ref[...]


def _ada(conds, w_ada, b_ada):
    depth, d, n = w_ada.shape
    rows = conds.shape[0]
    assert rows <= ADA_ROWS
    tn = COL_TILE
    cv = jnp.broadcast_to(conds[:, :, None], (rows, d, LANES))
    return pl.pallas_call(
        _ada_kernel,
        grid=(depth, n // tn),
        in_specs=[
            pl.BlockSpec((rows, d, LANES), lambda l, j: (0, 0, 0)),
            pl.BlockSpec((None, d, tn), lambda l, j: (l, 0, j)),
            pl.BlockSpec((None, 1, tn), lambda l, j: (l, 0, j)),
        ],
        out_specs=pl.BlockSpec((None, ADA_ROWS, tn), lambda l, j: (l, 0, j)),
        out_shape=jax.ShapeDtypeStruct((depth, ADA_ROWS, n), F32),
        scratch_shapes=[pltpu.VMEM((rows, d, LANES), F32)],
        compiler_params=_params(("arbitrary", "arbitrary"), 24),
        name="ada",
    )(cv, w_ada, b_ada.reshape(depth, 1, n))


def _cast_job_specs(jobs, n_i, n_j):
    args, in_specs, out_specs, out_shapes = [], [], [], []
    for w32, layer in jobs:
        _, rows, cols = w32.shape
        rb = rows // n_i
        n_ct = max(t for t in range(1, n_j + 1) if cols % (t * LANES) == 0)
        tw = cols // n_ct
        args.append(w32)
        in_specs.append(pl.BlockSpec((None, rb, tw),
                                     lambda i, j, layer=layer, n_ct=n_ct: (layer, i, jnp.minimum(j, n_ct - 1))))
        out_specs.append(pl.BlockSpec((rb, tw), lambda i, j, n_ct=n_ct: (i, jnp.minimum(j, n_ct - 1))))
        out_shapes.append(jax.ShapeDtypeStruct((rows, cols), BF16))
    return args, in_specs, out_specs, out_shapes


def _inproj_kernel(x_ref, g_ref, ada_ref, w_ref, *refs, rows, n_plain, n_jobs):
    o_ref, gate_ref = refs[n_jobs:n_jobs + 2]
    h_ref = refs[-1]
    for src_ref, dst_ref in zip(refs[:n_jobs], refs[n_jobs + 2:-1]):
        dst_ref[...] = src_ref[...].astype(BF16)
    j = pl.program_id(1)

    @pl.when(j == 0)
    def _():
        g = g_ref[...]
        shift = ada_ref[:, 0:D_MODEL]
        scale1 = 1.0 + ada_ref[:, D_MODEL:2 * D_MODEL]

        def body(r, carry):
            sl = pl.ds(pl.multiple_of(r * rows, rows), rows)
            x = x_ref[sl, :]
            ms = jnp.mean(x * x, axis=-1, keepdims=True)
            y = x * lax.rsqrt(ms + EPS) * g
            h_ref[sl, :] = (y * scale1 + shift).astype(BF16)
            return carry

        lax.fori_loop(0, x_ref.shape[0] // rows, body, 0, unroll=8)

    @pl.when(j < n_plain)
    def _():
        o_ref[...] = _dot(h_ref[...], w_ref[...]).astype(o_ref.dtype)

    @pl.when(j >= n_plain)
    def _():
        gate_ref[...] = _sigmoid(_dot(h_ref[...], w_ref[...])).astype(BF16)


def _inproj(x2d, norm_g3, ada4, ada_row, w_bf, l, cast_jobs=()):
    m, d = x2d.shape
    tm = min(ROW_TILE, m)
    tn = COL_TILE
    n_plain = OFF_MG // tn
    nxt_args, nxt_in, nxt_out, nxt_shape = _cast_job_specs(cast_jobs, m // tm, N_IN // tn)
    return pl.pallas_call(
        functools.partial(_inproj_kernel, rows=16, n_plain=n_plain, n_jobs=len(cast_jobs)),
        grid=(m // tm, N_IN // tn),
        in_specs=[
            pl.BlockSpec((tm, d), lambda i, j: (i, 0)),
            pl.BlockSpec((None, 1, d), lambda i, j: (l, 0, 0)),
            pl.BlockSpec((None, None, 1, 3 * d), lambda i, j: (l, ada_row(i * tm), 0, 0)),
            pl.BlockSpec((d, tn), lambda i, j: (0, j)),
        ] + nxt_in,
        out_specs=[pl.BlockSpec((tm, tn), lambda i, j: (i, jnp.minimum(j, n_plain - 1))),
                   pl.BlockSpec((tm, tn), lambda i, j: (i, jnp.maximum(j - n_plain, 0)))] + nxt_out,
        out_shape=[jax.ShapeDtypeStruct((m, OFF_MG), BF16),
                   jax.ShapeDtypeStruct((m, N_IN - OFF_MG), BF16)] + nxt_shape,
        scratch_shapes=[pltpu.VMEM((tm, d), BF16)],
        compiler_params=_params(("arbitrary", "arbitrary"), 48),
        name="inproj",
    )(x2d, norm_g3, ada4, w_bf, *nxt_args)


def _head_norm(x, g):
    ms = jnp.mean(x * x, axis=-1, keepdims=True)
    return x * lax.rsqrt(ms + EPS) * g


def _prep_lat_kernel(q_ref, k_ref, qg_ref, kg_ref, qn_ref, kn_ref):
    for h in range(N_HEADS):
        sl = slice(h * HEAD_DIM, (h + 1) * HEAD_DIM)
        qn_ref[:, sl] = _head_norm(q_ref[:, sl].astype(F32), qg_ref[...]).astype(BF16)
        kn_ref[:, sl] = _head_norm(k_ref[:, sl].astype(F32), kg_ref[...]).astype(BF16)


def _prep_lat(proj, qg3, kg3, l):
    m = proj.shape[0]
    tm = min(256, m)
    col = lambda c: pl.BlockSpec((tm, ATT_W), lambda i: (i, c))
    gspec = pl.BlockSpec((None, 1, HEAD_DIM), lambda i: (l, 0, 0))
    return pl.pallas_call(
        _prep_lat_kernel, grid=(m // tm,),
        in_specs=[col(OFF_Q // ATT_W), col(OFF_K // ATT_W), gspec, gspec],
        out_specs=[col(0)] * 2, out_shape=[jax.ShapeDtypeStruct((m, ATT_W), BF16)] * 2,
        compiler_params=_params(("arbitrary",), 8), name="prep_lat",
    )(proj, proj, qg3, kg3)


def _ctx_attn_kernel(q_ref, k_ref, v_ref, ga_ref, qg_ref, kg_ref, *refs, layer):
    o_ref, nk_ref, nv_ref = refs[-3:]
    if len(refs) == 3:
        for other in range(nk_ref.shape[0]):
            if other != layer:
                nk_ref[other] = jnp.zeros(nk_ref.shape[1:], F32)
                nv_ref[other] = jnp.zeros(nv_ref.shape[1:], F32)
        nk_ref, nv_ref = nk_ref.at[layer], nv_ref.at[layer]
    for h in range(N_HEADS):
        sl = slice(h * HEAD_DIM, (h + 1) * HEAD_DIM)
        q = _head_norm(q_ref[:, sl].astype(F32), qg_ref[...])
        k = _head_norm(k_ref[:, sl].astype(F32), kg_ref[...])
        v = v_ref[:, sl]
        nk_ref[:, sl] = k
        nv_ref[:, sl] = v.astype(F32)
        s = _dot_nt(q.astype(BF16), k.astype(BF16)) * (ATT_SCALE * LOG2_E)
        p = jnp.exp2(s - jnp.max(s, axis=-1, keepdims=True))
        inv = 1.0 / jnp.sum(p, axis=-1, keepdims=True)
        o = _dot(p.astype(BF16), v) * inv
        o_ref[:, sl] = (o * _silu(ga_ref[:, sl].astype(F32))).astype(BF16)


def _ctx_attn(proj, qg3, kg3, new_kv, l, seq, depth):
    m = proj.shape[0]
    kv_shape = jax.ShapeDtypeStruct((m // seq, depth, seq, ATT_W), F32)
    col = lambda c: pl.BlockSpec((seq, ATT_W), lambda b: (b, c))
    gspec = pl.BlockSpec((None, 1, HEAD_DIM), lambda b: (l, 0, 0))
    if new_kv:
        kv_spec = pl.BlockSpec((None, None, seq, ATT_W), lambda b: (b, l, 0, 0))
    else:
        kv_spec = pl.BlockSpec((None, depth, seq, ATT_W), lambda b: (b, 0, 0, 0))
    return pl.pallas_call(
        functools.partial(_ctx_attn_kernel, layer=l), grid=(m // seq,),
        in_specs=[col(OFF_Q // ATT_W), col(OFF_K // ATT_W), col(OFF_V // ATT_W), col(OFF_GA // ATT_W),
                  gspec, gspec] + [pl.BlockSpec(memory_space=pl.ANY)] * len(new_kv),
        out_specs=[col(0), kv_spec, kv_spec],
        out_shape=[jax.ShapeDtypeStruct((m, ATT_W), BF16), kv_shape, kv_shape],
        input_output_aliases={6: 1, 7: 2} if new_kv else {},
        compiler_params=_params(("arbitrary",), 16), name="ctx_attn",
    )(proj, proj, proj, proj, qg3, kg3, *new_kv)


def _window_bias(rpb):
    depth, heads, n_dr, n_dc = rpb.shape
    n_e = n_dr + 1
    c = np.arange(GRID_W)[:, None, None]
    p = np.arange(2)[None, :, None]
    kc = np.arange(GRID_W)[None, None, :]
    cs = np.clip(c - KW // 2, 0, GRID_W - KW)
    in_cols = np.broadcast_to((kc >= cs) & (kc < cs + KW), (GRID_W, 2, GRID_W)).reshape(-1)
    dc = np.broadcast_to(np.clip(kc - c + KW - 1, 0, n_dc - 1), (GRID_W, 2, GRID_W)).reshape(-1)
    member = np.broadcast_to(p, (GRID_W, 2, GRID_W)).reshape(-1)
    row_member = np.repeat(np.arange(2), n_dc)[:, None]
    row_dc = np.tile(np.arange(n_dc), 2)[:, None]
    onehot = ((jnp.asarray(row_member) == jnp.asarray(member)[None, :])
              & (jnp.asarray(row_dc) == jnp.asarray(dc)[None, :])).astype(F32)
    padded = jnp.pad(rpb, ((0, 0), (0, 0), (1, 1), (0, 0)))
    pairs = jnp.concatenate([padded[:, :, :-1], padded[:, :, 1:]], axis=-1)
    cb = jnp.einsum("lhek,kx->lhex", pairs, onehot, precision=lax.Precision.HIGHEST)
    row_ok = (np.arange(n_e)[:, None] + member[None, :] >= 1) & (np.arange(n_e)[:, None] + member[None, :] <= n_dr)
    cb = jnp.where(jnp.asarray(row_ok & in_cols[None, :]), cb * LOG2_E, NEG)
    return cb.reshape(depth, heads, n_e, GRID_W, 2 * GRID_W)


def _lat_attn_kernel(q_ref, k_ref, v_ref, ck_ref, cv_ref, bias_ref, ga_ref, o_ref,
                     ckb_ref, cvb_ref, *, rows, kh, rb, kr):
    g = pl.program_id(1)

    @pl.when(g == 0)
    def _():
        for h in range(N_HEADS):
            sl = slice(h * HEAD_DIM, (h + 1) * HEAD_DIM)
            ckb_ref[:, sl] = ck_ref[:, h, :].astype(BF16)
            cvb_ref[:, sl] = cv_ref[:, h, :].astype(BF16)

    r0 = g * rb
    ws = jnp.clip(r0 - kh // 2, 0, rows - kr)
    span = pl.ds(pl.multiple_of(ws * GRID_W, GRID_W), kr * GRID_W)
    q_row = r0 + lax.broadcasted_iota(jnp.int32, (rb * GRID_W, 1), 0) // GRID_W
    k_row = ws + lax.broadcasted_iota(jnp.int32, (1, kr * GRID_W), 1) // GRID_W
    q_rs = jnp.clip(q_row - kh // 2, 0, rows - kh)
    in_window = (k_row >= q_rs) & (k_row < q_rs + kh)
    n_pairs = bias_ref.shape[1]
    for h in range(N_HEADS):
        sl = slice(h * HEAD_DIM, (h + 1) * HEAD_DIM)
        q = q_ref[:, sl]
        bias = jnp.concatenate(
            [jnp.concatenate(
                [bias_ref[h, jnp.clip(ws + 2 * j - (r0 + qi) + MAX_KH, 0, n_pairs - 1)]
                 for j in range(kr // 2)], axis=-1)
             for qi in range(rb)], axis=0)
        s_win = jnp.where(in_window, _dot_nt(q, k_ref[span, sl]) * (ATT_SCALE * LOG2_E) + bias, NEG)
        s_ctx = _dot_nt(q, ckb_ref[:, sl]) * (ATT_SCALE * LOG2_E)
        mx = jnp.maximum(jnp.max(s_win, axis=-1, keepdims=True),
                         jnp.max(s_ctx, axis=-1, keepdims=True))
        p_win = jnp.exp2(s_win - mx)
        p_ctx = jnp.exp2(s_ctx - mx)
        inv = 1.0 / (jnp.sum(p_win, axis=-1, keepdims=True) + jnp.sum(p_ctx, axis=-1, keepdims=True))
        o = (_dot(p_win.astype(BF16), v_ref[span, sl]) + _dot(p_ctx.astype(BF16), cvb_ref[:, sl])) * inv
        o_ref[:, sl] = (o * _silu(ga_ref[:, sl].astype(F32))).astype(BF16)


def _lat_attn(qn, kn, cache_k, cache_v, bias, proj, l, seq):
    m = qn.shape[0]
    batch = m // seq
    rows = seq // GRID_W
    kh = min(MAX_KH, rows)
    rb = min(LAT_ROW_BLOCK, rows)
    kr = min(rows, kh + rb)
    assert rows % rb == 0 and kr % 2 == 0
    past = cache_k.shape[2]
    qspec = pl.BlockSpec((rb * GRID_W, ATT_W), lambda b, g: (b * (rows // rb) + g, 0))
    kspec = pl.BlockSpec((seq, ATT_W), lambda b, g: (b, 0))
    vspec = pl.BlockSpec((seq, ATT_W), lambda b, g: (b, OFF_V // ATT_W))
    cspec = pl.BlockSpec((None, None, past, N_HEADS, HEAD_DIM), lambda b, g: (b, l, 0, 0, 0))
    return pl.pallas_call(
        functools.partial(_lat_attn_kernel, rows=rows, kh=kh, rb=rb, kr=kr),
        grid=(batch, rows // rb),
        in_specs=[qspec, kspec, vspec, cspec, cspec,
                  pl.BlockSpec((None,) + bias.shape[1:], lambda b, g: (l, 0, 0, 0, 0)),
                  pl.BlockSpec((rb * GRID_W, ATT_W),
                               lambda b, g: (b * (rows // rb) + g, OFF_GA // ATT_W))],
        out_specs=qspec,
        out_shape=jax.ShapeDtypeStruct((m, ATT_W), BF16),
        scratch_shapes=[pltpu.VMEM((past, ATT_W), BF16), pltpu.VMEM((past, ATT_W), BF16)],
        compiler_params=_params(("arbitrary", "arbitrary"), 48), name="lat_attn",
    )(qn, kn, proj, cache_k, cache_v, bias, proj)


def _phase_cos_sin(phase, period):
    ang = (phase % period).astype(F32) * (2.0 * math.pi / period)
    return jnp.cos(ang), jnp.sin(ang)


def _trig_kernel(ca_ref, sa_ref, cb_ref, sb_ref, *refs, patch_row):
    o_ref = refs[-1]
    cb, sb = cb_ref[...], sb_ref[...]
    for a in range(ca_ref.shape[0]):
        blk = ca_ref[a:a + 1, :] * cb - sa_ref[a:a + 1, :] * sb
        if patch_row is not None:
            row = (pl.program_id(1) * ca_ref.shape[0] + a) * LANES + lax.broadcasted_iota(jnp.int32, (LANES, 1), 0)
            blk = jnp.where(row == patch_row, refs[0][...], blk)
        o_ref[a * LANES:(a + 1) * LANES, :] = blk.astype(o_ref.dtype)


def _trig_table(blk_freq, blk_off, col_pos, col_off, period, patch=None):
    n_j, n_col = col_pos.shape
    n_blk = blk_freq.shape[0]
    ca, sa = _phase_cos_sin(col_pos[:, None, :] * blk_freq[None, :, None] + blk_off[None, :, None]
                            + col_off[:, None, :], period)
    cb, sb = _phase_cos_sin(col_pos[:, None, :] * jnp.arange(LANES, dtype=jnp.int32)[None, :, None], period)
    ab = min(SUBLANES, n_blk)
    blk_spec = pl.BlockSpec((None, ab, n_col), lambda j, i: (j, i, 0))
    tab_spec = pl.BlockSpec((None, LANES, n_col), lambda j, i: (j, 0, 0))
    extra, extra_specs = ([], []) if patch is None else ([patch[1]], [pl.BlockSpec((1, n_col), lambda j, i: (0, 0))])
    return pl.pallas_call(
        functools.partial(_trig_kernel, patch_row=None if patch is None else patch[0]),
        grid=(n_j, n_blk // ab),
        in_specs=[blk_spec, blk_spec, tab_spec, tab_spec] + extra_specs,
        out_specs=pl.BlockSpec((None, ab * LANES, n_col), lambda j, i: (j, i, 0)),
        out_shape=jax.ShapeDtypeStruct((n_j, n_blk * LANES, n_col), BF16),
        compiler_params=_params(("arbitrary", "arbitrary"), 20), name="trig_table",
    )(ca, sa, cb, sb, *extra)


def _fnet_tables(seq):
    col = jnp.arange(2 * seq, dtype=jnp.int32)[None, :]
    blk = jnp.arange(seq // LANES, dtype=jnp.int32) * LANES
    tab = _trig_table(blk, jnp.zeros_like(blk), col % seq, (col // seq) * (seq // 4), seq)[0]
    g = jnp.arange(FNET_GDIM, dtype=jnp.int32)
    cc, sc = _phase_cos_sin(g[:, None] * g[None, :], FNET_GDIM)
    return tab, cc.astype(BF16), sc.astype(BF16)


def _fnet_kernel(u_ref, gb_ref, tl_ref, cc_ref, sc_ref, o_ref, t_ref, *, seq, tl, sb):
    i = pl.program_id(1)

    @pl.when(i == 0)
    def _():
        for s in range(sb):
            for g in range(FNET_GROUPS):
                ug = u_ref[s * seq:(s + 1) * seq, g * FNET_GDIM:(g + 1) * FNET_GDIM]
                lanes = slice(s * FNET_W + g * FNET_GDIM, s * FNET_W + (g + 1) * FNET_GDIM)
                t_ref[0:seq, lanes] = _dot(ug, cc_ref[...]).astype(BF16)
                t_ref[seq:2 * seq, lanes] = _dot(ug, sc_ref[...]).astype(BF16)

    y = _dot(tl_ref[...], t_ref[...]) * ((seq * FNET_GDIM) ** -0.5)
    for s in range(sb):
        rows = slice(s * tl, (s + 1) * tl)
        o_ref[rows, :] = (y[:, s * FNET_W:(s + 1) * FNET_W] * _silu(gb_ref[rows, :].astype(F32))).astype(BF16)


def _fnet(proj, tables, seq):
    m = proj.shape[0]
    tab_l, cc, sc = tables
    tl = min(512, seq)
    nt = seq // tl
    sb = min(4, m // seq) if nt == 1 else 1
    return pl.pallas_call(
        functools.partial(_fnet_kernel, seq=seq, tl=tl, sb=sb),
        grid=(m // (sb * seq), nt),
        in_specs=[
            pl.BlockSpec((sb * seq, FNET_W), lambda b, i: (b, OFF_UB // FNET_W)),
            pl.BlockSpec((sb * tl, FNET_W), lambda b, i: (b * nt + i, OFF_GB // FNET_W)),
            pl.BlockSpec((tl, 2 * seq), lambda b, i: (i, 0)),
            pl.BlockSpec((FNET_GDIM, FNET_GDIM), lambda b, i: (0, 0)),
            pl.BlockSpec((FNET_GDIM, FNET_GDIM), lambda b, i: (0, 0)),
        ],
        out_specs=pl.BlockSpec((sb * tl, FNET_W), lambda b, i: (b * nt + i, 0)),
        out_shape=jax.ShapeDtypeStruct((m, FNET_W), BF16),
        scratch_shapes=[pltpu.VMEM((2 * seq, sb * FNET_W), BF16)],
        compiler_params=_params(("arbitrary", "arbitrary"), 24), name="fnet",
    )(proj, proj, tab_l, cc, sc)


def _hyena_tiling(seq):
    if seq > 1024:
        return min(FREQ_CHUNK, seq), HY_W // 2, 2
    return min(FREQ_CHUNK, seq), HY_W, 4


def _hyena_tables(seq):
    fc = _hyena_tiling(seq)[0]
    nch = seq // fc
    period = 2 * seq
    pos = jnp.arange(seq, dtype=jnp.int32)
    zeros = jnp.zeros_like(pos)
    blk = jnp.arange(2 * seq // LANES, dtype=jnp.int32) * LANES
    k0 = blk % (2 * fc)
    blk_freq = (blk // (2 * fc)) * fc + k0 % fc
    blk_off = (k0 // fc) * (period // 4)
    nyquist = (1 - 2 * (pos % 2)).astype(F32)[None, :]
    fwd = _trig_table(blk_freq, blk_off, pos[None, :], zeros[None, :], period, patch=(fc, nyquist))
    k = jnp.arange(2 * fc, dtype=jnp.int32)
    f = jnp.arange(nch, dtype=jnp.int32)[:, None] * fc + (k % fc)[None, :]
    is_sec = jnp.broadcast_to((k >= fc)[None, :], f.shape)
    nyq = is_sec & (f == 0)
    row_blk = jnp.arange(seq // LANES, dtype=jnp.int32) * LANES
    inv = _trig_table(row_blk, jnp.zeros_like(row_blk), jnp.where(nyq, seq, f),
                      jnp.where(is_sec & ~nyq, period // 4, 0), period)
    return fwd.reshape(nch, 2 * fc, seq), inv


def _filter_features(seq):
    t = jnp.linspace(0.0, 1.0, seq, dtype=F32)[:, None]
    bands = (FILTER_EMB - 1) // 2
    w = (2.0 * math.pi / seq) * jnp.arange(seq, dtype=F32)[:, None]
    f = jnp.linspace(1e-4, bands - 1, bands, dtype=F32)[None, :]
    z = jnp.concatenate([t, jnp.cos(w * f), -jnp.sin(w * f)], axis=-1)
    z = jnp.pad(z, ((0, 0), (0, 128 - FILTER_EMB)))
    deltas = jnp.abs(jnp.linspace(MIN_DECAY, MAX_DECAY, HY_W, dtype=F32))[None, :]
    return z, t, deltas


def _dot_f32(a, b):
    return jnp.dot(a, b, preferred_element_type=F32, precision=lax.Precision.HIGHEST)


def _split_bf16(x):
    hi = x.astype(BF16)
    return hi, (x - hi.astype(F32)).astype(BF16)


def _dot_3pass(a, b_hi, b_lo):
    a_hi, a_lo = _split_bf16(a)
    return _dot(a_hi, b_hi) + (_dot(a_hi, b_lo) + _dot(a_lo, b_hi))


def _filter_kernel(z_ref, t_ref, dl_ref, w1_ref, b1_ref, fr_ref, w2_ref, b2_ref, w3_ref,
                   fwd_ref, o_ref, h_ref, sd_ref, nyq_ref, *, seq, fc, rows):
    j = pl.program_id(0)
    nblk = 2 * HYENA_ORDER
    nrc = seq // rows

    @pl.when(j == 0)
    def _():
        fr = fr_ref[...]
        w3_hi, w3_lo = _split_bf16(w3_ref[...])

        def taps(r, acc):
            sl = pl.ds(pl.multiple_of(r * rows, rows), rows)
            h = jnp.sin(fr * (_dot_f32(z_ref[sl, :], w1_ref[...]) + b1_ref[...]))
            h = jnp.sin(fr * (_dot_f32(h, w2_ref[...]) + b2_ref[...]))
            h = _dot_3pass(h, w3_hi, w3_lo)
            decay = jnp.exp(-t_ref[sl, :] * dl_ref[...])
            h = h * jnp.concatenate([decay] * nblk, axis=-1)
            h_ref[sl, :] = h
            return acc + jnp.sum(jnp.abs(h), axis=0, keepdims=True)

        tot = lax.fori_loop(0, nrc, taps, jnp.zeros((1, nblk * HY_W), F32))
        inv = 1.0 / (tot + EPS)
        sgn = (1 - 2 * (lax.broadcasted_iota(jnp.int32, (rows, 1), 0) % 2)).astype(F32)

        def fold(r, acc):
            sl = pl.ds(pl.multiple_of(r * rows, rows), rows)
            h = h_ref[sl, :] * inv
            pos = r * rows + lax.broadcasted_iota(jnp.int32, (rows, 1), 0)
            sums = []
            for o in range(HYENA_ORDER):
                fw = h[:, (2 * o) * HY_W:(2 * o + 1) * HY_W]
                bw = jnp.where(pos == 0, 0.0, h[:, (2 * o + 1) * HY_W:(2 * o + 2) * HY_W])
                sums.append(fw + bw)
                sd_ref[0, sl, o * HY_W:(o + 1) * HY_W] = (fw + bw).astype(BF16)
                sd_ref[1, sl, o * HY_W:(o + 1) * HY_W] = (fw - bw).astype(BF16)
            return acc + jnp.sum(jnp.concatenate(sums, axis=-1) * sgn, axis=0, keepdims=True)

        nyq_ref[...] = lax.fori_loop(0, nrc, fold, jnp.zeros((1, HYENA_ORDER * HY_W), F32))

    re = _dot(fwd_ref[0:fc, :], sd_ref[0])
    sec = _dot(fwd_ref[fc:2 * fc, :], sd_ref[1])
    first = (lax.broadcasted_iota(jnp.int32, (fc, 1), 0) == 0) & (j == 0)
    sec = jnp.where(first, nyq_ref[...], sec)
    weight = jnp.where(first, 0.5 / seq, 1.0 / seq)
    o_ref[0:fc, :] = re * weight
    o_ref[fc:2 * fc, :] = sec * weight


def _filter(feats, fwd, w1p, b1, fr, w2, b2, w3, l, seq):
    z, t, deltas = feats
    nch, fc2, _ = fwd.shape
    fc = fc2 // 2
    width = HYENA_ORDER * HY_W
    rows = min(256, seq)
    full = lambda a: pl.BlockSpec(a.shape, lambda j: (0,) * a.ndim)
    lay = lambda a: pl.BlockSpec((None,) + a.shape[1:], lambda j: (l,) + (0,) * (a.ndim - 1))
    return pl.pallas_call(
        functools.partial(_filter_kernel, seq=seq, fc=fc, rows=rows),
        grid=(nch,),
        in_specs=[full(z), full(t), full(deltas), lay(w1p), lay(b1), lay(fr), lay(w2), lay(b2), lay(w3),
                  pl.BlockSpec((None, 2 * fc, seq), lambda j: (j, 0, 0))],
        out_specs=pl.BlockSpec((None, 2 * fc, width), lambda j: (j, 0, 0)),
        out_shape=jax.ShapeDtypeStruct((nch, 2 * fc, width), F32),
        scratch_shapes=[pltpu.VMEM((seq, 2 * width), F32),
                        pltpu.VMEM((2, seq, width), BF16),
                        pltpu.VMEM((1, width), F32)],
        compiler_params=_params(("arbitrary",), 48), name="hyena_filter",
    )(z, t, deltas, w1p, b1, fr, w2, b2, w3, fwd)


def _hyena_kernel(v_ref, x1_ref, x2_ref, gc_ref, cw_ref, cb_ref, hb_ref, fwd_ref, inv_ref, kf_ref,
                  o_ref, z_ref, zb_ref, y_ref, *, seq, fc, sb, tc):
    o = pl.program_id(2)
    j = pl.program_id(3)
    last_j = pl.num_programs(3) - 1
    pos = lax.broadcasted_iota(jnp.int32, (seq, 1), 0)

    def short_conv(ref, k, s):
        x = ref[s * seq:(s + 1) * seq, :].astype(F32)
        prev = jnp.where(pos == 0, 0.0, pltpu.roll(x, 1, 0))
        nxt = jnp.where(pos == seq - 1, 0.0, pltpu.roll(x, seq - 1, 0))
        return (prev * cw_ref[0, k:k + 1, :] + x * cw_ref[1, k:k + 1, :]
                + nxt * cw_ref[2, k:k + 1, :] + cb_ref[k:k + 1, :])

    @pl.when((o == 0) & (j == 0))
    def _():
        for s in range(sb):
            lanes = slice(s * tc, (s + 1) * tc)
            z = short_conv(v_ref, 0, s)
            z_ref[:, lanes] = z
            zb_ref[:, lanes] = z.astype(BF16)
        y_ref[...] = jnp.zeros_like(y_ref)

    @pl.when((o == 1) & (j == 0))
    def _():
        for s in range(sb):
            lanes = slice(s * tc, (s + 1) * tc)
            z = short_conv(x1_ref, 1, s) * (y_ref[:, lanes] + hb_ref[0:1, :] * z_ref[:, lanes])
            z_ref[:, lanes] = z
            zb_ref[:, lanes] = z.astype(BF16)
        y_ref[...] = jnp.zeros_like(y_ref)

    zf = _dot(fwd_ref[...], zb_ref[...])
    kr, ks = kf_ref[0:fc, :], kf_ref[fc:2 * fc, :]
    nyq = (lax.broadcasted_iota(jnp.int32, (fc, 1), 0) == 0) & (j == 0)
    parts = []
    for s in range(sb):
        lanes = slice(s * tc, (s + 1) * tc)
        zr, zs = zf[0:fc, lanes], zf[fc:2 * fc, lanes]
        ss = zs * ks
        yr = zr * kr - jnp.where(nyq, 0.0, ss)
        ys = jnp.where(nyq, ss, zr * ks + zs * kr)
        parts.append(jnp.concatenate([yr, ys], axis=0).astype(BF16))
    y_ref[...] += _dot(inv_ref[...], jnp.concatenate(parts, axis=-1))

    @pl.when((o == 1) & (j == last_j))
    def _():
        for s in range(sb):
            lanes = slice(s * tc, (s + 1) * tc)
            z = short_conv(x2_ref, 2, s) * (y_ref[:, lanes] + hb_ref[1:2, :] * z_ref[:, lanes])
            o_ref[s * seq:(s + 1) * seq, :] = (z * _silu(gc_ref[s * seq:(s + 1) * seq, :].astype(F32))).astype(BF16)


def _hyena(proj, cw, cb, hb, fwd, inv, kf, seq):
    m = proj.shape[0]
    nch, fc2, _ = fwd.shape
    fc = fc2 // 2
    _, tc, sb = _hyena_tiling(seq)
    sb = min(sb, m // seq)
    nct = HY_W // tc
    hc0 = OFF_HC // tc
    col = lambda k: pl.BlockSpec((sb * seq, tc), lambda b, c, o, j: (b, hc0 + k * nct + c))
    return pl.pallas_call(
        functools.partial(_hyena_kernel, seq=seq, fc=fc, sb=sb, tc=tc),
        grid=(m // (sb * seq), nct, HYENA_ORDER, nch),
        in_specs=[
            col(0), col(1), col(2),
            pl.BlockSpec((sb * seq, tc), lambda b, c, o, j: (b, OFF_GC // tc + c)),
            pl.BlockSpec((3, 3, tc), lambda b, c, o, j: (0, 0, c)),
            pl.BlockSpec((3, tc), lambda b, c, o, j: (0, c)),
            pl.BlockSpec((HYENA_ORDER, tc), lambda b, c, o, j: (0, c)),
            pl.BlockSpec((None, 2 * fc, seq), lambda b, c, o, j: (j, 0, 0)),
            pl.BlockSpec((None, seq, 2 * fc), lambda b, c, o, j: (j, 0, 0)),
            pl.BlockSpec((None, 2 * fc, tc), lambda b, c, o, j: (j, 0, o * nct + c)),
        ],
        out_specs=pl.BlockSpec((sb * seq, tc), lambda b, c, o, j: (b, c)),
        out_shape=jax.ShapeDtypeStruct((m, HY_W), BF16),
        scratch_shapes=[pltpu.VMEM((seq, sb * tc), F32), pltpu.VMEM((seq, sb * tc), BF16),
                        pltpu.VMEM((seq, sb * tc), F32)],
        compiler_params=_params(("arbitrary",) * 4, 56 if seq > 1024 else 28), name="hyena",
    )(proj, proj, proj, proj, cw, cb, hb, fwd, inv, kf)


def _merge_kernel(ya_ref, yb_ref, yc_ref, ga_ref, gb_ref, gc_ref, wa_ref, wb_ref, wc_ref, o_ref):
    half = o_ref.shape[1] // 2
    for c in range(2):
        cols = slice(c * half, (c + 1) * half)
        acc = ga_ref[:, cols].astype(F32) * _dot(ya_ref[...], wa_ref[:, cols])
        acc += gb_ref[:, cols].astype(F32) * _dot(yb_ref[...], wb_ref[:, cols])
        acc += gc_ref[:, cols].astype(F32) * _dot(yc_ref[...], wc_ref[:, cols])
        o_ref[:, cols] = acc.astype(BF16)


def _merge(ya, yb, yc, gates, w_bf):
    m = ya.shape[0]
    tm = min(ROW_TILE, m)
    tn = COL_TILE
    d = D_MODEL
    row = lambda w: pl.BlockSpec((tm, w), lambda i, j: (i, 0))
    gate = lambda k: pl.BlockSpec((tm, tn), lambda i, j: (i, k * d // tn + j))
    return pl.pallas_call(
        _merge_kernel, grid=(m // tm, d // tn),
        in_specs=[row(ATT_W), row(FNET_W), row(HY_W), gate(0), gate(1), gate(2),
                  pl.BlockSpec((ATT_W, tn), lambda i, j: (0, j)),
                  pl.BlockSpec((FNET_W, tn), lambda i, j: (ATT_W // FNET_W, j)),
                  pl.BlockSpec((HY_W, tn), lambda i, j: ((ATT_W + FNET_W) // HY_W, j))],
        out_specs=pl.BlockSpec((tm, tn), lambda i, j: (i, j)),
        out_shape=jax.ShapeDtypeStruct((m, d), BF16),
        compiler_params=_params(("arbitrary", "arbitrary"), 36), name="merge",
    )(ya, yb, yc, gates, gates, gates, w_bf, w_bf, w_bf)


def _outproj_kernel(mg_ref, w_ref, x_ref, ada_ref, o_ref):
    o_ref[...] = x_ref[...] + ada_ref[...] * _dot(mg_ref[...], w_ref[...])


def _outproj(merged, x2d, ada4, ada_row, w_bf, l):
    m, d = x2d.shape
    tm = min(ROW_TILE, m)
    tn = COL_TILE
    return pl.pallas_call(
        _outproj_kernel, grid=(m // tm, d // tn),
        in_specs=[pl.BlockSpec((tm, d), lambda i, j: (i, 0)),
                  pl.BlockSpec((d, tn), lambda i, j: (0, j)),
                  pl.BlockSpec((tm, tn), lambda i, j: (i, j)),
                  pl.BlockSpec((None, None, 1, tn),
                               lambda i, j: (l, ada_row(i * tm), 0, 2 * d // tn + j))],
        out_specs=pl.BlockSpec((tm, tn), lambda i, j: (i, j)),
        out_shape=jax.ShapeDtypeStruct((m, d), F32),
        compiler_params=_params(("arbitrary", "arbitrary"), 36), name="outproj",
    )(merged, w_bf, x2d, ada4)


def kernel(x_prompt, x_sample, cache_k, cache_v, c, c_ctx, norm_g, w_ada, b_ada, w_in, q_norm_g, k_norm_g, rpb, conv_w, conv_b, f_w1, f_b1, f_freq, f_w2, f_b2, f_w3, hy_bias, w_br, w_out):
    batch, seq, d = x_prompt.shape
    dec_batch, dec_seq, _ = x_sample.shape
    depth = norm_g.shape[0]

    ada4 = _ada(jnp.concatenate([c_ctx[None, :], c], axis=0), w_ada, b_ada)
    ada4 = ada4.reshape(depth, ADA_ROWS, 1, 3 * d)
    ctx_row = lambda tok: 0
    lat_row = lambda tok: 1 + tok // dec_seq

    w_in_bf = w_in[0].astype(BF16)
    norm_g3 = norm_g.reshape(depth, 1, d)
    qg3 = q_norm_g.reshape(depth, 1, HEAD_DIM)
    kg3 = k_norm_g.reshape(depth, 1, HEAD_DIM)
    w1p = jnp.pad(f_w1, ((0, 0), (0, 128 - FILTER_EMB), (0, 0)))
    b1 = f_b1.reshape(depth, 1, FILTER_FF)
    fr = f_freq.reshape(depth, 1, FILTER_FF)
    b2 = f_b2.reshape(depth, 1, FILTER_FF)
    win_bias = _window_bias(rpb)

    groups = []
    for x, s, row in ((x_prompt, seq, ctx_row), (x_sample, dec_seq, lat_row)):
        fwd, inv = _hyena_tables(s)
        groups.append(dict(x=x.reshape(-1, d), seq=s, row=row, fnet=_fnet_tables(s),
                           fwd=fwd, inv=inv, feats=_filter_features(s)))

    new_kv = ()
    for l in range(depth):
        cw = conv_w[l].reshape(3, 3, HY_W)
        cb = conv_b[l].reshape(3, HY_W)
        hb = hy_bias[l]
        for gi, g in enumerate(groups):
            s = g["seq"]
            if gi == 0:
                jobs = [(w_br, l), (w_out, l)] + ([(w_in, l + 1)] if l + 1 < depth else [])
                proj, gates, w_br_bf, w_out_bf, *w_in_next = _inproj(
                    g["x"], norm_g3, ada4, g["row"], w_in_bf, l, jobs)
            else:
                proj, gates = _inproj(g["x"], norm_g3, ada4, g["row"], w_in_bf, l)
            if gi == 0:
                ya, *new_kv = _ctx_attn(proj, qg3, kg3, new_kv, l, s, depth)
            else:
                qn, kn = _prep_lat(proj, qg3, kg3, l)
                ya = _lat_attn(qn, kn, cache_k, cache_v, win_bias, proj, l, s)
            yb = _fnet(proj, g["fnet"], s)
            kf = _filter(g["feats"], g["fwd"], w1p, b1, fr, f_w2, b2, f_w3, l, s)
            yc = _hyena(proj, cw, cb, hb, g["fwd"], g["inv"], kf, s)
            merged = _merge(ya, yb, yc, gates, w_br_bf)
            g["x"] = _outproj(merged, g["x"], ada4, g["row"], w_out_bf, l)
        if w_in_next:
            (w_in_bf,) = w_in_next

    nk, nv = new_kv
    return (groups[0]["x"].reshape(batch, seq, d),
            groups[1]["x"].reshape(dec_batch, dec_seq, d),
            nk.reshape(batch, depth, seq, N_HEADS, HEAD_DIM),
            nv.reshape(batch, depth, seq, N_HEADS, HEAD_DIM))
```
